```python
import jax, jax.numpy as jnp
from jax import lax
import numpy as np

D_MODEL = 1024
BATCH = 8
SEQ = 8192
DEPTH = 4

CHUNK = 64
Q_BLOCK = 128
N_BRANCHES = 3
BRANCH_WIDTH = 512
SB_HEADS = 8
SB_HEAD_DIM = BRANCH_WIDTH // SB_HEADS
CONV_CHANNELS = BRANCH_WIDTH
CONV_WIDTH = 31
GLA_HEADS = 4
GLA_KEY_DIM = BRANCH_WIDTH // 2
GLA_VALUE_DIM = BRANCH_WIDTH
GLA_HEAD_K = GLA_KEY_DIM // GLA_HEADS
GLA_HEAD_V = GLA_VALUE_DIM // GLA_HEADS
GLA_GATE_RANK = 16
GLA_GATE_TAU = 16.0
D_FF = 2816
NORM_EPS = 1e-6

IN_WIDTHS = (
    BRANCH_WIDTH, BRANCH_WIDTH, BRANCH_WIDTH,
    2 * CONV_CHANNELS,
    GLA_KEY_DIM, GLA_KEY_DIM, GLA_VALUE_DIM,
    GLA_VALUE_DIM,
    GLA_GATE_RANK,
    N_BRANCHES * D_MODEL,
)
IN_WIDTH = sum(IN_WIDTHS)

kernel_name = "hybrid_stickbreak_conformer_gla_trunk"


def rms_norm(x, g):
    xf = x.astype(jnp.float32)
    y = xf * lax.rsqrt(jnp.mean(xf * xf, axis=-1, keepdims=True) + NORM_EPS)
    return (y * g.astype(jnp.float32)).astype(x.dtype)


def swiglu_ffn(h, w_gate, w_up, w_down):
    return (jax.nn.silu(h @ w_gate) * (h @ w_up)) @ w_down


def stick_breaking_attention(q, k, v):
    b, nh, s, dh = q.shape
    nb = s // Q_BLOCK
    q_blocks = q.reshape(b, nh, nb, Q_BLOCK, dh).transpose(2, 0, 1, 3, 4)
    key_pos = jnp.arange(s)
    scale = dh ** -0.5

    def one_block(args):
        q_blk, blk = args
        z = jnp.einsum('bhqd,bhkd->bhqk', q_blk, k).astype(jnp.float32) * scale
        query_pos = blk * Q_BLOCK + jnp.arange(Q_BLOCK)
        earlier = key_pos[None, :] < query_pos[:, None]
        log_keep = jnp.where(earlier, jax.nn.log_sigmoid(-z), 0.0)
        log_stick = lax.cumsum(log_keep, axis=3, reverse=True) - log_keep
        w = jnp.where(earlier, jnp.exp(jax.nn.log_sigmoid(z) + log_stick), 0.0)
        return jnp.einsum('bhqk,bhkd->bhqd', w.astype(v.dtype), v)

    out = lax.map(one_block, (q_blocks, jnp.arange(nb)))
    return out.transpose(1, 2, 0, 3, 4).reshape(b, nh, s, dh)


def conformer_conv(u_glu, conv_w, conv_b, ln_g, ln_b):
    a, g = jnp.split(u_glu, 2, axis=-1)
    u = a * jax.nn.sigmoid(g)
    y = lax.conv_general_dilated(
        u, conv_w[:, None, :], window_strides=(1,),
        padding=[(CONV_WIDTH - 1, 0)],
        dimension_numbers=('NWC', 'WIO', 'NWC'),
        feature_group_count=CONV_CHANNELS) + conv_b
    yf = y.astype(jnp.float32)
    mu = jnp.mean(yf, axis=-1, keepdims=True)
    var = jnp.mean(jnp.square(yf - mu), axis=-1, keepdims=True)
    yn = (yf - mu) * lax.rsqrt(var + NORM_EPS) * ln_g.astype(jnp.float32) + ln_b.astype(jnp.float32)
    return jax.nn.silu(yn).astype(u.dtype)


def gla_chunked(q, k, v, log_alpha):
    b, s, nh, dk = q.shape
    dv = v.shape[-1]
    n = s // CHUNK

    def to_chunks(t):
        return t.reshape(b, n, CHUNK, nh, t.shape[-1]).transpose(1, 0, 2, 3, 4)

    la_c = to_chunks(log_alpha.astype(jnp.float32))
    decay_to_end = lax.cumsum(la_c, axis=2, reverse=True) - la_c
    chunk_decay = jnp.exp(jnp.sum(la_c, axis=2))
    k_dec = to_chunks(k).astype(jnp.float32) * jnp.exp(decay_to_end)
    q_c = to_chunks(q).astype(jnp.float32) * (dk ** -0.5)
    v_c = to_chunks(v).astype(jnp.float32)

    def step(state, xs):
        q_n, k_n, v_n, lam = xs
        state = lam[..., None] * state + jnp.einsum('bchk,bchv->bhkv', k_n, v_n)
        return state, jnp.einsum('bchk,bhkv->bchv', q_n, state)

    state0 = jnp.zeros((b, nh, dk, dv), jnp.float32)
    _, o = lax.scan(step, state0, (q_c, k_dec, v_c, chunk_decay))
    return o.transpose(1, 0, 2, 3, 4).reshape(b, s, nh, dv).astype(v.dtype)


def hybrid_mixer(h, w_in, conv_w, conv_b, conv_ln_g, conv_ln_b,
                 gla_w_alpha, gla_b_alpha, gla_norm_g, w_branch, w_out):
    b, s, _ = h.shape
    proj = h @ w_in
    offsets = [int(o) for o in np.cumsum(IN_WIDTHS)[:-1]]
    (sb_q, sb_k, sb_v, conv_in, gla_q, gla_k, gla_v, gla_r, gla_lr,
     gate_logits) = jnp.split(proj, offsets, axis=-1)

    def heads(t, nh):
        return t.reshape(b, s, nh, -1)

    sb_out = stick_breaking_attention(
        heads(sb_q, SB_HEADS).transpose(0, 2, 1, 3),
        heads(sb_k, SB_HEADS).transpose(0, 2, 1, 3),
        heads(sb_v, SB_HEADS).transpose(0, 2, 1, 3))
    sb_out = sb_out.transpose(0, 2, 1, 3).reshape(b, s, BRANCH_WIDTH)

    conv_out = conformer_conv(conv_in, conv_w, conv_b, conv_ln_g, conv_ln_b)

    log_alpha = jax.nn.log_sigmoid((gla_lr @ gla_w_alpha + gla_b_alpha).astype(jnp.float32)) / GLA_GATE_TAU
    gla_o = gla_chunked(heads(gla_q, GLA_HEADS), heads(gla_k, GLA_HEADS),
                        heads(gla_v, GLA_HEADS), heads(log_alpha, GLA_HEADS))
    gla_o = rms_norm(gla_o, gla_norm_g.reshape(GLA_HEADS, GLA_HEAD_V))
    gla_out = gla_o.reshape(b, s, GLA_VALUE_DIM) * jax.nn.silu(gla_r)

    branches = jnp.stack([sb_out, conv_out, gla_out], axis=2)
    branch_d = jnp.einsum('bsgw,gwd->bsgd', branches, w_branch)
    gates = jax.nn.sigmoid(gate_logits.reshape(b, s, N_BRANCHES, D_MODEL))
    merged = jnp.sum(gates * branch_d, axis=2)
    return merged @ w_out


def _fwd_setup_inputs(seed: int = 0) -> dict:
    key = jax.random.key(seed)
    ks = jax.random.split(key, 20)

    def nrm(k, shape, scale):
        return jax.random.normal(k, shape, jnp.float32) * scale

    L, D = DEPTH, D_MODEL
    return {
        "x": nrm(ks[0], (BATCH, SEQ, D), 1.0),
        "norm_pre": 1.0 + nrm(ks[1], (L, 3, D), 0.05),
        "norm_post": 1.0 + nrm(ks[2], (L, 3, D), 0.05),
        "ffn1_w_gate": nrm(ks[3], (L, D, D_FF), D ** -0.5),
        "ffn1_w_up": nrm(ks[4], (L, D, D_FF), D ** -0.5),
        "ffn1_w_down": nrm(ks[5], (L, D_FF, D), D_FF ** -0.5),
        "ffn2_w_gate": nrm(ks[6], (L, D, D_FF), D ** -0.5),
        "ffn2_w_up": nrm(ks[7], (L, D, D_FF), D ** -0.5),
        "ffn2_w_down": nrm(ks[8], (L, D_FF, D), D_FF ** -0.5),
        "w_in": nrm(ks[9], (L, D, IN_WIDTH), D ** -0.5),
        "conv_w": nrm(ks[10], (L, CONV_WIDTH, CONV_CHANNELS), CONV_WIDTH ** -0.5),
        "conv_b": nrm(ks[11], (L, CONV_CHANNELS), 0.02),
        "conv_ln_g": 1.0 + nrm(ks[12], (L, CONV_CHANNELS), 0.05),
        "conv_ln_b": nrm(ks[13], (L, CONV_CHANNELS), 0.02),
        "gla_w_alpha": nrm(ks[14], (L, GLA_GATE_RANK, GLA_KEY_DIM), GLA_GATE_RANK ** -0.5),
        "gla_b_alpha": nrm(ks[15], (L, GLA_KEY_DIM), 0.02),
        "gla_norm_g": 1.0 + nrm(ks[16], (L, GLA_VALUE_DIM), 0.05),
        "w_branch": nrm(ks[17], (L, N_BRANCHES, BRANCH_WIDTH, D), BRANCH_WIDTH ** -0.5),
        "w_out": nrm(ks[18], (L, D, D), D ** -0.5),
    }


def _fwd_reference(x, norm_pre, norm_post, ffn1_w_gate, ffn1_w_up, ffn1_w_down,
              ffn2_w_gate, ffn2_w_up, ffn2_w_down, w_in, conv_w, conv_b,
              conv_ln_g, conv_ln_b, gla_w_alpha, gla_b_alpha, gla_norm_g,
              w_branch, w_out):
    for l in range(DEPTH):
        h = rms_norm(x, norm_pre[l, 0])
        x = x + 0.5 * rms_norm(swiglu_ffn(h, ffn1_w_gate[l], ffn1_w_up[l], ffn1_w_down[l]), norm_post[l, 0])
        h = rms_norm(x, norm_pre[l, 1])
        m = hybrid_mixer(h, w_in[l], conv_w[l], conv_b[l], conv_ln_g[l], conv_ln_b[l],
                         gla_w_alpha[l], gla_b_alpha[l], gla_norm_g[l], w_branch[l], w_out[l])
        x = x + rms_norm(m, norm_post[l, 1])
        h = rms_norm(x, norm_pre[l, 2])
        x = x + 0.5 * rms_norm(swiglu_ffn(h, ffn2_w_gate[l], ffn2_w_up[l], ffn2_w_down[l]), norm_post[l, 2])
    return x


import jax as _jax
import jax.numpy as _jnp

TWIN_FORMAT = 'train_step'
FWD_PARAMS = ['x', 'norm_pre', 'norm_post', 'ffn1_w_gate', 'ffn1_w_up', 'ffn1_w_down', 'ffn2_w_gate', 'ffn2_w_up', 'ffn2_w_down', 'w_in', 'conv_w', 'conv_b', 'conv_ln_g', 'conv_ln_b', 'gla_w_alpha', 'gla_b_alpha', 'gla_norm_g', 'w_branch', 'w_out']
TWIN_WEIGHTS = ['norm_pre', 'norm_post', 'ffn1_w_gate', 'ffn1_w_up', 'ffn1_w_down', 'ffn2_w_gate', 'ffn2_w_up', 'ffn2_w_down', 'w_in', 'conv_w', 'conv_b', 'conv_ln_g', 'conv_ln_b', 'gla_w_alpha', 'gla_b_alpha', 'gla_norm_g', 'w_branch', 'w_out']
TWIN_DIFF_INPUT = 'x'
TWIN_INPUTS = ['x', 'norm_pre', 'norm_post', 'ffn1_w_gate', 'ffn1_w_up', 'ffn1_w_down', 'ffn2_w_gate', 'ffn2_w_up', 'ffn2_w_down', 'w_in', 'conv_w', 'conv_b', 'conv_ln_g', 'conv_ln_b', 'gla_w_alpha', 'gla_b_alpha', 'gla_norm_g', 'w_branch', 'w_out', 'loss_target', 'm_norm_pre', 'm_norm_post', 'm_ffn1_w_gate', 'm_ffn1_w_up', 'm_ffn1_w_down', 'm_ffn2_w_gate', 'm_ffn2_w_up', 'm_ffn2_w_down', 'm_w_in', 'm_conv_w', 'm_conv_b', 'm_conv_ln_g', 'm_conv_ln_b', 'm_gla_w_alpha', 'm_gla_b_alpha', 'm_gla_norm_g', 'm_w_branch', 'm_w_out', 'v_norm_pre', 'v_norm_post', 'v_ffn1_w_gate', 'v_ffn1_w_up', 'v_ffn1_w_down', 'v_ffn2_w_gate', 'v_ffn2_w_up', 'v_ffn2_w_down', 'v_w_in', 'v_conv_w', 'v_conv_b', 'v_conv_ln_g', 'v_conv_ln_b', 'v_gla_w_alpha', 'v_gla_b_alpha', 'v_gla_norm_g', 'v_w_branch', 'v_w_out']
TWIN_OUTPUTS = ['loss', 'grad_x', 'grad_norm_pre', 'grad_norm_post', 'grad_ffn1_w_gate', 'grad_ffn1_w_up', 'grad_ffn1_w_down', 'grad_ffn2_w_gate', 'grad_ffn2_w_up', 'grad_ffn2_w_down', 'grad_w_in', 'grad_conv_w', 'grad_conv_b', 'grad_conv_ln_g', 'grad_conv_ln_b', 'grad_gla_w_alpha', 'grad_gla_b_alpha', 'grad_gla_norm_g', 'grad_w_branch', 'grad_w_out', 'delta_norm_pre', 'delta_norm_post', 'delta_ffn1_w_gate', 'delta_ffn1_w_up', 'delta_ffn1_w_down', 'delta_ffn2_w_gate', 'delta_ffn2_w_up', 'delta_ffn2_w_down', 'delta_w_in', 'delta_conv_w', 'delta_conv_b', 'delta_conv_ln_g', 'delta_conv_ln_b', 'delta_gla_w_alpha', 'delta_gla_b_alpha', 'delta_gla_norm_g', 'delta_w_branch', 'delta_w_out', 'new_m_norm_pre', 'new_m_norm_post', 'new_m_ffn1_w_gate', 'new_m_ffn1_w_up', 'new_m_ffn1_w_down', 'new_m_ffn2_w_gate', 'new_m_ffn2_w_up', 'new_m_ffn2_w_down', 'new_m_w_in', 'new_m_conv_w', 'new_m_conv_b', 'new_m_conv_ln_g', 'new_m_conv_ln_b', 'new_m_gla_w_alpha', 'new_m_gla_b_alpha', 'new_m_gla_norm_g', 'new_m_w_branch', 'new_m_w_out', 'new_v_norm_pre', 'new_v_norm_post', 'new_v_ffn1_w_gate', 'new_v_ffn1_w_up', 'new_v_ffn1_w_down', 'new_v_ffn2_w_gate', 'new_v_ffn2_w_up', 'new_v_ffn2_w_down', 'new_v_w_in', 'new_v_conv_w', 'new_v_conv_b', 'new_v_conv_ln_g', 'new_v_conv_ln_b', 'new_v_gla_w_alpha', 'new_v_gla_b_alpha', 'new_v_gla_norm_g', 'new_v_w_branch', 'new_v_w_out']
TWIN_LEAF_KINDS = {'loss': 'loss', 'grad_x': 'grad_x', 'grad_norm_pre': 'grad_w', 'grad_norm_post': 'grad_w', 'grad_ffn1_w_gate': 'grad_w', 'grad_ffn1_w_up': 'grad_w', 'grad_ffn1_w_down': 'grad_w', 'grad_ffn2_w_gate': 'grad_w', 'grad_ffn2_w_up': 'grad_w', 'grad_ffn2_w_down': 'grad_w', 'grad_w_in': 'grad_w', 'grad_conv_w': 'grad_w', 'grad_conv_b': 'grad_w', 'grad_conv_ln_g': 'grad_w', 'grad_conv_ln_b': 'grad_w', 'grad_gla_w_alpha': 'grad_w', 'grad_gla_b_alpha': 'grad_w', 'grad_gla_norm_g': 'grad_w', 'grad_w_branch': 'grad_w', 'grad_w_out': 'grad_w', 'delta_norm_pre': 'delta_w', 'delta_norm_post': 'delta_w', 'delta_ffn1_w_gate': 'delta_w', 'delta_ffn1_w_up': 'delta_w', 'delta_ffn1_w_down': 'delta_w', 'delta_ffn2_w_gate': 'delta_w', 'delta_ffn2_w_up': 'delta_w', 'delta_ffn2_w_down': 'delta_w', 'delta_w_in': 'delta_w', 'delta_conv_w': 'delta_w', 'delta_conv_b': 'delta_w', 'delta_conv_ln_g': 'delta_w', 'delta_conv_ln_b': 'delta_w', 'delta_gla_w_alpha': 'delta_w', 'delta_gla_b_alpha': 'delta_w', 'delta_gla_norm_g': 'delta_w', 'delta_w_branch': 'delta_w', 'delta_w_out': 'delta_w', 'new_m_norm_pre': 'new_m', 'new_m_norm_post': 'new_m', 'new_m_ffn1_w_gate': 'new_m', 'new_m_ffn1_w_up': 'new_m', 'new_m_ffn1_w_down': 'new_m', 'new_m_ffn2_w_gate': 'new_m', 'new_m_ffn2_w_up': 'new_m', 'new_m_ffn2_w_down': 'new_m', 'new_m_w_in': 'new_m', 'new_m_conv_w': 'new_m', 'new_m_conv_b': 'new_m', 'new_m_conv_ln_g': 'new_m', 'new_m_conv_ln_b': 'new_m', 'new_m_gla_w_alpha': 'new_m', 'new_m_gla_b_alpha': 'new_m', 'new_m_gla_norm_g': 'new_m', 'new_m_w_branch': 'new_m', 'new_m_w_out': 'new_m', 'new_v_norm_pre': 'new_v', 'new_v_norm_post': 'new_v', 'new_v_ffn1_w_gate': 'new_v', 'new_v_ffn1_w_up': 'new_v', 'new_v_ffn1_w_down': 'new_v', 'new_v_ffn2_w_gate': 'new_v', 'new_v_ffn2_w_up': 'new_v', 'new_v_ffn2_w_down': 'new_v', 'new_v_w_in': 'new_v', 'new_v_conv_w': 'new_v', 'new_v_conv_b': 'new_v', 'new_v_conv_ln_g': 'new_v', 'new_v_conv_ln_b': 'new_v', 'new_v_gla_w_alpha': 'new_v', 'new_v_gla_b_alpha': 'new_v', 'new_v_gla_norm_g': 'new_v', 'new_v_w_branch': 'new_v', 'new_v_w_out': 'new_v'}


def _forward(args):
    return _fwd_reference(*[args[k] for k in FWD_PARAMS])


def _output_shape():
    def fwd():
        inp = _fwd_setup_inputs(0)
        return _fwd_reference(*[inp[k] for k in FWD_PARAMS])
    out = _jax.eval_shape(fwd)
    return out.shape, out.dtype

N_MICROBATCH = 1
ADAM_LR = 0.001
ADAM_B1 = 0.9
ADAM_B2 = 0.999
ADAM_EPS = 1e-08
ADAM_WD = 0.01
ADAM_STEP = 10
PER_EXAMPLE_BATCH_AXIS = {'x': 0, 'loss_target': 0}
SHARED_INPUTS = []
_WEIGHT_DTYPES = {'norm_pre': _jnp.float32, 'norm_post': _jnp.float32, 'ffn1_w_gate': _jnp.float32, 'ffn1_w_up': _jnp.float32, 'ffn1_w_down': _jnp.float32, 'ffn2_w_gate': _jnp.float32, 'ffn2_w_up': _jnp.float32, 'ffn2_w_down': _jnp.float32, 'w_in': _jnp.float32, 'conv_w': _jnp.float32, 'conv_b': _jnp.float32, 'conv_ln_g': _jnp.float32, 'conv_ln_b': _jnp.float32, 'gla_w_alpha': _jnp.float32, 'gla_b_alpha': _jnp.float32, 'gla_norm_g': _jnp.float32, 'w_branch': _jnp.float32, 'w_out': _jnp.float32}
MOMENT_SCALE = {'norm_pre': 2.032398e+00, 'norm_post': 3.924825e+01, 'ffn1_w_gate': 7.038260e-01, 'ffn1_w_up': 7.558845e-01, 'ffn1_w_down': 1.253905e+00, 'ffn2_w_gate': 4.652841e-01, 'ffn2_w_up': 6.267251e-01, 'ffn2_w_down': 1.042512e+00, 'w_in': 1.044443e+00, 'conv_w': 2.037444e+00, 'conv_b': 2.847925e+01, 'conv_ln_g': 1.096556e+01, 'conv_ln_b': 1.684016e+01, 'gla_w_alpha': 1.956025e-01, 'gla_b_alpha': 8.456498e-01, 'gla_norm_g': 1.148398e+00, 'w_branch': 2.690307e+00, 'w_out': 4.742949e+00}


def _to_microbatches(a, axis):
    t = _jnp.moveaxis(a, axis, 0)
    t = t.reshape((N_MICROBATCH, t.shape[0] // N_MICROBATCH) + t.shape[1:])
    return _jnp.moveaxis(t, 1, axis + 1)


def setup_inputs(seed: int = 0) -> dict:
    inp = _fwd_setup_inputs(seed)
    key = _jax.random.fold_in(_jax.random.key(seed), 7919)
    shape, _ = _output_shape()
    out = dict(inp)
    out["loss_target"] = _jax.random.normal(_jax.random.fold_in(key, 0), shape, _jnp.float32)
    for i, name in enumerate(TWIN_WEIGHTS):
        w = inp[name].astype(_jnp.float32)
        if MOMENT_SCALE is None:
            s = _jnp.sqrt(_jnp.mean(_jnp.square(w)) + 1e-30)
        else:
            s = MOMENT_SCALE[name]
        km, kv = _jax.random.split(_jax.random.fold_in(key, i + 1))
        out[name] = w
        out["m_" + name] = s * _jax.random.normal(km, w.shape, _jnp.float32)
        out["v_" + name] = (s * s) * _jax.random.uniform(kv, w.shape, _jnp.float32, 0.5, 1.5)
    if N_MICROBATCH > 1:
        for name, axis in PER_EXAMPLE_BATCH_AXIS.items():
            out[name] = _to_microbatches(out[name], axis)
    return {'x': out['x'], 'norm_pre': out['norm_pre'], 'norm_post': out['norm_post'], 'ffn1_w_gate': out['ffn1_w_gate'], 'ffn1_w_up': out['ffn1_w_up'], 'ffn1_w_down': out['ffn1_w_down'], 'ffn2_w_gate': out['ffn2_w_gate'], 'ffn2_w_up': out['ffn2_w_up'], 'ffn2_w_down': out['ffn2_w_down'], 'w_in': out['w_in'], 'conv_w': out['conv_w'], 'conv_b': out['conv_b'], 'conv_ln_g': out['conv_ln_g'], 'conv_ln_b': out['conv_ln_b'], 'gla_w_alpha': out['gla_w_alpha'], 'gla_b_alpha': out['gla_b_alpha'], 'gla_norm_g': out['gla_norm_g'], 'w_branch': out['w_branch'], 'w_out': out['w_out'], 'loss_target': out['loss_target'], 'm_norm_pre': out['m_norm_pre'], 'm_norm_post': out['m_norm_post'], 'm_ffn1_w_gate': out['m_ffn1_w_gate'], 'm_ffn1_w_up': out['m_ffn1_w_up'], 'm_ffn1_w_down': out['m_ffn1_w_down'], 'm_ffn2_w_gate': out['m_ffn2_w_gate'], 'm_ffn2_w_up': out['m_ffn2_w_up'], 'm_ffn2_w_down': out['m_ffn2_w_down'], 'm_w_in': out['m_w_in'], 'm_conv_w': out['m_conv_w'], 'm_conv_b': out['m_conv_b'], 'm_conv_ln_g': out['m_conv_ln_g'], 'm_conv_ln_b': out['m_conv_ln_b'], 'm_gla_w_alpha': out['m_gla_w_alpha'], 'm_gla_b_alpha': out['m_gla_b_alpha'], 'm_gla_norm_g': out['m_gla_norm_g'], 'm_w_branch': out['m_w_branch'], 'm_w_out': out['m_w_out'], 'v_norm_pre': out['v_norm_pre'], 'v_norm_post': out['v_norm_post'], 'v_ffn1_w_gate': out['v_ffn1_w_gate'], 'v_ffn1_w_up': out['v_ffn1_w_up'], 'v_ffn1_w_down': out['v_ffn1_w_down'], 'v_ffn2_w_gate': out['v_ffn2_w_gate'], 'v_ffn2_w_up': out['v_ffn2_w_up'], 'v_ffn2_w_down': out['v_ffn2_w_down'], 'v_w_in': out['v_w_in'], 'v_conv_w': out['v_conv_w'], 'v_conv_b': out['v_conv_b'], 'v_conv_ln_g': out['v_conv_ln_g'], 'v_conv_ln_b': out['v_conv_ln_b'], 'v_gla_w_alpha': out['v_gla_w_alpha'], 'v_gla_b_alpha': out['v_gla_b_alpha'], 'v_gla_norm_g': out['v_gla_norm_g'], 'v_w_branch': out['v_w_branch'], 'v_w_out': out['v_w_out']}


def _loss(weights, diff, rest, loss_target):
    with _jax.named_scope("forward"):
        args = {**rest, TWIN_DIFF_INPUT: diff, **{k: w.astype(_WEIGHT_DTYPES[k]) for k, w in weights.items()}}
        y = _forward(args)
    with _jax.named_scope("loss_head"):
        err = _jnp.square(y.astype(_jnp.float32) - loss_target)
        return 0.5 * _jnp.sum(_jnp.mean(err, axis=-1)) if err.ndim else 0.5 * err


def _adamw(w, g, m, v):
    m = ADAM_B1 * m + (1.0 - ADAM_B1) * g
    v = ADAM_B2 * v + (1.0 - ADAM_B2) * _jnp.square(g)
    m_hat = m / (1.0 - ADAM_B1 ** ADAM_STEP)
    v_hat = v / (1.0 - ADAM_B2 ** ADAM_STEP)
    delta = -ADAM_LR * (m_hat / (_jnp.sqrt(v_hat) + ADAM_EPS) + ADAM_WD * w)
    return delta, m, v


def reference(x, norm_pre, norm_post, ffn1_w_gate, ffn1_w_up, ffn1_w_down, ffn2_w_gate, ffn2_w_up, ffn2_w_down, w_in, conv_w, conv_b, conv_ln_g, conv_ln_b, gla_w_alpha, gla_b_alpha, gla_norm_g, w_branch, w_out, loss_target, m_norm_pre, m_norm_post, m_ffn1_w_gate, m_ffn1_w_up, m_ffn1_w_down, m_ffn2_w_gate, m_ffn2_w_up, m_ffn2_w_down, m_w_in, m_conv_w, m_conv_b, m_conv_ln_g, m_conv_ln_b, m_gla_w_alpha, m_gla_b_alpha, m_gla_norm_g, m_w_branch, m_w_out, v_norm_pre, v_norm_post, v_ffn1_w_gate, v_ffn1_w_up, v_ffn1_w_down, v_ffn2_w_gate, v_ffn2_w_up, v_ffn2_w_down, v_w_in, v_conv_w, v_conv_b, v_conv_ln_g, v_conv_ln_b, v_gla_w_alpha, v_gla_b_alpha, v_gla_norm_g, v_w_branch, v_w_out):
    given = dict(x=x, norm_pre=norm_pre, norm_post=norm_post, ffn1_w_gate=ffn1_w_gate, ffn1_w_up=ffn1_w_up, ffn1_w_down=ffn1_w_down, ffn2_w_gate=ffn2_w_gate, ffn2_w_up=ffn2_w_up, ffn2_w_down=ffn2_w_down, w_in=w_in, conv_w=conv_w, conv_b=conv_b, conv_ln_g=conv_ln_g, conv_ln_b=conv_ln_b, gla_w_alpha=gla_w_alpha, gla_b_alpha=gla_b_alpha, gla_norm_g=gla_norm_g, w_branch=w_branch, w_out=w_out, loss_target=loss_target, m_norm_pre=m_norm_pre, m_norm_post=m_norm_post, m_ffn1_w_gate=m_ffn1_w_gate, m_ffn1_w_up=m_ffn1_w_up, m_ffn1_w_down=m_ffn1_w_down, m_ffn2_w_gate=m_ffn2_w_gate, m_ffn2_w_up=m_ffn2_w_up, m_ffn2_w_down=m_ffn2_w_down, m_w_in=m_w_in, m_conv_w=m_conv_w, m_conv_b=m_conv_b, m_conv_ln_g=m_conv_ln_g, m_conv_ln_b=m_conv_ln_b, m_gla_w_alpha=m_gla_w_alpha, m_gla_b_alpha=m_gla_b_alpha, m_gla_norm_g=m_gla_norm_g, m_w_branch=m_w_branch, m_w_out=m_w_out, v_norm_pre=v_norm_pre, v_norm_post=v_norm_post, v_ffn1_w_gate=v_ffn1_w_gate, v_ffn1_w_up=v_ffn1_w_up, v_ffn1_w_down=v_ffn1_w_down, v_ffn2_w_gate=v_ffn2_w_gate, v_ffn2_w_up=v_ffn2_w_up, v_ffn2_w_down=v_ffn2_w_down, v_w_in=v_w_in, v_conv_w=v_conv_w, v_conv_b=v_conv_b, v_conv_ln_g=v_conv_ln_g, v_conv_ln_b=v_conv_ln_b, v_gla_w_alpha=v_gla_w_alpha, v_gla_b_alpha=v_gla_b_alpha, v_gla_norm_g=v_gla_norm_g, v_w_branch=v_w_branch, v_w_out=v_w_out)
    weights = {n: given[n] for n in TWIN_WEIGHTS}
    shared = {n: given[n] for n in SHARED_INPUTS}
    per_example = {n: given[n] for n in ['x']}
    grad_fn = _jax.value_and_grad(_loss, argnums=(0, 1))

    def one_microbatch(ex, loss_target):
        ex = dict(ex)
        diff = ex.pop(TWIN_DIFF_INPUT)
        return grad_fn(weights, diff, {**shared, **ex}, loss_target)

    if N_MICROBATCH == 1:
        loss, (grad_w, grad_x) = one_microbatch(per_example, given["loss_target"])
    else:
        def body(carry, xs):
            loss_sum, grad_sum = carry
            l_k, (gw_k, gx_k) = one_microbatch(xs[0], xs[1])
            with _jax.named_scope("update"):
                return (loss_sum + l_k, _jax.tree.map(_jnp.add, grad_sum, gw_k)), gx_k

        init = (_jnp.zeros((), _jnp.float32), _jax.tree.map(_jnp.zeros_like, weights))
        (loss, grad_w), grad_x = _jax.lax.scan(body, init, (per_example, given["loss_target"]))
    with _jax.named_scope("update"):
        delta_w, new_m, new_v = {}, {}, {}
        for n in TWIN_WEIGHTS:
            delta_w[n], new_m[n], new_v[n] = _adamw(weights[n], grad_w[n], given["m_" + n], given["v_" + n])
    return (loss, grad_x, *[grad_w[n] for n in TWIN_WEIGHTS], *[delta_w[n] for n in TWIN_WEIGHTS],
            *[new_m[n] for n in TWIN_WEIGHTS], *[new_v[n] for n in TWIN_WEIGHTS])
```

```python
import functools
import math

import numpy as np
import jax
import jax.numpy as jnp
from jax import lax
from jax.experimental import pallas as pl
from jax.experimental.pallas import tpu as pltpu

F32 = jnp.float32
BF16 = jnp.bfloat16

VMEM_LIMIT_BYTES = 48 * 1024 * 1024
LANE = 128

NORM_EPS = 1e-6
CHUNK = 64
N_BRANCHES = 3
BRANCH_WIDTH = 512
SB_HEADS = 8
SB_HEAD_DIM = 64
CONV_WIDTH = 31
CONV_HALO = 32
GLA_HEADS = 4
GLA_HEAD_K = 64
GLA_HEAD_V = 128
GLA_KEY_DIM = GLA_HEADS * GLA_HEAD_K
GLA_VALUE_DIM = GLA_HEADS * GLA_HEAD_V
GLA_GATE_RANK = 16
GLA_GATE_TAU = 16.0
SB_BQ = 256
SB_BK = 128

ADAM_LR = 0.001
ADAM_B1 = 0.9
ADAM_B2 = 0.999
ADAM_EPS = 1e-08
ADAM_WD = 0.01
ADAM_STEP = 10

N_DEV = 8
MESH_AXES = ("x", "y", "c")
PACK_COLS = 1024


def _params(sem):
    return pltpu.CompilerParams(dimension_semantics=sem, vmem_limit_bytes=VMEM_LIMIT_BYTES)


def _div_tile(n, cap, unit):
    if n <= cap:
        return n
    best = None
    for t in range(unit, cap + 1, unit):
        if n % t == 0:
            best = t
    assert best is not None, (n, cap, unit)
    return best


_NN = (((1,), (0,)), ((), ()))
_NT = (((1,), (1,)), ((), ()))


def _dot(a, b, dims=_NN):
    return lax.dot_general(a, b, dims, preferred_element_type=F32)


def _sigmoid(v):
    return 1.0 / (1.0 + jnp.exp(-v))


def _mm(a, b, *, nt=False, out_dtype=F32, name):
    a = a.astype(BF16)
    b = b.astype(BF16)
    m, k = a.shape
    n = b.shape[0] if nt else b.shape[1]
    assert (b.shape[1] if nt else b.shape[0]) == k
    tm = _div_tile(m, 512, 16)
    tn = _div_tile(n, 2048, LANE)
    tk = _div_tile(k, 2048, LANE)
    nk = k // tk
    dims = _NT if nt else _NN

    def kern(a_ref, b_ref, o_ref, acc_ref):
        kk = pl.program_id(2)

        @pl.when(kk == 0)
        def _():
            acc_ref[...] = jnp.zeros_like(acc_ref)

        acc_ref[...] += _dot(a_ref[...], b_ref[...], dims)

        @pl.when(kk == nk - 1)
        def _():
            o_ref[...] = acc_ref[...].astype(o_ref.dtype)

    b_spec = (pl.BlockSpec((tn, tk), lambda i, j, kk: (j, kk)) if nt
              else pl.BlockSpec((tk, tn), lambda i, j, kk: (kk, j)))
    return pl.pallas_call(
        kern, name=name,
        out_shape=jax.ShapeDtypeStruct((m, n), out_dtype),
        grid=(m // tm, n // tn, nk),
        in_specs=[pl.BlockSpec((tm, tk), lambda i, j, kk: (i, kk)), b_spec],
        out_specs=pl.BlockSpec((tm, tn), lambda i, j, kk: (i, j)),
        scratch_shapes=[pltpu.VMEM((tm, tn), F32)],
        compiler_params=_params(("parallel", "parallel", "arbitrary")),
    )(a, b)


def _rowwise(name, body, mats, vecs, outs, sums=(), tm=256):
    mats = [m if isinstance(m, tuple) else (m, 0, m.shape[1]) for m in mats]
    t = mats[0][0].shape[0]
    tm = _div_tile(t, tm, 8)
    nm, nv, no, ns = len(mats), len(vecs), len(outs), len(sums)

    def kern(*refs):
        i = pl.program_id(0)
        ins = [r[...] for r in refs[:nm + nv]]
        res = body(*ins)
        out_vals, sum_vals = res[:no], res[no:]
        for r, val in zip(refs[nm + nv:nm + nv + no], out_vals):
            if isinstance(val, (list, tuple)):
                off = 0
                for piece in val:
                    w = piece.shape[1]
                    r[:, off:off + w] = piece.astype(r.dtype)
                    off += w
            else:
                r[...] = val.astype(r.dtype)
        if ns:
            sum_refs = refs[nm + nv + no:]

            @pl.when(i == 0)
            def _():
                for r in sum_refs:
                    r[...] = jnp.zeros_like(r)

            for r, val in zip(sum_refs, sum_vals):
                r[...] += jnp.sum(val, axis=0, keepdims=True)

    in_specs = [pl.BlockSpec((tm, w), functools.partial(lambda i, cb: (i, cb), cb=cb)) for (_, cb, w) in mats]
    in_specs += [pl.BlockSpec(v.shape, lambda i: (0, 0)) for v in vecs]
    out_specs = [pl.BlockSpec((tm, w), lambda i: (i, 0)) for (w, _) in outs]
    out_specs += [pl.BlockSpec((1, w), lambda i: (0, 0)) for w in sums]
    out_shape = [jax.ShapeDtypeStruct((t, w), dt) for (w, dt) in outs]
    out_shape += [jax.ShapeDtypeStruct((1, w), F32) for w in sums]
    return pl.pallas_call(
        kern, name=name, out_shape=out_shape, grid=(t // tm,),
        in_specs=in_specs, out_specs=out_specs,
        compiler_params=_params(("arbitrary",)),
    )(*[m[0] for m in mats], *vecs)


def _rms_fwd(x, g, name):
    d = x.shape[1]

    def body(xv, gv):
        r = lax.rsqrt(jnp.mean(xv * xv, axis=-1, keepdims=True) + NORM_EPS)
        return ((xv * r) * gv,)

    return _rowwise(name, body, [x], [g], [(d, BF16)])[0]


def _post_res(o, x, g, c, name):
    d = x.shape[1]

    def body(ov, xv, gv):
        r = lax.rsqrt(jnp.mean(ov * ov, axis=-1, keepdims=True) + NORM_EPS)
        return (xv + c * ((ov * r) * gv),)

    return _rowwise(name, body, [o, x], [g], [(d, F32)])[0]


def _post_bwd(dx, o, g, c, name):
    d = dx.shape[1]

    def body(dxv, ov, gv):
        r = lax.rsqrt(jnp.mean(ov * ov, axis=-1, keepdims=True) + NORM_EPS)
        n = ov * r
        dy = c * dxv
        dn = dy * gv
        do = r * (dn - n * jnp.mean(dn * n, axis=-1, keepdims=True))
        return (do, dy * n)

    return _rowwise(name, body, [dx, o], [g], [(d, BF16)], sums=[d])


def _rms_bwd(dh_parts, x, g, dx_res, name):
    d = x.shape[1]
    npart = len(dh_parts)

    def body(*vals):
        dh = vals[0]
        for p in vals[1:npart]:
            dh = dh + p
        xv, dres, gv = vals[npart], vals[npart + 1], vals[npart + 2]
        r = lax.rsqrt(jnp.mean(xv * xv, axis=-1, keepdims=True) + NORM_EPS)
        n = xv * r
        dn = dh * gv
        dx = dres + r * (dn - n * jnp.mean(dn * n, axis=-1, keepdims=True))
        return (dx, dh * n)

    return _rowwise(name, body, list(dh_parts) + [x, dx_res], [g], [(d, F32)], sums=[d], tm=128)


def _swiglu_fwd(gate, up, name):
    f = gate.shape[1]

    def body(gv, uv):
        gv = gv.astype(F32)
        uv = uv.astype(F32)
        return ((gv * _sigmoid(gv)) * uv,)

    return _rowwise(name, body, [gate, up], [], [(f, BF16)])[0]


def _swiglu_bwd(da, gate, up, name):
    f = gate.shape[1]

    def body(dav, gv, uv):
        gv = gv.astype(F32)
        uv = uv.astype(F32)
        s = _sigmoid(gv)
        silu = gv * s
        dgate = dav * uv * (s * (1.0 + gv * (1.0 - s)))
        dup = dav * silu
        return (dgate, dup)

    return _rowwise(name, body, [da, gate, up], [], [(f, BF16), (f, BF16)])


def _merge_fwd(bds, logits, name):
    d = bds[0].shape[1]

    def body(b0, b1, b2, l0, l1, l2):
        return (_sigmoid(l0) * b0 + _sigmoid(l1) * b1 + _sigmoid(l2) * b2,)

    mats = list(bds) + [(logits, j, d) for j in range(N_BRANCHES)]
    return _rowwise(name, body, mats, [], [(d, BF16)], tm=128)[0]


def _merge_bwd(dmerged, bds, logits, name):
    d = bds[0].shape[1]

    def body(dm, b0, b1, b2, l0, l1, l2):
        dbs, dls = [], []
        for b, l in ((b0, l0), (b1, l1), (b2, l2)):
            s = _sigmoid(l)
            dbs.append(dm * s)
            dls.append(dm * b * (s * (1.0 - s)))
        return (dbs[0], dbs[1], dbs[2], dls)

    mats = [dmerged] + list(bds) + [(logits, j, d) for j in range(N_BRANCHES)]
    outs = [(d, BF16)] * 3 + [(N_BRANCHES * d, BF16)]
    return _rowwise(name, body, mats, [], outs, tm=128)


def _la_fwd(pre, b, name):
    w = pre.shape[1]

    def body(pv, bv):
        p = pv + bv
        sp = jnp.maximum(-p, 0.0) + jnp.log(1.0 + jnp.exp(-jnp.abs(p)))
        return (-sp / GLA_GATE_TAU,)

    return _rowwise(name, body, [pre], [b], [(w, F32)])[0]


def _la_bwd(dla, pre, b, name):
    w = pre.shape[1]

    def body(dv, pv, bv):
        p = pv + bv
        dpre = (dv / GLA_GATE_TAU) * _sigmoid(-p)
        return (dpre, dpre)

    return _rowwise(name, body, [dla, pre], [b], [(w, BF16)], sums=[w])


def _gla_post_fwd(o, p_gla, r_block, gn, name):
    w = o.shape[1]

    def body(ov, rv, gv):
        pieces = []
        for h in range(GLA_HEADS):
            sl = slice(h * GLA_HEAD_V, (h + 1) * GLA_HEAD_V)
            oh = ov[:, sl]
            rr = lax.rsqrt(jnp.mean(oh * oh, axis=-1, keepdims=True) + NORM_EPS)
            rh = rv[:, sl]
            pieces.append(((oh * rr) * gv[:, sl]) * (rh * _sigmoid(rh)))
        return (pieces,)

    return _rowwise(name, body, [o, (p_gla, r_block, w)], [gn], [(w, BF16)])[0]


def _gla_post_bwd(dout, o, p_gla, r_block, gn, name):
    w = o.shape[1]

    def body(dv, ov, rv, gv):
        dos, drs, dgs = [], [], []
        for h in range(GLA_HEADS):
            sl = slice(h * GLA_HEAD_V, (h + 1) * GLA_HEAD_V)
            oh, rh, gh, dh = ov[:, sl], rv[:, sl], gv[:, sl], dv[:, sl]
            rr = lax.rsqrt(jnp.mean(oh * oh, axis=-1, keepdims=True) + NORM_EPS)
            nhat = oh * rr
            s = _sigmoid(rh)
            dn = dh * (rh * s)
            drs.append(dh * (nhat * gh) * (s * (1.0 + rh * (1.0 - s))))
            dnn = dn * gh
            dos.append(rr * (dnn - nhat * jnp.mean(dnn * nhat, axis=-1, keepdims=True)))
            dgs.append(dn * nhat)
        return (dos, drs, jnp.concatenate(dgs, axis=1))

    return _rowwise(name, body, [dout, o, (p_gla, r_block, w)], [gn], [(w, F32), (w, BF16)], sums=[w])


def _conv_bwd1(dout, y, ln_g, ln_b, name):
    w = y.shape[1]

    def body(dv, yv, gv, bv):
        mu = jnp.mean(yv, axis=-1, keepdims=True)
        yc = yv - mu
        rstd = lax.rsqrt(jnp.mean(yc * yc, axis=-1, keepdims=True) + NORM_EPS)
        xhat = yc * rstd
        yn = xhat * gv + bv
        s = _sigmoid(yn)
        dyn = dv * (s * (1.0 + yn * (1.0 - s)))
        dxh = dyn * gv
        dy = rstd * (dxh - jnp.mean(dxh, axis=-1, keepdims=True)
                     - xhat * jnp.mean(dxh * xhat, axis=-1, keepdims=True))
        return (dy, dyn * xhat, dyn, dy)

    return _rowwise(name, body, [dout, y], [ln_g, ln_b], [(w, F32)], sums=[w, w, w])


def _loss_fwd(y, target, name):
    d = y.shape[1]

    def body(yv, tv):
        e = yv - tv
        return (e / d, e * e)

    return _rowwise(name, body, [y, target], [], [(d, F32)], sums=[d])


def _adamw(w, g, m, v, name):
    cols = w.shape[1]
    c1 = 1.0 - ADAM_B1 ** ADAM_STEP
    c2 = 1.0 - ADAM_B2 ** ADAM_STEP

    def body(wv, gv, mv, vv):
        m2 = ADAM_B1 * mv + (1.0 - ADAM_B1) * gv
        v2 = ADAM_B2 * vv + (1.0 - ADAM_B2) * (gv * gv)
        m_hat = m2 / c1
        v_hat = v2 / c2
        delta = -ADAM_LR * (m_hat / (jnp.sqrt(v_hat) + ADAM_EPS) + ADAM_WD * wv)
        return (delta, m2, v2)

    return _rowwise(name, body, [w, g, m, v], [], [(cols, F32)] * 3)


def _split_bf16(v):
    hi = v.astype(BF16)
    lo = (v - hi.astype(F32)).astype(BF16)
    return hi, lo


def _tri(n, strict):
    r = lax.broadcasted_iota(jnp.int32, (n, n), 0)
    c = lax.broadcasted_iota(jnp.int32, (n, n), 1)
    return jnp.where(r > c if strict else r >= c, 1.0, 0.0).astype(BF16)


def _sb_tile(q, kt, t0, s0, scale):
    z = _dot(q, kt) * scale
    rows = t0 + lax.broadcasted_iota(jnp.int32, z.shape, 0)
    cols = s0 + lax.broadcasted_iota(jnp.int32, z.shape, 1)
    mask = cols < rows
    u = jnp.exp(-jnp.abs(z))
    sp = jnp.maximum(z, 0.0) + jnp.log(1.0 + u)
    return z, mask, sp, u


def _sb_fwd(q, kt, v, name):
    nh, t, dh = q.shape
    nkb, bk = kt.shape[1], kt.shape[3]
    bq = min(SB_BQ, t)
    scale = dh ** -0.5
    per = bq // bk

    def kern(q_ref, kt_ref, v_ref, o_ref):
        qi = pl.program_id(1)
        t0 = qi * bq
        qv = q_ref[...]
        mx = _tri(bk, True)

        def step(i, carry):
            c, acc = carry
            kb = (qi + 1) * per - 1 - i
            s0 = kb * bk
            z, mask, sp, _ = _sb_tile(qv, kt_ref[kb], t0, s0, scale)
            lk = jnp.where(mask, -sp, 0.0)
            hi, lo = _split_bf16(lk)
            suf = _dot(hi, mx) + _dot(lo, mx)
            w = jnp.where(mask, jnp.exp((z - sp) + suf + c), 0.0)
            acc = acc + _dot(w.astype(BF16), v_ref[kb])
            c = c + jnp.sum(lk, axis=1, keepdims=True)
            return c, acc

        c0 = jnp.zeros((bq, 1), F32)
        acc0 = jnp.zeros((bq, dh), F32)
        _, acc = lax.fori_loop(0, (qi + 1) * per, step, (c0, acc0))
        o_ref[...] = acc

    return pl.pallas_call(
        kern, name=name,
        out_shape=jax.ShapeDtypeStruct((nh, t, dh), F32),
        grid=(nh, t // bq),
        in_specs=[pl.BlockSpec((None, bq, dh), lambda h, i: (h, i, 0)),
                  pl.BlockSpec((None, nkb, dh, bk), lambda h, i: (h, 0, 0, 0)),
                  pl.BlockSpec((None, nkb, bk, dh), lambda h, i: (h, 0, 0, 0))],
        out_specs=pl.BlockSpec((None, bq, dh), lambda h, i: (h, i, 0)),
        compiler_params=_params(("parallel", "arbitrary")),
    )(q, kt, v)


def _sb_bwd(q, qt, kt, k, vt, out, dout, doutt, name):
    nh, t, dh = q.shape
    nkb, bk = kt.shape[1], kt.shape[3]
    bq = min(SB_BQ, t)
    scale = dh ** -0.5
    per = bq // bk

    def kern(q_ref, qt_ref, kt_ref, k_ref, vt_ref, o_ref, do_ref, dot_ref, dq_ref, dkt_ref, dvt_ref):
        qi = pl.program_id(1)
        t0 = qi * bq

        @pl.when(qi == 0)
        def _():
            dkt_ref[...] = jnp.zeros_like(dkt_ref)
            dvt_ref[...] = jnp.zeros_like(dvt_ref)

        qv = q_ref[...]
        qtv = qt_ref[...]
        dov = do_ref[...]
        dob = dov.astype(BF16)
        dotv = dot_ref[...]
        dsum = jnp.sum(dob.astype(F32) * o_ref[...], axis=1, keepdims=True)
        mx = _tri(bk, True)
        mi = _tri(bk, False)

        def step(i, carry):
            c, ce, dq = carry
            kb = (qi + 1) * per - 1 - i
            s0 = kb * bk
            z, mask, sp, u = _sb_tile(qv, kt_ref[kb], t0, s0, scale)
            lk = jnp.where(mask, -sp, 0.0)
            hi, lo = _split_bf16(lk)
            suf = _dot(hi, mx) + _dot(lo, mx)
            wb = jnp.where(mask, jnp.exp((z - sp) + suf + c), 0.0).astype(BF16)
            dw = _dot(dob, vt_ref[kb])
            e = dw * wb.astype(F32)
            ehi, elo = _split_bf16(e)
            sufe = _dot(ehi, mi) + _dot(elo, mi)
            before = dsum - (sufe + ce)
            rcp = 1.0 / (1.0 + u)
            sig = jnp.where(z >= 0.0, rcp, u * rcp)
            dz = jnp.where(mask, (e * (1.0 - sig) - sig * before) * scale, 0.0).astype(BF16)
            dq = dq + _dot(dz, k_ref[kb])
            dkt_ref[kb] += _dot(qtv, dz)
            dvt_ref[kb] += _dot(dotv, wb)
            c = c + jnp.sum(lk, axis=1, keepdims=True)
            ce = ce + jnp.sum(e, axis=1, keepdims=True)
            return c, ce, dq

        zero = jnp.zeros((bq, 1), F32)
        _, _, dq = lax.fori_loop(0, (qi + 1) * per, step, (zero, zero, jnp.zeros((bq, dh), F32)))
        dq_ref[...] = dq

    row = pl.BlockSpec((None, bq, dh), lambda h, i: (h, i, 0))
    col = pl.BlockSpec((None, dh, bq), lambda h, i: (h, 0, i))
    whole_t = pl.BlockSpec((None, nkb, dh, bk), lambda h, i: (h, 0, 0, 0))
    whole = pl.BlockSpec((None, nkb, bk, dh), lambda h, i: (h, 0, 0, 0))
    return pl.pallas_call(
        kern, name=name,
        out_shape=[jax.ShapeDtypeStruct((nh, t, dh), F32),
                   jax.ShapeDtypeStruct((nh, nkb, dh, bk), F32),
                   jax.ShapeDtypeStruct((nh, nkb, dh, bk), F32)],
        grid=(nh, t // bq),
        in_specs=[row, col, whole_t, whole, whole_t, row, row, col],
        out_specs=[row, whole_t, whole_t],
        compiler_params=_params(("parallel", "arbitrary")),
    )(q, qt, kt, k, vt, out, dout, doutt)


def _conv_fwd(p_conv, conv_w, conv_b, ln_g, ln_b, name):
    t, c2 = p_conv.shape
    c = c2 // 2
    tm = min(256, t)
    hb = tm // CONV_HALO

    def kern(a_ref, g_ref, ah_ref, gh_ref, w_ref, b_ref, lg_ref, lb_ref, o_ref, u_ref, y_ref, ubuf):
        i = pl.program_id(0)
        u = a_ref[...] * _sigmoid(g_ref[...])
        uh = ah_ref[...] * _sigmoid(gh_ref[...])
        ubuf[0:CONV_HALO, :] = jnp.where(i > 0, uh, 0.0)
        ubuf[CONV_HALO:CONV_HALO + tm, :] = u
        y = jnp.zeros((tm, c), F32) + b_ref[...]
        for j in range(CONV_WIDTH):
            off = CONV_HALO - (CONV_WIDTH - 1) + j
            y = y + ubuf[off:off + tm, :] * w_ref[j:j + 1, :]
        mu = jnp.mean(y, axis=-1, keepdims=True)
        yc = y - mu
        rstd = lax.rsqrt(jnp.mean(yc * yc, axis=-1, keepdims=True) + NORM_EPS)
        yn = (yc * rstd) * lg_ref[...] + lb_ref[...]
        o_ref[...] = (yn * _sigmoid(yn)).astype(o_ref.dtype)
        u_ref[...] = u
        y_ref[...] = y

    def halo(cb):
        return pl.BlockSpec((CONV_HALO, c), lambda i: (jnp.maximum(i * hb - 1, 0), cb))

    vec = pl.BlockSpec((1, c), lambda i: (0, 0))
    tile = pl.BlockSpec((tm, c), lambda i: (i, 0))
    return pl.pallas_call(
        kern, name=name,
        out_shape=[jax.ShapeDtypeStruct((t, c), BF16), jax.ShapeDtypeStruct((t, c), F32),
                   jax.ShapeDtypeStruct((t, c), F32)],
        grid=(t // tm,),
        in_specs=[tile, pl.BlockSpec((tm, c), lambda i: (i, 1)), halo(0), halo(1),
                  pl.BlockSpec(conv_w.shape, lambda i: (0, 0)), vec, vec, vec],
        out_specs=[tile, tile, tile],
        scratch_shapes=[pltpu.VMEM((tm + CONV_HALO, c), F32)],
        compiler_params=_params(("arbitrary",)),
    )(p_conv, p_conv, p_conv, p_conv, conv_w, conv_b, ln_g, ln_b)


def _conv_bwd2(dy, u, p_conv, conv_w, name):
    t, c = dy.shape
    tm = min(256, t)
    hb = tm // CONV_HALO
    nt = t // tm
    last_halo = t // CONV_HALO - 1

    def kern(dy_ref, dyh_ref, u_ref, uh_ref, a_ref, g_ref, w_ref, dp_ref, dw_ref, dybuf, ubuf):
        i = pl.program_id(0)

        @pl.when(i == 0)
        def _():
            dw_ref[...] = jnp.zeros_like(dw_ref)

        dyv = dy_ref[...]
        dybuf[0:tm, :] = dyv
        dybuf[tm:tm + CONV_HALO, :] = jnp.where(i < nt - 1, dyh_ref[...], 0.0)
        ubuf[0:CONV_HALO, :] = jnp.where(i > 0, uh_ref[...], 0.0)
        ubuf[CONV_HALO:CONV_HALO + tm, :] = u_ref[...]
        du = jnp.zeros((tm, c), F32)
        for j in range(CONV_WIDTH):
            off = CONV_WIDTH - 1 - j
            du = du + dybuf[off:off + tm, :] * w_ref[j:j + 1, :]
            uoff = CONV_HALO - (CONV_WIDTH - 1) + j
            dw_ref[j:j + 1, :] += jnp.sum(dyv * ubuf[uoff:uoff + tm, :], axis=0, keepdims=True)
        a = a_ref[...]
        s = _sigmoid(g_ref[...])
        dp_ref[:, 0:c] = (du * s).astype(dp_ref.dtype)
        dp_ref[:, c:2 * c] = (du * a * (s * (1.0 - s))).astype(dp_ref.dtype)

    tile = pl.BlockSpec((tm, c), lambda i: (i, 0))
    return pl.pallas_call(
        kern, name=name,
        out_shape=[jax.ShapeDtypeStruct((t, 2 * c), BF16), jax.ShapeDtypeStruct((CONV_HALO, c), F32)],
        grid=(nt,),
        in_specs=[tile,
                  pl.BlockSpec((CONV_HALO, c), lambda i: (jnp.minimum((i + 1) * hb, last_halo), 0)),
                  tile,
                  pl.BlockSpec((CONV_HALO, c), lambda i: (jnp.maximum(i * hb - 1, 0), 0)),
                  tile, pl.BlockSpec((tm, c), lambda i: (i, 1)),
                  pl.BlockSpec(conv_w.shape, lambda i: (0, 0))],
        out_specs=[pl.BlockSpec((tm, 2 * c), lambda i: (i, 0)),
                   pl.BlockSpec((CONV_HALO, c), lambda i: (0, 0))],
        scratch_shapes=[pltpu.VMEM((tm + CONV_HALO, c), F32), pltpu.VMEM((tm + CONV_HALO, c), F32)],
        compiler_params=_params(("arbitrary",)),
    )(dy, dy, u, u, p_conv, p_conv, conv_w)


GLA_CHUNKS_PER_STEP = 16


def _gla_fwd(q_ck, k_kc, la_kc, v, name):
    nh, nc, ch, dk = q_ck.shape
    dv = v.shape[3]
    qscale = dk ** -0.5
    cb = min(GLA_CHUNKS_PER_STEP, nc)

    def kern(q_ref, k_ref, la_ref, v_ref, o_ref, st_ref, state_ref):
        mx = _tri(ch, True)

        @pl.when(pl.program_id(1) == 0)
        def _():
            state_ref[...] = jnp.zeros_like(state_ref)

        def step(n, state):
            la = la_ref[n]
            hi, lo = _split_bf16(la)
            de = _dot(hi, mx) + _dot(lo, mx)
            lam = jnp.exp(jnp.sum(la, axis=1, keepdims=True))
            kd = (k_ref[n] * jnp.exp(de)).astype(BF16)
            state = lam * state + _dot(kd, v_ref[n])
            st_ref[n] = state
            qs = (q_ref[n].astype(F32) * qscale).astype(BF16)
            o_ref[n] = _dot(qs, state.astype(BF16))
            return state

        state_ref[...] = lax.fori_loop(0, cb, step, state_ref[...])

    def spec(a, b):
        return pl.BlockSpec((None, cb, a, b), lambda h, j: (h, j, 0, 0))

    return pl.pallas_call(
        kern, name=name,
        out_shape=[jax.ShapeDtypeStruct((nh, nc, ch, dv), F32), jax.ShapeDtypeStruct((nh, nc, dk, dv), F32)],
        grid=(nh, nc // cb),
        in_specs=[spec(ch, dk), spec(dk, ch), spec(dk, ch), spec(ch, dv)],
        out_specs=[spec(ch, dv), spec(dk, dv)],
        scratch_shapes=[pltpu.VMEM((dk, dv), F32)],
        compiler_params=_params(("parallel", "arbitrary")),
    )(q_ck, k_kc, la_kc, v)


def _gla_bwd(q_ck, q_kc, k_ck, k_kc, la_ck, la_kc, v, states, states_prev, do, name):
    nh, nc, ch, dk = q_ck.shape
    dv = v.shape[3]
    qscale = dk ** -0.5
    cb = min(GLA_CHUNKS_PER_STEP, nc)
    nb = nc // cb

    def kern(q_ref, qt_ref, kck_ref, kkc_ref, lack_ref, lakc_ref, v_ref, st_ref, stp_ref, do_ref,
             dq_ref, dk_ref, dv_ref, dla_ref, g_ref):
        mx = _tri(ch, True)
        mxt = jnp.where(lax.broadcasted_iota(jnp.int32, (ch, ch), 0)
                        < lax.broadcasted_iota(jnp.int32, (ch, ch), 1), 1.0, 0.0).astype(BF16)

        @pl.when(pl.program_id(1) == 0)
        def _():
            g_ref[...] = jnp.zeros_like(g_ref)

        def step(i, g):
            n = cb - 1 - i
            la_kc = lakc_ref[n]
            hi, lo = _split_bf16(la_kc)
            de_kc = _dot(hi, mx) + _dot(lo, mx)
            lam = jnp.exp(jnp.sum(la_kc, axis=1, keepdims=True))
            hi2, lo2 = _split_bf16(lack_ref[n])
            de_ck = _dot(mxt, hi2) + _dot(mxt, lo2)
            edk = jnp.exp(de_kc)
            kd_kc = kkc_ref[n] * edk
            kd_ck = (kck_ref[n] * jnp.exp(de_ck)).astype(BF16)
            dob = do_ref[n].astype(BF16)
            dq_ref[n] = _dot(dob, st_ref[n].astype(BF16), _NT) * qscale
            qts = (qt_ref[n].astype(F32) * qscale).astype(BF16)
            ds = _dot(qts, dob) + g
            dsb = ds.astype(BF16)
            dlam = jnp.sum(ds * stp_ref[n], axis=1, keepdims=True)
            dkd = _dot(dsb, v_ref[n], _NT)
            dv_ref[n] = _dot(kd_ck, dsb)
            dk_ref[n] = dkd * edk
            dde = dkd * kd_kc
            h3, l3 = _split_bf16(dde)
            dla_ref[n] = _dot(h3, mxt) + _dot(l3, mxt) + dlam * lam
            return lam * ds

        g_ref[...] = lax.fori_loop(0, cb, step, g_ref[...])

    def spec(a, b):
        return pl.BlockSpec((None, cb, a, b), lambda h, j: (h, nb - 1 - j, 0, 0))

    return pl.pallas_call(
        kern, name=name,
        out_shape=[jax.ShapeDtypeStruct((nh, nc, ch, dk), F32), jax.ShapeDtypeStruct((nh, nc, dk, ch), F32),
                   jax.ShapeDtypeStruct((nh, nc, ch, dv), F32), jax.ShapeDtypeStruct((nh, nc, dk, ch), F32)],
        grid=(nh, nb),
        in_specs=[spec(ch, dk), spec(dk, ch), spec(ch, dk), spec(dk, ch), spec(ch, dk), spec(dk, ch),
                  spec(ch, dv), spec(dk, dv), spec(dk, dv), spec(ch, dv)],
        out_specs=[spec(ch, dk), spec(dk, ch), spec(ch, dv), spec(dk, ch)],
        scratch_shapes=[pltpu.VMEM((dk, dv), F32)],
        compiler_params=_params(("parallel", "arbitrary")),
    )(q_ck, q_kc, k_ck, k_kc, la_ck, la_kc, v, states, states_prev, do)


def _heads_rows(a, nh, blk):
    t = a.shape[0]
    d = a.shape[1] // nh
    return a.reshape(t // blk, blk, nh, d).transpose(2, 0, 1, 3)


def _heads_cols(a, nh, blk):
    t = a.shape[0]
    d = a.shape[1] // nh
    return a.reshape(t // blk, blk, nh, d).transpose(2, 0, 3, 1)


def _unheads_rows(a):
    nh, nb, blk, d = a.shape
    return a.transpose(1, 2, 0, 3).reshape(nb * blk, nh * d)


def _unheads_cols(a):
    nh, nb, d, blk = a.shape
    return a.transpose(1, 3, 0, 2).reshape(nb * blk, nh * d)


def _ffn_fwd(x, w, pre, post, tag):
    h = _rms_fwd(x, pre, f"{tag}_rms")
    gate = _mm(h, w["gate"], out_dtype=BF16, name=f"{tag}_gate")
    up = _mm(h, w["up"], out_dtype=BF16, name=f"{tag}_up")
    act = _swiglu_fwd(gate, up, f"{tag}_act")
    o = _mm(act, w["down"], name=f"{tag}_down")
    x_out = _post_res(o, x, post, 0.5, f"{tag}_res")
    return x_out, (x, h, gate, up, act, o)


def _ffn_bwd(dx_out, saved, w, pre, post, tag):
    x, h, gate, up, act, o = saved
    do, dpost = _post_bwd(dx_out, o, post, 0.5, f"{tag}_bres")
    da = _mm(do, w["down"], nt=True, name=f"{tag}_bda")
    d_down = _mm(act.T, do, name=f"{tag}_bwdown")
    dgate, dup = _swiglu_bwd(da, gate, up, f"{tag}_bact")
    dh_g = _mm(dgate, w["gate"], nt=True, name=f"{tag}_bdhg")
    dh_u = _mm(dup, w["up"], nt=True, name=f"{tag}_bdhu")
    ht = h.T
    d_gate = _mm(ht, dgate, name=f"{tag}_bwgate")
    d_up = _mm(ht, dup, name=f"{tag}_bwup")
    dx, dpre = _rms_bwd([dh_g, dh_u], x, pre, dx_out, f"{tag}_brms")
    return dx, dict(gate=d_gate, up=d_up, down=d_down, pre=dpre, post=dpost)


def _mixer_fwd(x, w, tag):
    t, d = x.shape
    bw = BRANCH_WIDTH
    h = _rms_fwd(x, w["pre"], f"{tag}_rms")
    p_sb = _mm(h, w["in_sb"], out_dtype=BF16, name=f"{tag}_insb")
    p_conv = _mm(h, w["in_conv"], name=f"{tag}_inconv")
    p_gla = _mm(h, w["in_gla"], name=f"{tag}_ingla")
    p_gate = _mm(h, w["in_gate"], name=f"{tag}_ingate")

    bk = min(SB_BK, t)
    q = p_sb[:, 0:bw].reshape(t, SB_HEADS, SB_HEAD_DIM).transpose(1, 0, 2)
    kt = _heads_cols(p_sb[:, bw:2 * bw], SB_HEADS, bk)
    v = _heads_rows(p_sb[:, 2 * bw:3 * bw], SB_HEADS, bk)
    sb_o = _sb_fwd(q, kt, v, f"{tag}_sb")
    sb_out = sb_o.transpose(1, 0, 2).reshape(t, bw)

    conv_out, conv_u, conv_y = _conv_fwd(p_conv, w["conv_w"], w["conv_b"], w["ln_g"], w["ln_b"], f"{tag}_conv")

    kd, vd = GLA_KEY_DIM, GLA_VALUE_DIM
    lr = p_gla[:, 2 * kd + 2 * vd:]
    pre_a = _mm(lr, w["alpha"], name=f"{tag}_alpha")
    la = _la_fwd(pre_a, w["b_alpha"], f"{tag}_la")
    gq = _heads_rows(p_gla[:, 0:kd].astype(BF16), GLA_HEADS, CHUNK)
    gk_kc = _heads_cols(p_gla[:, kd:2 * kd], GLA_HEADS, CHUNK)
    gv = _heads_rows(p_gla[:, 2 * kd:2 * kd + vd].astype(BF16), GLA_HEADS, CHUNK)
    la_kc = _heads_cols(la, GLA_HEADS, CHUNK)
    gla_o4, states = _gla_fwd(gq, gk_kc, la_kc, gv, f"{tag}_gla")
    gla_o = _unheads_rows(gla_o4)
    r_block = (2 * kd + vd) // vd
    gla_out = _gla_post_fwd(gla_o, p_gla, r_block, w["gnorm"], f"{tag}_glapost")

    branches = (sb_out.astype(BF16), conv_out, gla_out)
    bds = [_mm(branches[j], w["branch"][j], name=f"{tag}_br{j}") for j in range(N_BRANCHES)]
    merged = _merge_fwd(bds, p_gate, f"{tag}_merge")
    mo = _mm(merged, w["out"], name=f"{tag}_out")
    x_out = _post_res(mo, x, w["post"], 1.0, f"{tag}_res")
    saved = dict(x=x, h=h, p_sb=p_sb, p_conv=p_conv, p_gla=p_gla, p_gate=p_gate, sb_o=sb_o, conv_u=conv_u,
                 conv_y=conv_y, pre_a=pre_a, la=la, states=states, gla_o=gla_o, branches=branches, bds=bds,
                 merged=merged, mo=mo)
    return x_out, saved


def _mixer_bwd(dx_out, s, w, tag):
    x = s["x"]
    t, d = x.shape
    bw = BRANCH_WIDTH
    kd, vd = GLA_KEY_DIM, GLA_VALUE_DIM
    grads = {}
    dmo, grads["post"] = _post_bwd(dx_out, s["mo"], w["post"], 1.0, f"{tag}_bres")
    dmerged = _mm(dmo, w["out"], nt=True, name=f"{tag}_bdmerged")
    grads["out"] = _mm(s["merged"].T, dmo, name=f"{tag}_bwout")
    dbd0, dbd1, dbd2, dp_gate = _merge_bwd(dmerged, s["bds"], s["p_gate"], f"{tag}_bmerge")
    dbds = (dbd0, dbd1, dbd2)
    dbranch = [_mm(dbds[j], w["branch"][j], nt=True, name=f"{tag}_bdbr{j}") for j in range(N_BRANCHES)]
    grads["branch"] = jnp.stack([_mm(s["branches"][j].T, dbds[j], name=f"{tag}_bwbr{j}")
                                 for j in range(N_BRANCHES)])

    p_sb = s["p_sb"]
    bk = min(SB_BK, t)
    qf, kf, vf = p_sb[:, 0:bw], p_sb[:, bw:2 * bw], p_sb[:, 2 * bw:3 * bw]
    q = qf.reshape(t, SB_HEADS, SB_HEAD_DIM).transpose(1, 0, 2)
    qt = qf.reshape(t, SB_HEADS, SB_HEAD_DIM).transpose(1, 2, 0)
    kt = _heads_cols(kf, SB_HEADS, bk)
    k = _heads_rows(kf, SB_HEADS, bk)
    vt = _heads_cols(vf, SB_HEADS, bk)
    dsb = dbranch[0].reshape(t, SB_HEADS, SB_HEAD_DIM)
    dout = dsb.transpose(1, 0, 2)
    doutt = dsb.transpose(1, 2, 0).astype(BF16)
    dq, dkt, dvt = _sb_bwd(q, qt, kt, k, vt, s["sb_o"], dout, doutt, f"{tag}_bsb")
    dp_sb = jnp.concatenate([dq.transpose(1, 0, 2).reshape(t, bw), _unheads_cols(dkt), _unheads_cols(dvt)],
                            axis=1).astype(BF16)

    dy, d_lng, d_lnb, d_cb = _conv_bwd1(dbranch[1], s["conv_y"], w["ln_g"], w["ln_b"], f"{tag}_bconv1")
    dp_conv, d_cw = _conv_bwd2(dy, s["conv_u"], s["p_conv"], w["conv_w"], f"{tag}_bconv2")
    grads.update(conv_w=d_cw[:CONV_WIDTH], conv_b=d_cb, ln_g=d_lng, ln_b=d_lnb)

    p_gla = s["p_gla"]
    r_block = (2 * kd + vd) // vd
    do, dr, grads["gnorm"] = _gla_post_bwd(dbranch[2], s["gla_o"], p_gla, r_block, w["gnorm"], f"{tag}_bglapost")
    gqf, gkf = p_gla[:, 0:kd], p_gla[:, kd:2 * kd]
    gq_b = gqf.astype(BF16)
    dq4, dk4, dv4, dla4 = _gla_bwd(
        _heads_rows(gq_b, GLA_HEADS, CHUNK), _heads_cols(gq_b, GLA_HEADS, CHUNK),
        _heads_rows(gkf, GLA_HEADS, CHUNK), _heads_cols(gkf, GLA_HEADS, CHUNK),
        _heads_rows(s["la"], GLA_HEADS, CHUNK), _heads_cols(s["la"], GLA_HEADS, CHUNK),
        _heads_rows(p_gla[:, 2 * kd:2 * kd + vd].astype(BF16), GLA_HEADS, CHUNK),
        s["states"], jnp.pad(s["states"][:, :-1], ((0, 0), (1, 0), (0, 0), (0, 0))),
        _heads_rows(do, GLA_HEADS, CHUNK), f"{tag}_bgla")
    dla = _unheads_cols(dla4)
    dpre_a, grads["b_alpha"] = _la_bwd(dla, s["pre_a"], w["b_alpha"], f"{tag}_bla")
    lr = p_gla[:, 2 * kd + 2 * vd:]
    dlr = _mm(dpre_a, w["alpha"], nt=True, out_dtype=BF16, name=f"{tag}_bdlr")
    grads["alpha"] = _mm(lr.astype(BF16).T, dpre_a, name=f"{tag}_bwalpha")
    dp_gla = jnp.concatenate([_unheads_rows(dq4).astype(BF16), _unheads_cols(dk4).astype(BF16),
                              _unheads_rows(dv4).astype(BF16), dr, dlr], axis=1)

    ht = s["h"].T
    dps = dict(in_sb=dp_sb, in_conv=dp_conv, in_gla=dp_gla, in_gate=dp_gate)
    dh_parts = []
    for key, dp in dps.items():
        dh_parts.append(_mm(dp, w[key], nt=True, name=f"{tag}_bdh_{key}"))
        grads[key] = _mm(ht, dp, name=f"{tag}_bw_{key}")
    dx, grads["pre"] = _rms_bwd(dh_parts, x, w["pre"], dx_out, f"{tag}_brms")
    return dx, grads


def _local_step(x, target, layers):
    saved = []
    for l, w in enumerate(layers):
        x, s1 = _ffn_fwd(x, w["ffn1"], w["pre0"], w["post0"], f"l{l}_f1")
        x, s2 = _mixer_fwd(x, w["mix"], f"l{l}_mx")
        x, s3 = _ffn_fwd(x, w["ffn2"], w["pre2"], w["post2"], f"l{l}_f2")
        saved.append((s1, s2, s3))
    dx, sq = _loss_fwd(x, target, "loss")
    grads = [None] * len(layers)
    for l in reversed(range(len(layers))):
        w = layers[l]
        s1, s2, s3 = saved[l]
        dx, g3 = _ffn_bwd(dx, s3, w["ffn2"], w["pre2"], w["post2"], f"l{l}_f2")
        dx, g2 = _mixer_bwd(dx, s2, w["mix"], f"l{l}_mx")
        dx, g1 = _ffn_bwd(dx, s1, w["ffn1"], w["pre0"], w["post0"], f"l{l}_f1")
        grads[l] = dict(ffn1=g1, mix=g2, ffn2=g3)
    return sq, dx, grads


_ANY = pl.BlockSpec(memory_space=pl.ANY)
_MESH = pl.DeviceIdType.MESH


def _all_gather(p, name):
    def body(p_ref, out_ref, send_sems, recv_sems, local_sem):
        x, y, c = lax.axis_index("x"), lax.axis_index("y"), lax.axis_index("c")
        me, sibling = (x, y, c), (x, y, 1 - c)
        chips = [(1 - x, y), (x, 1 - y), (1 - x, 1 - y)]

        def rows(px, py, pc):
            return out_ref.at[4 * pc + 2 * px + py]

        def copy(k, block, to, src=None):
            return pltpu.make_async_remote_copy(
                src_ref=rows(*block) if src is None else src, dst_ref=rows(*block),
                send_sem=send_sems.at[k], recv_sem=recv_sems.at[k], device_id=to, device_id_type=_MESH)

        mine = pltpu.make_async_copy(p_ref, rows(*me), local_sem)
        mine.start()
        first = [copy(0, me, sibling, src=p_ref)]
        first += [copy(1 + j, me, (*chip, c), src=p_ref) for j, chip in enumerate(chips)]
        for cp in first:
            cp.start()
        passed = [copy(4 + j, (*chip, c), sibling) for j, chip in enumerate(chips)]
        for j, chip in enumerate(chips):
            copy(1 + j, (*chip, c), me).wait_recv()
            passed[j].start()
        copy(0, sibling, me).wait_recv()
        for j, chip in enumerate(chips):
            copy(4 + j, (*chip, 1 - c), me).wait_recv()
        for cp in first + passed:
            cp.wait_send()
        mine.wait()

    return pl.pallas_call(
        body, name=name,
        out_shape=jax.ShapeDtypeStruct((N_DEV,) + p.shape, p.dtype),
        in_specs=[_ANY], out_specs=_ANY,
        scratch_shapes=[pltpu.SemaphoreType.DMA((7,)), pltpu.SemaphoreType.DMA((7,)), pltpu.SemaphoreType.DMA],
    )(p)


def _exchange_sibling(src, name):
    def body(src_ref, out_ref, send_sem, recv_sem):
        x, y, c = lax.axis_index("x"), lax.axis_index("y"), lax.axis_index("c")
        cp = pltpu.make_async_remote_copy(
            src_ref=src_ref.at[pl.ds(4 * (1 - c), 4)], dst_ref=out_ref,
            send_sem=send_sem, recv_sem=recv_sem, device_id=(x, y, 1 - c), device_id_type=_MESH)
        cp.start()
        cp.wait()

    return pl.pallas_call(
        body, name=name,
        out_shape=jax.ShapeDtypeStruct((4,) + src.shape[1:], src.dtype),
        in_specs=[_ANY], out_specs=_ANY,
        scratch_shapes=[pltpu.SemaphoreType.DMA, pltpu.SemaphoreType.DMA],
    )(src)


def _add_own(src, got, name):
    _, r, cols = src.shape
    tr = _div_tile(r, 512, 8)
    c = lax.axis_index("c").astype(jnp.int32).reshape(1)

    def kern(c_ref, a_ref, b_ref, o_ref):
        o_ref[...] = a_ref[...] + b_ref[...]

    grid_spec = pltpu.PrefetchScalarGridSpec(
        num_scalar_prefetch=1, grid=(4, r // tr),
        in_specs=[pl.BlockSpec((None, tr, cols), lambda j, i, c_ref: (4 * c_ref[0] + j, i, 0)),
                  pl.BlockSpec((None, tr, cols), lambda j, i, c_ref: (j, i, 0))],
        out_specs=pl.BlockSpec((None, tr, cols), lambda j, i, c_ref: (j, i, 0)))
    return pl.pallas_call(
        kern, name=name, grid_spec=grid_spec,
        out_shape=jax.ShapeDtypeStruct((4, r, cols), src.dtype),
        compiler_params=_params(("arbitrary", "arbitrary")),
    )(c, src, got)


def _exchange_chips(part, name):
    def body(part_ref, out_ref, send_sems, recv_sems, local_sem):
        x, y, c = lax.axis_index("x"), lax.axis_index("y"), lax.axis_index("c")
        my_chip = 2 * x + y
        chips = [(1 - x, y), (x, 1 - y), (1 - x, 1 - y)]
        mine = pltpu.make_async_copy(part_ref.at[my_chip], out_ref.at[my_chip], local_sem)
        mine.start()
        copies = []
        for k, (px, py) in enumerate(chips):
            copies.append(pltpu.make_async_remote_copy(
                src_ref=part_ref.at[2 * px + py], dst_ref=out_ref.at[my_chip],
                send_sem=send_sems.at[k], recv_sem=recv_sems.at[k], device_id=(px, py, c), device_id_type=_MESH))
        for cp in copies:
            cp.start()
        for k, (px, py) in enumerate(chips):
            pltpu.make_async_remote_copy(
                src_ref=part_ref.at[my_chip], dst_ref=out_ref.at[2 * px + py],
                send_sem=send_sems.at[k], recv_sem=recv_sems.at[k], device_id=(px, py, c),
                device_id_type=_MESH).wait_recv()
        for cp in copies:
            cp.wait_send()
        mine.wait()

    return pl.pallas_call(
        body, name=name,
        out_shape=jax.ShapeDtypeStruct(part.shape, part.dtype),
        in_specs=[_ANY], out_specs=_ANY,
        scratch_shapes=[pltpu.SemaphoreType.DMA((3,)), pltpu.SemaphoreType.DMA((3,)), pltpu.SemaphoreType.DMA],
    )(part)


def _sum_chips(parts, name):
    _, r, cols = parts.shape
    tr = _div_tile(r, 512, 8)

    def kern(p_ref, o_ref):
        o_ref[...] = ((p_ref[0] + p_ref[1]) + p_ref[2]) + p_ref[3]

    return pl.pallas_call(
        kern, name=name, grid=(r // tr,),
        in_specs=[pl.BlockSpec((4, tr, cols), lambda i: (0, i, 0))],
        out_specs=pl.BlockSpec((tr, cols), lambda i: (i, 0)),
        out_shape=jax.ShapeDtypeStruct((r, cols), parts.dtype),
        compiler_params=_params(("arbitrary",)),
    )(parts)


_SHARDED = {
    "norm_pre": (1, True), "norm_post": (1, True),
    "ffn1_w_gate": (1, False), "ffn1_w_up": (1, False), "ffn1_w_down": (0, False),
    "ffn2_w_gate": (1, False), "ffn2_w_up": (1, False), "ffn2_w_down": (0, False),
    "w_in": (1, False), "conv_w": (1, True), "gla_w_alpha": (1, False),
    "w_branch": (2, False), "w_out": (0, False),
}
_REPLICATED = ("conv_b", "conv_ln_g", "conv_ln_b", "gla_b_alpha", "gla_norm_g")
_WEIGHTS = ("norm_pre", "norm_post", "ffn1_w_gate", "ffn1_w_up", "ffn1_w_down", "ffn2_w_gate", "ffn2_w_up",
            "ffn2_w_down", "w_in", "conv_w", "conv_b", "conv_ln_g", "conv_ln_b", "gla_w_alpha", "gla_b_alpha",
            "gla_norm_g", "w_branch", "w_out")
_BLOCK_OF_SLOT = np.array([4 * x + 2 * y + c for c in range(2) for x in range(2) for y in range(2)])
_SLOT_OF_BLOCK = np.argsort(_BLOCK_OF_SLOT)


def _to_rows(flat, unit):
    n = flat.shape[-1]
    rows = -(-n // PACK_COLS)
    rows = -(-rows // unit) * unit
    pad = [(0, 0)] * (flat.ndim - 1) + [(0, rows * PACK_COLS - n)]
    return jnp.pad(flat, pad).reshape(flat.shape[:-1] + (rows, PACK_COLS))


def _pack_weights(shards, l):
    pieces = []
    for name, (_, exact) in _SHARDED.items():
        a = shards[name][l]
        if exact:
            pieces.append(lax.bitcast_convert_type(a, BF16).reshape(-1))
        else:
            pieces.append(a.astype(BF16).reshape(-1))
    return _to_rows(jnp.concatenate(pieces), 16)


def _unpack_weights(gathered, shards, l):
    flat = gathered.reshape(N_DEV, -1)[_SLOT_OF_BLOCK]
    full, off = {}, 0
    for name, (axis, exact) in _SHARDED.items():
        shp = shards[name].shape[1:]
        n = math.prod(shp) * (2 if exact else 1)
        seg = flat[:, off:off + n]
        off += n
        if exact:
            seg = lax.bitcast_convert_type(seg.reshape((N_DEV,) + shp + (2,)), F32)
        else:
            seg = seg.reshape((N_DEV,) + shp)
        seg = jnp.moveaxis(seg, 0, axis)
        full[name] = seg.reshape(shp[:axis] + (N_DEV * shp[axis],) + shp[axis + 1:])
    return full


def _pack_grads(full_grads, repl_grads):
    pieces = []
    for name, (axis, _) in _SHARDED.items():
        g = full_grads[name]
        shp = g.shape
        g = g.reshape(shp[:axis] + (N_DEV, shp[axis] // N_DEV) + shp[axis + 1:])
        g = jnp.moveaxis(g, axis, 0).reshape(N_DEV, -1)[_BLOCK_OF_SLOT]
        pieces.append(g)
    for name in _REPLICATED:
        pieces.append(jnp.broadcast_to(repl_grads[name].reshape(1, -1), (N_DEV, repl_grads[name].size)))
    return _to_rows(jnp.concatenate(pieces, axis=1), 8)


def _pack_local(arrs, l):
    pieces = [arrs[name][l].reshape(-1) for name in _SHARDED] + [arrs[name][l].reshape(-1) for name in _REPLICATED]
    return _to_rows(jnp.concatenate(pieces), 8)


def _unpack_local(flat, arrs):
    nl = flat.shape[0]
    flat = flat.reshape(nl, -1)
    out, off = {}, 0
    for name in tuple(_SHARDED) + _REPLICATED:
        shp = arrs[name].shape[1:]
        n = math.prod(shp)
        out[name] = flat[:, off:off + n].reshape((nl,) + shp)
        off += n
    return out


def _layer_weights(full, repl, l, d_model):
    bw = BRANCH_WIDTH
    w_in = full["w_in"]
    o_conv, o_gq, o_lr = 3 * bw, 5 * bw, 5 * bw + 2 * GLA_KEY_DIM + 2 * GLA_VALUE_DIM
    o_gate = o_lr + GLA_GATE_RANK
    lr_pad = LANE - GLA_GATE_RANK
    in_gla = jnp.concatenate([w_in[:, o_gq:o_gate], jnp.zeros((d_model, lr_pad), w_in.dtype)], axis=1)
    alpha = jnp.concatenate([full["gla_w_alpha"], jnp.zeros((lr_pad, GLA_KEY_DIM), BF16)], axis=0)
    row = lambda a: a.reshape(1, -1)
    mix = dict(pre=row(full["norm_pre"][1]), post=row(full["norm_post"][1]),
               in_sb=w_in[:, 0:o_conv], in_conv=w_in[:, o_conv:o_gq], in_gla=in_gla, in_gate=w_in[:, o_gate:],
               conv_w=full["conv_w"], conv_b=row(repl["conv_b"][l]), ln_g=row(repl["conv_ln_g"][l]),
               ln_b=row(repl["conv_ln_b"][l]), alpha=alpha, b_alpha=row(repl["gla_b_alpha"][l]),
               gnorm=row(repl["gla_norm_g"][l]), branch=full["w_branch"], out=full["w_out"])
    return dict(
        pre0=row(full["norm_pre"][0]), post0=row(full["norm_post"][0]),
        pre2=row(full["norm_pre"][2]), post2=row(full["norm_post"][2]),
        ffn1=dict(gate=full["ffn1_w_gate"], up=full["ffn1_w_up"], down=full["ffn1_w_down"]),
        ffn2=dict(gate=full["ffn2_w_gate"], up=full["ffn2_w_up"], down=full["ffn2_w_down"]),
        mix=mix)


def _full_grads(g):
    mix = g["mix"]
    kd, vd = GLA_KEY_DIM, GLA_VALUE_DIM
    d_in = jnp.concatenate([mix["in_sb"], mix["in_conv"], mix["in_gla"][:, :2 * kd + 2 * vd + GLA_GATE_RANK],
                            mix["in_gate"]], axis=1)
    full = {
        "norm_pre": jnp.concatenate([g["ffn1"]["pre"], mix["pre"], g["ffn2"]["pre"]], axis=0),
        "norm_post": jnp.concatenate([g["ffn1"]["post"], mix["post"], g["ffn2"]["post"]], axis=0),
        "ffn1_w_gate": g["ffn1"]["gate"], "ffn1_w_up": g["ffn1"]["up"], "ffn1_w_down": g["ffn1"]["down"],
        "ffn2_w_gate": g["ffn2"]["gate"], "ffn2_w_up": g["ffn2"]["up"], "ffn2_w_down": g["ffn2"]["down"],
        "w_in": d_in, "conv_w": mix["conv_w"], "gla_w_alpha": mix["alpha"][:GLA_GATE_RANK],
        "w_branch": mix["branch"], "w_out": mix["out"],
    }
    repl = {"conv_b": mix["conv_b"], "conv_ln_g": mix["ln_g"], "conv_ln_b": mix["ln_b"],
            "gla_b_alpha": mix["b_alpha"], "gla_norm_g": mix["gnorm"]}
    return full, repl


def kernel(x, norm_pre, norm_post, ffn1_w_gate, ffn1_w_up, ffn1_w_down, ffn2_w_gate, ffn2_w_up, ffn2_w_down, w_in, conv_w, conv_b, conv_ln_g, conv_ln_b, gla_w_alpha, gla_b_alpha, gla_norm_g, w_branch, w_out, loss_target, m_norm_pre, m_norm_post, m_ffn1_w_gate, m_ffn1_w_up, m_ffn1_w_down, m_ffn2_w_gate, m_ffn2_w_up, m_ffn2_w_down, m_w_in, m_conv_w, m_conv_b, m_conv_ln_g, m_conv_ln_b, m_gla_w_alpha, m_gla_b_alpha, m_gla_norm_g, m_w_branch, m_w_out, v_norm_pre, v_norm_post, v_ffn1_w_gate, v_ffn1_w_up, v_ffn1_w_down, v_ffn2_w_gate, v_ffn2_w_up, v_ffn2_w_down, v_w_in, v_conv_w, v_conv_b, v_conv_ln_g, v_conv_ln_b, v_gla_w_alpha, v_gla_b_alpha, v_gla_norm_g, v_w_branch, v_w_out):
    weights = dict(norm_pre=norm_pre, norm_post=norm_post, ffn1_w_gate=ffn1_w_gate, ffn1_w_up=ffn1_w_up,
                   ffn1_w_down=ffn1_w_down, ffn2_w_gate=ffn2_w_gate, ffn2_w_up=ffn2_w_up, ffn2_w_down=ffn2_w_down,
                   w_in=w_in, conv_w=conv_w, conv_b=conv_b, conv_ln_g=conv_ln_g, conv_ln_b=conv_ln_b,
                   gla_w_alpha=gla_w_alpha, gla_b_alpha=gla_b_alpha, gla_norm_g=gla_norm_g, w_branch=w_branch,
                   w_out=w_out)
    moments_m = dict(norm_pre=m_norm_pre, norm_post=m_norm_post, ffn1_w_gate=m_ffn1_w_gate, ffn1_w_up=m_ffn1_w_up,
                     ffn1_w_down=m_ffn1_w_down, ffn2_w_gate=m_ffn2_w_gate, ffn2_w_up=m_ffn2_w_up,
                     ffn2_w_down=m_ffn2_w_down, w_in=m_w_in, conv_w=m_conv_w, conv_b=m_conv_b,
                     conv_ln_g=m_conv_ln_g, conv_ln_b=m_conv_ln_b, gla_w_alpha=m_gla_w_alpha,
                     gla_b_alpha=m_gla_b_alpha, gla_norm_g=m_gla_norm_g, w_branch=m_w_branch, w_out=m_w_out)
    moments_v = dict(norm_pre=v_norm_pre, norm_post=v_norm_post, ffn1_w_gate=v_ffn1_w_gate, ffn1_w_up=v_ffn1_w_up,
                     ffn1_w_down=v_ffn1_w_down, ffn2_w_gate=v_ffn2_w_gate, ffn2_w_up=v_ffn2_w_up,
                     ffn2_w_down=v_ffn2_w_down, w_in=v_w_in, conv_w=v_conv_w, conv_b=v_conv_b,
                     conv_ln_g=v_conv_ln_g, conv_ln_b=v_conv_ln_b, gla_w_alpha=v_gla_w_alpha,
                     gla_b_alpha=v_gla_b_alpha, gla_norm_g=v_gla_norm_g, w_branch=v_w_branch, w_out=v_w_out)
    n_layers = norm_pre.shape[0]
    t, d_model = x.shape[1], x.shape[2]

    layers = []
    for l in range(n_layers):
        gathered = _all_gather(_pack_weights(weights, l), f"gather_l{l}")
        full = _unpack_weights(gathered, weights, l)
        layers.append(_layer_weights(full, weights, l, d_model))

    sq, dx, grads = _local_step(x[0], loss_target[0], layers)
    loss = lax.psum(0.5 * jnp.sum(sq) / d_model, MESH_AXES)

    summed = []
    for l in range(n_layers):
        packed = _pack_grads(*_full_grads(grads[l]))
        got = _exchange_sibling(packed, f"rs_sibling_l{l}")
        part = _add_own(packed, got, f"rs_add_l{l}")
        parts = _exchange_chips(part, f"rs_chips_l{l}")
        summed.append(_sum_chips(parts, f"rs_sum_l{l}"))
    g_flat = jnp.stack(summed)
    nl, rows, cols = g_flat.shape

    def flat_local(arrs):
        return jnp.stack([_pack_local(arrs, l) for l in range(n_layers)]).reshape(nl * rows, cols)

    delta, new_m, new_v = _adamw(flat_local(weights), g_flat.reshape(nl * rows, cols), flat_local(moments_m),
                                 flat_local(moments_v), "adamw")
    outs = [_unpack_local(a.reshape(nl, rows, cols), weights) for a in (g_flat, delta, new_m, new_v)]
    result = [loss, dx[None]]
    for o in outs:
        result += [o[name] for name in _WEIGHTS]
    return tuple(result)
```

```python
import functools
import math

import jax
import jax.numpy as jnp
from jax import lax
from jax.experimental import pallas as pl
from jax.experimental.pallas import tpu as pltpu

F32 = jnp.float32
BF16 = jnp.bfloat16

VMEM_LIMIT_BYTES = 48 * 1024 * 1024
LANE = 128

NORM_EPS = 1e-6
CHUNK = 64
N_BRANCHES = 3
BRANCH_WIDTH = 512
SB_HEADS = 8
SB_HEAD_DIM = 64
CONV_WIDTH = 31
CONV_HALO = 32
GLA_HEADS = 4
GLA_HEAD_K = 64
GLA_HEAD_V = 128
GLA_KEY_DIM = GLA_HEADS * GLA_HEAD_K
GLA_VALUE_DIM = GLA_HEADS * GLA_HEAD_V
GLA_GATE_RANK = 16
GLA_GATE_TAU = 16.0
SB_BQ = 1024
SB_BK = 128

ADAM_LR = 0.001
ADAM_B1 = 0.9
ADAM_B2 = 0.999
ADAM_EPS = 1e-08
ADAM_WD = 0.01
ADAM_STEP = 10

N_DEV = 8
MESH_AXES = ("x", "y", "c")
PACK_COLS = 1024


def _params(sem):
    return pltpu.CompilerParams(dimension_semantics=sem, vmem_limit_bytes=VMEM_LIMIT_BYTES)


def _div_tile(n, cap, unit):
    if n <= cap:
        return n
    best = None
    for t in range(unit, cap + 1, unit):
        if n % t == 0:
            best = t
    assert best is not None, (n, cap, unit)
    return best


_NN = (((1,), (0,)), ((), ()))
_NT = (((1,), (1,)), ((), ()))
_TN = (((0,), (0,)), ((), ()))


def _dot(a, b, dims=_NN):
    return lax.dot_general(a, b, dims, preferred_element_type=F32)


def _sigmoid(v):
    return 1.0 / (1.0 + jnp.exp(-v))


def _mm(a, b, *, nt=False, tn=False, out_dtype=F32, name):
    a = a.astype(BF16)
    b = b.astype(BF16)
    k, m = a.shape[::-1] if not tn else a.shape
    n = b.shape[0] if nt else b.shape[1]
    assert (b.shape[1] if nt else b.shape[0]) == k
    tm = _div_tile(m, 512, LANE if tn else 16)
    tn_ = _div_tile(n, 2048, LANE)
    tk = _div_tile(k, 2048, LANE)
    nk = k // tk
    dims = _NT if nt else (_TN if tn else _NN)

    def kern(a_ref, b_ref, o_ref, acc_ref):
        kk = pl.program_id(2)

        @pl.when(kk == 0)
        def _():
            acc_ref[...] = jnp.zeros_like(acc_ref)

        acc_ref[...] += _dot(a_ref[...], b_ref[...], dims)

        @pl.when(kk == nk - 1)
        def _():
            o_ref[...] = acc_ref[...].astype(o_ref.dtype)

    b_spec = (pl.BlockSpec((tn_, tk), lambda i, j, kk: (j, kk)) if nt
              else pl.BlockSpec((tk, tn_), lambda i, j, kk: (kk, j)))
    return pl.pallas_call(
        kern, name=name,
        out_shape=jax.ShapeDtypeStruct((m, n), out_dtype),
        grid=(m // tm, n // tn_, nk),
        in_specs=[pl.BlockSpec((tk, tm), lambda i, j, kk: (kk, i)) if tn
                  else pl.BlockSpec((tm, tk), lambda i, j, kk: (i, kk)), b_spec],
        out_specs=pl.BlockSpec((tm, tn_), lambda i, j, kk: (i, j)),
        scratch_shapes=[pltpu.VMEM((tm, tn_), F32)],
        compiler_params=_params(("parallel", "parallel", "arbitrary")),
    )(a, b)


def _rowwise(name, body, mats, vecs, outs, sums=(), tm=256):
    mats = [m if isinstance(m, tuple) else (m, 0, m.shape[1]) for m in mats]
    t = mats[0][0].shape[0]
    tm = _div_tile(t, tm, 8)
    nm, nv, no, ns = len(mats), len(vecs), len(outs), len(sums)

    def kern(*refs):
        i = pl.program_id(0)
        ins = [r[...] for r in refs[:nm + nv]]
        res = body(*ins)
        out_vals, sum_vals = res[:no], res[no:]
        for r, val in zip(refs[nm + nv:nm + nv + no], out_vals):
            if isinstance(val, (list, tuple)):
                off = 0
                for piece in val:
                    w = piece.shape[1]
                    r[:, off:off + w] = piece.astype(r.dtype)
                    off += w
            else:
                r[...] = val.astype(r.dtype)
        if ns:
            sum_refs = refs[nm + nv + no:]

            @pl.when(i == 0)
            def _():
                for r in sum_refs:
                    r[...] = jnp.zeros_like(r)

            for r, val in zip(sum_refs, sum_vals):
                r[...] += jnp.sum(val, axis=0, keepdims=True)

    in_specs = [pl.BlockSpec((tm, w), functools.partial(lambda i, cb: (i, cb), cb=cb)) for (_, cb, w) in mats]
    in_specs += [pl.BlockSpec(v.shape, lambda i: (0, 0)) for v in vecs]
    out_specs = [pl.BlockSpec((tm, w), lambda i: (i, 0)) for (w, _) in outs]
    out_specs += [pl.BlockSpec((1, w), lambda i: (0, 0)) for w in sums]
    out_shape = [jax.ShapeDtypeStruct((t, w), dt) for (w, dt) in outs]
    out_shape += [jax.ShapeDtypeStruct((1, w), F32) for w in sums]
    return pl.pallas_call(
        kern, name=name, out_shape=out_shape, grid=(t // tm,),
        in_specs=in_specs, out_specs=out_specs,
        compiler_params=_params(("arbitrary",)),
    )(*[m[0] for m in mats], *vecs)


def _rms_fwd(x, g, name):
    d = x.shape[1]

    def body(xv, gv):
        r = lax.rsqrt(jnp.mean(xv * xv, axis=-1, keepdims=True) + NORM_EPS)
        return ((xv * r) * gv,)

    return _rowwise(name, body, [x], [g], [(d, BF16)])[0]


def _post_res(o, x, g, c, name):
    d = x.shape[1]

    def body(ov, xv, gv):
        r = lax.rsqrt(jnp.mean(ov * ov, axis=-1, keepdims=True) + NORM_EPS)
        return (xv + c * ((ov * r) * gv),)

    return _rowwise(name, body, [o, x], [g], [(d, F32)])[0]


def _post_bwd(dx, o, g, c, name):
    d = dx.shape[1]

    def body(dxv, ov, gv):
        r = lax.rsqrt(jnp.mean(ov * ov, axis=-1, keepdims=True) + NORM_EPS)
        n = ov * r
        dy = c * dxv
        dn = dy * gv
        do = r * (dn - n * jnp.mean(dn * n, axis=-1, keepdims=True))
        return (do, dy * n)

    return _rowwise(name, body, [dx, o], [g], [(d, BF16)], sums=[d])


def _rms_bwd(dh_parts, x, g, dx_res, name):
    d = x.shape[1]
    npart = len(dh_parts)

    def body(*vals):
        dh = vals[0]
        for p in vals[1:npart]:
            dh = dh + p
        xv, dres, gv = vals[npart], vals[npart + 1], vals[npart + 2]
        r = lax.rsqrt(jnp.mean(xv * xv, axis=-1, keepdims=True) + NORM_EPS)
        n = xv * r
        dn = dh * gv
        dx = dres + r * (dn - n * jnp.mean(dn * n, axis=-1, keepdims=True))
        return (dx, dh * n)

    return _rowwise(name, body, list(dh_parts) + [x, dx_res], [g], [(d, F32)], sums=[d], tm=128)


def _swiglu_fwd(gate, up, name):
    f = gate.shape[1]

    def body(gv, uv):
        gv = gv.astype(F32)
        uv = uv.astype(F32)
        return ((gv * _sigmoid(gv)) * uv,)

    return _rowwise(name, body, [gate, up], [], [(f, BF16)])[0]


def _swiglu_bwd(da, gate, up, name):
    f = gate.shape[1]

    def body(dav, gv, uv):
        gv = gv.astype(F32)
        uv = uv.astype(F32)
        s = _sigmoid(gv)
        silu = gv * s
        dgate = dav * uv * (s * (1.0 + gv * (1.0 - s)))
        dup = dav * silu
        return (dgate, dup)

    return _rowwise(name, body, [da, gate, up], [], [(f, BF16), (f, BF16)])


def _merge_fwd(bds, logits, name):
    d = bds[0].shape[1]

    def body(b0, b1, b2, l0, l1, l2):
        return (_sigmoid(l0) * b0 + _sigmoid(l1) * b1 + _sigmoid(l2) * b2,)

    mats = list(bds) + [(logits, j, d) for j in range(N_BRANCHES)]
    return _rowwise(name, body, mats, [], [(d, BF16)], tm=128)[0]


def _merge_bwd(dmerged, bds, logits, name):
    d = bds[0].shape[1]

    def body(dm, b0, b1, b2, l0, l1, l2):
        dbs, dls = [], []
        for b, l in ((b0, l0), (b1, l1), (b2, l2)):
            s = _sigmoid(l)
            dbs.append(dm * s)
            dls.append(dm * b * (s * (1.0 - s)))
        return (dbs[0], dbs[1], dbs[2], dls)

    mats = [dmerged] + list(bds) + [(logits, j, d) for j in range(N_BRANCHES)]
    outs = [(d, BF16)] * 3 + [(N_BRANCHES * d, BF16)]
    return _rowwise(name, body, mats, [], outs, tm=128)


def _la_fwd(pre, b, name):
    w = pre.shape[1]

    def body(pv, bv):
        p = pv + bv
        sp = jnp.maximum(-p, 0.0) + jnp.log(1.0 + jnp.exp(-jnp.abs(p)))
        return (-sp / GLA_GATE_TAU,)

    return _rowwise(name, body, [pre], [b], [(w, F32)])[0]


def _la_bwd(dla, pre, b, name):
    w = pre.shape[1]

    def body(dv, pv, bv):
        p = pv + bv
        dpre = (dv / GLA_GATE_TAU) * _sigmoid(-p)
        return (dpre, dpre)

    return _rowwise(name, body, [dla, pre], [b], [(w, BF16)], sums=[w])


def _gla_post_fwd(o, p_gla, r_block, gn, name):
    w = o.shape[1]

    def body(ov, rv, gv):
        pieces = []
        for h in range(GLA_HEADS):
            sl = slice(h * GLA_HEAD_V, (h + 1) * GLA_HEAD_V)
            oh = ov[:, sl]
            rr = lax.rsqrt(jnp.mean(oh * oh, axis=-1, keepdims=True) + NORM_EPS)
            rh = rv[:, sl]
            pieces.append(((oh * rr) * gv[:, sl]) * (rh * _sigmoid(rh)))
        return (pieces,)

    return _rowwise(name, body, [o, (p_gla, r_block, w)], [gn], [(w, BF16)])[0]


def _gla_post_bwd(dout, o, p_gla, r_block, gn, name):
    w = o.shape[1]

    def body(dv, ov, rv, gv):
        dos, drs, dgs = [], [], []
        for h in range(GLA_HEADS):
            sl = slice(h * GLA_HEAD_V, (h + 1) * GLA_HEAD_V)
            oh, rh, gh, dh = ov[:, sl], rv[:, sl], gv[:, sl], dv[:, sl]
            rr = lax.rsqrt(jnp.mean(oh * oh, axis=-1, keepdims=True) + NORM_EPS)
            nhat = oh * rr
            s = _sigmoid(rh)
            dn = dh * (rh * s)
            drs.append(dh * (nhat * gh) * (s * (1.0 + rh * (1.0 - s))))
            dnn = dn * gh
            dos.append(rr * (dnn - nhat * jnp.mean(dnn * nhat, axis=-1, keepdims=True)))
            dgs.append(dn * nhat)
        return (dos, drs, jnp.concatenate(dgs, axis=1))

    return _rowwise(name, body, [dout, o, (p_gla, r_block, w)], [gn], [(w, F32), (w, BF16)], sums=[w])


def _conv_bwd1(dout, y, ln_g, ln_b, name):
    w = y.shape[1]

    def body(dv, yv, gv, bv):
        mu = jnp.mean(yv, axis=-1, keepdims=True)
        yc = yv - mu
        rstd = lax.rsqrt(jnp.mean(yc * yc, axis=-1, keepdims=True) + NORM_EPS)
        xhat = yc * rstd
        yn = xhat * gv + bv
        s = _sigmoid(yn)
        dyn = dv * (s * (1.0 + yn * (1.0 - s)))
        dxh = dyn * gv
        dy = rstd * (dxh - jnp.mean(dxh, axis=-1, keepdims=True)
                     - xhat * jnp.mean(dxh * xhat, axis=-1, keepdims=True))
        return (dy, dyn * xhat, dyn, dy)

    return _rowwise(name, body, [dout, y], [ln_g, ln_b], [(w, F32)], sums=[w, w, w])


def _loss_fwd(y, target, name):
    d = y.shape[1]

    def body(yv, tv):
        e = yv - tv
        return (e / d, e * e)

    return _rowwise(name, body, [y, target], [], [(d, F32)], sums=[d])


def _adamw(w, g, m, v, name):
    cols = w.shape[1]
    c1 = 1.0 - ADAM_B1 ** ADAM_STEP
    c2 = 1.0 - ADAM_B2 ** ADAM_STEP

    def body(wv, gv, mv, vv):
        m2 = ADAM_B1 * mv + (1.0 - ADAM_B1) * gv
        v2 = ADAM_B2 * vv + (1.0 - ADAM_B2) * (gv * gv)
        m_hat = m2 / c1
        v_hat = v2 / c2
        delta = -ADAM_LR * (m_hat / (jnp.sqrt(v_hat) + ADAM_EPS) + ADAM_WD * wv)
        return (delta, m2, v2)

    return _rowwise(name, body, [w, g, m, v], [], [(cols, F32)] * 3)


def _split_bf16(v):
    hi = v.astype(BF16)
    lo = (v - hi.astype(F32)).astype(BF16)
    return hi, lo


def _tri(n, strict):
    r = lax.broadcasted_iota(jnp.int32, (n, n), 0)
    c = lax.broadcasted_iota(jnp.int32, (n, n), 1)
    return jnp.where(r > c if strict else r >= c, 1.0, 0.0).astype(BF16)


SB_HEADS_PER_STEP = 1


def _sb_weights(qv, kt, c, mx, scale, diag_offset):
    z = _dot(qv, kt) * scale
    u = jnp.exp(-jnp.abs(z))
    sp = jnp.maximum(z, 0.0) + jnp.log(1.0 + u)
    if diag_offset is None:
        mask = None
        lk = -sp
    else:
        rows = lax.broadcasted_iota(jnp.int32, z.shape, 0)
        cols = lax.broadcasted_iota(jnp.int32, z.shape, 1)
        mask = cols + diag_offset < rows
        lk = jnp.where(mask, -sp, 0.0)
    hi, lo = _split_bf16(lk)
    suf = _dot(hi, mx) + _dot(lo, mx)
    w = jnp.exp((z - sp) + suf + c)
    if mask is not None:
        w = jnp.where(mask, w, 0.0)
    return z, u, lk, w, mask


def _sb_fwd(q, kt, vt, name):
    nh, t, dh = q.shape
    nkb, bk = kt.shape[1], kt.shape[3]
    bq = min(SB_BQ, t)
    scale = dh ** -0.5
    per = bq // bk
    hp = SB_HEADS_PER_STEP

    def kern(q_ref, kt_ref, vt_ref, o_ref):
        qi = pl.program_id(1)
        top = (qi + 1) * per - 1
        qs = [q_ref[h] for h in range(hp)]
        mx = _tri(bk, True)

        def tile(h, kb, carry, diag_offset):
            c, acc = carry
            _, _, lk, w, _ = _sb_weights(qs[h], kt_ref[h, kb], c, mx, scale, diag_offset)
            acc = acc + _dot(w.astype(BF16), vt_ref[h, kb], _NT)
            return c + jnp.sum(lk, axis=1, keepdims=True), acc

        carry = tuple((jnp.zeros((bq, 1), F32), jnp.zeros((bq, dh), F32)) for _ in range(hp))

        def diag_step(i, carry):
            return tuple(tile(h, top - i, carry[h], (per - 1 - i) * bk) for h in range(hp))

        def step(i, carry):
            return tuple(tile(h, top - i, carry[h], None) for h in range(hp))

        carry = lax.fori_loop(0, per, diag_step, carry)
        carry = lax.fori_loop(per, (qi + 1) * per, step, carry)
        for h in range(hp):
            o_ref[h] = carry[h][1]

    return pl.pallas_call(
        kern, name=name,
        out_shape=jax.ShapeDtypeStruct((nh, t, dh), F32),
        grid=(nh // hp, t // bq),
        in_specs=[pl.BlockSpec((hp, bq, dh), lambda h, i: (h, i, 0)),
                  pl.BlockSpec((hp, nkb, dh, bk), lambda h, i: (h, 0, 0, 0)),
                  pl.BlockSpec((hp, nkb, dh, bk), lambda h, i: (h, 0, 0, 0))],
        out_specs=pl.BlockSpec((hp, bq, dh), lambda h, i: (h, i, 0)),
        compiler_params=_params(("parallel", "arbitrary")),
    )(q, kt, vt)


def _sb_bwd(q, qt, kt, vt, out, dout, doutt, name):
    nh, t, dh = q.shape
    nkb, bk = kt.shape[1], kt.shape[3]
    bq = min(SB_BQ, t)
    scale = dh ** -0.5
    per = bq // bk
    hp = SB_HEADS_PER_STEP

    def kern(q_ref, qt_ref, kt_ref, vt_ref, o_ref, do_ref, dot_ref, dq_ref, dkt_ref, dvt_ref):
        qi = pl.program_id(1)
        top = (qi + 1) * per - 1

        @pl.when(qi == 0)
        def _():
            dkt_ref[...] = jnp.zeros_like(dkt_ref)
            dvt_ref[...] = jnp.zeros_like(dvt_ref)

        qs = [q_ref[h] for h in range(hp)]
        qts = [qt_ref[h] for h in range(hp)]
        dobs = [do_ref[h].astype(BF16) for h in range(hp)]
        dots = [dot_ref[h] for h in range(hp)]
        dsums = [jnp.sum(dobs[h].astype(F32) * o_ref[h], axis=1, keepdims=True) for h in range(hp)]
        mx = _tri(bk, True)
        mi = _tri(bk, False)

        def tile(h, kb, carry, diag_offset):
            c, ce, dq = carry
            kt_blk = kt_ref[h, kb]
            z, u, lk, w, mask = _sb_weights(qs[h], kt_blk, c, mx, scale, diag_offset)
            wb = w.astype(BF16)
            e = _dot(dobs[h], vt_ref[h, kb]) * wb.astype(F32)
            ehi, elo = _split_bf16(e)
            before = dsums[h] - (_dot(ehi, mi) + _dot(elo, mi) + ce)
            rcp = 1.0 / (1.0 + u)
            sig = jnp.where(z >= 0.0, rcp, u * rcp)
            dz = (e * (1.0 - sig) - sig * before) * scale
            if mask is not None:
                dz = jnp.where(mask, dz, 0.0)
            dz = dz.astype(BF16)
            dq = dq + _dot(dz, kt_blk, _NT)
            dkt_ref[h, kb] += _dot(qts[h], dz)
            dvt_ref[h, kb] += _dot(dots[h], wb)
            return c + jnp.sum(lk, axis=1, keepdims=True), ce + jnp.sum(e, axis=1, keepdims=True), dq

        zero = jnp.zeros((bq, 1), F32)
        carry = tuple((zero, zero, jnp.zeros((bq, dh), F32)) for _ in range(hp))

        def diag_step(i, carry):
            return tuple(tile(h, top - i, carry[h], (per - 1 - i) * bk) for h in range(hp))

        def step(i, carry):
            return tuple(tile(h, top - i, carry[h], None) for h in range(hp))

        carry = lax.fori_loop(0, per, diag_step, carry)
        carry = lax.fori_loop(per, (qi + 1) * per, step, carry)
        for h in range(hp):
            dq_ref[h] = carry[h][2]

    row = pl.BlockSpec((hp, bq, dh), lambda h, i: (h, i, 0))
    col = pl.BlockSpec((hp, dh, bq), lambda h, i: (h, 0, i))
    whole = pl.BlockSpec((hp, nkb, dh, bk), lambda h, i: (h, 0, 0, 0))
    return pl.pallas_call(
        kern, name=name,
        out_shape=[jax.ShapeDtypeStruct((nh, t, dh), F32),
                   jax.ShapeDtypeStruct((nh, nkb, dh, bk), F32),
                   jax.ShapeDtypeStruct((nh, nkb, dh, bk), F32)],
        grid=(nh // hp, t // bq),
        in_specs=[row, col, whole, whole, row, row, col],
        out_specs=[row, whole, whole],
        compiler_params=_params(("parallel", "arbitrary")),
    )(q, qt, kt, vt, out, dout, doutt)


def _conv_fwd(p_conv, conv_w, conv_b, ln_g, ln_b, name):
    t, c2 = p_conv.shape
    c = c2 // 2
    tm = min(256, t)
    hb = tm // CONV_HALO

    def kern(a_ref, g_ref, ah_ref, gh_ref, w_ref, b_ref, lg_ref, lb_ref, o_ref, u_ref, y_ref, ubuf):
        i = pl.program_id(0)
        u = a_ref[...] * _sigmoid(g_ref[...])
        uh = ah_ref[...] * _sigmoid(gh_ref[...])
        ubuf[0:CONV_HALO, :] = jnp.where(i > 0, uh, 0.0)
        ubuf[CONV_HALO:CONV_HALO + tm, :] = u
        y = jnp.zeros((tm, c), F32) + b_ref[...]
        for j in range(CONV_WIDTH):
            off = CONV_HALO - (CONV_WIDTH - 1) + j
            y = y + ubuf[off:off + tm, :] * w_ref[j:j + 1, :]
        mu = jnp.mean(y, axis=-1, keepdims=True)
        yc = y - mu
        rstd = lax.rsqrt(jnp.mean(yc * yc, axis=-1, keepdims=True) + NORM_EPS)
        yn = (yc * rstd) * lg_ref[...] + lb_ref[...]
        o_ref[...] = (yn * _sigmoid(yn)).astype(o_ref.dtype)
        u_ref[...] = u
        y_ref[...] = y

    def halo(cb):
        return pl.BlockSpec((CONV_HALO, c), lambda i: (jnp.maximum(i * hb - 1, 0), cb))

    vec = pl.BlockSpec((1, c), lambda i: (0, 0))
    tile = pl.BlockSpec((tm, c), lambda i: (i, 0))
    return pl.pallas_call(
        kern, name=name,
        out_shape=[jax.ShapeDtypeStruct((t, c), BF16), jax.ShapeDtypeStruct((t, c), F32),
                   jax.ShapeDtypeStruct((t, c), F32)],
        grid=(t // tm,),
        in_specs=[tile, pl.BlockSpec((tm, c), lambda i: (i, 1)), halo(0), halo(1),
                  pl.BlockSpec(conv_w.shape, lambda i: (0, 0)), vec, vec, vec],
        out_specs=[tile, tile, tile],
        scratch_shapes=[pltpu.VMEM((tm + CONV_HALO, c), F32)],
        compiler_params=_params(("arbitrary",)),
    )(p_conv, p_conv, p_conv, p_conv, conv_w, conv_b, ln_g, ln_b)


def _conv_bwd2(dy, u, p_conv, conv_w, name):
    t, c = dy.shape
    tm = min(256, t)
    hb = tm // CONV_HALO
    nt = t // tm
    last_halo = t // CONV_HALO - 1

    def kern(dy_ref, dyh_ref, u_ref, uh_ref, a_ref, g_ref, w_ref, dp_ref, dw_ref, dybuf, ubuf):
        i = pl.program_id(0)

        @pl.when(i == 0)
        def _():
            dw_ref[...] = jnp.zeros_like(dw_ref)

        dyv = dy_ref[...]
        dybuf[0:tm, :] = dyv
        dybuf[tm:tm + CONV_HALO, :] = jnp.where(i < nt - 1, dyh_ref[...], 0.0)
        ubuf[0:CONV_HALO, :] = jnp.where(i > 0, uh_ref[...], 0.0)
        ubuf[CONV_HALO:CONV_HALO + tm, :] = u_ref[...]
        du = jnp.zeros((tm, c), F32)
        for j in range(CONV_WIDTH):
            off = CONV_WIDTH - 1 - j
            du = du + dybuf[off:off + tm, :] * w_ref[j:j + 1, :]
            uoff = CONV_HALO - (CONV_WIDTH - 1) + j
            dw_ref[j:j + 1, :] += jnp.sum(dyv * ubuf[uoff:uoff + tm, :], axis=0, keepdims=True)
        a = a_ref[...]
        s = _sigmoid(g_ref[...])
        dp_ref[:, 0:c] = (du * s).astype(dp_ref.dtype)
        dp_ref[:, c:2 * c] = (du * a * (s * (1.0 - s))).astype(dp_ref.dtype)

    tile = pl.BlockSpec((tm, c), lambda i: (i, 0))
    return pl.pallas_call(
        kern, name=name,
        out_shape=[jax.ShapeDtypeStruct((t, 2 * c), BF16), jax.ShapeDtypeStruct((CONV_HALO, c), F32)],
        grid=(nt,),
        in_specs=[tile,
                  pl.BlockSpec((CONV_HALO, c), lambda i: (jnp.minimum((i + 1) * hb, last_halo), 0)),
                  tile,
                  pl.BlockSpec((CONV_HALO, c), lambda i: (jnp.maximum(i * hb - 1, 0), 0)),
                  tile, pl.BlockSpec((tm, c), lambda i: (i, 1)),
                  pl.BlockSpec(conv_w.shape, lambda i: (0, 0))],
        out_specs=[pl.BlockSpec((tm, 2 * c), lambda i: (i, 0)),
                   pl.BlockSpec((CONV_HALO, c), lambda i: (0, 0))],
        scratch_shapes=[pltpu.VMEM((tm + CONV_HALO, c), F32), pltpu.VMEM((tm + CONV_HALO, c), F32)],
        compiler_params=_params(("arbitrary",)),
    )(dy, dy, u, u, p_conv, p_conv, conv_w)


GLA_CHUNKS_PER_STEP = 16


def _gla_fwd(q_ck, k_kc, la_kc, v, name):
    nh, nc, ch, dk = q_ck.shape
    dv = v.shape[3]
    qscale = dk ** -0.5
    cb = min(GLA_CHUNKS_PER_STEP, nc)

    def kern(q_ref, k_ref, la_ref, v_ref, o_ref, st_ref, state_ref):
        mx = _tri(ch, True)

        @pl.when(pl.program_id(1) == 0)
        def _():
            state_ref[...] = jnp.zeros_like(state_ref)

        def step(n, state):
            la = la_ref[n]
            hi, lo = _split_bf16(la)
            de = _dot(hi, mx) + _dot(lo, mx)
            lam = jnp.exp(jnp.sum(la, axis=1, keepdims=True))
            kd = (k_ref[n] * jnp.exp(de)).astype(BF16)
            state = lam * state + _dot(kd, v_ref[n])
            st_ref[n] = state
            qs = (q_ref[n].astype(F32) * qscale).astype(BF16)
            o_ref[n] = _dot(qs, state.astype(BF16))
            return state

        state_ref[...] = lax.fori_loop(0, cb, step, state_ref[...])

    def spec(a, b):
        return pl.BlockSpec((None, cb, a, b), lambda h, j: (h, j, 0, 0))

    return pl.pallas_call(
        kern, name=name,
        out_shape=[jax.ShapeDtypeStruct((nh, nc, ch, dv), F32), jax.ShapeDtypeStruct((nh, nc, dk, dv), F32)],
        grid=(nh, nc // cb),
        in_specs=[spec(ch, dk), spec(dk, ch), spec(dk, ch), spec(ch, dv)],
        out_specs=[spec(ch, dv), spec(dk, dv)],
        scratch_shapes=[pltpu.VMEM((dk, dv), F32)],
        compiler_params=_params(("parallel", "arbitrary")),
    )(q_ck, k_kc, la_kc, v)


def _gla_bwd(q_ck, q_kc, k_ck, k_kc, la_ck, la_kc, v, states, states_prev, do, name):
    nh, nc, ch, dk = q_ck.shape
    dv = v.shape[3]
    qscale = dk ** -0.5
    cb = min(GLA_CHUNKS_PER_STEP, nc)
    nb = nc // cb

    def kern(q_ref, qt_ref, kck_ref, kkc_ref, lack_ref, lakc_ref, v_ref, st_ref, stp_ref, do_ref,
             dq_ref, dk_ref, dv_ref, dla_ref, g_ref):
        mx = _tri(ch, True)
        mxt = jnp.where(lax.broadcasted_iota(jnp.int32, (ch, ch), 0)
                        < lax.broadcasted_iota(jnp.int32, (ch, ch), 1), 1.0, 0.0).astype(BF16)

        @pl.when(pl.program_id(1) == 0)
        def _():
            g_ref[...] = jnp.zeros_like(g_ref)

        def step(i, g):
            n = cb - 1 - i
            la_kc = lakc_ref[n]
            hi, lo = _split_bf16(la_kc)
            de_kc = _dot(hi, mx) + _dot(lo, mx)
            lam = jnp.exp(jnp.sum(la_kc, axis=1, keepdims=True))
            hi2, lo2 = _split_bf16(lack_ref[n])
            de_ck = _dot(mxt, hi2) + _dot(mxt, lo2)
            edk = jnp.exp(de_kc)
            kd_kc = kkc_ref[n] * edk
            kd_ck = (kck_ref[n] * jnp.exp(de_ck)).astype(BF16)
            dob = do_ref[n].astype(BF16)
            dq_ref[n] = _dot(dob, st_ref[n].astype(BF16), _NT) * qscale
            qts = (qt_ref[n].astype(F32) * qscale).astype(BF16)
            ds = _dot(qts, dob) + g
            dsb = ds.astype(BF16)
            dlam = jnp.sum(ds * stp_ref[n], axis=1, keepdims=True)
            dkd = _dot(dsb, v_ref[n], _NT)
            dv_ref[n] = _dot(kd_ck, dsb)
            dk_ref[n] = dkd * edk
            dde = dkd * kd_kc
            h3, l3 = _split_bf16(dde)
            dla_ref[n] = _dot(h3, mxt) + _dot(l3, mxt) + dlam * lam
            return lam * ds

        g_ref[...] = lax.fori_loop(0, cb, step, g_ref[...])

    def spec(a, b):
        return pl.BlockSpec((None, cb, a, b), lambda h, j: (h, nb - 1 - j, 0, 0))

    return pl.pallas_call(
        kern, name=name,
        out_shape=[jax.ShapeDtypeStruct((nh, nc, ch, dk), F32), jax.ShapeDtypeStruct((nh, nc, dk, ch), F32),
                   jax.ShapeDtypeStruct((nh, nc, ch, dv), F32), jax.ShapeDtypeStruct((nh, nc, dk, ch), F32)],
        grid=(nh, nb),
        in_specs=[spec(ch, dk), spec(dk, ch), spec(ch, dk), spec(dk, ch), spec(ch, dk), spec(dk, ch),
                  spec(ch, dv), spec(dk, dv), spec(dk, dv), spec(ch, dv)],
        out_specs=[spec(ch, dk), spec(dk, ch), spec(ch, dv), spec(dk, ch)],
        scratch_shapes=[pltpu.VMEM((dk, dv), F32)],
        compiler_params=_params(("parallel", "arbitrary")),
    )(q_ck, q_kc, k_ck, k_kc, la_ck, la_kc, v, states, states_prev, do)


def _heads_rows(a, nh, blk):
    t = a.shape[0]
    d = a.shape[1] // nh
    return a.reshape(t // blk, blk, nh, d).transpose(2, 0, 1, 3)


def _heads_cols(a, nh, blk):
    t = a.shape[0]
    d = a.shape[1] // nh
    return a.reshape(t // blk, blk, nh, d).transpose(2, 0, 3, 1)


def _unheads_rows(a):
    nh, nb, blk, d = a.shape
    return a.transpose(1, 2, 0, 3).reshape(nb * blk, nh * d)


def _unheads_cols(a):
    nh, nb, d, blk = a.shape
    return a.transpose(1, 3, 0, 2).reshape(nb * blk, nh * d)


def _ffn_fwd(x, w, pre, post, tag):
    h = _rms_fwd(x, pre, f"{tag}_rms")
    gate = _mm(h, w["gate"], out_dtype=BF16, name=f"{tag}_gate")
    up = _mm(h, w["up"], out_dtype=BF16, name=f"{tag}_up")
    act = _swiglu_fwd(gate, up, f"{tag}_act")
    o = _mm(act, w["down"], name=f"{tag}_down")
    x_out = _post_res(o, x, post, 0.5, f"{tag}_res")
    return x_out, (x, h, gate, up, act, o)


def _ffn_bwd(dx_out, saved, w, pre, post, tag):
    x, h, gate, up, act, o = saved
    do, dpost = _post_bwd(dx_out, o, post, 0.5, f"{tag}_bres")
    da = _mm(do, w["down"], nt=True, name=f"{tag}_bda")
    d_down = _mm(act, do, tn=True, name=f"{tag}_bwdown")
    dgate, dup = _swiglu_bwd(da, gate, up, f"{tag}_bact")
    dh_g = _mm(dgate, w["gate"], nt=True, name=f"{tag}_bdhg")
    dh_u = _mm(dup, w["up"], nt=True, name=f"{tag}_bdhu")
    d_gate = _mm(h, dgate, tn=True, name=f"{tag}_bwgate")
    d_up = _mm(h, dup, tn=True, name=f"{tag}_bwup")
    dx, dpre = _rms_bwd([dh_g, dh_u], x, pre, dx_out, f"{tag}_brms")
    return dx, dict(gate=d_gate, up=d_up, down=d_down, pre=dpre, post=dpost)


def _mixer_fwd(x, w, tag):
    t, d = x.shape
    bw = BRANCH_WIDTH
    h = _rms_fwd(x, w["pre"], f"{tag}_rms")
    p_sb = _mm(h, w["in_sb"], out_dtype=BF16, name=f"{tag}_insb")
    p_conv = _mm(h, w["in_conv"], name=f"{tag}_inconv")
    p_gla = _mm(h, w["in_gla"], name=f"{tag}_ingla")
    p_gate = _mm(h, w["in_gate"], name=f"{tag}_ingate")

    bk = min(SB_BK, t)
    q = p_sb[:, 0:bw].reshape(t, SB_HEADS, SB_HEAD_DIM).transpose(1, 0, 2)
    kt = _heads_cols(p_sb[:, bw:2 * bw], SB_HEADS, bk)
    vt = _heads_cols(p_sb[:, 2 * bw:3 * bw], SB_HEADS, bk)
    sb_o = _sb_fwd(q, kt, vt, f"{tag}_sb")
    sb_out = sb_o.transpose(1, 0, 2).reshape(t, bw)

    conv_out, conv_u, conv_y = _conv_fwd(p_conv, w["conv_w"], w["conv_b"], w["ln_g"], w["ln_b"], f"{tag}_conv")

    kd, vd = GLA_KEY_DIM, GLA_VALUE_DIM
    lr = p_gla[:, 2 * kd + 2 * vd:]
    pre_a = _mm(lr, w["alpha"], name=f"{tag}_alpha")
    la = _la_fwd(pre_a, w["b_alpha"], f"{tag}_la")
    gq = _heads_rows(p_gla[:, 0:kd].astype(BF16), GLA_HEADS, CHUNK)
    gk_kc = _heads_cols(p_gla[:, kd:2 * kd], GLA_HEADS, CHUNK)
    gv = _heads_rows(p_gla[:, 2 * kd:2 * kd + vd].astype(BF16), GLA_HEADS, CHUNK)
    la_kc = _heads_cols(la, GLA_HEADS, CHUNK)
    gla_o4, states = _gla_fwd(gq, gk_kc, la_kc, gv, f"{tag}_gla")
    gla_o = _unheads_rows(gla_o4)
    r_block = (2 * kd + vd) // vd
    gla_out = _gla_post_fwd(gla_o, p_gla, r_block, w["gnorm"], f"{tag}_glapost")

    branches = (sb_out.astype(BF16), conv_out, gla_out)
    bds = [_mm(branches[j], w["branch"][j], name=f"{tag}_br{j}") for j in range(N_BRANCHES)]
    merged = _merge_fwd(bds, p_gate, f"{tag}_merge")
    mo = _mm(merged, w["out"], name=f"{tag}_out")
    x_out = _post_res(mo, x, w["post"], 1.0, f"{tag}_res")
    saved = dict(x=x, h=h, p_sb=p_sb, p_conv=p_conv, p_gla=p_gla, p_gate=p_gate, sb_o=sb_o, conv_u=conv_u,
                 conv_y=conv_y, pre_a=pre_a, la=la, states=states, gla_o=gla_o, branches=branches, bds=bds,
                 merged=merged, mo=mo)
    return x_out, saved


def _mixer_bwd(dx_out, s, w, tag):
    x = s["x"]
    t, d = x.shape
    bw = BRANCH_WIDTH
    kd, vd = GLA_KEY_DIM, GLA_VALUE_DIM
    grads = {}
    dmo, grads["post"] = _post_bwd(dx_out, s["mo"], w["post"], 1.0, f"{tag}_bres")
    dmerged = _mm(dmo, w["out"], nt=True, name=f"{tag}_bdmerged")
    grads["out"] = _mm(s["merged"], dmo, tn=True, name=f"{tag}_bwout")
    dbd0, dbd1, dbd2, dp_gate = _merge_bwd(dmerged, s["bds"], s["p_gate"], f"{tag}_bmerge")
    dbds = (dbd0, dbd1, dbd2)
    dbranch = [_mm(dbds[j], w["branch"][j], nt=True, name=f"{tag}_bdbr{j}") for j in range(N_BRANCHES)]
    grads["branch"] = jnp.stack([_mm(s["branches"][j], dbds[j], tn=True, name=f"{tag}_bwbr{j}")
                                 for j in range(N_BRANCHES)])

    p_sb = s["p_sb"]
    bk = min(SB_BK, t)
    qf, kf, vf = p_sb[:, 0:bw], p_sb[:, bw:2 * bw], p_sb[:, 2 * bw:3 * bw]
    q = qf.reshape(t, SB_HEADS, SB_HEAD_DIM).transpose(1, 0, 2)
    qt = qf.reshape(t, SB_HEADS, SB_HEAD_DIM).transpose(1, 2, 0)
    kt = _heads_cols(kf, SB_HEADS, bk)
    vt = _heads_cols(vf, SB_HEADS, bk)
    dsb = dbranch[0].reshape(t, SB_HEADS, SB_HEAD_DIM)
    dout = dsb.transpose(1, 0, 2)
    doutt = dsb.transpose(1, 2, 0).astype(BF16)
    dq, dkt, dvt = _sb_bwd(q, qt, kt, vt, s["sb_o"], dout, doutt, f"{tag}_bsb")
    dp_sb = jnp.concatenate([dq.transpose(1, 0, 2).reshape(t, bw), _unheads_cols(dkt), _unheads_cols(dvt)],
                            axis=1).astype(BF16)

    dy, d_lng, d_lnb, d_cb = _conv_bwd1(dbranch[1], s["conv_y"], w["ln_g"], w["ln_b"], f"{tag}_bconv1")
    dp_conv, d_cw = _conv_bwd2(dy, s["conv_u"], s["p_conv"], w["conv_w"], f"{tag}_bconv2")
    grads.update(conv_w=d_cw[:CONV_WIDTH], conv_b=d_cb, ln_g=d_lng, ln_b=d_lnb)

    p_gla = s["p_gla"]
    r_block = (2 * kd + vd) // vd
    do, dr, grads["gnorm"] = _gla_post_bwd(dbranch[2], s["gla_o"], p_gla, r_block, w["gnorm"], f"{tag}_bglapost")
    gqf, gkf = p_gla[:, 0:kd], p_gla[:, kd:2 * kd]
    gq_b = gqf.astype(BF16)
    dq4, dk4, dv4, dla4 = _gla_bwd(
        _heads_rows(gq_b, GLA_HEADS, CHUNK), _heads_cols(gq_b, GLA_HEADS, CHUNK),
        _heads_rows(gkf, GLA_HEADS, CHUNK), _heads_cols(gkf, GLA_HEADS, CHUNK),
        _heads_rows(s["la"], GLA_HEADS, CHUNK), _heads_cols(s["la"], GLA_HEADS, CHUNK),
        _heads_rows(p_gla[:, 2 * kd:2 * kd + vd].astype(BF16), GLA_HEADS, CHUNK),
        s["states"], jnp.pad(s["states"][:, :-1], ((0, 0), (1, 0), (0, 0), (0, 0))),
        _heads_rows(do, GLA_HEADS, CHUNK), f"{tag}_bgla")
    dla = _unheads_cols(dla4)
    dpre_a, grads["b_alpha"] = _la_bwd(dla, s["pre_a"], w["b_alpha"], f"{tag}_bla")
    lr = p_gla[:, 2 * kd + 2 * vd:]
    dlr = _mm(dpre_a, w["alpha"], nt=True, out_dtype=BF16, name=f"{tag}_bdlr")
    grads["alpha"] = _mm(lr, dpre_a, tn=True, name=f"{tag}_bwalpha")
    dp_gla = jnp.concatenate([_unheads_rows(dq4).astype(BF16), _unheads_cols(dk4).astype(BF16),
                              _unheads_rows(dv4).astype(BF16), dr, dlr], axis=1)

    dps = dict(in_sb=dp_sb, in_conv=dp_conv, in_gla=dp_gla, in_gate=dp_gate)
    dh_parts = []
    for key, dp in dps.items():
        dh_parts.append(_mm(dp, w[key], nt=True, name=f"{tag}_bdh_{key}"))
        grads[key] = _mm(s["h"], dp, tn=True, name=f"{tag}_bw_{key}")
    dx, grads["pre"] = _rms_bwd(dh_parts, x, w["pre"], dx_out, f"{tag}_brms")
    return dx, grads


def _local_step(x, target, layers):
    saved = []
    for l, w in enumerate(layers):
        x, s1 = _ffn_fwd(x, w["ffn1"], w["pre0"], w["post0"], f"l{l}_f1")
        x, s2 = _mixer_fwd(x, w["mix"], f"l{l}_mx")
        x, s3 = _ffn_fwd(x, w["ffn2"], w["pre2"], w["post2"], f"l{l}_f2")
        saved.append((s1, s2, s3))
    dx, sq = _loss_fwd(x, target, "loss")
    grads = [None] * len(layers)
    for l in reversed(range(len(layers))):
        w = layers[l]
        s1, s2, s3 = saved[l]
        dx, g3 = _ffn_bwd(dx, s3, w["ffn2"], w["pre2"], w["post2"], f"l{l}_f2")
        dx, g2 = _mixer_bwd(dx, s2, w["mix"], f"l{l}_mx")
        dx, g1 = _ffn_bwd(dx, s1, w["ffn1"], w["pre0"], w["post0"], f"l{l}_f1")
        grads[l] = dict(ffn1=g1, mix=g2, ffn2=g3)
    return sq, dx, grads


_ANY = pl.BlockSpec(memory_space=pl.ANY)
_MESH = pl.DeviceIdType.MESH
N_CHIPS = 4


def _all_gather(p, name):
    def body(p_ref, out_ref, send_sems, recv_sems, local_sem):
        x, y, c = lax.axis_index("x"), lax.axis_index("y"), lax.axis_index("c")
        me, sibling = (x, y, c), (x, y, 1 - c)
        chips = [(1 - x, y), (x, 1 - y), (1 - x, 1 - y)]

        def rows(px, py, pc):
            return out_ref.at[4 * px + 2 * py + pc]

        def copy(k, block, to, src=None):
            return pltpu.make_async_remote_copy(
                src_ref=rows(*block) if src is None else src, dst_ref=rows(*block),
                send_sem=send_sems.at[k], recv_sem=recv_sems.at[k], device_id=to, device_id_type=_MESH)

        mine = pltpu.make_async_copy(p_ref, rows(*me), local_sem)
        mine.start()
        first = [copy(0, me, sibling, src=p_ref)]
        first += [copy(1 + j, me, (*chip, c), src=p_ref) for j, chip in enumerate(chips)]
        for cp in first:
            cp.start()
        passed = [copy(4 + j, (*chip, c), sibling) for j, chip in enumerate(chips)]
        for j, chip in enumerate(chips):
            copy(1 + j, (*chip, c), me).wait_recv()
            passed[j].start()
        copy(0, sibling, me).wait_recv()
        for j, chip in enumerate(chips):
            copy(4 + j, (*chip, 1 - c), me).wait_recv()
        for cp in first + passed:
            cp.wait_send()
        mine.wait()

    return pl.pallas_call(
        body, name=name,
        out_shape=jax.ShapeDtypeStruct((N_DEV,) + p.shape, p.dtype),
        in_specs=[_ANY], out_specs=_ANY,
        scratch_shapes=[pltpu.SemaphoreType.DMA((7,)), pltpu.SemaphoreType.DMA((7,)), pltpu.SemaphoreType.DMA],
    )(p)


def _exchange_sibling(src, name):
    def body(src_ref, out_ref, send_sems, recv_sems):
        x, y, c = lax.axis_index("x"), lax.axis_index("y"), lax.axis_index("c")
        copies = [pltpu.make_async_remote_copy(
            src_ref=src_ref.at[j, 1 - c], dst_ref=out_ref.at[j], send_sem=send_sems.at[j],
            recv_sem=recv_sems.at[j], device_id=(x, y, 1 - c), device_id_type=_MESH) for j in range(N_CHIPS)]
        for cp in copies:
            cp.start()
        for cp in copies:
            cp.wait()

    return pl.pallas_call(
        body, name=name,
        out_shape=jax.ShapeDtypeStruct((N_CHIPS,) + src.shape[2:], src.dtype),
        in_specs=[_ANY], out_specs=_ANY,
        scratch_shapes=[pltpu.SemaphoreType.DMA((N_CHIPS,)), pltpu.SemaphoreType.DMA((N_CHIPS,))],
    )(src)


def _add_own(src, got, name):
    _, _, r, cols = src.shape
    tr = _div_tile(r, 512, 16)
    c = lax.axis_index("c").astype(jnp.int32).reshape(1)

    def kern(c_ref, a_ref, b_ref, o_ref):
        o_ref[...] = (a_ref[...].astype(F32) + b_ref[...].astype(F32)).astype(o_ref.dtype)

    grid_spec = pltpu.PrefetchScalarGridSpec(
        num_scalar_prefetch=1, grid=(N_CHIPS, r // tr),
        in_specs=[pl.BlockSpec((None, None, tr, cols), lambda j, i, c_ref: (j, c_ref[0], i, 0)),
                  pl.BlockSpec((None, tr, cols), lambda j, i, c_ref: (j, i, 0))],
        out_specs=pl.BlockSpec((None, tr, cols), lambda j, i, c_ref: (j, i, 0)))
    return pl.pallas_call(
        kern, name=name, grid_spec=grid_spec,
        out_shape=jax.ShapeDtypeStruct((N_CHIPS, r, cols), src.dtype),
        compiler_params=_params(("arbitrary", "arbitrary")),
    )(c, src, got)


def _exchange_chips(part, name):
    def body(part_ref, out_ref, send_sems, recv_sems, local_sem):
        x, y, c = lax.axis_index("x"), lax.axis_index("y"), lax.axis_index("c")
        my_chip = 2 * x + y
        chips = [(1 - x, y), (x, 1 - y), (1 - x, 1 - y)]
        mine = pltpu.make_async_copy(part_ref.at[my_chip], out_ref.at[my_chip], local_sem)
        mine.start()
        copies = []
        for k, (px, py) in enumerate(chips):
            copies.append(pltpu.make_async_remote_copy(
                src_ref=part_ref.at[2 * px + py], dst_ref=out_ref.at[my_chip],
                send_sem=send_sems.at[k], recv_sem=recv_sems.at[k], device_id=(px, py, c), device_id_type=_MESH))
        for cp in copies:
            cp.start()
        for k, (px, py) in enumerate(chips):
            pltpu.make_async_remote_copy(
                src_ref=part_ref.at[my_chip], dst_ref=out_ref.at[2 * px + py],
                send_sem=send_sems.at[k], recv_sem=recv_sems.at[k], device_id=(px, py, c),
                device_id_type=_MESH).wait_recv()
        for cp in copies:
            cp.wait_send()
        mine.wait()

    return pl.pallas_call(
        body, name=name,
        out_shape=jax.ShapeDtypeStruct(part.shape, part.dtype),
        in_specs=[_ANY], out_specs=_ANY,
        scratch_shapes=[pltpu.SemaphoreType.DMA((3,)), pltpu.SemaphoreType.DMA((3,)), pltpu.SemaphoreType.DMA],
    )(part)


def _sum_chips(parts, name):
    _, r, cols = parts.shape
    tr = _div_tile(r, 512, 16)

    def kern(p_ref, o_ref):
        p = [p_ref[j].astype(F32) for j in range(N_CHIPS)]
        o_ref[...] = ((p[0] + p[1]) + p[2]) + p[3]

    return pl.pallas_call(
        kern, name=name, grid=(r // tr,),
        in_specs=[pl.BlockSpec((4, tr, cols), lambda i: (0, i, 0))],
        out_specs=pl.BlockSpec((tr, cols), lambda i: (i, 0)),
        out_shape=jax.ShapeDtypeStruct((r, cols), F32),
        compiler_params=_params(("arbitrary",)),
    )(parts)


_SHARDED = {
    "norm_pre": (1, True), "norm_post": (1, True),
    "ffn1_w_gate": (1, False), "ffn1_w_up": (1, False), "ffn1_w_down": (0, False),
    "ffn2_w_gate": (1, False), "ffn2_w_up": (1, False), "ffn2_w_down": (0, False),
    "w_in": (1, False), "conv_w": (1, True), "gla_w_alpha": (1, False),
    "w_branch": (2, False), "w_out": (0, False),
}
_REPLICATED = ("conv_b", "conv_ln_g", "conv_ln_b", "gla_b_alpha", "gla_norm_g")
_WEIGHTS = ("norm_pre", "norm_post", "ffn1_w_gate", "ffn1_w_up", "ffn1_w_down", "ffn2_w_gate", "ffn2_w_up",
            "ffn2_w_down", "w_in", "conv_w", "conv_b", "conv_ln_g", "conv_ln_b", "gla_w_alpha", "gla_b_alpha",
            "gla_norm_g", "w_branch", "w_out")


def _to_rows(flat, unit):
    n = flat.shape[-1]
    rows = -(-n // PACK_COLS)
    rows = -(-rows // unit) * unit
    pad = [(0, 0)] * (flat.ndim - 1) + [(0, rows * PACK_COLS - n)]
    return jnp.pad(flat, pad).reshape(flat.shape[:-1] + (rows, PACK_COLS))


def _pack_weights(shards, l):
    pieces = []
    for name, (_, exact) in _SHARDED.items():
        a = shards[name][l]
        if exact:
            pieces.append(lax.bitcast_convert_type(a, BF16).reshape(-1))
        else:
            pieces.append(a.astype(BF16).reshape(-1))
    return _to_rows(jnp.concatenate(pieces), 16)


def _unpack_weights(gathered, shards, l):
    flat = gathered.reshape(N_DEV, -1)
    full, off = {}, 0
    for name, (axis, exact) in _SHARDED.items():
        shp = shards[name].shape[1:]
        n = math.prod(shp) * (2 if exact else 1)
        seg = flat[:, off:off + n]
        off += n
        if exact:
            seg = lax.bitcast_convert_type(seg.reshape((N_DEV,) + shp + (2,)), F32)
        else:
            seg = seg.reshape((N_DEV,) + shp)
        seg = jnp.moveaxis(seg, 0, axis)
        full[name] = seg.reshape(shp[:axis] + (N_DEV * shp[axis],) + shp[axis + 1:])
    return full


def _pack_grads(full_grads, repl_grads):
    pieces = []
    for name, (axis, _) in _SHARDED.items():
        g = full_grads[name]
        shp = g.shape
        g = g.reshape(shp[:axis] + (N_DEV, shp[axis] // N_DEV) + shp[axis + 1:])
        pieces.append(jnp.moveaxis(g, axis, 0).reshape(N_DEV, -1).astype(BF16))
    for name in _REPLICATED:
        pieces.append(jnp.broadcast_to(repl_grads[name].reshape(1, -1).astype(BF16),
                                       (N_DEV, repl_grads[name].size)))
    rows = _to_rows(jnp.concatenate(pieces, axis=1), 16)
    return rows.reshape((N_CHIPS, 2) + rows.shape[1:])


def _pack_local(arrs, l):
    pieces = [arrs[name][l].reshape(-1) for name in _SHARDED] + [arrs[name][l].reshape(-1) for name in _REPLICATED]
    return _to_rows(jnp.concatenate(pieces), 16)


def _unpack_local(flat, arrs):
    nl = flat.shape[0]
    flat = flat.reshape(nl, -1)
    out, off = {}, 0
    for name in tuple(_SHARDED) + _REPLICATED:
        shp = arrs[name].shape[1:]
        n = math.prod(shp)
        out[name] = flat[:, off:off + n].reshape((nl,) + shp)
        off += n
    return out


def _layer_weights(full, repl, l, d_model):
    bw = BRANCH_WIDTH
    w_in = full["w_in"]
    o_conv, o_gq, o_lr = 3 * bw, 5 * bw, 5 * bw + 2 * GLA_KEY_DIM + 2 * GLA_VALUE_DIM
    o_gate = o_lr + GLA_GATE_RANK
    lr_pad = LANE - GLA_GATE_RANK
    in_gla = jnp.concatenate([w_in[:, o_gq:o_gate], jnp.zeros((d_model, lr_pad), w_in.dtype)], axis=1)
    alpha = jnp.concatenate([full["gla_w_alpha"], jnp.zeros((lr_pad, GLA_KEY_DIM), BF16)], axis=0)
    row = lambda a: a.reshape(1, -1)
    mix = dict(pre=row(full["norm_pre"][1]), post=row(full["norm_post"][1]),
               in_sb=w_in[:, 0:o_conv], in_conv=w_in[:, o_conv:o_gq], in_gla=in_gla, in_gate=w_in[:, o_gate:],
               conv_w=full["conv_w"], conv_b=row(repl["conv_b"][l]), ln_g=row(repl["conv_ln_g"][l]),
               ln_b=row(repl["conv_ln_b"][l]), alpha=alpha, b_alpha=row(repl["gla_b_alpha"][l]),
               gnorm=row(repl["gla_norm_g"][l]), branch=full["w_branch"], out=full["w_out"])
    return dict(
        pre0=row(full["norm_pre"][0]), post0=row(full["norm_post"][0]),
        pre2=row(full["norm_pre"][2]), post2=row(full["norm_post"][2]),
        ffn1=dict(gate=full["ffn1_w_gate"], up=full["ffn1_w_up"], down=full["ffn1_w_down"]),
        ffn2=dict(gate=full["ffn2_w_gate"], up=full["ffn2_w_up"], down=full["ffn2_w_down"]),
        mix=mix)


def _full_grads(g):
    mix = g["mix"]
    kd, vd = GLA_KEY_DIM, GLA_VALUE_DIM
    d_in = jnp.concatenate([mix["in_sb"], mix["in_conv"], mix["in_gla"][:, :2 * kd + 2 * vd + GLA_GATE_RANK],
                            mix["in_gate"]], axis=1)
    full = {
        "norm_pre": jnp.concatenate([g["ffn1"]["pre"], mix["pre"], g["ffn2"]["pre"]], axis=0),
        "norm_post": jnp.concatenate([g["ffn1"]["post"], mix["post"], g["ffn2"]["post"]], axis=0),
        "ffn1_w_gate": g["ffn1"]["gate"], "ffn1_w_up": g["ffn1"]["up"], "ffn1_w_down": g["ffn1"]["down"],
        "ffn2_w_gate": g["ffn2"]["gate"], "ffn2_w_up": g["ffn2"]["up"], "ffn2_w_down": g["ffn2"]["down"],
        "w_in": d_in, "conv_w": mix["conv_w"], "gla_w_alpha": mix["alpha"][:GLA_GATE_RANK],
        "w_branch": mix["branch"], "w_out": mix["out"],
    }
    repl = {"conv_b": mix["conv_b"], "conv_ln_g": mix["ln_g"], "conv_ln_b": mix["ln_b"],
            "gla_b_alpha": mix["b_alpha"], "gla_norm_g": mix["gnorm"]}
    return full, repl


def kernel(x, norm_pre, norm_post, ffn1_w_gate, ffn1_w_up, ffn1_w_down, ffn2_w_gate, ffn2_w_up, ffn2_w_down, w_in, conv_w, conv_b, conv_ln_g, conv_ln_b, gla_w_alpha, gla_b_alpha, gla_norm_g, w_branch, w_out, loss_target, m_norm_pre, m_norm_post, m_ffn1_w_gate, m_ffn1_w_up, m_ffn1_w_down, m_ffn2_w_gate, m_ffn2_w_up, m_ffn2_w_down, m_w_in, m_conv_w, m_conv_b, m_conv_ln_g, m_conv_ln_b, m_gla_w_alpha, m_gla_b_alpha, m_gla_norm_g, m_w_branch, m_w_out, v_norm_pre, v_norm_post, v_ffn1_w_gate, v_ffn1_w_up, v_ffn1_w_down, v_ffn2_w_gate, v_ffn2_w_up, v_ffn2_w_down, v_w_in, v_conv_w, v_conv_b, v_conv_ln_g, v_conv_ln_b, v_gla_w_alpha, v_gla_b_alpha, v_gla_norm_g, v_w_branch, v_w_out):
    weights = dict(norm_pre=norm_pre, norm_post=norm_post, ffn1_w_gate=ffn1_w_gate, ffn1_w_up=ffn1_w_up,
                   ffn1_w_down=ffn1_w_down, ffn2_w_gate=ffn2_w_gate, ffn2_w_up=ffn2_w_up, ffn2_w_down=ffn2_w_down,
                   w_in=w_in, conv_w=conv_w, conv_b=conv_b, conv_ln_g=conv_ln_g, conv_ln_b=conv_ln_b,
                   gla_w_alpha=gla_w_alpha, gla_b_alpha=gla_b_alpha, gla_norm_g=gla_norm_g, w_branch=w_branch,
                   w_out=w_out)
    moments_m = dict(norm_pre=m_norm_pre, norm_post=m_norm_post, ffn1_w_gate=m_ffn1_w_gate, ffn1_w_up=m_ffn1_w_up,
                     ffn1_w_down=m_ffn1_w_down, ffn2_w_gate=m_ffn2_w_gate, ffn2_w_up=m_ffn2_w_up,
                     ffn2_w_down=m_ffn2_w_down, w_in=m_w_in, conv_w=m_conv_w, conv_b=m_conv_b,
                     conv_ln_g=m_conv_ln_g, conv_ln_b=m_conv_ln_b, gla_w_alpha=m_gla_w_alpha,
                     gla_b_alpha=m_gla_b_alpha, gla_norm_g=m_gla_norm_g, w_branch=m_w_branch, w_out=m_w_out)
    moments_v = dict(norm_pre=v_norm_pre, norm_post=v_norm_post, ffn1_w_gate=v_ffn1_w_gate, ffn1_w_up=v_ffn1_w_up,
                     ffn1_w_down=v_ffn1_w_down, ffn2_w_gate=v_ffn2_w_gate, ffn2_w_up=v_ffn2_w_up,
                     ffn2_w_down=v_ffn2_w_down, w_in=v_w_in, conv_w=v_conv_w, conv_b=v_conv_b,
                     conv_ln_g=v_conv_ln_g, conv_ln_b=v_conv_ln_b, gla_w_alpha=v_gla_w_alpha,
                     gla_b_alpha=v_gla_b_alpha, gla_norm_g=v_gla_norm_g, w_branch=v_w_branch, w_out=v_w_out)
    n_layers = norm_pre.shape[0]
    t, d_model = x.shape[1], x.shape[2]

    layers = []
    for l in range(n_layers):
        gathered = _all_gather(_pack_weights(weights, l), f"gather_l{l}")
        full = _unpack_weights(gathered, weights, l)
        layers.append(_layer_weights(full, weights, l, d_model))

    sq, dx, grads = _local_step(x[0], loss_target[0], layers)
    loss = lax.psum(0.5 * jnp.sum(sq) / d_model, MESH_AXES)

    summed = []
    for l in range(n_layers):
        packed = _pack_grads(*_full_grads(grads[l]))
        got = _exchange_sibling(packed, f"rs_sibling_l{l}")
        part = _add_own(packed, got, f"rs_add_l{l}")
        parts = _exchange_chips(part, f"rs_chips_l{l}")
        summed.append(_sum_chips(parts, f"rs_sum_l{l}"))
    g_flat = jnp.stack(summed)
    nl, rows, cols = g_flat.shape

    def flat_local(arrs):
        return jnp.stack([_pack_local(arrs, l) for l in range(n_layers)]).reshape(nl * rows, cols)

    delta, new_m, new_v = _adamw(flat_local(weights), g_flat.reshape(nl * rows, cols), flat_local(moments_m),
                                 flat_local(moments_v), "adamw")
    outs = [_unpack_local(a.reshape(nl, rows, cols), weights) for a in (g_flat, delta, new_m, new_v)]
    result = [loss, dx[None]]
    for o in outs:
        result += [o[name] for name in _WEIGHTS]
    return tuple(result)
```

```python
import functools
import math

import jax
import jax.numpy as jnp
from jax import lax
from jax.experimental import pallas as pl
from jax.experimental.pallas import tpu as pltpu

F32 = jnp.float32
BF16 = jnp.bfloat16

VMEM_LIMIT_BYTES = 48 * 1024 * 1024
LANE = 128

NORM_EPS = 1e-6
CHUNK = 64
N_BRANCHES = 3
BRANCH_WIDTH = 512
SB_HEADS = 8
SB_HEAD_DIM = 64
CONV_WIDTH = 31
CONV_HALO = 32
GLA_HEADS = 4
GLA_HEAD_K = 64
GLA_HEAD_V = 128
GLA_KEY_DIM = GLA_HEADS * GLA_HEAD_K
GLA_VALUE_DIM = GLA_HEADS * GLA_HEAD_V
GLA_GATE_RANK = 16
GLA_GATE_TAU = 16.0
SB_BQ = 1024
SB_BK = 128

ADAM_LR = 0.001
ADAM_B1 = 0.9
ADAM_B2 = 0.999
ADAM_EPS = 1e-08
ADAM_WD = 0.01
ADAM_STEP = 10

N_DEV = 8
MESH_AXES = ("x", "y", "c")
PACK_COLS = 1024


def _params(sem):
    return pltpu.CompilerParams(dimension_semantics=sem, vmem_limit_bytes=VMEM_LIMIT_BYTES)


def _div_tile(n, cap, unit):
    if n <= cap:
        return n
    best = None
    for t in range(unit, cap + 1, unit):
        if n % t == 0:
            best = t
    assert best is not None, (n, cap, unit)
    return best


_NN = (((1,), (0,)), ((), ()))
_NT = (((1,), (1,)), ((), ()))
_TN = (((0,), (0,)), ((), ()))


def _dot(a, b, dims=_NN):
    return lax.dot_general(a, b, dims, preferred_element_type=F32)


def _sigmoid(v):
    return 1.0 / (1.0 + jnp.exp(-v))


def _mm(a, b, *, nt=False, tn=False, out_dtype=F32, name):
    a = a.astype(BF16)
    b = b.astype(BF16)
    k, m = a.shape[::-1] if not tn else a.shape
    n = b.shape[0] if nt else b.shape[1]
    assert (b.shape[1] if nt else b.shape[0]) == k
    tm = _div_tile(m, 512, LANE if tn else 16)
    tn_ = _div_tile(n, 2048, LANE)
    tk = _div_tile(k, 2048, LANE)
    nk = k // tk
    dims = _NT if nt else (_TN if tn else _NN)

    def kern(a_ref, b_ref, o_ref, acc_ref):
        kk = pl.program_id(2)

        @pl.when(kk == 0)
        def _():
            acc_ref[...] = jnp.zeros_like(acc_ref)

        acc_ref[...] += _dot(a_ref[...], b_ref[...], dims)

        @pl.when(kk == nk - 1)
        def _():
            o_ref[...] = acc_ref[...].astype(o_ref.dtype)

    b_spec = (pl.BlockSpec((tn_, tk), lambda i, j, kk: (j, kk)) if nt
              else pl.BlockSpec((tk, tn_), lambda i, j, kk: (kk, j)))
    return pl.pallas_call(
        kern, name=name,
        out_shape=jax.ShapeDtypeStruct((m, n), out_dtype),
        grid=(m // tm, n // tn_, nk),
        in_specs=[pl.BlockSpec((tk, tm), lambda i, j, kk: (kk, i)) if tn
                  else pl.BlockSpec((tm, tk), lambda i, j, kk: (i, kk)), b_spec],
        out_specs=pl.BlockSpec((tm, tn_), lambda i, j, kk: (i, j)),
        scratch_shapes=[pltpu.VMEM((tm, tn_), F32)],
        compiler_params=_params(("parallel", "parallel", "arbitrary")),
    )(a, b)


def _rowwise(name, body, mats, vecs, outs, sums=(), tm=256):
    mats = [m if isinstance(m, tuple) else (m, 0, m.shape[1]) for m in mats]
    t = mats[0][0].shape[0]
    tm = _div_tile(t, tm, 8)
    nm, nv, no, ns = len(mats), len(vecs), len(outs), len(sums)

    def kern(*refs):
        i = pl.program_id(0)
        ins = [r[...] for r in refs[:nm + nv]]
        res = body(*ins)
        out_vals, sum_vals = res[:no], res[no:]
        for r, val in zip(refs[nm + nv:nm + nv + no], out_vals):
            if isinstance(val, (list, tuple)):
                off = 0
                for piece in val:
                    w = piece.shape[1]
                    r[:, off:off + w] = piece.astype(r.dtype)
                    off += w
            else:
                r[...] = val.astype(r.dtype)
        if ns:
            sum_refs = refs[nm + nv + no:]

            @pl.when(i == 0)
            def _():
                for r in sum_refs:
                    r[...] = jnp.zeros_like(r)

            for r, val in zip(sum_refs, sum_vals):
                r[...] += jnp.sum(val, axis=0, keepdims=True)

    in_specs = [pl.BlockSpec((tm, w), functools.partial(lambda i, cb: (i, cb), cb=cb)) for (_, cb, w) in mats]
    in_specs += [pl.BlockSpec(v.shape, lambda i: (0, 0)) for v in vecs]
    out_specs = [pl.BlockSpec((tm, w), lambda i: (i, 0)) for (w, _) in outs]
    out_specs += [pl.BlockSpec((1, w), lambda i: (0, 0)) for w in sums]
    out_shape = [jax.ShapeDtypeStruct((t, w), dt) for (w, dt) in outs]
    out_shape += [jax.ShapeDtypeStruct((1, w), F32) for w in sums]
    return pl.pallas_call(
        kern, name=name, out_shape=out_shape, grid=(t // tm,),
        in_specs=in_specs, out_specs=out_specs,
        compiler_params=_params(("arbitrary",)),
    )(*[m[0] for m in mats], *vecs)


def _rms_fwd(x, g, name):
    d = x.shape[1]

    def body(xv, gv):
        r = lax.rsqrt(jnp.mean(xv * xv, axis=-1, keepdims=True) + NORM_EPS)
        return ((xv * r) * gv,)

    return _rowwise(name, body, [x], [g], [(d, BF16)])[0]


def _post_res(o, x, g, c, name):
    d = x.shape[1]

    def body(ov, xv, gv):
        r = lax.rsqrt(jnp.mean(ov * ov, axis=-1, keepdims=True) + NORM_EPS)
        return (xv + c * ((ov * r) * gv),)

    return _rowwise(name, body, [o, x], [g], [(d, F32)])[0]


def _post_bwd(dx, o, g, c, name):
    d = dx.shape[1]

    def body(dxv, ov, gv):
        r = lax.rsqrt(jnp.mean(ov * ov, axis=-1, keepdims=True) + NORM_EPS)
        n = ov * r
        dy = c * dxv
        dn = dy * gv
        do = r * (dn - n * jnp.mean(dn * n, axis=-1, keepdims=True))
        return (do, dy * n)

    return _rowwise(name, body, [dx, o], [g], [(d, BF16)], sums=[d])


def _rms_bwd(dh_parts, x, g, dx_res, name):
    d = x.shape[1]
    npart = len(dh_parts)

    def body(*vals):
        dh = vals[0]
        for p in vals[1:npart]:
            dh = dh + p
        xv, dres, gv = vals[npart], vals[npart + 1], vals[npart + 2]
        r = lax.rsqrt(jnp.mean(xv * xv, axis=-1, keepdims=True) + NORM_EPS)
        n = xv * r
        dn = dh * gv
        dx = dres + r * (dn - n * jnp.mean(dn * n, axis=-1, keepdims=True))
        return (dx, dh * n)

    return _rowwise(name, body, list(dh_parts) + [x, dx_res], [g], [(d, F32)], sums=[d], tm=128)


def _swiglu_fwd(gate, up, name):
    f = gate.shape[1]

    def body(gv, uv):
        gv = gv.astype(F32)
        uv = uv.astype(F32)
        return ((gv * _sigmoid(gv)) * uv,)

    return _rowwise(name, body, [gate, up], [], [(f, BF16)])[0]


def _swiglu_bwd(da, gate, up, name):
    f = gate.shape[1]

    def body(dav, gv, uv):
        gv = gv.astype(F32)
        uv = uv.astype(F32)
        s = _sigmoid(gv)
        silu = gv * s
        dgate = dav * uv * (s * (1.0 + gv * (1.0 - s)))
        dup = dav * silu
        return (dgate, dup)

    return _rowwise(name, body, [da, gate, up], [], [(f, BF16), (f, BF16)])


def _merge_fwd(bds, logits, name):
    d = bds[0].shape[1]

    def body(b0, b1, b2, l0, l1, l2):
        return (_sigmoid(l0) * b0 + _sigmoid(l1) * b1 + _sigmoid(l2) * b2,)

    mats = list(bds) + [(logits, j, d) for j in range(N_BRANCHES)]
    return _rowwise(name, body, mats, [], [(d, BF16)], tm=128)[0]


def _merge_bwd(dmerged, bds, logits, name):
    d = bds[0].shape[1]

    def body(dm, b0, b1, b2, l0, l1, l2):
        dbs, dls = [], []
        for b, l in ((b0, l0), (b1, l1), (b2, l2)):
            s = _sigmoid(l)
            dbs.append(dm * s)
            dls.append(dm * b * (s * (1.0 - s)))
        return (dbs[0], dbs[1], dbs[2], dls)

    mats = [dmerged] + list(bds) + [(logits, j, d) for j in range(N_BRANCHES)]
    outs = [(d, BF16)] * 3 + [(N_BRANCHES * d, BF16)]
    return _rowwise(name, body, mats, [], outs, tm=128)


def _la_fwd(pre, b, name):
    w = pre.shape[1]

    def body(pv, bv):
        p = pv + bv
        sp = jnp.maximum(-p, 0.0) + jnp.log(1.0 + jnp.exp(-jnp.abs(p)))
        return (-sp / GLA_GATE_TAU,)

    return _rowwise(name, body, [pre], [b], [(w, F32)])[0]


def _la_bwd(dla, pre, b, name):
    w = pre.shape[1]

    def body(dv, pv, bv):
        p = pv + bv
        dpre = (dv / GLA_GATE_TAU) * _sigmoid(-p)
        return (dpre, dpre)

    return _rowwise(name, body, [dla, pre], [b], [(w, BF16)], sums=[w])


def _gla_post_fwd(o, p_gla, r_block, gn, name):
    w = o.shape[1]

    def body(ov, rv, gv):
        pieces = []
        for h in range(GLA_HEADS):
            sl = slice(h * GLA_HEAD_V, (h + 1) * GLA_HEAD_V)
            oh = ov[:, sl]
            rr = lax.rsqrt(jnp.mean(oh * oh, axis=-1, keepdims=True) + NORM_EPS)
            rh = rv[:, sl]
            pieces.append(((oh * rr) * gv[:, sl]) * (rh * _sigmoid(rh)))
        return (pieces,)

    return _rowwise(name, body, [o, (p_gla, r_block, w)], [gn], [(w, BF16)])[0]


def _gla_post_bwd(dout, o, p_gla, r_block, gn, name):
    w = o.shape[1]

    def body(dv, ov, rv, gv):
        dos, drs, dgs = [], [], []
        for h in range(GLA_HEADS):
            sl = slice(h * GLA_HEAD_V, (h + 1) * GLA_HEAD_V)
            oh, rh, gh, dh = ov[:, sl], rv[:, sl], gv[:, sl], dv[:, sl]
            rr = lax.rsqrt(jnp.mean(oh * oh, axis=-1, keepdims=True) + NORM_EPS)
            nhat = oh * rr
            s = _sigmoid(rh)
            dn = dh * (rh * s)
            drs.append(dh * (nhat * gh) * (s * (1.0 + rh * (1.0 - s))))
            dnn = dn * gh
            dos.append(rr * (dnn - nhat * jnp.mean(dnn * nhat, axis=-1, keepdims=True)))
            dgs.append(dn * nhat)
        return (dos, drs, jnp.concatenate(dgs, axis=1))

    return _rowwise(name, body, [dout, o, (p_gla, r_block, w)], [gn], [(w, F32), (w, BF16)], sums=[w])


def _conv_bwd1(dout, y, ln_g, ln_b, name):
    w = y.shape[1]

    def body(dv, yv, gv, bv):
        mu = jnp.mean(yv, axis=-1, keepdims=True)
        yc = yv - mu
        rstd = lax.rsqrt(jnp.mean(yc * yc, axis=-1, keepdims=True) + NORM_EPS)
        xhat = yc * rstd
        yn = xhat * gv + bv
        s = _sigmoid(yn)
        dyn = dv * (s * (1.0 + yn * (1.0 - s)))
        dxh = dyn * gv
        dy = rstd * (dxh - jnp.mean(dxh, axis=-1, keepdims=True)
                     - xhat * jnp.mean(dxh * xhat, axis=-1, keepdims=True))
        return (dy, dyn * xhat, dyn, dy)

    return _rowwise(name, body, [dout, y], [ln_g, ln_b], [(w, F32)], sums=[w, w, w])


def _loss_fwd(y, target, name):
    d = y.shape[1]

    def body(yv, tv):
        e = yv - tv
        return (e / d, e * e)

    return _rowwise(name, body, [y, target], [], [(d, F32)], sums=[d])


def _adamw(w, g, m, v, name):
    cols = w.shape[1]
    c1 = 1.0 - ADAM_B1 ** ADAM_STEP
    c2 = 1.0 - ADAM_B2 ** ADAM_STEP

    def body(wv, gv, mv, vv):
        m2 = ADAM_B1 * mv + (1.0 - ADAM_B1) * gv
        v2 = ADAM_B2 * vv + (1.0 - ADAM_B2) * (gv * gv)
        m_hat = m2 / c1
        v_hat = v2 / c2
        delta = -ADAM_LR * (m_hat / (jnp.sqrt(v_hat) + ADAM_EPS) + ADAM_WD * wv)
        return (delta, m2, v2)

    return _rowwise(name, body, [w, g, m, v], [], [(cols, F32)] * 3)


def _split_bf16(v):
    hi = v.astype(BF16)
    lo = (v - hi.astype(F32)).astype(BF16)
    return hi, lo


def _tri(n, strict):
    r = lax.broadcasted_iota(jnp.int32, (n, n), 0)
    c = lax.broadcasted_iota(jnp.int32, (n, n), 1)
    return jnp.where(r > c if strict else r >= c, 1.0, 0.0).astype(BF16)


def _sb_weights(qs, kt, c, mx, diag_offset):
    z = _dot(qs, kt)
    u = jnp.exp(-jnp.abs(z))
    sp = jnp.maximum(z, 0.0) + jnp.log(1.0 + u)
    if diag_offset is None:
        mask = None
        spm = sp
    else:
        rows = lax.broadcasted_iota(jnp.int32, z.shape, 0)
        cols = lax.broadcasted_iota(jnp.int32, z.shape, 1)
        mask = cols + diag_offset < rows
        spm = jnp.where(mask, sp, 0.0)
    suf = _dot(spm.astype(BF16), mx)
    w = jnp.exp(((z - sp) - suf) - c)
    if mask is not None:
        w = jnp.where(mask, w, 0.0)
    return z, u, spm, w, mask


def _sb_fwd(q, kt, vt, name):
    nh, t, dh = q.shape
    nkb, bk = kt.shape[1], kt.shape[3]
    bq = min(SB_BQ, t)
    scale = dh ** -0.5
    per = bq // bk

    def kern(q_ref, kt_ref, vt_ref, o_ref, c_ref):
        qi = pl.program_id(1)
        top = (qi + 1) * per - 1
        qs = (q_ref[...].astype(F32) * scale).astype(BF16)
        mx = _tri(bk, True)
        o_ref[...] = jnp.zeros_like(o_ref)
        c_ref[...] = jnp.zeros_like(c_ref)

        def tile(kb, diag_offset):
            _, _, spm, w, _ = _sb_weights(qs, kt_ref[kb], c_ref[...], mx, diag_offset)
            o_ref[...] += _dot(w.astype(BF16), vt_ref[kb], _NT)
            c_ref[...] += jnp.sum(spm, axis=1, keepdims=True)

        def diag_step(i, carry):
            tile(top - i, (per - 1 - i) * bk)
            return carry

        def step(i, carry):
            tile(top - i, None)
            return carry

        lax.fori_loop(0, per, diag_step, 0)
        lax.fori_loop(per, (qi + 1) * per, step, 0)

    return pl.pallas_call(
        kern, name=name,
        out_shape=jax.ShapeDtypeStruct((nh, t, dh), F32),
        grid=(nh, t // bq),
        in_specs=[pl.BlockSpec((None, bq, dh), lambda h, i: (h, i, 0)),
                  pl.BlockSpec((None, nkb, dh, bk), lambda h, i: (h, 0, 0, 0)),
                  pl.BlockSpec((None, nkb, dh, bk), lambda h, i: (h, 0, 0, 0))],
        out_specs=pl.BlockSpec((None, bq, dh), lambda h, i: (h, i, 0)),
        scratch_shapes=[pltpu.VMEM((bq, 1), F32)],
        compiler_params=_params(("parallel", "arbitrary")),
    )(q, kt, vt)


def _sb_bwd(q, qt, kt, vt, out, dout, doutt, name):
    nh, t, dh = q.shape
    nkb, bk = kt.shape[1], kt.shape[3]
    bq = min(SB_BQ, t)
    scale = dh ** -0.5
    per = bq // bk

    def kern(q_ref, qt_ref, kt_ref, vt_ref, o_ref, do_ref, dot_ref, dq_ref, dkt_ref, dvt_ref, c_ref, ce_ref):
        qi = pl.program_id(1)
        top = (qi + 1) * per - 1

        @pl.when(qi == 0)
        def _():
            dkt_ref[...] = jnp.zeros_like(dkt_ref)
            dvt_ref[...] = jnp.zeros_like(dvt_ref)

        qs = (q_ref[...].astype(F32) * scale).astype(BF16)
        qtv = qt_ref[...]
        dob = do_ref[...].astype(BF16)
        dotv = dot_ref[...]
        dsum = jnp.sum(dob.astype(F32) * o_ref[...], axis=1, keepdims=True)
        mx = _tri(bk, True)
        mi = _tri(bk, False)
        dq_ref[...] = jnp.zeros_like(dq_ref)
        c_ref[...] = jnp.zeros_like(c_ref)
        ce_ref[...] = jnp.zeros_like(ce_ref)

        def tile(kb, diag_offset):
            kt_blk = kt_ref[kb]
            z, u, spm, w, mask = _sb_weights(qs, kt_blk, c_ref[...], mx, diag_offset)
            wb = w.astype(BF16)
            e = _dot(dob, vt_ref[kb]) * wb.astype(F32)
            ehi, elo = _split_bf16(e)
            before = dsum - (_dot(ehi, mi) + _dot(elo, mi) + ce_ref[...])
            rcp = 1.0 / (1.0 + u)
            sig = jnp.where(z >= 0.0, rcp, u * rcp)
            dz = (e * (1.0 - sig) - sig * before) * scale
            if mask is not None:
                dz = jnp.where(mask, dz, 0.0)
            dz = dz.astype(BF16)
            dq_ref[...] += _dot(dz, kt_blk, _NT)
            dkt_ref[kb] += _dot(qtv, dz)
            dvt_ref[kb] += _dot(dotv, wb)
            c_ref[...] += jnp.sum(spm, axis=1, keepdims=True)
            ce_ref[...] += jnp.sum(e, axis=1, keepdims=True)

        def diag_step(i, carry):
            tile(top - i, (per - 1 - i) * bk)
            return carry

        def step(i, carry):
            tile(top - i, None)
            return carry

        lax.fori_loop(0, per, diag_step, 0)
        lax.fori_loop(per, (qi + 1) * per, step, 0)

    row = pl.BlockSpec((None, bq, dh), lambda h, i: (h, i, 0))
    col = pl.BlockSpec((None, dh, bq), lambda h, i: (h, 0, i))
    whole = pl.BlockSpec((None, nkb, dh, bk), lambda h, i: (h, 0, 0, 0))
    return pl.pallas_call(
        kern, name=name,
        out_shape=[jax.ShapeDtypeStruct((nh, t, dh), F32),
                   jax.ShapeDtypeStruct((nh, nkb, dh, bk), F32),
                   jax.ShapeDtypeStruct((nh, nkb, dh, bk), F32)],
        grid=(nh, t // bq),
        in_specs=[row, col, whole, whole, row, row, col],
        out_specs=[row, whole, whole],
        scratch_shapes=[pltpu.VMEM((bq, 1), F32), pltpu.VMEM((bq, 1), F32)],
        compiler_params=_params(("parallel", "arbitrary")),
    )(q, qt, kt, vt, out, dout, doutt)


def _conv_fwd(p_conv, conv_w, conv_b, ln_g, ln_b, name):
    t, c2 = p_conv.shape
    c = c2 // 2
    tm = min(256, t)
    hb = tm // CONV_HALO

    def kern(a_ref, g_ref, ah_ref, gh_ref, w_ref, b_ref, lg_ref, lb_ref, o_ref, u_ref, y_ref, ubuf):
        i = pl.program_id(0)
        u = a_ref[...] * _sigmoid(g_ref[...])
        uh = ah_ref[...] * _sigmoid(gh_ref[...])
        ubuf[0:CONV_HALO, :] = jnp.where(i > 0, uh, 0.0)
        ubuf[CONV_HALO:CONV_HALO + tm, :] = u
        y = jnp.zeros((tm, c), F32) + b_ref[...]
        for j in range(CONV_WIDTH):
            off = CONV_HALO - (CONV_WIDTH - 1) + j
            y = y + ubuf[off:off + tm, :] * w_ref[j:j + 1, :]
        mu = jnp.mean(y, axis=-1, keepdims=True)
        yc = y - mu
        rstd = lax.rsqrt(jnp.mean(yc * yc, axis=-1, keepdims=True) + NORM_EPS)
        yn = (yc * rstd) * lg_ref[...] + lb_ref[...]
        o_ref[...] = (yn * _sigmoid(yn)).astype(o_ref.dtype)
        u_ref[...] = u
        y_ref[...] = y

    def halo(cb):
        return pl.BlockSpec((CONV_HALO, c), lambda i: (jnp.maximum(i * hb - 1, 0), cb))

    vec = pl.BlockSpec((1, c), lambda i: (0, 0))
    tile = pl.BlockSpec((tm, c), lambda i: (i, 0))
    return pl.pallas_call(
        kern, name=name,
        out_shape=[jax.ShapeDtypeStruct((t, c), BF16), jax.ShapeDtypeStruct((t, c), F32),
                   jax.ShapeDtypeStruct((t, c), F32)],
        grid=(t // tm,),
        in_specs=[tile, pl.BlockSpec((tm, c), lambda i: (i, 1)), halo(0), halo(1),
                  pl.BlockSpec(conv_w.shape, lambda i: (0, 0)), vec, vec, vec],
        out_specs=[tile, tile, tile],
        scratch_shapes=[pltpu.VMEM((tm + CONV_HALO, c), F32)],
        compiler_params=_params(("arbitrary",)),
    )(p_conv, p_conv, p_conv, p_conv, conv_w, conv_b, ln_g, ln_b)


def _conv_bwd2(dy, u, p_conv, conv_w, name):
    t, c = dy.shape
    tm = min(256, t)
    hb = tm // CONV_HALO
    nt = t // tm
    last_halo = t // CONV_HALO - 1

    def kern(dy_ref, dyh_ref, u_ref, uh_ref, a_ref, g_ref, w_ref, dp_ref, dw_ref, dybuf, ubuf):
        i = pl.program_id(0)

        @pl.when(i == 0)
        def _():
            dw_ref[...] = jnp.zeros_like(dw_ref)

        dyv = dy_ref[...]
        dybuf[0:tm, :] = dyv
        dybuf[tm:tm + CONV_HALO, :] = jnp.where(i < nt - 1, dyh_ref[...], 0.0)
        ubuf[0:CONV_HALO, :] = jnp.where(i > 0, uh_ref[...], 0.0)
        ubuf[CONV_HALO:CONV_HALO + tm, :] = u_ref[...]
        du = jnp.zeros((tm, c), F32)
        for j in range(CONV_WIDTH):
            off = CONV_WIDTH - 1 - j
            du = du + dybuf[off:off + tm, :] * w_ref[j:j + 1, :]
            uoff = CONV_HALO - (CONV_WIDTH - 1) + j
            dw_ref[j:j + 1, :] += jnp.sum(dyv * ubuf[uoff:uoff + tm, :], axis=0, keepdims=True)
        a = a_ref[...]
        s = _sigmoid(g_ref[...])
        dp_ref[:, 0:c] = (du * s).astype(dp_ref.dtype)
        dp_ref[:, c:2 * c] = (du * a * (s * (1.0 - s))).astype(dp_ref.dtype)

    tile = pl.BlockSpec((tm, c), lambda i: (i, 0))
    return pl.pallas_call(
        kern, name=name,
        out_shape=[jax.ShapeDtypeStruct((t, 2 * c), BF16), jax.ShapeDtypeStruct((CONV_HALO, c), F32)],
        grid=(nt,),
        in_specs=[tile,
                  pl.BlockSpec((CONV_HALO, c), lambda i: (jnp.minimum((i + 1) * hb, last_halo), 0)),
                  tile,
                  pl.BlockSpec((CONV_HALO, c), lambda i: (jnp.maximum(i * hb - 1, 0), 0)),
                  tile, pl.BlockSpec((tm, c), lambda i: (i, 1)),
                  pl.BlockSpec(conv_w.shape, lambda i: (0, 0))],
        out_specs=[pl.BlockSpec((tm, 2 * c), lambda i: (i, 0)),
                   pl.BlockSpec((CONV_HALO, c), lambda i: (0, 0))],
        scratch_shapes=[pltpu.VMEM((tm + CONV_HALO, c), F32), pltpu.VMEM((tm + CONV_HALO, c), F32)],
        compiler_params=_params(("arbitrary",)),
    )(dy, dy, u, u, p_conv, p_conv, conv_w)


GLA_CHUNKS_PER_STEP = 16


def _gla_fwd(q_ck, k_kc, la_kc, v, name):
    nh, nc, ch, dk = q_ck.shape
    dv = v.shape[3]
    qscale = dk ** -0.5
    cb = min(GLA_CHUNKS_PER_STEP, nc)

    def kern(q_ref, k_ref, la_ref, v_ref, o_ref, st_ref, state_ref):
        mx = _tri(ch, True)

        @pl.when(pl.program_id(1) == 0)
        def _():
            state_ref[...] = jnp.zeros_like(state_ref)

        def step(n, state):
            la = la_ref[n]
            hi, lo = _split_bf16(la)
            de = _dot(hi, mx) + _dot(lo, mx)
            lam = jnp.exp(jnp.sum(la, axis=1, keepdims=True))
            kd = (k_ref[n] * jnp.exp(de)).astype(BF16)
            state = lam * state + _dot(kd, v_ref[n])
            st_ref[n] = state
            qs = (q_ref[n].astype(F32) * qscale).astype(BF16)
            o_ref[n] = _dot(qs, state.astype(BF16))
            return state

        state_ref[...] = lax.fori_loop(0, cb, step, state_ref[...])

    def spec(a, b):
        return pl.BlockSpec((None, cb, a, b), lambda h, j: (h, j, 0, 0))

    return pl.pallas_call(
        kern, name=name,
        out_shape=[jax.ShapeDtypeStruct((nh, nc, ch, dv), F32), jax.ShapeDtypeStruct((nh, nc, dk, dv), F32)],
        grid=(nh, nc // cb),
        in_specs=[spec(ch, dk), spec(dk, ch), spec(dk, ch), spec(ch, dv)],
        out_specs=[spec(ch, dv), spec(dk, dv)],
        scratch_shapes=[pltpu.VMEM((dk, dv), F32)],
        compiler_params=_params(("parallel", "arbitrary")),
    )(q_ck, k_kc, la_kc, v)


def _gla_bwd(q_ck, q_kc, k_ck, k_kc, la_ck, la_kc, v, states, states_prev, do, name):
    nh, nc, ch, dk = q_ck.shape
    dv = v.shape[3]
    qscale = dk ** -0.5
    cb = min(GLA_CHUNKS_PER_STEP, nc)
    nb = nc // cb

    def kern(q_ref, qt_ref, kck_ref, kkc_ref, lack_ref, lakc_ref, v_ref, st_ref, stp_ref, do_ref,
             dq_ref, dk_ref, dv_ref, dla_ref, g_ref):
        mx = _tri(ch, True)
        mxt = jnp.where(lax.broadcasted_iota(jnp.int32, (ch, ch), 0)
                        < lax.broadcasted_iota(jnp.int32, (ch, ch), 1), 1.0, 0.0).astype(BF16)

        @pl.when(pl.program_id(1) == 0)
        def _():
            g_ref[...] = jnp.zeros_like(g_ref)

        def step(i, g):
            n = cb - 1 - i
            la_kc = lakc_ref[n]
            hi, lo = _split_bf16(la_kc)
            de_kc = _dot(hi, mx) + _dot(lo, mx)
            lam = jnp.exp(jnp.sum(la_kc, axis=1, keepdims=True))
            hi2, lo2 = _split_bf16(lack_ref[n])
            de_ck = _dot(mxt, hi2) + _dot(mxt, lo2)
            edk = jnp.exp(de_kc)
            kd_kc = kkc_ref[n] * edk
            kd_ck = (kck_ref[n] * jnp.exp(de_ck)).astype(BF16)
            dob = do_ref[n].astype(BF16)
            dq_ref[n] = _dot(dob, st_ref[n].astype(BF16), _NT) * qscale
            qts = (qt_ref[n].astype(F32) * qscale).astype(BF16)
            ds = _dot(qts, dob) + g
            dsb = ds.astype(BF16)
            dlam = jnp.sum(ds * stp_ref[n], axis=1, keepdims=True)
            dkd = _dot(dsb, v_ref[n], _NT)
            dv_ref[n] = _dot(kd_ck, dsb)
            dk_ref[n] = dkd * edk
            dde = dkd * kd_kc
            h3, l3 = _split_bf16(dde)
            dla_ref[n] = _dot(h3, mxt) + _dot(l3, mxt) + dlam * lam
            return lam * ds

        g_ref[...] = lax.fori_loop(0, cb, step, g_ref[...])

    def spec(a, b):
        return pl.BlockSpec((None, cb, a, b), lambda h, j: (h, nb - 1 - j, 0, 0))

    return pl.pallas_call(
        kern, name=name,
        out_shape=[jax.ShapeDtypeStruct((nh, nc, ch, dk), F32), jax.ShapeDtypeStruct((nh, nc, dk, ch), F32),
                   jax.ShapeDtypeStruct((nh, nc, ch, dv), F32), jax.ShapeDtypeStruct((nh, nc, dk, ch), F32)],
        grid=(nh, nb),
        in_specs=[spec(ch, dk), spec(dk, ch), spec(ch, dk), spec(dk, ch), spec(ch, dk), spec(dk, ch),
                  spec(ch, dv), spec(dk, dv), spec(dk, dv), spec(ch, dv)],
        out_specs=[spec(ch, dk), spec(dk, ch), spec(ch, dv), spec(dk, ch)],
        scratch_shapes=[pltpu.VMEM((dk, dv), F32)],
        compiler_params=_params(("parallel", "arbitrary")),
    )(q_ck, q_kc, k_ck, k_kc, la_ck, la_kc, v, states, states_prev, do)


def _heads_rows(a, nh, blk):
    t = a.shape[0]
    d = a.shape[1] // nh
    return a.reshape(t // blk, blk, nh, d).transpose(2, 0, 1, 3)


def _heads_cols(a, nh, blk):
    t = a.shape[0]
    d = a.shape[1] // nh
    return a.reshape(t // blk, blk, nh, d).transpose(2, 0, 3, 1)


def _unheads_rows(a):
    nh, nb, blk, d = a.shape
    return a.transpose(1, 2, 0, 3).reshape(nb * blk, nh * d)


def _unheads_cols(a):
    nh, nb, d, blk = a.shape
    return a.transpose(1, 3, 0, 2).reshape(nb * blk, nh * d)


def _ffn_fwd(x, w, pre, post, tag):
    h = _rms_fwd(x, pre, f"{tag}_rms")
    gate = _mm(h, w["gate"], out_dtype=BF16, name=f"{tag}_gate")
    up = _mm(h, w["up"], out_dtype=BF16, name=f"{tag}_up")
    act = _swiglu_fwd(gate, up, f"{tag}_act")
    o = _mm(act, w["down"], name=f"{tag}_down")
    x_out = _post_res(o, x, post, 0.5, f"{tag}_res")
    return x_out, (x, h, gate, up, act, o)


def _ffn_bwd(dx_out, saved, w, pre, post, tag):
    x, h, gate, up, act, o = saved
    do, dpost = _post_bwd(dx_out, o, post, 0.5, f"{tag}_bres")
    da = _mm(do, w["down"], nt=True, name=f"{tag}_bda")
    d_down = _mm(act, do, tn=True, name=f"{tag}_bwdown")
    dgate, dup = _swiglu_bwd(da, gate, up, f"{tag}_bact")
    dh_g = _mm(dgate, w["gate"], nt=True, name=f"{tag}_bdhg")
    dh_u = _mm(dup, w["up"], nt=True, name=f"{tag}_bdhu")
    d_gate = _mm(h, dgate, tn=True, name=f"{tag}_bwgate")
    d_up = _mm(h, dup, tn=True, name=f"{tag}_bwup")
    dx, dpre = _rms_bwd([dh_g, dh_u], x, pre, dx_out, f"{tag}_brms")
    return dx, dict(gate=d_gate, up=d_up, down=d_down, pre=dpre, post=dpost)


def _mixer_fwd(x, w, tag):
    t, d = x.shape
    bw = BRANCH_WIDTH
    h = _rms_fwd(x, w["pre"], f"{tag}_rms")
    p_sb = _mm(h, w["in_sb"], out_dtype=BF16, name=f"{tag}_insb")
    p_conv = _mm(h, w["in_conv"], name=f"{tag}_inconv")
    p_gla = _mm(h, w["in_gla"], name=f"{tag}_ingla")
    p_gate = _mm(h, w["in_gate"], name=f"{tag}_ingate")

    bk = min(SB_BK, t)
    q = p_sb[:, 0:bw].reshape(t, SB_HEADS, SB_HEAD_DIM).transpose(1, 0, 2)
    kt = _heads_cols(p_sb[:, bw:2 * bw], SB_HEADS, bk)
    vt = _heads_cols(p_sb[:, 2 * bw:3 * bw], SB_HEADS, bk)
    sb_o = _sb_fwd(q, kt, vt, f"{tag}_sb")
    sb_out = sb_o.transpose(1, 0, 2).reshape(t, bw)

    conv_out, conv_u, conv_y = _conv_fwd(p_conv, w["conv_w"], w["conv_b"], w["ln_g"], w["ln_b"], f"{tag}_conv")

    kd, vd = GLA_KEY_DIM, GLA_VALUE_DIM
    lr = p_gla[:, 2 * kd + 2 * vd:]
    pre_a = _mm(lr, w["alpha"], name=f"{tag}_alpha")
    la = _la_fwd(pre_a, w["b_alpha"], f"{tag}_la")
    gq = _heads_rows(p_gla[:, 0:kd].astype(BF16), GLA_HEADS, CHUNK)
    gk_kc = _heads_cols(p_gla[:, kd:2 * kd], GLA_HEADS, CHUNK)
    gv = _heads_rows(p_gla[:, 2 * kd:2 * kd + vd].astype(BF16), GLA_HEADS, CHUNK)
    la_kc = _heads_cols(la, GLA_HEADS, CHUNK)
    gla_o4, states = _gla_fwd(gq, gk_kc, la_kc, gv, f"{tag}_gla")
    gla_o = _unheads_rows(gla_o4)
    r_block = (2 * kd + vd) // vd
    gla_out = _gla_post_fwd(gla_o, p_gla, r_block, w["gnorm"], f"{tag}_glapost")

    branches = (sb_out.astype(BF16), conv_out, gla_out)
    bds = [_mm(branches[j], w["branch"][j], name=f"{tag}_br{j}") for j in range(N_BRANCHES)]
    merged = _merge_fwd(bds, p_gate, f"{tag}_merge")
    mo = _mm(merged, w["out"], name=f"{tag}_out")
    x_out = _post_res(mo, x, w["post"], 1.0, f"{tag}_res")
    saved = dict(x=x, h=h, p_sb=p_sb, p_conv=p_conv, p_gla=p_gla, p_gate=p_gate, sb_o=sb_o, conv_u=conv_u,
                 conv_y=conv_y, pre_a=pre_a, la=la, states=states, gla_o=gla_o, branches=branches, bds=bds,
                 merged=merged, mo=mo)
    return x_out, saved


def _mixer_bwd(dx_out, s, w, tag):
    x = s["x"]
    t, d = x.shape
    bw = BRANCH_WIDTH
    kd, vd = GLA_KEY_DIM, GLA_VALUE_DIM
    grads = {}
    dmo, grads["post"] = _post_bwd(dx_out, s["mo"], w["post"], 1.0, f"{tag}_bres")
    dmerged = _mm(dmo, w["out"], nt=True, name=f"{tag}_bdmerged")
    grads["out"] = _mm(s["merged"], dmo, tn=True, name=f"{tag}_bwout")
    dbd0, dbd1, dbd2, dp_gate = _merge_bwd(dmerged, s["bds"], s["p_gate"], f"{tag}_bmerge")
    dbds = (dbd0, dbd1, dbd2)
    dbranch = [_mm(dbds[j], w["branch"][j], nt=True, name=f"{tag}_bdbr{j}") for j in range(N_BRANCHES)]
    grads["branch"] = jnp.stack([_mm(s["branches"][j], dbds[j], tn=True, name=f"{tag}_bwbr{j}")
                                 for j in range(N_BRANCHES)])

    p_sb = s["p_sb"]
    bk = min(SB_BK, t)
    qf, kf, vf = p_sb[:, 0:bw], p_sb[:, bw:2 * bw], p_sb[:, 2 * bw:3 * bw]
    q = qf.reshape(t, SB_HEADS, SB_HEAD_DIM).transpose(1, 0, 2)
    qt = qf.reshape(t, SB_HEADS, SB_HEAD_DIM).transpose(1, 2, 0)
    kt = _heads_cols(kf, SB_HEADS, bk)
    vt = _heads_cols(vf, SB_HEADS, bk)
    dsb = dbranch[0].reshape(t, SB_HEADS, SB_HEAD_DIM)
    dout = dsb.transpose(1, 0, 2)
    doutt = dsb.transpose(1, 2, 0).astype(BF16)
    dq, dkt, dvt = _sb_bwd(q, qt, kt, vt, s["sb_o"], dout, doutt, f"{tag}_bsb")
    dp_sb = jnp.concatenate([dq.transpose(1, 0, 2).reshape(t, bw), _unheads_cols(dkt), _unheads_cols(dvt)],
                            axis=1).astype(BF16)

    dy, d_lng, d_lnb, d_cb = _conv_bwd1(dbranch[1], s["conv_y"], w["ln_g"], w["ln_b"], f"{tag}_bconv1")
    dp_conv, d_cw = _conv_bwd2(dy, s["conv_u"], s["p_conv"], w["conv_w"], f"{tag}_bconv2")
    grads.update(conv_w=d_cw[:CONV_WIDTH], conv_b=d_cb, ln_g=d_lng, ln_b=d_lnb)

    p_gla = s["p_gla"]
    r_block = (2 * kd + vd) // vd
    do, dr, grads["gnorm"] = _gla_post_bwd(dbranch[2], s["gla_o"], p_gla, r_block, w["gnorm"], f"{tag}_bglapost")
    gqf, gkf = p_gla[:, 0:kd], p_gla[:, kd:2 * kd]
    gq_b = gqf.astype(BF16)
    dq4, dk4, dv4, dla4 = _gla_bwd(
        _heads_rows(gq_b, GLA_HEADS, CHUNK), _heads_cols(gq_b, GLA_HEADS, CHUNK),
        _heads_rows(gkf, GLA_HEADS, CHUNK), _heads_cols(gkf, GLA_HEADS, CHUNK),
        _heads_rows(s["la"], GLA_HEADS, CHUNK), _heads_cols(s["la"], GLA_HEADS, CHUNK),
        _heads_rows(p_gla[:, 2 * kd:2 * kd + vd].astype(BF16), GLA_HEADS, CHUNK),
        s["states"], jnp.pad(s["states"][:, :-1], ((0, 0), (1, 0), (0, 0), (0, 0))),
        _heads_rows(do, GLA_HEADS, CHUNK), f"{tag}_bgla")
    dla = _unheads_cols(dla4)
    dpre_a, grads["b_alpha"] = _la_bwd(dla, s["pre_a"], w["b_alpha"], f"{tag}_bla")
    lr = p_gla[:, 2 * kd + 2 * vd:]
    dlr = _mm(dpre_a, w["alpha"], nt=True, out_dtype=BF16, name=f"{tag}_bdlr")
    grads["alpha"] = _mm(lr, dpre_a, tn=True, name=f"{tag}_bwalpha")
    dp_gla = jnp.concatenate([_unheads_rows(dq4).astype(BF16), _unheads_cols(dk4).astype(BF16),
                              _unheads_rows(dv4).astype(BF16), dr, dlr], axis=1)

    dps = dict(in_sb=dp_sb, in_conv=dp_conv, in_gla=dp_gla, in_gate=dp_gate)
    dh_parts = []
    for key, dp in dps.items():
        dh_parts.append(_mm(dp, w[key], nt=True, name=f"{tag}_bdh_{key}"))
        grads[key] = _mm(s["h"], dp, tn=True, name=f"{tag}_bw_{key}")
    dx, grads["pre"] = _rms_bwd(dh_parts, x, w["pre"], dx_out, f"{tag}_brms")
    return dx, grads


def _local_step(x, target, layers):
    saved = []
    for l, w in enumerate(layers):
        x, s1 = _ffn_fwd(x, w["ffn1"], w["pre0"], w["post0"], f"l{l}_f1")
        x, s2 = _mixer_fwd(x, w["mix"], f"l{l}_mx")
        x, s3 = _ffn_fwd(x, w["ffn2"], w["pre2"], w["post2"], f"l{l}_f2")
        saved.append((s1, s2, s3))
    dx, sq = _loss_fwd(x, target, "loss")
    grads = [None] * len(layers)
    for l in reversed(range(len(layers))):
        w = layers[l]
        s1, s2, s3 = saved[l]
        dx, g3 = _ffn_bwd(dx, s3, w["ffn2"], w["pre2"], w["post2"], f"l{l}_f2")
        dx, g2 = _mixer_bwd(dx, s2, w["mix"], f"l{l}_mx")
        dx, g1 = _ffn_bwd(dx, s1, w["ffn1"], w["pre0"], w["post0"], f"l{l}_f1")
        grads[l] = dict(ffn1=g1, mix=g2, ffn2=g3)
    return sq, dx, grads


_ANY = pl.BlockSpec(memory_space=pl.ANY)
_MESH = pl.DeviceIdType.MESH
N_CHIPS = 4


def _all_gather(p, name):
    def body(p_ref, out_ref, send_sems, recv_sems, local_sem):
        x, y, c = lax.axis_index("x"), lax.axis_index("y"), lax.axis_index("c")
        me, sibling = (x, y, c), (x, y, 1 - c)
        chips = [(1 - x, y), (x, 1 - y), (1 - x, 1 - y)]

        def rows(px, py, pc):
            return out_ref.at[4 * px + 2 * py + pc]

        def copy(k, block, to, src=None):
            return pltpu.make_async_remote_copy(
                src_ref=rows(*block) if src is None else src, dst_ref=rows(*block),
                send_sem=send_sems.at[k], recv_sem=recv_sems.at[k], device_id=to, device_id_type=_MESH)

        mine = pltpu.make_async_copy(p_ref, rows(*me), local_sem)
        mine.start()
        first = [copy(0, me, sibling, src=p_ref)]
        first += [copy(1 + j, me, (*chip, c), src=p_ref) for j, chip in enumerate(chips)]
        for cp in first:
            cp.start()
        passed = [copy(4 + j, (*chip, c), sibling) for j, chip in enumerate(chips)]
        for j, chip in enumerate(chips):
            copy(1 + j, (*chip, c), me).wait_recv()
            passed[j].start()
        copy(0, sibling, me).wait_recv()
        for j, chip in enumerate(chips):
            copy(4 + j, (*chip, 1 - c), me).wait_recv()
        for cp in first + passed:
            cp.wait_send()
        mine.wait()

    return pl.pallas_call(
        body, name=name,
        out_shape=jax.ShapeDtypeStruct((N_DEV,) + p.shape, p.dtype),
        in_specs=[_ANY], out_specs=_ANY,
        scratch_shapes=[pltpu.SemaphoreType.DMA((7,)), pltpu.SemaphoreType.DMA((7,)), pltpu.SemaphoreType.DMA],
    )(p)


def _exchange_sibling(src, name):
    def body(src_ref, out_ref, send_sems, recv_sems):
        x, y, c = lax.axis_index("x"), lax.axis_index("y"), lax.axis_index("c")
        copies = [pltpu.make_async_remote_copy(
            src_ref=src_ref.at[j, 1 - c], dst_ref=out_ref.at[j], send_sem=send_sems.at[j],
            recv_sem=recv_sems.at[j], device_id=(x, y, 1 - c), device_id_type=_MESH) for j in range(N_CHIPS)]
        for cp in copies:
            cp.start()
        for cp in copies:
            cp.wait()

    return pl.pallas_call(
        body, name=name,
        out_shape=jax.ShapeDtypeStruct((N_CHIPS,) + src.shape[2:], src.dtype),
        in_specs=[_ANY], out_specs=_ANY,
        scratch_shapes=[pltpu.SemaphoreType.DMA((N_CHIPS,)), pltpu.SemaphoreType.DMA((N_CHIPS,))],
    )(src)


def _add_own(src, got, name):
    _, _, r, cols = src.shape
    tr = _div_tile(r, 512, 16)
    c = lax.axis_index("c").astype(jnp.int32).reshape(1)

    def kern(c_ref, a_ref, b_ref, o_ref):
        o_ref[...] = (a_ref[...].astype(F32) + b_ref[...].astype(F32)).astype(o_ref.dtype)

    grid_spec = pltpu.PrefetchScalarGridSpec(
        num_scalar_prefetch=1, grid=(N_CHIPS, r // tr),
        in_specs=[pl.BlockSpec((None, None, tr, cols), lambda j, i, c_ref: (j, c_ref[0], i, 0)),
                  pl.BlockSpec((None, tr, cols), lambda j, i, c_ref: (j, i, 0))],
        out_specs=pl.BlockSpec((None, tr, cols), lambda j, i, c_ref: (j, i, 0)))
    return pl.pallas_call(
        kern, name=name, grid_spec=grid_spec,
        out_shape=jax.ShapeDtypeStruct((N_CHIPS, r, cols), src.dtype),
        compiler_params=_params(("arbitrary", "arbitrary")),
    )(c, src, got)


def _exchange_chips(part, name):
    def body(part_ref, out_ref, send_sems, recv_sems, local_sem):
        x, y, c = lax.axis_index("x"), lax.axis_index("y"), lax.axis_index("c")
        my_chip = 2 * x + y
        chips = [(1 - x, y), (x, 1 - y), (1 - x, 1 - y)]
        mine = pltpu.make_async_copy(part_ref.at[my_chip], out_ref.at[my_chip], local_sem)
        mine.start()
        copies = []
        for k, (px, py) in enumerate(chips):
            copies.append(pltpu.make_async_remote_copy(
                src_ref=part_ref.at[2 * px + py], dst_ref=out_ref.at[my_chip],
                send_sem=send_sems.at[k], recv_sem=recv_sems.at[k], device_id=(px, py, c), device_id_type=_MESH))
        for cp in copies:
            cp.start()
        for k, (px, py) in enumerate(chips):
            pltpu.make_async_remote_copy(
                src_ref=part_ref.at[my_chip], dst_ref=out_ref.at[2 * px + py],
                send_sem=send_sems.at[k], recv_sem=recv_sems.at[k], device_id=(px, py, c),
                device_id_type=_MESH).wait_recv()
        for cp in copies:
            cp.wait_send()
        mine.wait()

    return pl.pallas_call(
        body, name=name,
        out_shape=jax.ShapeDtypeStruct(part.shape, part.dtype),
        in_specs=[_ANY], out_specs=_ANY,
        scratch_shapes=[pltpu.SemaphoreType.DMA((3,)), pltpu.SemaphoreType.DMA((3,)), pltpu.SemaphoreType.DMA],
    )(part)


def _sum_chips(parts, name):
    _, r, cols = parts.shape
    tr = _div_tile(r, 512, 16)

    def kern(p_ref, o_ref):
        p = [p_ref[j].astype(F32) for j in range(N_CHIPS)]
        o_ref[...] = ((p[0] + p[1]) + p[2]) + p[3]

    return pl.pallas_call(
        kern, name=name, grid=(r // tr,),
        in_specs=[pl.BlockSpec((4, tr, cols), lambda i: (0, i, 0))],
        out_specs=pl.BlockSpec((tr, cols), lambda i: (i, 0)),
        out_shape=jax.ShapeDtypeStruct((r, cols), F32),
        compiler_params=_params(("arbitrary",)),
    )(parts)


_SHARDED = {
    "norm_pre": (1, True), "norm_post": (1, True),
    "ffn1_w_gate": (1, False), "ffn1_w_up": (1, False), "ffn1_w_down": (0, False),
    "ffn2_w_gate": (1, False), "ffn2_w_up": (1, False), "ffn2_w_down": (0, False),
    "w_in": (1, False), "conv_w": (1, True), "gla_w_alpha": (1, False),
    "w_branch": (2, False), "w_out": (0, False),
}
_REPLICATED = ("conv_b", "conv_ln_g", "conv_ln_b", "gla_b_alpha", "gla_norm_g")
_WEIGHTS = ("norm_pre", "norm_post", "ffn1_w_gate", "ffn1_w_up", "ffn1_w_down", "ffn2_w_gate", "ffn2_w_up",
            "ffn2_w_down", "w_in", "conv_w", "conv_b", "conv_ln_g", "conv_ln_b", "gla_w_alpha", "gla_b_alpha",
            "gla_norm_g", "w_branch", "w_out")


PACK_ROW_UNIT = 16
BIG_PIECE = PACK_ROW_UNIT * PACK_COLS


def _rows_of(a, nlead):
    lead = a.shape[:nlead]
    n = math.prod(a.shape[nlead:])
    if n % PACK_COLS:
        flat = a.reshape(lead + (n,))
        flat = jnp.pad(flat, [(0, 0)] * nlead + [(0, -n % PACK_COLS)])
        n += -n % PACK_COLS
        a = flat
    rows = a.reshape(lead + (n // PACK_COLS, PACK_COLS))
    return jnp.pad(rows, [(0, 0)] * nlead + [(0, -rows.shape[nlead] % PACK_ROW_UNIT), (0, 0)])


def _padded_rows(n):
    rows = -(-n // PACK_COLS)
    return -(-rows // PACK_ROW_UNIT) * PACK_ROW_UNIT


def _is_big(shape, exact):
    return not exact and math.prod(shape) >= BIG_PIECE


def _pack_weights(shards, l):
    big, small = [], []
    for name, (_, exact) in _SHARDED.items():
        a = shards[name][l]
        if _is_big(a.shape, exact):
            big.append(_rows_of(a.astype(BF16), 0))
        elif exact:
            small.append(lax.bitcast_convert_type(a, BF16).reshape(-1))
        else:
            small.append(a.astype(BF16).reshape(-1))
    return jnp.concatenate(big + [_rows_of(jnp.concatenate(small), 0)], axis=0)


def _unpack_weights(gathered, shards, l):
    full, r0 = {}, 0

    def merge(seg, axis, shp):
        seg = jnp.moveaxis(seg, 0, axis)
        return seg.reshape(shp[:axis] + (N_DEV * shp[axis],) + shp[axis + 1:])

    for name, (axis, exact) in _SHARDED.items():
        shp = shards[name].shape[1:]
        if _is_big(shp, exact):
            n = math.prod(shp)
            nr = _padded_rows(n)
            seg = gathered[:, r0:r0 + nr].reshape(N_DEV, -1)[:, :n] if n % PACK_COLS else \
                gathered[:, r0:r0 + n // PACK_COLS]
            full[name] = merge(seg.reshape((N_DEV,) + shp), axis, shp)
            r0 += nr
    flat, off = gathered[:, r0:].reshape(N_DEV, -1), 0
    for name, (axis, exact) in _SHARDED.items():
        shp = shards[name].shape[1:]
        if not _is_big(shp, exact):
            n = math.prod(shp) * (2 if exact else 1)
            seg = flat[:, off:off + n]
            off += n
            if exact:
                seg = lax.bitcast_convert_type(seg.reshape((N_DEV,) + shp + (2,)), F32)
            full[name] = merge(seg.reshape((N_DEV,) + shp), axis, shp)
    return full


def _pack_grads(full_grads, repl_grads):
    big, small = [], []
    for name, (axis, _) in _SHARDED.items():
        g = full_grads[name]
        shp = g.shape
        g = g.reshape(shp[:axis] + (N_DEV, shp[axis] // N_DEV) + shp[axis + 1:])
        g = jnp.moveaxis(g, axis, 0).astype(BF16)
        if _is_big(g.shape[1:], False):
            big.append(_rows_of(g, 1))
        else:
            small.append(g.reshape(N_DEV, -1))
    for name in _REPLICATED:
        small.append(jnp.broadcast_to(repl_grads[name].reshape(1, -1).astype(BF16),
                                      (N_DEV, repl_grads[name].size)))
    rows = jnp.concatenate(big + [_rows_of(jnp.concatenate(small, axis=1), 1)], axis=1)
    return rows.reshape((N_CHIPS, 2) + rows.shape[1:])


def _unpack_grads(summed, shards):
    out, r0 = {}, 0
    for name in _SHARDED:
        shp = shards[name].shape[1:]
        if _is_big(shp, False):
            n = math.prod(shp)
            seg = summed[r0:r0 + _padded_rows(n)].reshape(-1)[:n] if n % PACK_COLS else \
                summed[r0:r0 + n // PACK_COLS]
            out[name] = seg.reshape(shp)
            r0 += _padded_rows(n)
    flat, off = summed[r0:].reshape(-1), 0
    for name in tuple(_SHARDED) + _REPLICATED:
        shp = shards[name].shape[1:]
        if name in _REPLICATED or not _is_big(shp, False):
            n = math.prod(shp)
            out[name] = flat[off:off + n].reshape(shp)
            off += n
    return out


def _adamw_natural(w, g, m, v, name):
    shp = w.shape
    view = lambda a: a.reshape(-1, shp[-1])
    outs = _adamw(view(w), view(g), view(m), view(v), name)
    return [o.reshape(shp) for o in outs]


def _layer_weights(full, repl, l, d_model):
    bw = BRANCH_WIDTH
    w_in = full["w_in"]
    o_conv, o_gq, o_lr = 3 * bw, 5 * bw, 5 * bw + 2 * GLA_KEY_DIM + 2 * GLA_VALUE_DIM
    o_gate = o_lr + GLA_GATE_RANK
    lr_pad = LANE - GLA_GATE_RANK
    in_gla = jnp.concatenate([w_in[:, o_gq:o_gate], jnp.zeros((d_model, lr_pad), w_in.dtype)], axis=1)
    alpha = jnp.concatenate([full["gla_w_alpha"], jnp.zeros((lr_pad, GLA_KEY_DIM), BF16)], axis=0)
    row = lambda a: a.reshape(1, -1)
    mix = dict(pre=row(full["norm_pre"][1]), post=row(full["norm_post"][1]),
               in_sb=w_in[:, 0:o_conv], in_conv=w_in[:, o_conv:o_gq], in_gla=in_gla, in_gate=w_in[:, o_gate:],
               conv_w=full["conv_w"], conv_b=row(repl["conv_b"][l]), ln_g=row(repl["conv_ln_g"][l]),
               ln_b=row(repl["conv_ln_b"][l]), alpha=alpha, b_alpha=row(repl["gla_b_alpha"][l]),
               gnorm=row(repl["gla_norm_g"][l]), branch=full["w_branch"], out=full["w_out"])
    return dict(
        pre0=row(full["norm_pre"][0]), post0=row(full["norm_post"][0]),
        pre2=row(full["norm_pre"][2]), post2=row(full["norm_post"][2]),
        ffn1=dict(gate=full["ffn1_w_gate"], up=full["ffn1_w_up"], down=full["ffn1_w_down"]),
        ffn2=dict(gate=full["ffn2_w_gate"], up=full["ffn2_w_up"], down=full["ffn2_w_down"]),
        mix=mix)


def _full_grads(g):
    mix = g["mix"]
    kd, vd = GLA_KEY_DIM, GLA_VALUE_DIM
    d_in = jnp.concatenate([mix["in_sb"], mix["in_conv"], mix["in_gla"][:, :2 * kd + 2 * vd + GLA_GATE_RANK],
                            mix["in_gate"]], axis=1)
    full = {
        "norm_pre": jnp.concatenate([g["ffn1"]["pre"], mix["pre"], g["ffn2"]["pre"]], axis=0),
        "norm_post": jnp.concatenate([g["ffn1"]["post"], mix["post"], g["ffn2"]["post"]], axis=0),
        "ffn1_w_gate": g["ffn1"]["gate"], "ffn1_w_up": g["ffn1"]["up"], "ffn1_w_down": g["ffn1"]["down"],
        "ffn2_w_gate": g["ffn2"]["gate"], "ffn2_w_up": g["ffn2"]["up"], "ffn2_w_down": g["ffn2"]["down"],
        "w_in": d_in, "conv_w": mix["conv_w"], "gla_w_alpha": mix["alpha"][:GLA_GATE_RANK],
        "w_branch": mix["branch"], "w_out": mix["out"],
    }
    repl = {"conv_b": mix["conv_b"], "conv_ln_g": mix["ln_g"], "conv_ln_b": mix["ln_b"],
            "gla_b_alpha": mix["b_alpha"], "gla_norm_g": mix["gnorm"]}
    return full, repl


def kernel(x, norm_pre, norm_post, ffn1_w_gate, ffn1_w_up, ffn1_w_down, ffn2_w_gate, ffn2_w_up, ffn2_w_down, w_in, conv_w, conv_b, conv_ln_g, conv_ln_b, gla_w_alpha, gla_b_alpha, gla_norm_g, w_branch, w_out, loss_target, m_norm_pre, m_norm_post, m_ffn1_w_gate, m_ffn1_w_up, m_ffn1_w_down, m_ffn2_w_gate, m_ffn2_w_up, m_ffn2_w_down, m_w_in, m_conv_w, m_conv_b, m_conv_ln_g, m_conv_ln_b, m_gla_w_alpha, m_gla_b_alpha, m_gla_norm_g, m_w_branch, m_w_out, v_norm_pre, v_norm_post, v_ffn1_w_gate, v_ffn1_w_up, v_ffn1_w_down, v_ffn2_w_gate, v_ffn2_w_up, v_ffn2_w_down, v_w_in, v_conv_w, v_conv_b, v_conv_ln_g, v_conv_ln_b, v_gla_w_alpha, v_gla_b_alpha, v_gla_norm_g, v_w_branch, v_w_out):
    weights = dict(norm_pre=norm_pre, norm_post=norm_post, ffn1_w_gate=ffn1_w_gate, ffn1_w_up=ffn1_w_up,
                   ffn1_w_down=ffn1_w_down, ffn2_w_gate=ffn2_w_gate, ffn2_w_up=ffn2_w_up, ffn2_w_down=ffn2_w_down,
                   w_in=w_in, conv_w=conv_w, conv_b=conv_b, conv_ln_g=conv_ln_g, conv_ln_b=conv_ln_b,
                   gla_w_alpha=gla_w_alpha, gla_b_alpha=gla_b_alpha, gla_norm_g=gla_norm_g, w_branch=w_branch,
                   w_out=w_out)
    moments_m = dict(norm_pre=m_norm_pre, norm_post=m_norm_post, ffn1_w_gate=m_ffn1_w_gate, ffn1_w_up=m_ffn1_w_up,
                     ffn1_w_down=m_ffn1_w_down, ffn2_w_gate=m_ffn2_w_gate, ffn2_w_up=m_ffn2_w_up,
                     ffn2_w_down=m_ffn2_w_down, w_in=m_w_in, conv_w=m_conv_w, conv_b=m_conv_b,
                     conv_ln_g=m_conv_ln_g, conv_ln_b=m_conv_ln_b, gla_w_alpha=m_gla_w_alpha,
                     gla_b_alpha=m_gla_b_alpha, gla_norm_g=m_gla_norm_g, w_branch=m_w_branch, w_out=m_w_out)
    moments_v = dict(norm_pre=v_norm_pre, norm_post=v_norm_post, ffn1_w_gate=v_ffn1_w_gate, ffn1_w_up=v_ffn1_w_up,
                     ffn1_w_down=v_ffn1_w_down, ffn2_w_gate=v_ffn2_w_gate, ffn2_w_up=v_ffn2_w_up,
                     ffn2_w_down=v_ffn2_w_down, w_in=v_w_in, conv_w=v_conv_w, conv_b=v_conv_b,
                     conv_ln_g=v_conv_ln_g, conv_ln_b=v_conv_ln_b, gla_w_alpha=v_gla_w_alpha,
                     gla_b_alpha=v_gla_b_alpha, gla_norm_g=v_gla_norm_g, w_branch=v_w_branch, w_out=v_w_out)
    n_layers = norm_pre.shape[0]
    t, d_model = x.shape[1], x.shape[2]

    layers = []
    for l in range(n_layers):
        gathered = _all_gather(_pack_weights(weights, l), f"gather_l{l}")
        full = _unpack_weights(gathered, weights, l)
        layers.append(_layer_weights(full, weights, l, d_model))

    sq, dx, grads = _local_step(x[0], loss_target[0], layers)
    loss = lax.psum(0.5 * jnp.sum(sq) / d_model, MESH_AXES)

    layer_grads = []
    for l in range(n_layers):
        packed = _pack_grads(*_full_grads(grads[l]))
        got = _exchange_sibling(packed, f"rs_sibling_l{l}")
        part = _add_own(packed, got, f"rs_add_l{l}")
        parts = _exchange_chips(part, f"rs_chips_l{l}")
        layer_grads.append(_unpack_grads(_sum_chips(parts, f"rs_sum_l{l}"), weights))

    g_out, d_out, m_out, v_out = [], [], [], []
    for name in _WEIGHTS:
        g = jnp.stack([lg[name] for lg in layer_grads])
        delta, new_m, new_v = _adamw_natural(weights[name], g, moments_m[name], moments_v[name], f"adamw_{name}")
        g_out.append(g)
        d_out.append(delta)
        m_out.append(new_m)
        v_out.append(new_v)
    return tuple([loss, dx[None]] + g_out + d_out + m_out + v_out)
```

```python
import functools
import math

import jax
import jax.numpy as jnp
from jax import lax
from jax.experimental import pallas as pl
from jax.experimental.pallas import tpu as pltpu

F32 = jnp.float32
BF16 = jnp.bfloat16

VMEM_LIMIT_BYTES = 48 * 1024 * 1024
LANE = 128

NORM_EPS = 1e-6
CHUNK = 64
N_BRANCHES = 3
BRANCH_WIDTH = 512
SB_HEADS = 8
SB_HEAD_DIM = 64
CONV_WIDTH = 31
CONV_HALO = 32
GLA_HEADS = 4
GLA_HEAD_K = 64
GLA_HEAD_V = 128
GLA_KEY_DIM = GLA_HEADS * GLA_HEAD_K
GLA_VALUE_DIM = GLA_HEADS * GLA_HEAD_V
GLA_GATE_RANK = 16
GLA_GATE_TAU = 16.0
SB_BQ = 1024
SB_BK = 128

ADAM_LR = 0.001
ADAM_B1 = 0.9
ADAM_B2 = 0.999
ADAM_EPS = 1e-08
ADAM_WD = 0.01
ADAM_STEP = 10

N_DEV = 8
MESH_AXES = ("x", "y", "c")
PACK_COLS = 1024


def _params(sem):
    return pltpu.CompilerParams(dimension_semantics=sem, vmem_limit_bytes=VMEM_LIMIT_BYTES)


def _div_tile(n, cap, unit):
    if n <= cap:
        return n
    best = None
    for t in range(unit, cap + 1, unit):
        if n % t == 0:
            best = t
    assert best is not None, (n, cap, unit)
    return best


_NN = (((1,), (0,)), ((), ()))
_NT = (((1,), (1,)), ((), ()))
_TN = (((0,), (0,)), ((), ()))


def _dot(a, b, dims=_NN):
    return lax.dot_general(a, b, dims, preferred_element_type=F32)


def _sigmoid(v):
    return 1.0 / (1.0 + jnp.exp(-v))


def _mm(a, b, *, nt=False, tn=False, out_dtype=F32, name):
    a = a.astype(BF16)
    b = b.astype(BF16)
    k, m = a.shape[::-1] if not tn else a.shape
    n = b.shape[0] if nt else b.shape[1]
    assert (b.shape[1] if nt else b.shape[0]) == k
    tm = _div_tile(m, 512, LANE if tn else 16)
    tn_ = _div_tile(n, 2048, LANE)
    tk = _div_tile(k, 2048, LANE)
    nk = k // tk
    dims = _NT if nt else (_TN if tn else _NN)

    def kern(a_ref, b_ref, o_ref, acc_ref):
        kk = pl.program_id(2)

        @pl.when(kk == 0)
        def _():
            acc_ref[...] = jnp.zeros_like(acc_ref)

        acc_ref[...] += _dot(a_ref[...], b_ref[...], dims)

        @pl.when(kk == nk - 1)
        def _():
            o_ref[...] = acc_ref[...].astype(o_ref.dtype)

    b_spec = (pl.BlockSpec((tn_, tk), lambda i, j, kk: (j, kk)) if nt
              else pl.BlockSpec((tk, tn_), lambda i, j, kk: (kk, j)))
    return pl.pallas_call(
        kern, name=name,
        out_shape=jax.ShapeDtypeStruct((m, n), out_dtype),
        grid=(m // tm, n // tn_, nk),
        in_specs=[pl.BlockSpec((tk, tm), lambda i, j, kk: (kk, i)) if tn
                  else pl.BlockSpec((tm, tk), lambda i, j, kk: (i, kk)), b_spec],
        out_specs=pl.BlockSpec((tm, tn_), lambda i, j, kk: (i, j)),
        scratch_shapes=[pltpu.VMEM((tm, tn_), F32)],
        compiler_params=_params(("parallel", "parallel", "arbitrary")),
    )(a, b)


def _rowwise(name, body, mats, vecs, outs, sums=(), tm=256):
    mats = [m if isinstance(m, tuple) else (m, 0, m.shape[1]) for m in mats]
    t = mats[0][0].shape[0]
    tm = _div_tile(t, tm, 8)
    nm, nv, no, ns = len(mats), len(vecs), len(outs), len(sums)

    def kern(*refs):
        i = pl.program_id(0)
        ins = [r[...] for r in refs[:nm + nv]]
        res = body(*ins)
        out_vals, sum_vals = res[:no], res[no:]
        for r, val in zip(refs[nm + nv:nm + nv + no], out_vals):
            if isinstance(val, (list, tuple)):
                off = 0
                for piece in val:
                    w = piece.shape[1]
                    r[:, off:off + w] = piece.astype(r.dtype)
                    off += w
            else:
                r[...] = val.astype(r.dtype)
        if ns:
            sum_refs = refs[nm + nv + no:]

            @pl.when(i == 0)
            def _():
                for r in sum_refs:
                    r[...] = jnp.zeros_like(r)

            for r, val in zip(sum_refs, sum_vals):
                r[...] += jnp.sum(val, axis=0, keepdims=True)

    in_specs = [pl.BlockSpec((tm, w), functools.partial(lambda i, cb: (i, cb), cb=cb)) for (_, cb, w) in mats]
    in_specs += [pl.BlockSpec(v.shape, lambda i: (0, 0)) for v in vecs]
    out_specs = [pl.BlockSpec((tm, w), lambda i: (i, 0)) for (w, _) in outs]
    out_specs += [pl.BlockSpec((1, w), lambda i: (0, 0)) for w in sums]
    out_shape = [jax.ShapeDtypeStruct((t, w), dt) for (w, dt) in outs]
    out_shape += [jax.ShapeDtypeStruct((1, w), F32) for w in sums]
    return pl.pallas_call(
        kern, name=name, out_shape=out_shape, grid=(t // tm,),
        in_specs=in_specs, out_specs=out_specs,
        compiler_params=_params(("arbitrary",)),
    )(*[m[0] for m in mats], *vecs)


def _rms_fwd(x, g, name):
    d = x.shape[1]

    def body(xv, gv):
        r = lax.rsqrt(jnp.mean(xv * xv, axis=-1, keepdims=True) + NORM_EPS)
        return ((xv * r) * gv,)

    return _rowwise(name, body, [x], [g], [(d, BF16)])[0]


def _post_res(o, x, g, c, name):
    d = x.shape[1]

    def body(ov, xv, gv):
        r = lax.rsqrt(jnp.mean(ov * ov, axis=-1, keepdims=True) + NORM_EPS)
        return (xv + c * ((ov * r) * gv),)

    return _rowwise(name, body, [o, x], [g], [(d, F32)])[0]


def _post_bwd(dx, o, g, c, name):
    d = dx.shape[1]

    def body(dxv, ov, gv):
        r = lax.rsqrt(jnp.mean(ov * ov, axis=-1, keepdims=True) + NORM_EPS)
        n = ov * r
        dy = c * dxv
        dn = dy * gv
        do = r * (dn - n * jnp.mean(dn * n, axis=-1, keepdims=True))
        return (do, dy * n)

    return _rowwise(name, body, [dx, o], [g], [(d, BF16)], sums=[d])


def _rms_bwd(dh_parts, x, g, dx_res, name):
    d = x.shape[1]
    npart = len(dh_parts)

    def body(*vals):
        dh = vals[0]
        for p in vals[1:npart]:
            dh = dh + p
        xv, dres, gv = vals[npart], vals[npart + 1], vals[npart + 2]
        r = lax.rsqrt(jnp.mean(xv * xv, axis=-1, keepdims=True) + NORM_EPS)
        n = xv * r
        dn = dh * gv
        dx = dres + r * (dn - n * jnp.mean(dn * n, axis=-1, keepdims=True))
        return (dx, dh * n)

    return _rowwise(name, body, list(dh_parts) + [x, dx_res], [g], [(d, F32)], sums=[d], tm=128)


def _swiglu_fwd(gate, up, name):
    f = gate.shape[1]

    def body(gv, uv):
        gv = gv.astype(F32)
        uv = uv.astype(F32)
        return ((gv * _sigmoid(gv)) * uv,)

    return _rowwise(name, body, [gate, up], [], [(f, BF16)])[0]


def _swiglu_bwd(da, gate, up, name):
    f = gate.shape[1]

    def body(dav, gv, uv):
        gv = gv.astype(F32)
        uv = uv.astype(F32)
        s = _sigmoid(gv)
        silu = gv * s
        dgate = dav * uv * (s * (1.0 + gv * (1.0 - s)))
        dup = dav * silu
        return (dgate, dup)

    return _rowwise(name, body, [da, gate, up], [], [(f, BF16), (f, BF16)])


def _merge_fwd(bds, logits, name):
    d = bds[0].shape[1]

    def body(b0, b1, b2, l0, l1, l2):
        return (_sigmoid(l0) * b0 + _sigmoid(l1) * b1 + _sigmoid(l2) * b2,)

    mats = list(bds) + [(logits, j, d) for j in range(N_BRANCHES)]
    return _rowwise(name, body, mats, [], [(d, BF16)], tm=128)[0]


def _merge_bwd(dmerged, bds, logits, name):
    d = bds[0].shape[1]

    def body(dm, b0, b1, b2, l0, l1, l2):
        dbs, dls = [], []
        for b, l in ((b0, l0), (b1, l1), (b2, l2)):
            s = _sigmoid(l)
            dbs.append(dm * s)
            dls.append(dm * b * (s * (1.0 - s)))
        return (dbs[0], dbs[1], dbs[2], dls)

    mats = [dmerged] + list(bds) + [(logits, j, d) for j in range(N_BRANCHES)]
    outs = [(d, BF16)] * 3 + [(N_BRANCHES * d, BF16)]
    return _rowwise(name, body, mats, [], outs, tm=128)


def _la_fwd(pre, b, name):
    w = pre.shape[1]

    def body(pv, bv):
        p = pv + bv
        sp = jnp.maximum(-p, 0.0) + jnp.log(1.0 + jnp.exp(-jnp.abs(p)))
        return (-sp / GLA_GATE_TAU,)

    return _rowwise(name, body, [pre], [b], [(w, F32)])[0]


def _la_bwd(dla, pre, b, name):
    w = pre.shape[1]

    def body(dv, pv, bv):
        p = pv + bv
        dpre = (dv / GLA_GATE_TAU) * _sigmoid(-p)
        return (dpre, dpre)

    return _rowwise(name, body, [dla, pre], [b], [(w, BF16)], sums=[w])


def _gla_post_fwd(o, p_gla, r_block, gn, name):
    w = o.shape[1]

    def body(ov, rv, gv):
        pieces = []
        for h in range(GLA_HEADS):
            sl = slice(h * GLA_HEAD_V, (h + 1) * GLA_HEAD_V)
            oh = ov[:, sl]
            rr = lax.rsqrt(jnp.mean(oh * oh, axis=-1, keepdims=True) + NORM_EPS)
            rh = rv[:, sl]
            pieces.append(((oh * rr) * gv[:, sl]) * (rh * _sigmoid(rh)))
        return (pieces,)

    return _rowwise(name, body, [o, (p_gla, r_block, w)], [gn], [(w, BF16)])[0]


def _gla_post_bwd(dout, o, p_gla, r_block, gn, name):
    w = o.shape[1]

    def body(dv, ov, rv, gv):
        dos, drs, dgs = [], [], []
        for h in range(GLA_HEADS):
            sl = slice(h * GLA_HEAD_V, (h + 1) * GLA_HEAD_V)
            oh, rh, gh, dh = ov[:, sl], rv[:, sl], gv[:, sl], dv[:, sl]
            rr = lax.rsqrt(jnp.mean(oh * oh, axis=-1, keepdims=True) + NORM_EPS)
            nhat = oh * rr
            s = _sigmoid(rh)
            dn = dh * (rh * s)
            drs.append(dh * (nhat * gh) * (s * (1.0 + rh * (1.0 - s))))
            dnn = dn * gh
            dos.append(rr * (dnn - nhat * jnp.mean(dnn * nhat, axis=-1, keepdims=True)))
            dgs.append(dn * nhat)
        return (dos, drs, jnp.concatenate(dgs, axis=1))

    return _rowwise(name, body, [dout, o, (p_gla, r_block, w)], [gn], [(w, F32), (w, BF16)], sums=[w])


def _conv_bwd1(dout, y, ln_g, ln_b, name):
    w = y.shape[1]

    def body(dv, yv, gv, bv):
        mu = jnp.mean(yv, axis=-1, keepdims=True)
        yc = yv - mu
        rstd = lax.rsqrt(jnp.mean(yc * yc, axis=-1, keepdims=True) + NORM_EPS)
        xhat = yc * rstd
        yn = xhat * gv + bv
        s = _sigmoid(yn)
        dyn = dv * (s * (1.0 + yn * (1.0 - s)))
        dxh = dyn * gv
        dy = rstd * (dxh - jnp.mean(dxh, axis=-1, keepdims=True)
                     - xhat * jnp.mean(dxh * xhat, axis=-1, keepdims=True))
        return (dy, dyn * xhat, dyn, dy)

    return _rowwise(name, body, [dout, y], [ln_g, ln_b], [(w, F32)], sums=[w, w, w])


def _loss_fwd(y, target, name):
    d = y.shape[1]

    def body(yv, tv):
        e = yv - tv
        return (e / d, e * e)

    return _rowwise(name, body, [y, target], [], [(d, F32)], sums=[d])


def _adamw(w, g, m, v, name):
    cols = w.shape[1]
    c1 = 1.0 - ADAM_B1 ** ADAM_STEP
    c2 = 1.0 - ADAM_B2 ** ADAM_STEP

    def body(wv, gv, mv, vv):
        m2 = ADAM_B1 * mv + (1.0 - ADAM_B1) * gv
        v2 = ADAM_B2 * vv + (1.0 - ADAM_B2) * (gv * gv)
        m_hat = m2 / c1
        v_hat = v2 / c2
        delta = -ADAM_LR * (m_hat / (jnp.sqrt(v_hat) + ADAM_EPS) + ADAM_WD * wv)
        return (delta, m2, v2)

    return _rowwise(name, body, [w, g, m, v], [], [(cols, F32)] * 3)


def _split_bf16(v):
    hi = v.astype(BF16)
    lo = (v - hi.astype(F32)).astype(BF16)
    return hi, lo


def _tri(n, strict):
    r = lax.broadcasted_iota(jnp.int32, (n, n), 0)
    c = lax.broadcasted_iota(jnp.int32, (n, n), 1)
    return jnp.where(r > c if strict else r >= c, 1.0, 0.0).astype(BF16)


def _sb_weights(z, c, mx, diag_offset):
    u = jnp.exp(-jnp.abs(z))
    sp = jnp.maximum(z, 0.0) + jnp.log(1.0 + u)
    if diag_offset is None:
        mask = None
        spm = sp
    else:
        rows = lax.broadcasted_iota(jnp.int32, z.shape, 0)
        cols = lax.broadcasted_iota(jnp.int32, z.shape, 1)
        mask = cols + diag_offset < rows
        spm = jnp.where(mask, sp, 0.0)
    suf = _dot(spm.astype(BF16), mx)
    w = jnp.exp(((z - sp) - suf) - c)
    if mask is not None:
        w = jnp.where(mask, w, 0.0)
    return u, sp, spm, w, mask


def _sb_fwd(q, kt, vt, name):
    nh, t, dh = q.shape
    nkb, bk = kt.shape[1], kt.shape[3]
    bq = min(SB_BQ, t)
    scale = dh ** -0.5
    per = bq // bk

    def kern(q_ref, kt_ref, vt_ref, o_ref, c_ref, zbuf, wbuf):
        qi = pl.program_id(1)
        top = (qi + 1) * per - 1
        ntiles = (qi + 1) * per
        qs = (q_ref[...].astype(F32) * scale).astype(BF16)
        mx = _tri(bk, True)
        o_ref[...] = jnp.zeros_like(o_ref)
        c_ref[...] = jnp.zeros_like(c_ref)
        zbuf[0] = _dot(qs, kt_ref[top])
        wbuf[0] = jnp.zeros((bq, bk), BF16)

        def tile(i, diag_offset):
            kb = top - i
            p = i % 2
            z = zbuf[p]
            w_prev = wbuf[p]
            c = c_ref[...]
            zbuf[1 - p] = _dot(qs, kt_ref[jnp.maximum(kb - 1, 0)])
            o_ref[...] += _dot(w_prev, vt_ref[jnp.minimum(kb + 1, nkb - 1)], _NT)
            _, _, spm, w, _ = _sb_weights(z, c, mx, diag_offset)
            wbuf[1 - p] = w.astype(BF16)
            c_ref[...] += jnp.sum(spm, axis=1, keepdims=True)

        def diag_step(i, carry):
            tile(i, (per - 1 - i) * bk)
            return carry

        def step(i, carry):
            tile(i, None)
            return carry

        lax.fori_loop(0, per, diag_step, 0)
        lax.fori_loop(per, ntiles, step, 0)
        o_ref[...] += _dot(wbuf[ntiles % 2], vt_ref[0], _NT)

    return pl.pallas_call(
        kern, name=name,
        out_shape=jax.ShapeDtypeStruct((nh, t, dh), F32),
        grid=(nh, t // bq),
        in_specs=[pl.BlockSpec((None, bq, dh), lambda h, i: (h, i, 0)),
                  pl.BlockSpec((None, nkb, dh, bk), lambda h, i: (h, 0, 0, 0)),
                  pl.BlockSpec((None, nkb, dh, bk), lambda h, i: (h, 0, 0, 0))],
        out_specs=pl.BlockSpec((None, bq, dh), lambda h, i: (h, i, 0)),
        scratch_shapes=[pltpu.VMEM((bq, 1), F32), pltpu.VMEM((2, bq, bk), F32), pltpu.VMEM((2, bq, bk), BF16)],
        compiler_params=_params(("parallel", "arbitrary")),
    )(q, kt, vt)


def _sb_bwd(q, qt, kt, vt, out, dout, doutt, name):
    nh, t, dh = q.shape
    nkb, bk = kt.shape[1], kt.shape[3]
    bq = min(SB_BQ, t)
    scale = dh ** -0.5
    per = bq // bk

    def kern(q_ref, qt_ref, kt_ref, vt_ref, o_ref, do_ref, dot_ref, dq_ref, dkt_ref, dvt_ref, c_ref, ce_ref):
        qi = pl.program_id(1)
        top = (qi + 1) * per - 1

        @pl.when(qi == 0)
        def _():
            dkt_ref[...] = jnp.zeros_like(dkt_ref)
            dvt_ref[...] = jnp.zeros_like(dvt_ref)

        qs = (q_ref[...].astype(F32) * scale).astype(BF16)
        qts = (qt_ref[...].astype(F32) * scale).astype(BF16)
        dob = do_ref[...].astype(BF16)
        dotv = dot_ref[...]
        dsum = jnp.sum(dob.astype(F32) * o_ref[...], axis=1, keepdims=True)
        mx = _tri(bk, True)
        mi = _tri(bk, False)
        dq_ref[...] = jnp.zeros_like(dq_ref)
        c_ref[...] = jnp.zeros_like(c_ref)
        ce_ref[...] = jnp.zeros_like(ce_ref)

        def tile(kb, diag_offset):
            kt_blk = kt_ref[kb]
            z = _dot(qs, kt_blk)
            u, sp, spm, w, mask = _sb_weights(z, c_ref[...], mx, diag_offset)
            wb = w.astype(BF16)
            e = _dot(dob, vt_ref[kb]) * wb.astype(F32)
            ehi, elo = _split_bf16(e)
            before = dsum - (_dot(ehi, mi) + _dot(elo, mi) + ce_ref[...])
            sig = jnp.exp(z - sp)
            dz = e - sig * (e + before)
            if mask is not None:
                dz = jnp.where(mask, dz, 0.0)
            dz = dz.astype(BF16)
            dq_ref[...] += _dot(dz, kt_blk, _NT)
            dkt_ref[kb] += _dot(qts, dz)
            dvt_ref[kb] += _dot(dotv, wb)
            c_ref[...] += jnp.sum(spm, axis=1, keepdims=True)
            ce_ref[...] += jnp.sum(e, axis=1, keepdims=True)

        def diag_step(i, carry):
            tile(top - i, (per - 1 - i) * bk)
            return carry

        def step(i, carry):
            tile(top - i, None)
            return carry

        lax.fori_loop(0, per, diag_step, 0)
        lax.fori_loop(per, (qi + 1) * per, step, 0)
        dq_ref[...] = dq_ref[...] * scale

    row = pl.BlockSpec((None, bq, dh), lambda h, i: (h, i, 0))
    col = pl.BlockSpec((None, dh, bq), lambda h, i: (h, 0, i))
    whole = pl.BlockSpec((None, nkb, dh, bk), lambda h, i: (h, 0, 0, 0))
    return pl.pallas_call(
        kern, name=name,
        out_shape=[jax.ShapeDtypeStruct((nh, t, dh), F32),
                   jax.ShapeDtypeStruct((nh, nkb, dh, bk), F32),
                   jax.ShapeDtypeStruct((nh, nkb, dh, bk), F32)],
        grid=(nh, t // bq),
        in_specs=[row, col, whole, whole, row, row, col],
        out_specs=[row, whole, whole],
        scratch_shapes=[pltpu.VMEM((bq, 1), F32), pltpu.VMEM((bq, 1), F32)],
        compiler_params=_params(("parallel", "arbitrary")),
    )(q, qt, kt, vt, out, dout, doutt)


def _conv_fwd(p_conv, conv_w, conv_b, ln_g, ln_b, name):
    t, c2 = p_conv.shape
    c = c2 // 2
    tm = min(256, t)
    hb = tm // CONV_HALO

    def kern(a_ref, g_ref, ah_ref, gh_ref, w_ref, b_ref, lg_ref, lb_ref, o_ref, u_ref, y_ref, ubuf):
        i = pl.program_id(0)
        u = a_ref[...] * _sigmoid(g_ref[...])
        uh = ah_ref[...] * _sigmoid(gh_ref[...])
        ubuf[0:CONV_HALO, :] = jnp.where(i > 0, uh, 0.0)
        ubuf[CONV_HALO:CONV_HALO + tm, :] = u
        y = jnp.zeros((tm, c), F32) + b_ref[...]
        for j in range(CONV_WIDTH):
            off = CONV_HALO - (CONV_WIDTH - 1) + j
            y = y + ubuf[off:off + tm, :] * w_ref[j:j + 1, :]
        mu = jnp.mean(y, axis=-1, keepdims=True)
        yc = y - mu
        rstd = lax.rsqrt(jnp.mean(yc * yc, axis=-1, keepdims=True) + NORM_EPS)
        yn = (yc * rstd) * lg_ref[...] + lb_ref[...]
        o_ref[...] = (yn * _sigmoid(yn)).astype(o_ref.dtype)
        u_ref[...] = u
        y_ref[...] = y

    def halo(cb):
        return pl.BlockSpec((CONV_HALO, c), lambda i: (jnp.maximum(i * hb - 1, 0), cb))

    vec = pl.BlockSpec((1, c), lambda i: (0, 0))
    tile = pl.BlockSpec((tm, c), lambda i: (i, 0))
    return pl.pallas_call(
        kern, name=name,
        out_shape=[jax.ShapeDtypeStruct((t, c), BF16), jax.ShapeDtypeStruct((t, c), F32),
                   jax.ShapeDtypeStruct((t, c), F32)],
        grid=(t // tm,),
        in_specs=[tile, pl.BlockSpec((tm, c), lambda i: (i, 1)), halo(0), halo(1),
                  pl.BlockSpec(conv_w.shape, lambda i: (0, 0)), vec, vec, vec],
        out_specs=[tile, tile, tile],
        scratch_shapes=[pltpu.VMEM((tm + CONV_HALO, c), F32)],
        compiler_params=_params(("arbitrary",)),
    )(p_conv, p_conv, p_conv, p_conv, conv_w, conv_b, ln_g, ln_b)


def _conv_bwd2(dy, u, p_conv, conv_w, name):
    t, c = dy.shape
    tm = min(256, t)
    hb = tm // CONV_HALO
    nt = t // tm
    last_halo = t // CONV_HALO - 1

    def kern(dy_ref, dyh_ref, u_ref, uh_ref, a_ref, g_ref, w_ref, dp_ref, dw_ref, dybuf, ubuf):
        i = pl.program_id(0)

        @pl.when(i == 0)
        def _():
            dw_ref[...] = jnp.zeros_like(dw_ref)

        dyv = dy_ref[...]
        dybuf[0:tm, :] = dyv
        dybuf[tm:tm + CONV_HALO, :] = jnp.where(i < nt - 1, dyh_ref[...], 0.0)
        ubuf[0:CONV_HALO, :] = jnp.where(i > 0, uh_ref[...], 0.0)
        ubuf[CONV_HALO:CONV_HALO + tm, :] = u_ref[...]
        du = jnp.zeros((tm, c), F32)
        for j in range(CONV_WIDTH):
            off = CONV_WIDTH - 1 - j
            du = du + dybuf[off:off + tm, :] * w_ref[j:j + 1, :]
            uoff = CONV_HALO - (CONV_WIDTH - 1) + j
            dw_ref[j:j + 1, :] += jnp.sum(dyv * ubuf[uoff:uoff + tm, :], axis=0, keepdims=True)
        a = a_ref[...]
        s = _sigmoid(g_ref[...])
        dp_ref[:, 0:c] = (du * s).astype(dp_ref.dtype)
        dp_ref[:, c:2 * c] = (du * a * (s * (1.0 - s))).astype(dp_ref.dtype)

    tile = pl.BlockSpec((tm, c), lambda i: (i, 0))
    return pl.pallas_call(
        kern, name=name,
        out_shape=[jax.ShapeDtypeStruct((t, 2 * c), BF16), jax.ShapeDtypeStruct((CONV_HALO, c), F32)],
        grid=(nt,),
        in_specs=[tile,
                  pl.BlockSpec((CONV_HALO, c), lambda i: (jnp.minimum((i + 1) * hb, last_halo), 0)),
                  tile,
                  pl.BlockSpec((CONV_HALO, c), lambda i: (jnp.maximum(i * hb - 1, 0), 0)),
                  tile, pl.BlockSpec((tm, c), lambda i: (i, 1)),
                  pl.BlockSpec(conv_w.shape, lambda i: (0, 0))],
        out_specs=[pl.BlockSpec((tm, 2 * c), lambda i: (i, 0)),
                   pl.BlockSpec((CONV_HALO, c), lambda i: (0, 0))],
        scratch_shapes=[pltpu.VMEM((tm + CONV_HALO, c), F32), pltpu.VMEM((tm + CONV_HALO, c), F32)],
        compiler_params=_params(("arbitrary",)),
    )(dy, dy, u, u, p_conv, p_conv, conv_w)


GLA_CHUNKS_PER_STEP = 16


def _gla_fwd(q_ck, k_kc, la_kc, v, name):
    nh, nc, ch, dk = q_ck.shape
    dv = v.shape[3]
    qscale = dk ** -0.5
    cb = min(GLA_CHUNKS_PER_STEP, nc)

    def kern(q_ref, k_ref, la_ref, v_ref, o_ref, st_ref, state_ref):
        mx = _tri(ch, True)

        @pl.when(pl.program_id(1) == 0)
        def _():
            state_ref[...] = jnp.zeros_like(state_ref)

        def step(n, state):
            la = la_ref[n]
            hi, lo = _split_bf16(la)
            de = _dot(hi, mx) + _dot(lo, mx)
            lam = jnp.exp(jnp.sum(la, axis=1, keepdims=True))
            kd = (k_ref[n] * jnp.exp(de)).astype(BF16)
            state = lam * state + _dot(kd, v_ref[n])
            st_ref[n] = state
            qs = (q_ref[n].astype(F32) * qscale).astype(BF16)
            o_ref[n] = _dot(qs, state.astype(BF16))
            return state

        state_ref[...] = lax.fori_loop(0, cb, step, state_ref[...])

    def spec(a, b):
        return pl.BlockSpec((None, cb, a, b), lambda h, j: (h, j, 0, 0))

    return pl.pallas_call(
        kern, name=name,
        out_shape=[jax.ShapeDtypeStruct((nh, nc, ch, dv), F32), jax.ShapeDtypeStruct((nh, nc, dk, dv), F32)],
        grid=(nh, nc // cb),
        in_specs=[spec(ch, dk), spec(dk, ch), spec(dk, ch), spec(ch, dv)],
        out_specs=[spec(ch, dv), spec(dk, dv)],
        scratch_shapes=[pltpu.VMEM((dk, dv), F32)],
        compiler_params=_params(("parallel", "arbitrary")),
    )(q_ck, k_kc, la_kc, v)


def _gla_bwd(q_ck, q_kc, k_ck, k_kc, la_ck, la_kc, v, states, states_prev, do, name):
    nh, nc, ch, dk = q_ck.shape
    dv = v.shape[3]
    qscale = dk ** -0.5
    cb = min(GLA_CHUNKS_PER_STEP, nc)
    nb = nc // cb

    def kern(q_ref, qt_ref, kck_ref, kkc_ref, lack_ref, lakc_ref, v_ref, st_ref, stp_ref, do_ref,
             dq_ref, dk_ref, dv_ref, dla_ref, g_ref):
        mx = _tri(ch, True)
        mxt = jnp.where(lax.broadcasted_iota(jnp.int32, (ch, ch), 0)
                        < lax.broadcasted_iota(jnp.int32, (ch, ch), 1), 1.0, 0.0).astype(BF16)

        @pl.when(pl.program_id(1) == 0)
        def _():
            g_ref[...] = jnp.zeros_like(g_ref)

        def step(i, g):
            n = cb - 1 - i
            la_kc = lakc_ref[n]
            hi, lo = _split_bf16(la_kc)
            de_kc = _dot(hi, mx) + _dot(lo, mx)
            lam = jnp.exp(jnp.sum(la_kc, axis=1, keepdims=True))
            hi2, lo2 = _split_bf16(lack_ref[n])
            de_ck = _dot(mxt, hi2) + _dot(mxt, lo2)
            edk = jnp.exp(de_kc)
            kd_kc = kkc_ref[n] * edk
            kd_ck = (kck_ref[n] * jnp.exp(de_ck)).astype(BF16)
            dob = do_ref[n].astype(BF16)
            dq_ref[n] = _dot(dob, st_ref[n].astype(BF16), _NT) * qscale
            qts = (qt_ref[n].astype(F32) * qscale).astype(BF16)
            ds = _dot(qts, dob) + g
            dsb = ds.astype(BF16)
            dlam = jnp.sum(ds * stp_ref[n], axis=1, keepdims=True)
            dkd = _dot(dsb, v_ref[n], _NT)
            dv_ref[n] = _dot(kd_ck, dsb)
            dk_ref[n] = dkd * edk
            dde = dkd * kd_kc
            h3, l3 = _split_bf16(dde)
            dla_ref[n] = _dot(h3, mxt) + _dot(l3, mxt) + dlam * lam
            return lam * ds

        g_ref[...] = lax.fori_loop(0, cb, step, g_ref[...])

    def spec(a, b):
        return pl.BlockSpec((None, cb, a, b), lambda h, j: (h, nb - 1 - j, 0, 0))

    return pl.pallas_call(
        kern, name=name,
        out_shape=[jax.ShapeDtypeStruct((nh, nc, ch, dk), F32), jax.ShapeDtypeStruct((nh, nc, dk, ch), F32),
                   jax.ShapeDtypeStruct((nh, nc, ch, dv), F32), jax.ShapeDtypeStruct((nh, nc, dk, ch), F32)],
        grid=(nh, nb),
        in_specs=[spec(ch, dk), spec(dk, ch), spec(ch, dk), spec(dk, ch), spec(ch, dk), spec(dk, ch),
                  spec(ch, dv), spec(dk, dv), spec(dk, dv), spec(ch, dv)],
        out_specs=[spec(ch, dk), spec(dk, ch), spec(ch, dv), spec(dk, ch)],
        scratch_shapes=[pltpu.VMEM((dk, dv), F32)],
        compiler_params=_params(("parallel", "arbitrary")),
    )(q_ck, q_kc, k_ck, k_kc, la_ck, la_kc, v, states, states_prev, do)


def _heads_rows(a, nh, blk):
    t = a.shape[0]
    d = a.shape[1] // nh
    return a.reshape(t // blk, blk, nh, d).transpose(2, 0, 1, 3)


def _heads_cols(a, nh, blk):
    t = a.shape[0]
    d = a.shape[1] // nh
    return a.reshape(t // blk, blk, nh, d).transpose(2, 0, 3, 1)


def _unheads_rows(a):
    nh, nb, blk, d = a.shape
    return a.transpose(1, 2, 0, 3).reshape(nb * blk, nh * d)


def _unheads_cols(a):
    nh, nb, d, blk = a.shape
    return a.transpose(1, 3, 0, 2).reshape(nb * blk, nh * d)


def _ffn_fwd(x, w, pre, post, tag):
    h = _rms_fwd(x, pre, f"{tag}_rms")
    gate = _mm(h, w["gate"], nt=True, out_dtype=BF16, name=f"{tag}_gate")
    up = _mm(h, w["up"], nt=True, out_dtype=BF16, name=f"{tag}_up")
    act = _swiglu_fwd(gate, up, f"{tag}_act")
    o = _mm(act, w["down"], name=f"{tag}_down")
    x_out = _post_res(o, x, post, 0.5, f"{tag}_res")
    return x_out, (x, h, gate, up, act, o)


def _ffn_bwd(dx_out, saved, w, pre, post, tag):
    x, h, gate, up, act, o = saved
    do, dpost = _post_bwd(dx_out, o, post, 0.5, f"{tag}_bres")
    da = _mm(do, w["down"], nt=True, name=f"{tag}_bda")
    d_down = _mm(act, do, tn=True, out_dtype=BF16, name=f"{tag}_bwdown")
    dgate, dup = _swiglu_bwd(da, gate, up, f"{tag}_bact")
    dh_g = _mm(dgate, w["gate"], name=f"{tag}_bdhg")
    dh_u = _mm(dup, w["up"], name=f"{tag}_bdhu")
    d_gate = _mm(dgate, h, tn=True, out_dtype=BF16, name=f"{tag}_bwgate")
    d_up = _mm(dup, h, tn=True, out_dtype=BF16, name=f"{tag}_bwup")
    dx, dpre = _rms_bwd([dh_g, dh_u], x, pre, dx_out, f"{tag}_brms")
    return dx, dict(gate=d_gate, up=d_up, down=d_down, pre=dpre, post=dpost)


def _mixer_fwd(x, w, tag):
    t, d = x.shape
    bw = BRANCH_WIDTH
    h = _rms_fwd(x, w["pre"], f"{tag}_rms")
    p_sb = _mm(h, w["in_sb"], nt=True, out_dtype=BF16, name=f"{tag}_insb")
    p_conv = _mm(h, w["in_conv"], nt=True, name=f"{tag}_inconv")
    p_gla = _mm(h, w["in_gla"], nt=True, name=f"{tag}_ingla")
    p_gate = _mm(h, w["in_gate"], nt=True, name=f"{tag}_ingate")

    bk = min(SB_BK, t)
    q = p_sb[:, 0:bw].reshape(t, SB_HEADS, SB_HEAD_DIM).transpose(1, 0, 2)
    kt = _heads_cols(p_sb[:, bw:2 * bw], SB_HEADS, bk)
    vt = _heads_cols(p_sb[:, 2 * bw:3 * bw], SB_HEADS, bk)
    sb_o = _sb_fwd(q, kt, vt, f"{tag}_sb")
    sb_out = sb_o.transpose(1, 0, 2).reshape(t, bw)

    conv_out, conv_u, conv_y = _conv_fwd(p_conv, w["conv_w"], w["conv_b"], w["ln_g"], w["ln_b"], f"{tag}_conv")

    kd, vd = GLA_KEY_DIM, GLA_VALUE_DIM
    lr = p_gla[:, 2 * kd + 2 * vd:]
    pre_a = _mm(lr, w["alpha"], name=f"{tag}_alpha")
    la = _la_fwd(pre_a, w["b_alpha"], f"{tag}_la")
    gq = _heads_rows(p_gla[:, 0:kd].astype(BF16), GLA_HEADS, CHUNK)
    gk_kc = _heads_cols(p_gla[:, kd:2 * kd], GLA_HEADS, CHUNK)
    gv = _heads_rows(p_gla[:, 2 * kd:2 * kd + vd].astype(BF16), GLA_HEADS, CHUNK)
    la_kc = _heads_cols(la, GLA_HEADS, CHUNK)
    gla_o4, states = _gla_fwd(gq, gk_kc, la_kc, gv, f"{tag}_gla")
    gla_o = _unheads_rows(gla_o4)
    r_block = (2 * kd + vd) // vd
    gla_out = _gla_post_fwd(gla_o, p_gla, r_block, w["gnorm"], f"{tag}_glapost")

    branches = (sb_out.astype(BF16), conv_out, gla_out)
    bds = [_mm(branches[j], w["branch"][j], nt=True, name=f"{tag}_br{j}") for j in range(N_BRANCHES)]
    merged = _merge_fwd(bds, p_gate, f"{tag}_merge")
    mo = _mm(merged, w["out"], name=f"{tag}_out")
    x_out = _post_res(mo, x, w["post"], 1.0, f"{tag}_res")
    saved = dict(x=x, h=h, p_sb=p_sb, p_conv=p_conv, p_gla=p_gla, p_gate=p_gate, sb_o=sb_o, conv_u=conv_u,
                 conv_y=conv_y, pre_a=pre_a, la=la, states=states, gla_o=gla_o, branches=branches, bds=bds,
                 merged=merged, mo=mo)
    return x_out, saved


def _mixer_bwd(dx_out, s, w, tag):
    x = s["x"]
    t, d = x.shape
    bw = BRANCH_WIDTH
    kd, vd = GLA_KEY_DIM, GLA_VALUE_DIM
    grads = {}
    dmo, grads["post"] = _post_bwd(dx_out, s["mo"], w["post"], 1.0, f"{tag}_bres")
    dmerged = _mm(dmo, w["out"], nt=True, name=f"{tag}_bdmerged")
    grads["out"] = _mm(s["merged"], dmo, tn=True, out_dtype=BF16, name=f"{tag}_bwout")
    dbd0, dbd1, dbd2, dp_gate = _merge_bwd(dmerged, s["bds"], s["p_gate"], f"{tag}_bmerge")
    dbds = (dbd0, dbd1, dbd2)
    dbranch = [_mm(dbds[j], w["branch"][j], name=f"{tag}_bdbr{j}") for j in range(N_BRANCHES)]
    grads["branch"] = jnp.stack([_mm(dbds[j], s["branches"][j], tn=True, out_dtype=BF16, name=f"{tag}_bwbr{j}")
                                 for j in range(N_BRANCHES)])

    p_sb = s["p_sb"]
    bk = min(SB_BK, t)
    qf, kf, vf = p_sb[:, 0:bw], p_sb[:, bw:2 * bw], p_sb[:, 2 * bw:3 * bw]
    q = qf.reshape(t, SB_HEADS, SB_HEAD_DIM).transpose(1, 0, 2)
    qt = qf.reshape(t, SB_HEADS, SB_HEAD_DIM).transpose(1, 2, 0)
    kt = _heads_cols(kf, SB_HEADS, bk)
    vt = _heads_cols(vf, SB_HEADS, bk)
    dsb = dbranch[0].reshape(t, SB_HEADS, SB_HEAD_DIM)
    dout = dsb.transpose(1, 0, 2)
    doutt = dsb.transpose(1, 2, 0).astype(BF16)
    dq, dkt, dvt = _sb_bwd(q, qt, kt, vt, s["sb_o"], dout, doutt, f"{tag}_bsb")
    dp_sb = jnp.concatenate([dq.transpose(1, 0, 2).reshape(t, bw), _unheads_cols(dkt), _unheads_cols(dvt)],
                            axis=1).astype(BF16)

    dy, d_lng, d_lnb, d_cb = _conv_bwd1(dbranch[1], s["conv_y"], w["ln_g"], w["ln_b"], f"{tag}_bconv1")
    dp_conv, d_cw = _conv_bwd2(dy, s["conv_u"], s["p_conv"], w["conv_w"], f"{tag}_bconv2")
    grads.update(conv_w=d_cw[:CONV_WIDTH], conv_b=d_cb, ln_g=d_lng, ln_b=d_lnb)

    p_gla = s["p_gla"]
    r_block = (2 * kd + vd) // vd
    do, dr, grads["gnorm"] = _gla_post_bwd(dbranch[2], s["gla_o"], p_gla, r_block, w["gnorm"], f"{tag}_bglapost")
    gqf, gkf = p_gla[:, 0:kd], p_gla[:, kd:2 * kd]
    gq_b = gqf.astype(BF16)
    dq4, dk4, dv4, dla4 = _gla_bwd(
        _heads_rows(gq_b, GLA_HEADS, CHUNK), _heads_cols(gq_b, GLA_HEADS, CHUNK),
        _heads_rows(gkf, GLA_HEADS, CHUNK), _heads_cols(gkf, GLA_HEADS, CHUNK),
        _heads_rows(s["la"], GLA_HEADS, CHUNK), _heads_cols(s["la"], GLA_HEADS, CHUNK),
        _heads_rows(p_gla[:, 2 * kd:2 * kd + vd].astype(BF16), GLA_HEADS, CHUNK),
        s["states"], jnp.pad(s["states"][:, :-1], ((0, 0), (1, 0), (0, 0), (0, 0))),
        _heads_rows(do, GLA_HEADS, CHUNK), f"{tag}_bgla")
    dla = _unheads_cols(dla4)
    dpre_a, grads["b_alpha"] = _la_bwd(dla, s["pre_a"], w["b_alpha"], f"{tag}_bla")
    lr = p_gla[:, 2 * kd + 2 * vd:]
    dlr = _mm(dpre_a, w["alpha"], nt=True, out_dtype=BF16, name=f"{tag}_bdlr")
    grads["alpha"] = _mm(lr, dpre_a, tn=True, name=f"{tag}_bwalpha")
    dp_gla = jnp.concatenate([_unheads_rows(dq4).astype(BF16), _unheads_cols(dk4).astype(BF16),
                              _unheads_rows(dv4).astype(BF16), dr, dlr], axis=1)

    dps = dict(in_sb=dp_sb, in_conv=dp_conv, in_gla=dp_gla, in_gate=dp_gate)
    dh_parts = []
    for key, dp in dps.items():
        dh_parts.append(_mm(dp, w[key], name=f"{tag}_bdh_{key}"))
        grads[key] = _mm(dp, s["h"], tn=True, out_dtype=BF16, name=f"{tag}_bw_{key}")
    dx, grads["pre"] = _rms_bwd(dh_parts, x, w["pre"], dx_out, f"{tag}_brms")
    return dx, grads


def _local_step(x, target, layers):
    saved = []
    for l, w in enumerate(layers):
        x, s1 = _ffn_fwd(x, w["ffn1"], w["pre0"], w["post0"], f"l{l}_f1")
        x, s2 = _mixer_fwd(x, w["mix"], f"l{l}_mx")
        x, s3 = _ffn_fwd(x, w["ffn2"], w["pre2"], w["post2"], f"l{l}_f2")
        saved.append((s1, s2, s3))
    dx, sq = _loss_fwd(x, target, "loss")
    grads = [None] * len(layers)
    for l in reversed(range(len(layers))):
        w = layers[l]
        s1, s2, s3 = saved[l]
        dx, g3 = _ffn_bwd(dx, s3, w["ffn2"], w["pre2"], w["post2"], f"l{l}_f2")
        dx, g2 = _mixer_bwd(dx, s2, w["mix"], f"l{l}_mx")
        dx, g1 = _ffn_bwd(dx, s1, w["ffn1"], w["pre0"], w["post0"], f"l{l}_f1")
        grads[l] = dict(ffn1=g1, mix=g2, ffn2=g3)
    return sq, dx, grads


_ANY = pl.BlockSpec(memory_space=pl.ANY)
_MESH = pl.DeviceIdType.MESH
N_CHIPS = 4


def _all_gather(p, name):
    def body(p_ref, out_ref, send_sems, recv_sems, local_sem):
        x, y, c = lax.axis_index("x"), lax.axis_index("y"), lax.axis_index("c")
        me, sibling = (x, y, c), (x, y, 1 - c)
        chips = [(1 - x, y), (x, 1 - y), (1 - x, 1 - y)]

        def rows(px, py, pc):
            return out_ref.at[4 * px + 2 * py + pc]

        def copy(k, block, to, src=None):
            return pltpu.make_async_remote_copy(
                src_ref=rows(*block) if src is None else src, dst_ref=rows(*block),
                send_sem=send_sems.at[k], recv_sem=recv_sems.at[k], device_id=to, device_id_type=_MESH)

        mine = pltpu.make_async_copy(p_ref, rows(*me), local_sem)
        mine.start()
        first = [copy(0, me, sibling, src=p_ref)]
        first += [copy(1 + j, me, (*chip, c), src=p_ref) for j, chip in enumerate(chips)]
        for cp in first:
            cp.start()
        passed = [copy(4 + j, (*chip, c), sibling) for j, chip in enumerate(chips)]
        for j, chip in enumerate(chips):
            copy(1 + j, (*chip, c), me).wait_recv()
            passed[j].start()
        copy(0, sibling, me).wait_recv()
        for j, chip in enumerate(chips):
            copy(4 + j, (*chip, 1 - c), me).wait_recv()
        for cp in first + passed:
            cp.wait_send()
        mine.wait()

    return pl.pallas_call(
        body, name=name,
        out_shape=jax.ShapeDtypeStruct((N_DEV,) + p.shape, p.dtype),
        in_specs=[_ANY], out_specs=_ANY,
        scratch_shapes=[pltpu.SemaphoreType.DMA((7,)), pltpu.SemaphoreType.DMA((7,)), pltpu.SemaphoreType.DMA],
    )(p)


def _exchange_sibling(src, name):
    def body(src_ref, out_ref, send_sems, recv_sems):
        x, y, c = lax.axis_index("x"), lax.axis_index("y"), lax.axis_index("c")
        copies = [pltpu.make_async_remote_copy(
            src_ref=src_ref.at[j, 1 - c], dst_ref=out_ref.at[j], send_sem=send_sems.at[j],
            recv_sem=recv_sems.at[j], device_id=(x, y, 1 - c), device_id_type=_MESH) for j in range(N_CHIPS)]
        for cp in copies:
            cp.start()
        for cp in copies:
            cp.wait()

    return pl.pallas_call(
        body, name=name,
        out_shape=jax.ShapeDtypeStruct((N_CHIPS,) + src.shape[2:], src.dtype),
        in_specs=[_ANY], out_specs=_ANY,
        scratch_shapes=[pltpu.SemaphoreType.DMA((N_CHIPS,)), pltpu.SemaphoreType.DMA((N_CHIPS,))],
    )(src)


def _add_own(src, got, name):
    _, _, r, cols = src.shape
    tr = _div_tile(r, 512, 16)
    c = lax.axis_index("c").astype(jnp.int32).reshape(1)

    def kern(c_ref, a_ref, b_ref, o_ref):
        o_ref[...] = (a_ref[...].astype(F32) + b_ref[...].astype(F32)).astype(o_ref.dtype)

    grid_spec = pltpu.PrefetchScalarGridSpec(
        num_scalar_prefetch=1, grid=(N_CHIPS, r // tr),
        in_specs=[pl.BlockSpec((None, None, tr, cols), lambda j, i, c_ref: (j, c_ref[0], i, 0)),
                  pl.BlockSpec((None, tr, cols), lambda j, i, c_ref: (j, i, 0))],
        out_specs=pl.BlockSpec((None, tr, cols), lambda j, i, c_ref: (j, i, 0)))
    return pl.pallas_call(
        kern, name=name, grid_spec=grid_spec,
        out_shape=jax.ShapeDtypeStruct((N_CHIPS, r, cols), src.dtype),
        compiler_params=_params(("arbitrary", "arbitrary")),
    )(c, src, got)


def _exchange_chips(part, name):
    def body(part_ref, out_ref, send_sems, recv_sems, local_sem):
        x, y, c = lax.axis_index("x"), lax.axis_index("y"), lax.axis_index("c")
        my_chip = 2 * x + y
        chips = [(1 - x, y), (x, 1 - y), (1 - x, 1 - y)]
        mine = pltpu.make_async_copy(part_ref.at[my_chip], out_ref.at[my_chip], local_sem)
        mine.start()
        copies = []
        for k, (px, py) in enumerate(chips):
            copies.append(pltpu.make_async_remote_copy(
                src_ref=part_ref.at[2 * px + py], dst_ref=out_ref.at[my_chip],
                send_sem=send_sems.at[k], recv_sem=recv_sems.at[k], device_id=(px, py, c), device_id_type=_MESH))
        for cp in copies:
            cp.start()
        for k, (px, py) in enumerate(chips):
            pltpu.make_async_remote_copy(
                src_ref=part_ref.at[my_chip], dst_ref=out_ref.at[2 * px + py],
                send_sem=send_sems.at[k], recv_sem=recv_sems.at[k], device_id=(px, py, c),
                device_id_type=_MESH).wait_recv()
        for cp in copies:
            cp.wait_send()
        mine.wait()

    return pl.pallas_call(
        body, name=name,
        out_shape=jax.ShapeDtypeStruct(part.shape, part.dtype),
        in_specs=[_ANY], out_specs=_ANY,
        scratch_shapes=[pltpu.SemaphoreType.DMA((3,)), pltpu.SemaphoreType.DMA((3,)), pltpu.SemaphoreType.DMA],
    )(part)


def _sum_chips(parts, name):
    _, r, cols = parts.shape
    tr = _div_tile(r, 512, 16)

    def kern(p_ref, o_ref):
        p = [p_ref[j].astype(F32) for j in range(N_CHIPS)]
        o_ref[...] = ((p[0] + p[1]) + p[2]) + p[3]

    return pl.pallas_call(
        kern, name=name, grid=(r // tr,),
        in_specs=[pl.BlockSpec((4, tr, cols), lambda i: (0, i, 0))],
        out_specs=pl.BlockSpec((tr, cols), lambda i: (i, 0)),
        out_shape=jax.ShapeDtypeStruct((r, cols), F32),
        compiler_params=_params(("arbitrary",)),
    )(parts)


_SHARDED = {
    "norm_pre": (1, True), "norm_post": (1, True),
    "ffn1_w_gate": (1, False), "ffn1_w_up": (1, False), "ffn1_w_down": (0, False),
    "ffn2_w_gate": (1, False), "ffn2_w_up": (1, False), "ffn2_w_down": (0, False),
    "w_in": (1, False), "conv_w": (1, True), "gla_w_alpha": (1, False),
    "w_branch": (2, False), "w_out": (0, False),
}
_TRANSPOSED = ("ffn1_w_gate", "ffn1_w_up", "ffn2_w_gate", "ffn2_w_up", "w_in", "w_branch")


def _storage(name, a):
    return jnp.swapaxes(a, -1, -2) if name in _TRANSPOSED else a


def _storage_axis(name, shape):
    axis = _SHARDED[name][0]
    if name in _TRANSPOSED and axis >= len(shape) - 2:
        axis = 2 * len(shape) - 3 - axis
    return axis


_REPLICATED = ("conv_b", "conv_ln_g", "conv_ln_b", "gla_b_alpha", "gla_norm_g")
_WEIGHTS = ("norm_pre", "norm_post", "ffn1_w_gate", "ffn1_w_up", "ffn1_w_down", "ffn2_w_gate", "ffn2_w_up",
            "ffn2_w_down", "w_in", "conv_w", "conv_b", "conv_ln_g", "conv_ln_b", "gla_w_alpha", "gla_b_alpha",
            "gla_norm_g", "w_branch", "w_out")


PACK_ROW_UNIT = 16
BIG_PIECE = PACK_ROW_UNIT * PACK_COLS


def _rows_of(a, nlead):
    lead = a.shape[:nlead]
    n = math.prod(a.shape[nlead:])
    if n % PACK_COLS:
        flat = a.reshape(lead + (n,))
        flat = jnp.pad(flat, [(0, 0)] * nlead + [(0, -n % PACK_COLS)])
        n += -n % PACK_COLS
        a = flat
    rows = a.reshape(lead + (n // PACK_COLS, PACK_COLS))
    return jnp.pad(rows, [(0, 0)] * nlead + [(0, -rows.shape[nlead] % PACK_ROW_UNIT), (0, 0)])


def _padded_rows(n):
    rows = -(-n // PACK_COLS)
    return -(-rows // PACK_ROW_UNIT) * PACK_ROW_UNIT


def _is_big(shape, exact):
    return not exact and math.prod(shape) >= BIG_PIECE


def _pack_weights(shards, l):
    big, small = [], []
    for name, (_, exact) in _SHARDED.items():
        a = _storage(name, shards[name][l])
        if _is_big(a.shape, exact):
            big.append(_rows_of(a.astype(BF16), 0))
        elif exact:
            small.append(lax.bitcast_convert_type(a, BF16).reshape(-1))
        else:
            small.append(a.astype(BF16).reshape(-1))
    return jnp.concatenate(big + [_rows_of(jnp.concatenate(small), 0)], axis=0)


def _unpack_weights(gathered, shards, l):
    full, r0 = {}, 0

    def merge(seg, axis, shp):
        seg = jnp.moveaxis(seg, 0, axis)
        return seg.reshape(shp[:axis] + (N_DEV * shp[axis],) + shp[axis + 1:])

    def shape_axis(name):
        shp = jax.eval_shape(lambda a: _storage(name, a), shards[name][l]).shape
        return shp, _storage_axis(name, shp)

    for name, (_, exact) in _SHARDED.items():
        shp, axis = shape_axis(name)
        if _is_big(shp, exact):
            n = math.prod(shp)
            nr = _padded_rows(n)
            seg = gathered[:, r0:r0 + nr].reshape(N_DEV, -1)[:, :n] if n % PACK_COLS else \
                gathered[:, r0:r0 + n // PACK_COLS]
            full[name] = merge(seg.reshape((N_DEV,) + shp), axis, shp)
            r0 += nr
    flat, off = gathered[:, r0:].reshape(N_DEV, -1), 0
    for name, (_, exact) in _SHARDED.items():
        shp, axis = shape_axis(name)
        if not _is_big(shp, exact):
            n = math.prod(shp) * (2 if exact else 1)
            seg = flat[:, off:off + n]
            off += n
            if exact:
                seg = lax.bitcast_convert_type(seg.reshape((N_DEV,) + shp + (2,)), F32)
            full[name] = merge(seg.reshape((N_DEV,) + shp), axis, shp)
    return full


def _pack_grads(full_grads, repl_grads):
    big, small = [], []
    for name in _SHARDED:
        g = full_grads[name].astype(BF16)
        shp = g.shape
        axis = _storage_axis(name, shp)
        g = g.reshape(shp[:axis] + (N_DEV, shp[axis] // N_DEV) + shp[axis + 1:])
        g = jnp.moveaxis(g, axis, 0)
        if _is_big(g.shape[1:], False):
            big.append(_rows_of(g, 1))
        else:
            small.append(g.reshape(N_DEV, -1))
    for name in _REPLICATED:
        small.append(jnp.broadcast_to(repl_grads[name].reshape(1, -1).astype(BF16),
                                      (N_DEV, repl_grads[name].size)))
    rows = jnp.concatenate(big + [_rows_of(jnp.concatenate(small, axis=1), 1)], axis=1)
    return rows.reshape((N_CHIPS, 2) + rows.shape[1:])


def _unpack_grads(summed, shards):
    out, r0 = {}, 0

    def storage_shape(name):
        return jax.eval_shape(lambda a: _storage(name, a), shards[name][0]).shape

    for name in _SHARDED:
        shp = storage_shape(name)
        if _is_big(shp, False):
            n = math.prod(shp)
            seg = summed[r0:r0 + _padded_rows(n)].reshape(-1)[:n] if n % PACK_COLS else \
                summed[r0:r0 + n // PACK_COLS]
            out[name] = _storage(name, seg.reshape(shp))
            r0 += _padded_rows(n)
    flat, off = summed[r0:].reshape(-1), 0
    for name in tuple(_SHARDED) + _REPLICATED:
        shp = shards[name].shape[1:] if name in _REPLICATED else storage_shape(name)
        if name in _REPLICATED or not _is_big(shp, False):
            n = math.prod(shp)
            seg = flat[off:off + n].reshape(shp)
            out[name] = seg if name in _REPLICATED else _storage(name, seg)
            off += n
    return out


def _adamw_natural(w, g, m, v, name):
    shp = w.shape
    view = lambda a: a.reshape(-1, shp[-1])
    outs = _adamw(view(w), view(g), view(m), view(v), name)
    return [o.reshape(shp) for o in outs]


def _layer_weights(full, repl, l, d_model):
    bw = BRANCH_WIDTH
    w_in = full["w_in"]
    o_conv, o_gq, o_lr = 3 * bw, 5 * bw, 5 * bw + 2 * GLA_KEY_DIM + 2 * GLA_VALUE_DIM
    o_gate = o_lr + GLA_GATE_RANK
    lr_pad = LANE - GLA_GATE_RANK
    in_gla = jnp.concatenate([w_in[o_gq:o_gate], jnp.zeros((lr_pad, d_model), w_in.dtype)], axis=0)
    alpha = jnp.concatenate([full["gla_w_alpha"], jnp.zeros((lr_pad, GLA_KEY_DIM), BF16)], axis=0)
    row = lambda a: a.reshape(1, -1)
    mix = dict(pre=row(full["norm_pre"][1]), post=row(full["norm_post"][1]),
               in_sb=w_in[0:o_conv], in_conv=w_in[o_conv:o_gq], in_gla=in_gla, in_gate=w_in[o_gate:],
               conv_w=full["conv_w"], conv_b=row(repl["conv_b"][l]), ln_g=row(repl["conv_ln_g"][l]),
               ln_b=row(repl["conv_ln_b"][l]), alpha=alpha, b_alpha=row(repl["gla_b_alpha"][l]),
               gnorm=row(repl["gla_norm_g"][l]), branch=full["w_branch"], out=full["w_out"])
    return dict(
        pre0=row(full["norm_pre"][0]), post0=row(full["norm_post"][0]),
        pre2=row(full["norm_pre"][2]), post2=row(full["norm_post"][2]),
        ffn1=dict(gate=full["ffn1_w_gate"], up=full["ffn1_w_up"], down=full["ffn1_w_down"]),
        ffn2=dict(gate=full["ffn2_w_gate"], up=full["ffn2_w_up"], down=full["ffn2_w_down"]),
        mix=mix)


def _full_grads(g):
    mix = g["mix"]
    kd, vd = GLA_KEY_DIM, GLA_VALUE_DIM
    d_in = jnp.concatenate([mix["in_sb"], mix["in_conv"], mix["in_gla"][:2 * kd + 2 * vd + GLA_GATE_RANK],
                            mix["in_gate"]], axis=0)
    full = {
        "norm_pre": jnp.concatenate([g["ffn1"]["pre"], mix["pre"], g["ffn2"]["pre"]], axis=0),
        "norm_post": jnp.concatenate([g["ffn1"]["post"], mix["post"], g["ffn2"]["post"]], axis=0),
        "ffn1_w_gate": g["ffn1"]["gate"], "ffn1_w_up": g["ffn1"]["up"], "ffn1_w_down": g["ffn1"]["down"],
        "ffn2_w_gate": g["ffn2"]["gate"], "ffn2_w_up": g["ffn2"]["up"], "ffn2_w_down": g["ffn2"]["down"],
        "w_in": d_in, "conv_w": mix["conv_w"], "gla_w_alpha": mix["alpha"][:GLA_GATE_RANK],
        "w_branch": mix["branch"], "w_out": mix["out"],
    }
    repl = {"conv_b": mix["conv_b"], "conv_ln_g": mix["ln_g"], "conv_ln_b": mix["ln_b"],
            "gla_b_alpha": mix["b_alpha"], "gla_norm_g": mix["gnorm"]}
    return full, repl


def kernel(x, norm_pre, norm_post, ffn1_w_gate, ffn1_w_up, ffn1_w_down, ffn2_w_gate, ffn2_w_up, ffn2_w_down, w_in, conv_w, conv_b, conv_ln_g, conv_ln_b, gla_w_alpha, gla_b_alpha, gla_norm_g, w_branch, w_out, loss_target, m_norm_pre, m_norm_post, m_ffn1_w_gate, m_ffn1_w_up, m_ffn1_w_down, m_ffn2_w_gate, m_ffn2_w_up, m_ffn2_w_down, m_w_in, m_conv_w, m_conv_b, m_conv_ln_g, m_conv_ln_b, m_gla_w_alpha, m_gla_b_alpha, m_gla_norm_g, m_w_branch, m_w_out, v_norm_pre, v_norm_post, v_ffn1_w_gate, v_ffn1_w_up, v_ffn1_w_down, v_ffn2_w_gate, v_ffn2_w_up, v_ffn2_w_down, v_w_in, v_conv_w, v_conv_b, v_conv_ln_g, v_conv_ln_b, v_gla_w_alpha, v_gla_b_alpha, v_gla_norm_g, v_w_branch, v_w_out):
    weights = dict(norm_pre=norm_pre, norm_post=norm_post, ffn1_w_gate=ffn1_w_gate, ffn1_w_up=ffn1_w_up,
                   ffn1_w_down=ffn1_w_down, ffn2_w_gate=ffn2_w_gate, ffn2_w_up=ffn2_w_up, ffn2_w_down=ffn2_w_down,
                   w_in=w_in, conv_w=conv_w, conv_b=conv_b, conv_ln_g=conv_ln_g, conv_ln_b=conv_ln_b,
                   gla_w_alpha=gla_w_alpha, gla_b_alpha=gla_b_alpha, gla_norm_g=gla_norm_g, w_branch=w_branch,
                   w_out=w_out)
    moments_m = dict(norm_pre=m_norm_pre, norm_post=m_norm_post, ffn1_w_gate=m_ffn1_w_gate, ffn1_w_up=m_ffn1_w_up,
                     ffn1_w_down=m_ffn1_w_down, ffn2_w_gate=m_ffn2_w_gate, ffn2_w_up=m_ffn2_w_up,
                     ffn2_w_down=m_ffn2_w_down, w_in=m_w_in, conv_w=m_conv_w, conv_b=m_conv_b,
                     conv_ln_g=m_conv_ln_g, conv_ln_b=m_conv_ln_b, gla_w_alpha=m_gla_w_alpha,
                     gla_b_alpha=m_gla_b_alpha, gla_norm_g=m_gla_norm_g, w_branch=m_w_branch, w_out=m_w_out)
    moments_v = dict(norm_pre=v_norm_pre, norm_post=v_norm_post, ffn1_w_gate=v_ffn1_w_gate, ffn1_w_up=v_ffn1_w_up,
                     ffn1_w_down=v_ffn1_w_down, ffn2_w_gate=v_ffn2_w_gate, ffn2_w_up=v_ffn2_w_up,
                     ffn2_w_down=v_ffn2_w_down, w_in=v_w_in, conv_w=v_conv_w, conv_b=v_conv_b,
                     conv_ln_g=v_conv_ln_g, conv_ln_b=v_conv_ln_b, gla_w_alpha=v_gla_w_alpha,
                     gla_b_alpha=v_gla_b_alpha, gla_norm_g=v_gla_norm_g, w_branch=v_w_branch, w_out=v_w_out)
    n_layers = norm_pre.shape[0]
    t, d_model = x.shape[1], x.shape[2]

    layers = []
    for l in range(n_layers):
        gathered = _all_gather(_pack_weights(weights, l), f"gather_l{l}")
        full = _unpack_weights(gathered, weights, l)
        layers.append(_layer_weights(full, weights, l, d_model))

    sq, dx, grads = _local_step(x[0], loss_target[0], layers)
    loss = lax.psum(0.5 * jnp.sum(sq) / d_model, MESH_AXES)

    layer_grads = []
    for l in range(n_layers):
        packed = _pack_grads(*_full_grads(grads[l]))
        got = _exchange_sibling(packed, f"rs_sibling_l{l}")
        part = _add_own(packed, got, f"rs_add_l{l}")
        parts = _exchange_chips(part, f"rs_chips_l{l}")
        layer_grads.append(_unpack_grads(_sum_chips(parts, f"rs_sum_l{l}"), weights))

    g_out, d_out, m_out, v_out = [], [], [], []
    for name in _WEIGHTS:
        g = jnp.stack([lg[name] for lg in layer_grads])
        delta, new_m, new_v = _adamw_natural(weights[name], g, moments_m[name], moments_v[name], f"adamw_{name}")
        g_out.append(g)
        d_out.append(delta)
        m_out.append(new_m)
        v_out.append(new_v)
    return tuple([loss, dx[None]] + g_out + d_out + m_out + v_out)
```

```python
import functools
import math

import jax
import jax.numpy as jnp
from jax import lax
from jax.experimental import pallas as pl
from jax.experimental.pallas import tpu as pltpu

F32 = jnp.float32
BF16 = jnp.bfloat16

VMEM_LIMIT_BYTES = 48 * 1024 * 1024
LANE = 128

NORM_EPS = 1e-6
CHUNK = 64
N_BRANCHES = 3
BRANCH_WIDTH = 512
SB_HEADS = 8
SB_HEAD_DIM = 64
CONV_WIDTH = 31
CONV_HALO = 32
GLA_HEADS = 4
GLA_HEAD_K = 64
GLA_HEAD_V = 128
GLA_KEY_DIM = GLA_HEADS * GLA_HEAD_K
GLA_VALUE_DIM = GLA_HEADS * GLA_HEAD_V
GLA_GATE_RANK = 16
GLA_GATE_TAU = 16.0
SB_BQ = 1024
SB_BK = 128

ADAM_LR = 0.001
ADAM_B1 = 0.9
ADAM_B2 = 0.999
ADAM_EPS = 1e-08
ADAM_WD = 0.01
ADAM_STEP = 10

N_DEV = 8
MESH_AXES = ("x", "y", "c")
PACK_COLS = 1024


def _params(sem):
    return pltpu.CompilerParams(dimension_semantics=sem, vmem_limit_bytes=VMEM_LIMIT_BYTES)


def _div_tile(n, cap, unit):
    if n <= cap:
        return n
    best = None
    for t in range(unit, cap + 1, unit):
        if n % t == 0:
            best = t
    assert best is not None, (n, cap, unit)
    return best


_NN = (((1,), (0,)), ((), ()))
_NT = (((1,), (1,)), ((), ()))
_TN = (((0,), (0,)), ((), ()))


def _dot(a, b, dims=_NN):
    return lax.dot_general(a, b, dims, preferred_element_type=F32)


def _sigmoid(v):
    return 1.0 / (1.0 + jnp.exp(-v))


def _mm(a, b, *, nt=False, tn=False, out_dtype=F32, name):
    a = a.astype(BF16)
    b = b.astype(BF16)
    k, m = a.shape[::-1] if not tn else a.shape
    n = b.shape[0] if nt else b.shape[1]
    assert (b.shape[1] if nt else b.shape[0]) == k
    tm = _div_tile(m, 2048, LANE) if tn else _div_tile(m, 512, 16)
    tn_ = _div_tile(n, 2048, LANE)
    tk = _div_tile(k, 2048, LANE)
    nk = k // tk
    dims = _NT if nt else (_TN if tn else _NN)

    def kern(a_ref, b_ref, o_ref, acc_ref):
        kk = pl.program_id(2)

        @pl.when(kk == 0)
        def _():
            acc_ref[...] = jnp.zeros_like(acc_ref)

        acc_ref[...] += _dot(a_ref[...], b_ref[...], dims)

        @pl.when(kk == nk - 1)
        def _():
            o_ref[...] = acc_ref[...].astype(o_ref.dtype)

    b_spec = (pl.BlockSpec((tn_, tk), lambda i, j, kk: (j, kk)) if nt
              else pl.BlockSpec((tk, tn_), lambda i, j, kk: (kk, j)))
    return pl.pallas_call(
        kern, name=name,
        out_shape=jax.ShapeDtypeStruct((m, n), out_dtype),
        grid=(m // tm, n // tn_, nk),
        in_specs=[pl.BlockSpec((tk, tm), lambda i, j, kk: (kk, i)) if tn
                  else pl.BlockSpec((tm, tk), lambda i, j, kk: (i, kk)), b_spec],
        out_specs=pl.BlockSpec((tm, tn_), lambda i, j, kk: (i, j)),
        scratch_shapes=[pltpu.VMEM((tm, tn_), F32)],
        compiler_params=_params(("parallel", "parallel", "arbitrary")),
    )(a, b)


def _rowwise(name, body, mats, vecs, outs, sums=(), tm=256):
    mats = [m if isinstance(m, tuple) else (m, 0, m.shape[1]) for m in mats]
    t = mats[0][0].shape[0]
    tm = _div_tile(t, tm, 8)
    nm, nv, no, ns = len(mats), len(vecs), len(outs), len(sums)

    def kern(*refs):
        i = pl.program_id(0)
        ins = [r[...] for r in refs[:nm + nv]]
        res = body(*ins)
        out_vals, sum_vals = res[:no], res[no:]
        for r, val in zip(refs[nm + nv:nm + nv + no], out_vals):
            if isinstance(val, (list, tuple)):
                off = 0
                for piece in val:
                    w = piece.shape[1]
                    r[:, off:off + w] = piece.astype(r.dtype)
                    off += w
            else:
                r[...] = val.astype(r.dtype)
        if ns:
            sum_refs = refs[nm + nv + no:]

            @pl.when(i == 0)
            def _():
                for r in sum_refs:
                    r[...] = jnp.zeros_like(r)

            for r, val in zip(sum_refs, sum_vals):
                r[...] += jnp.sum(val, axis=0, keepdims=True)

    in_specs = [pl.BlockSpec((tm, w), functools.partial(lambda i, cb: (i, cb), cb=cb)) for (_, cb, w) in mats]
    in_specs += [pl.BlockSpec(v.shape, lambda i: (0, 0)) for v in vecs]
    out_specs = [pl.BlockSpec((tm, w), lambda i: (i, 0)) for (w, _) in outs]
    out_specs += [pl.BlockSpec((1, w), lambda i: (0, 0)) for w in sums]
    out_shape = [jax.ShapeDtypeStruct((t, w), dt) for (w, dt) in outs]
    out_shape += [jax.ShapeDtypeStruct((1, w), F32) for w in sums]
    return pl.pallas_call(
        kern, name=name, out_shape=out_shape, grid=(t // tm,),
        in_specs=in_specs, out_specs=out_specs,
        compiler_params=_params(("arbitrary",)),
    )(*[m[0] for m in mats], *vecs)


def _rms_fwd(x, g, name):
    d = x.shape[1]

    def body(xv, gv):
        r = lax.rsqrt(jnp.mean(xv * xv, axis=-1, keepdims=True) + NORM_EPS)
        return ((xv * r) * gv,)

    return _rowwise(name, body, [x], [g], [(d, BF16)])[0]


def _post_res(o, x, g, c, name):
    d = x.shape[1]

    def body(ov, xv, gv):
        r = lax.rsqrt(jnp.mean(ov * ov, axis=-1, keepdims=True) + NORM_EPS)
        return (xv + c * ((ov * r) * gv),)

    return _rowwise(name, body, [o, x], [g], [(d, F32)])[0]


def _post_bwd(dx, o, g, c, name):
    d = dx.shape[1]

    def body(dxv, ov, gv):
        r = lax.rsqrt(jnp.mean(ov * ov, axis=-1, keepdims=True) + NORM_EPS)
        n = ov * r
        dy = c * dxv
        dn = dy * gv
        do = r * (dn - n * jnp.mean(dn * n, axis=-1, keepdims=True))
        return (do, dy * n)

    return _rowwise(name, body, [dx, o], [g], [(d, BF16)], sums=[d])


def _rms_bwd(dh_parts, x, g, dx_res, name):
    d = x.shape[1]
    npart = len(dh_parts)

    def body(*vals):
        dh = vals[0]
        for p in vals[1:npart]:
            dh = dh + p
        xv, dres, gv = vals[npart], vals[npart + 1], vals[npart + 2]
        r = lax.rsqrt(jnp.mean(xv * xv, axis=-1, keepdims=True) + NORM_EPS)
        n = xv * r
        dn = dh * gv
        dx = dres + r * (dn - n * jnp.mean(dn * n, axis=-1, keepdims=True))
        return (dx, dh * n)

    return _rowwise(name, body, list(dh_parts) + [x, dx_res], [g], [(d, F32)], sums=[d], tm=128)


def _swiglu_fwd(gate, up, name):
    f = gate.shape[1]

    def body(gv, uv):
        gv = gv.astype(F32)
        uv = uv.astype(F32)
        return ((gv * _sigmoid(gv)) * uv,)

    return _rowwise(name, body, [gate, up], [], [(f, BF16)])[0]


def _swiglu_bwd(da, gate, up, name):
    f = gate.shape[1]

    def body(dav, gv, uv):
        gv = gv.astype(F32)
        uv = uv.astype(F32)
        s = _sigmoid(gv)
        silu = gv * s
        dgate = dav * uv * (s * (1.0 + gv * (1.0 - s)))
        dup = dav * silu
        return (dgate, dup)

    return _rowwise(name, body, [da, gate, up], [], [(f, BF16), (f, BF16)])


def _merge_fwd(bds, logits, name):
    d = bds[0].shape[1]

    def body(b0, b1, b2, l0, l1, l2):
        return (_sigmoid(l0) * b0 + _sigmoid(l1) * b1 + _sigmoid(l2) * b2,)

    mats = list(bds) + [(logits, j, d) for j in range(N_BRANCHES)]
    return _rowwise(name, body, mats, [], [(d, BF16)], tm=128)[0]


def _merge_bwd(dmerged, bds, logits, name):
    d = bds[0].shape[1]

    def body(dm, b0, b1, b2, l0, l1, l2):
        dbs, dls = [], []
        for b, l in ((b0, l0), (b1, l1), (b2, l2)):
            s = _sigmoid(l)
            dbs.append(dm * s)
            dls.append(dm * b * (s * (1.0 - s)))
        return (dbs[0], dbs[1], dbs[2], dls)

    mats = [dmerged] + list(bds) + [(logits, j, d) for j in range(N_BRANCHES)]
    outs = [(d, BF16)] * 3 + [(N_BRANCHES * d, BF16)]
    return _rowwise(name, body, mats, [], outs, tm=128)


def _la_fwd(pre, b, name):
    w = pre.shape[1]

    def body(pv, bv):
        p = pv + bv
        sp = jnp.maximum(-p, 0.0) + jnp.log(1.0 + jnp.exp(-jnp.abs(p)))
        return (-sp / GLA_GATE_TAU,)

    return _rowwise(name, body, [pre], [b], [(w, F32)])[0]


def _la_bwd(dla, pre, b, name):
    w = pre.shape[1]

    def body(dv, pv, bv):
        p = pv + bv
        dpre = (dv / GLA_GATE_TAU) * _sigmoid(-p)
        return (dpre, dpre)

    return _rowwise(name, body, [dla, pre], [b], [(w, BF16)], sums=[w])


def _gla_post_fwd(o, p_gla, r_block, gn, name):
    w = o.shape[1]

    def body(ov, rv, gv):
        pieces = []
        for h in range(GLA_HEADS):
            sl = slice(h * GLA_HEAD_V, (h + 1) * GLA_HEAD_V)
            oh = ov[:, sl]
            rr = lax.rsqrt(jnp.mean(oh * oh, axis=-1, keepdims=True) + NORM_EPS)
            rh = rv[:, sl]
            pieces.append(((oh * rr) * gv[:, sl]) * (rh * _sigmoid(rh)))
        return (pieces,)

    return _rowwise(name, body, [o, (p_gla, r_block, w)], [gn], [(w, BF16)])[0]


def _gla_post_bwd(dout, o, p_gla, r_block, gn, name):
    w = o.shape[1]

    def body(dv, ov, rv, gv):
        dos, drs, dgs = [], [], []
        for h in range(GLA_HEADS):
            sl = slice(h * GLA_HEAD_V, (h + 1) * GLA_HEAD_V)
            oh, rh, gh, dh = ov[:, sl], rv[:, sl], gv[:, sl], dv[:, sl]
            rr = lax.rsqrt(jnp.mean(oh * oh, axis=-1, keepdims=True) + NORM_EPS)
            nhat = oh * rr
            s = _sigmoid(rh)
            dn = dh * (rh * s)
            drs.append(dh * (nhat * gh) * (s * (1.0 + rh * (1.0 - s))))
            dnn = dn * gh
            dos.append(rr * (dnn - nhat * jnp.mean(dnn * nhat, axis=-1, keepdims=True)))
            dgs.append(dn * nhat)
        return (dos, drs, jnp.concatenate(dgs, axis=1))

    return _rowwise(name, body, [dout, o, (p_gla, r_block, w)], [gn], [(w, F32), (w, BF16)], sums=[w])


def _conv_bwd1(dout, y, ln_g, ln_b, name):
    w = y.shape[1]

    def body(dv, yv, gv, bv):
        mu = jnp.mean(yv, axis=-1, keepdims=True)
        yc = yv - mu
        rstd = lax.rsqrt(jnp.mean(yc * yc, axis=-1, keepdims=True) + NORM_EPS)
        xhat = yc * rstd
        yn = xhat * gv + bv
        s = _sigmoid(yn)
        dyn = dv * (s * (1.0 + yn * (1.0 - s)))
        dxh = dyn * gv
        dy = rstd * (dxh - jnp.mean(dxh, axis=-1, keepdims=True)
                     - xhat * jnp.mean(dxh * xhat, axis=-1, keepdims=True))
        return (dy, dyn * xhat, dyn, dy)

    return _rowwise(name, body, [dout, y], [ln_g, ln_b], [(w, F32)], sums=[w, w, w])


def _loss_fwd(y, target, name):
    d = y.shape[1]

    def body(yv, tv):
        e = yv - tv
        return (e / d, e * e)

    return _rowwise(name, body, [y, target], [], [(d, F32)], sums=[d])


def _adamw(w, g, m, v, name):
    cols = w.shape[1]
    c1 = 1.0 - ADAM_B1 ** ADAM_STEP
    c2 = 1.0 - ADAM_B2 ** ADAM_STEP

    def body(wv, gv, mv, vv):
        m2 = ADAM_B1 * mv + (1.0 - ADAM_B1) * gv
        v2 = ADAM_B2 * vv + (1.0 - ADAM_B2) * (gv * gv)
        m_hat = m2 / c1
        v_hat = v2 / c2
        delta = -ADAM_LR * (m_hat / (jnp.sqrt(v_hat) + ADAM_EPS) + ADAM_WD * wv)
        return (delta, m2, v2)

    return _rowwise(name, body, [w, g, m, v], [], [(cols, F32)] * 3)


def _split_bf16(v):
    hi = v.astype(BF16)
    lo = (v - hi.astype(F32)).astype(BF16)
    return hi, lo


def _tri(n, strict):
    r = lax.broadcasted_iota(jnp.int32, (n, n), 0)
    c = lax.broadcasted_iota(jnp.int32, (n, n), 1)
    return jnp.where(r > c if strict else r >= c, 1.0, 0.0).astype(BF16)


def _sb_weights(z, c, mx, diag_offset):
    u = jnp.exp(-jnp.abs(z))
    sp = jnp.maximum(z, 0.0) + jnp.log(1.0 + u)
    if diag_offset is None:
        mask = None
        spm = sp
    else:
        rows = lax.broadcasted_iota(jnp.int32, z.shape, 0)
        cols = lax.broadcasted_iota(jnp.int32, z.shape, 1)
        mask = cols + diag_offset < rows
        spm = jnp.where(mask, sp, 0.0)
    suf = _dot(spm.astype(BF16), mx)
    w = jnp.exp(((z - sp) - suf) - c)
    if mask is not None:
        w = jnp.where(mask, w, 0.0)
    return sp, w, mask, suf[:, 0:1] + spm[:, 0:1]


def _sb_fwd(q, kt, vt, name):
    nh, t, dh = q.shape
    nkb, bk = kt.shape[1], kt.shape[3]
    bq = min(SB_BQ, t)
    scale = dh ** -0.5
    per = bq // bk

    def kern(q_ref, kt_ref, vt_ref, o_ref, c_ref, zbuf, wbuf):
        qi = pl.program_id(1)
        top = (qi + 1) * per - 1
        ntiles = (qi + 1) * per
        qs = (q_ref[...].astype(F32) * scale).astype(BF16)
        mx = _tri(bk, True)
        o_ref[...] = jnp.zeros_like(o_ref)
        c_ref[...] = jnp.zeros_like(c_ref)
        zbuf[0] = _dot(qs, kt_ref[top])
        wbuf[0] = jnp.zeros((bq, bk), BF16)

        def tile(i, diag_offset):
            kb = top - i
            p = i % 2
            z = zbuf[p]
            w_prev = wbuf[p]
            c = c_ref[...]
            zbuf[1 - p] = _dot(qs, kt_ref[jnp.maximum(kb - 1, 0)])
            o_ref[...] += _dot(w_prev, vt_ref[jnp.minimum(kb + 1, nkb - 1)], _NT)
            _, w, _, tot = _sb_weights(z, c, mx, diag_offset)
            wbuf[1 - p] = w.astype(BF16)
            c_ref[...] = c + tot

        def diag_step(i, carry):
            tile(i, (per - 1 - i) * bk)
            return carry

        def step(i, carry):
            tile(i, None)
            return carry

        lax.fori_loop(0, per, diag_step, 0)
        lax.fori_loop(per, ntiles, step, 0)
        o_ref[...] += _dot(wbuf[ntiles % 2], vt_ref[0], _NT)

    return pl.pallas_call(
        kern, name=name,
        out_shape=jax.ShapeDtypeStruct((nh, t, dh), F32),
        grid=(nh, t // bq),
        in_specs=[pl.BlockSpec((None, bq, dh), lambda h, i: (h, i, 0)),
                  pl.BlockSpec((None, nkb, dh, bk), lambda h, i: (h, 0, 0, 0)),
                  pl.BlockSpec((None, nkb, dh, bk), lambda h, i: (h, 0, 0, 0))],
        out_specs=pl.BlockSpec((None, bq, dh), lambda h, i: (h, i, 0)),
        scratch_shapes=[pltpu.VMEM((bq, 1), F32), pltpu.VMEM((2, bq, bk), F32), pltpu.VMEM((2, bq, bk), BF16)],
        compiler_params=_params(("parallel", "arbitrary")),
    )(q, kt, vt)


def _sb_bwd(q, qt, kt, vt, out, dout, doutt, name):
    nh, t, dh = q.shape
    nkb, bk = kt.shape[1], kt.shape[3]
    bq = min(SB_BQ, t)
    scale = dh ** -0.5
    per = bq // bk

    def kern(q_ref, qt_ref, kt_ref, vt_ref, o_ref, do_ref, dot_ref, dq_ref, dkt_ref, dvt_ref, c_ref, ce_ref):
        qi = pl.program_id(1)
        top = (qi + 1) * per - 1

        @pl.when(qi == 0)
        def _():
            dkt_ref[...] = jnp.zeros_like(dkt_ref)
            dvt_ref[...] = jnp.zeros_like(dvt_ref)

        qs = (q_ref[...].astype(F32) * scale).astype(BF16)
        qts = (qt_ref[...].astype(F32) * scale).astype(BF16)
        dob = do_ref[...].astype(BF16)
        dotv = dot_ref[...]
        dsum = jnp.sum(dob.astype(F32) * o_ref[...], axis=1, keepdims=True)
        mx = _tri(bk, True)
        mi = _tri(bk, False)
        dq_ref[...] = jnp.zeros_like(dq_ref)
        c_ref[...] = jnp.zeros_like(c_ref)
        ce_ref[...] = jnp.zeros_like(ce_ref)

        def tile(kb, diag_offset):
            kt_blk = kt_ref[kb]
            z = _dot(qs, kt_blk)
            c = c_ref[...]
            sp, w, mask, tot = _sb_weights(z, c, mx, diag_offset)
            wb = w.astype(BF16)
            e = _dot(dob, vt_ref[kb]) * wb.astype(F32)
            before = dsum - (_dot(e.astype(BF16), mi) + ce_ref[...])
            sig = jnp.exp(z - sp)
            dz = e - sig * (e + before)
            if mask is not None:
                dz = jnp.where(mask, dz, 0.0)
            dz = dz.astype(BF16)
            dq_ref[...] += _dot(dz, kt_blk, _NT)
            dkt_ref[kb] += _dot(qts, dz)
            dvt_ref[kb] += _dot(dotv, wb)
            c_ref[...] = c + tot
            ce_ref[...] += jnp.sum(e, axis=1, keepdims=True)

        def diag_step(i, carry):
            tile(top - i, (per - 1 - i) * bk)
            return carry

        def step(i, carry):
            tile(top - i, None)
            return carry

        lax.fori_loop(0, per, diag_step, 0)
        lax.fori_loop(per, (qi + 1) * per, step, 0)
        dq_ref[...] = dq_ref[...] * scale

    row = pl.BlockSpec((None, bq, dh), lambda h, i: (h, i, 0))
    col = pl.BlockSpec((None, dh, bq), lambda h, i: (h, 0, i))
    whole = pl.BlockSpec((None, nkb, dh, bk), lambda h, i: (h, 0, 0, 0))
    return pl.pallas_call(
        kern, name=name,
        out_shape=[jax.ShapeDtypeStruct((nh, t, dh), F32),
                   jax.ShapeDtypeStruct((nh, nkb, dh, bk), F32),
                   jax.ShapeDtypeStruct((nh, nkb, dh, bk), F32)],
        grid=(nh, t // bq),
        in_specs=[row, col, whole, whole, row, row, col],
        out_specs=[row, whole, whole],
        scratch_shapes=[pltpu.VMEM((bq, 1), F32), pltpu.VMEM((bq, 1), F32)],
        compiler_params=_params(("parallel", "arbitrary")),
    )(q, qt, kt, vt, out, dout, doutt)


def _conv_fwd(p_conv, conv_w, conv_b, ln_g, ln_b, name):
    t, c2 = p_conv.shape
    c = c2 // 2
    tm = min(256, t)
    hb = tm // CONV_HALO

    def kern(a_ref, g_ref, ah_ref, gh_ref, w_ref, b_ref, lg_ref, lb_ref, o_ref, u_ref, y_ref, ubuf):
        i = pl.program_id(0)
        u = a_ref[...] * _sigmoid(g_ref[...])
        uh = ah_ref[...] * _sigmoid(gh_ref[...])
        ubuf[0:CONV_HALO, :] = jnp.where(i > 0, uh, 0.0)
        ubuf[CONV_HALO:CONV_HALO + tm, :] = u
        y = jnp.zeros((tm, c), F32) + b_ref[...]
        for j in range(CONV_WIDTH):
            off = CONV_HALO - (CONV_WIDTH - 1) + j
            y = y + ubuf[off:off + tm, :] * w_ref[j:j + 1, :]
        mu = jnp.mean(y, axis=-1, keepdims=True)
        yc = y - mu
        rstd = lax.rsqrt(jnp.mean(yc * yc, axis=-1, keepdims=True) + NORM_EPS)
        yn = (yc * rstd) * lg_ref[...] + lb_ref[...]
        o_ref[...] = (yn * _sigmoid(yn)).astype(o_ref.dtype)
        u_ref[...] = u
        y_ref[...] = y

    def halo(cb):
        return pl.BlockSpec((CONV_HALO, c), lambda i: (jnp.maximum(i * hb - 1, 0), cb))

    vec = pl.BlockSpec((1, c), lambda i: (0, 0))
    tile = pl.BlockSpec((tm, c), lambda i: (i, 0))
    return pl.pallas_call(
        kern, name=name,
        out_shape=[jax.ShapeDtypeStruct((t, c), BF16), jax.ShapeDtypeStruct((t, c), F32),
                   jax.ShapeDtypeStruct((t, c), F32)],
        grid=(t // tm,),
        in_specs=[tile, pl.BlockSpec((tm, c), lambda i: (i, 1)), halo(0), halo(1),
                  pl.BlockSpec(conv_w.shape, lambda i: (0, 0)), vec, vec, vec],
        out_specs=[tile, tile, tile],
        scratch_shapes=[pltpu.VMEM((tm + CONV_HALO, c), F32)],
        compiler_params=_params(("arbitrary",)),
    )(p_conv, p_conv, p_conv, p_conv, conv_w, conv_b, ln_g, ln_b)


def _conv_bwd2(dy, u, p_conv, conv_w, name):
    t, c = dy.shape
    tm = min(256, t)
    hb = tm // CONV_HALO
    nt = t // tm
    last_halo = t // CONV_HALO - 1

    def kern(dy_ref, dyh_ref, u_ref, uh_ref, a_ref, g_ref, w_ref, dp_ref, dw_ref, dybuf, ubuf):
        i = pl.program_id(0)

        @pl.when(i == 0)
        def _():
            dw_ref[...] = jnp.zeros_like(dw_ref)

        dyv = dy_ref[...]
        dybuf[0:tm, :] = dyv
        dybuf[tm:tm + CONV_HALO, :] = jnp.where(i < nt - 1, dyh_ref[...], 0.0)
        ubuf[0:CONV_HALO, :] = jnp.where(i > 0, uh_ref[...], 0.0)
        ubuf[CONV_HALO:CONV_HALO + tm, :] = u_ref[...]
        du = jnp.zeros((tm, c), F32)
        for j in range(CONV_WIDTH):
            off = CONV_WIDTH - 1 - j
            du = du + dybuf[off:off + tm, :] * w_ref[j:j + 1, :]
            uoff = CONV_HALO - (CONV_WIDTH - 1) + j
            dw_ref[j:j + 1, :] += jnp.sum(dyv * ubuf[uoff:uoff + tm, :], axis=0, keepdims=True)
        a = a_ref[...]
        s = _sigmoid(g_ref[...])
        dp_ref[:, 0:c] = (du * s).astype(dp_ref.dtype)
        dp_ref[:, c:2 * c] = (du * a * (s * (1.0 - s))).astype(dp_ref.dtype)

    tile = pl.BlockSpec((tm, c), lambda i: (i, 0))
    return pl.pallas_call(
        kern, name=name,
        out_shape=[jax.ShapeDtypeStruct((t, 2 * c), BF16), jax.ShapeDtypeStruct((CONV_HALO, c), F32)],
        grid=(nt,),
        in_specs=[tile,
                  pl.BlockSpec((CONV_HALO, c), lambda i: (jnp.minimum((i + 1) * hb, last_halo), 0)),
                  tile,
                  pl.BlockSpec((CONV_HALO, c), lambda i: (jnp.maximum(i * hb - 1, 0), 0)),
                  tile, pl.BlockSpec((tm, c), lambda i: (i, 1)),
                  pl.BlockSpec(conv_w.shape, lambda i: (0, 0))],
        out_specs=[pl.BlockSpec((tm, 2 * c), lambda i: (i, 0)),
                   pl.BlockSpec((CONV_HALO, c), lambda i: (0, 0))],
        scratch_shapes=[pltpu.VMEM((tm + CONV_HALO, c), F32), pltpu.VMEM((tm + CONV_HALO, c), F32)],
        compiler_params=_params(("arbitrary",)),
    )(dy, dy, u, u, p_conv, p_conv, conv_w)


GLA_CHUNKS_PER_STEP = 16


def _gla_fwd(q_ck, k_kc, la_kc, v, name):
    nh, nc, ch, dk = q_ck.shape
    dv = v.shape[3]
    qscale = dk ** -0.5
    cb = min(GLA_CHUNKS_PER_STEP, nc)

    def kern(q_ref, k_ref, la_ref, v_ref, o_ref, st_ref, state_ref):
        mx = _tri(ch, True)

        @pl.when(pl.program_id(1) == 0)
        def _():
            state_ref[...] = jnp.zeros_like(state_ref)

        def step(n, state):
            la = la_ref[n]
            hi, lo = _split_bf16(la)
            de = _dot(hi, mx) + _dot(lo, mx)
            lam = jnp.exp(jnp.sum(la, axis=1, keepdims=True))
            kd = (k_ref[n] * jnp.exp(de)).astype(BF16)
            state = lam * state + _dot(kd, v_ref[n])
            st_ref[n] = state
            qs = (q_ref[n].astype(F32) * qscale).astype(BF16)
            o_ref[n] = _dot(qs, state.astype(BF16))
            return state

        state_ref[...] = lax.fori_loop(0, cb, step, state_ref[...])

    def spec(a, b):
        return pl.BlockSpec((None, cb, a, b), lambda h, j: (h, j, 0, 0))

    return pl.pallas_call(
        kern, name=name,
        out_shape=[jax.ShapeDtypeStruct((nh, nc, ch, dv), F32), jax.ShapeDtypeStruct((nh, nc, dk, dv), F32)],
        grid=(nh, nc // cb),
        in_specs=[spec(ch, dk), spec(dk, ch), spec(dk, ch), spec(ch, dv)],
        out_specs=[spec(ch, dv), spec(dk, dv)],
        scratch_shapes=[pltpu.VMEM((dk, dv), F32)],
        compiler_params=_params(("parallel", "arbitrary")),
    )(q_ck, k_kc, la_kc, v)


def _gla_bwd(q_ck, q_kc, k_ck, k_kc, la_ck, la_kc, v, states, states_prev, do, name):
    nh, nc, ch, dk = q_ck.shape
    dv = v.shape[3]
    qscale = dk ** -0.5
    cb = min(GLA_CHUNKS_PER_STEP, nc)
    nb = nc // cb

    def kern(q_ref, qt_ref, kck_ref, kkc_ref, lack_ref, lakc_ref, v_ref, st_ref, stp_ref, do_ref,
             dq_ref, dk_ref, dv_ref, dla_ref, g_ref):
        mx = _tri(ch, True)
        mxt = jnp.where(lax.broadcasted_iota(jnp.int32, (ch, ch), 0)
                        < lax.broadcasted_iota(jnp.int32, (ch, ch), 1), 1.0, 0.0).astype(BF16)

        @pl.when(pl.program_id(1) == 0)
        def _():
            g_ref[...] = jnp.zeros_like(g_ref)

        def step(i, g):
            n = cb - 1 - i
            la_kc = lakc_ref[n]
            hi, lo = _split_bf16(la_kc)
            de_kc = _dot(hi, mx) + _dot(lo, mx)
            lam = jnp.exp(jnp.sum(la_kc, axis=1, keepdims=True))
            hi2, lo2 = _split_bf16(lack_ref[n])
            de_ck = _dot(mxt, hi2) + _dot(mxt, lo2)
            edk = jnp.exp(de_kc)
            kd_kc = kkc_ref[n] * edk
            kd_ck = (kck_ref[n] * jnp.exp(de_ck)).astype(BF16)
            dob = do_ref[n].astype(BF16)
            dq_ref[n] = _dot(dob, st_ref[n].astype(BF16), _NT) * qscale
            qts = (qt_ref[n].astype(F32) * qscale).astype(BF16)
            ds = _dot(qts, dob) + g
            dsb = ds.astype(BF16)
            dlam = jnp.sum(ds * stp_ref[n], axis=1, keepdims=True)
            dkd = _dot(dsb, v_ref[n], _NT)
            dv_ref[n] = _dot(kd_ck, dsb)
            dk_ref[n] = dkd * edk
            dde = dkd * kd_kc
            h3, l3 = _split_bf16(dde)
            dla_ref[n] = _dot(h3, mxt) + _dot(l3, mxt) + dlam * lam
            return lam * ds

        g_ref[...] = lax.fori_loop(0, cb, step, g_ref[...])

    def spec(a, b):
        return pl.BlockSpec((None, cb, a, b), lambda h, j: (h, nb - 1 - j, 0, 0))

    return pl.pallas_call(
        kern, name=name,
        out_shape=[jax.ShapeDtypeStruct((nh, nc, ch, dk), F32), jax.ShapeDtypeStruct((nh, nc, dk, ch), F32),
                   jax.ShapeDtypeStruct((nh, nc, ch, dv), F32), jax.ShapeDtypeStruct((nh, nc, dk, ch), F32)],
        grid=(nh, nb),
        in_specs=[spec(ch, dk), spec(dk, ch), spec(ch, dk), spec(dk, ch), spec(ch, dk), spec(dk, ch),
                  spec(ch, dv), spec(dk, dv), spec(dk, dv), spec(ch, dv)],
        out_specs=[spec(ch, dk), spec(dk, ch), spec(ch, dv), spec(dk, ch)],
        scratch_shapes=[pltpu.VMEM((dk, dv), F32)],
        compiler_params=_params(("parallel", "arbitrary")),
    )(q_ck, q_kc, k_ck, k_kc, la_ck, la_kc, v, states, states_prev, do)


def _heads_rows(a, nh, blk):
    t = a.shape[0]
    d = a.shape[1] // nh
    return a.reshape(t // blk, blk, nh, d).transpose(2, 0, 1, 3)


def _heads_cols(a, nh, blk):
    t = a.shape[0]
    d = a.shape[1] // nh
    return a.reshape(t // blk, blk, nh, d).transpose(2, 0, 3, 1)


def _unheads_rows(a):
    nh, nb, blk, d = a.shape
    return a.transpose(1, 2, 0, 3).reshape(nb * blk, nh * d)


def _unheads_cols(a):
    nh, nb, d, blk = a.shape
    return a.transpose(1, 3, 0, 2).reshape(nb * blk, nh * d)


def _ffn_fwd(x, w, pre, post, tag):
    h = _rms_fwd(x, pre, f"{tag}_rms")
    gate = _mm(h, w["gate"], nt=True, out_dtype=BF16, name=f"{tag}_gate")
    up = _mm(h, w["up"], nt=True, out_dtype=BF16, name=f"{tag}_up")
    act = _swiglu_fwd(gate, up, f"{tag}_act")
    o = _mm(act, w["down"], name=f"{tag}_down")
    x_out = _post_res(o, x, post, 0.5, f"{tag}_res")
    return x_out, (x, h, gate, up, act, o)


def _ffn_bwd(dx_out, saved, w, pre, post, tag):
    x, h, gate, up, act, o = saved
    do, dpost = _post_bwd(dx_out, o, post, 0.5, f"{tag}_bres")
    da = _mm(do, w["down"], nt=True, name=f"{tag}_bda")
    d_down = _mm(act, do, tn=True, out_dtype=BF16, name=f"{tag}_bwdown")
    dgate, dup = _swiglu_bwd(da, gate, up, f"{tag}_bact")
    dh_g = _mm(dgate, w["gate"], name=f"{tag}_bdhg")
    dh_u = _mm(dup, w["up"], name=f"{tag}_bdhu")
    d_gate = _mm(dgate, h, tn=True, out_dtype=BF16, name=f"{tag}_bwgate")
    d_up = _mm(dup, h, tn=True, out_dtype=BF16, name=f"{tag}_bwup")
    dx, dpre = _rms_bwd([dh_g, dh_u], x, pre, dx_out, f"{tag}_brms")
    return dx, dict(gate=d_gate, up=d_up, down=d_down, pre=dpre, post=dpost)


def _mixer_fwd(x, w, tag):
    t, d = x.shape
    bw = BRANCH_WIDTH
    h = _rms_fwd(x, w["pre"], f"{tag}_rms")
    p_sb = _mm(h, w["in_sb"], nt=True, out_dtype=BF16, name=f"{tag}_insb")
    p_conv = _mm(h, w["in_conv"], nt=True, name=f"{tag}_inconv")
    p_gla = _mm(h, w["in_gla"], nt=True, name=f"{tag}_ingla")
    p_gate = _mm(h, w["in_gate"], nt=True, name=f"{tag}_ingate")

    bk = min(SB_BK, t)
    q = p_sb[:, 0:bw].reshape(t, SB_HEADS, SB_HEAD_DIM).transpose(1, 0, 2)
    kt = _heads_cols(p_sb[:, bw:2 * bw], SB_HEADS, bk)
    vt = _heads_cols(p_sb[:, 2 * bw:3 * bw], SB_HEADS, bk)
    sb_o = _sb_fwd(q, kt, vt, f"{tag}_sb")
    sb_out = sb_o.transpose(1, 0, 2).reshape(t, bw)

    conv_out, conv_u, conv_y = _conv_fwd(p_conv, w["conv_w"], w["conv_b"], w["ln_g"], w["ln_b"], f"{tag}_conv")

    kd, vd = GLA_KEY_DIM, GLA_VALUE_DIM
    lr = p_gla[:, 2 * kd + 2 * vd:]
    pre_a = _mm(lr, w["alpha"], name=f"{tag}_alpha")
    la = _la_fwd(pre_a, w["b_alpha"], f"{tag}_la")
    gq = _heads_rows(p_gla[:, 0:kd].astype(BF16), GLA_HEADS, CHUNK)
    gk_kc = _heads_cols(p_gla[:, kd:2 * kd], GLA_HEADS, CHUNK)
    gv = _heads_rows(p_gla[:, 2 * kd:2 * kd + vd].astype(BF16), GLA_HEADS, CHUNK)
    la_kc = _heads_cols(la, GLA_HEADS, CHUNK)
    gla_o4, states = _gla_fwd(gq, gk_kc, la_kc, gv, f"{tag}_gla")
    gla_o = _unheads_rows(gla_o4)
    r_block = (2 * kd + vd) // vd
    gla_out = _gla_post_fwd(gla_o, p_gla, r_block, w["gnorm"], f"{tag}_glapost")

    branches = (sb_out.astype(BF16), conv_out, gla_out)
    bds = [_mm(branches[j], w["branch"][j], nt=True, name=f"{tag}_br{j}") for j in range(N_BRANCHES)]
    merged = _merge_fwd(bds, p_gate, f"{tag}_merge")
    mo = _mm(merged, w["out"], name=f"{tag}_out")
    x_out = _post_res(mo, x, w["post"], 1.0, f"{tag}_res")
    saved = dict(x=x, h=h, p_sb=p_sb, p_conv=p_conv, p_gla=p_gla, p_gate=p_gate, sb_o=sb_o, conv_u=conv_u,
                 sb_q=q, sb_kt=kt, sb_vt=vt, gla_q=gq, gla_k_kc=gk_kc, gla_v=gv, gla_la_kc=la_kc,
                 conv_y=conv_y, pre_a=pre_a, la=la, states=states, gla_o=gla_o, branches=branches, bds=bds,
                 merged=merged, mo=mo)
    return x_out, saved


def _mixer_bwd(dx_out, s, w, tag):
    x = s["x"]
    t, d = x.shape
    bw = BRANCH_WIDTH
    kd, vd = GLA_KEY_DIM, GLA_VALUE_DIM
    grads = {}
    dmo, grads["post"] = _post_bwd(dx_out, s["mo"], w["post"], 1.0, f"{tag}_bres")
    dmerged = _mm(dmo, w["out"], nt=True, name=f"{tag}_bdmerged")
    grads["out"] = _mm(s["merged"], dmo, tn=True, out_dtype=BF16, name=f"{tag}_bwout")
    dbd0, dbd1, dbd2, dp_gate = _merge_bwd(dmerged, s["bds"], s["p_gate"], f"{tag}_bmerge")
    dbds = (dbd0, dbd1, dbd2)
    dbranch = [_mm(dbds[j], w["branch"][j], name=f"{tag}_bdbr{j}") for j in range(N_BRANCHES)]
    grads["branch"] = jnp.stack([_mm(dbds[j], s["branches"][j], tn=True, out_dtype=BF16, name=f"{tag}_bwbr{j}")
                                 for j in range(N_BRANCHES)])

    p_sb = s["p_sb"]
    bk = min(SB_BK, t)
    q, kt, vt = s["sb_q"], s["sb_kt"], s["sb_vt"]
    qt = p_sb[:, 0:bw].reshape(t, SB_HEADS, SB_HEAD_DIM).transpose(1, 2, 0)
    dsb = dbranch[0].reshape(t, SB_HEADS, SB_HEAD_DIM)
    dout = dsb.transpose(1, 0, 2)
    doutt = dsb.transpose(1, 2, 0).astype(BF16)
    dq, dkt, dvt = _sb_bwd(q, qt, kt, vt, s["sb_o"], dout, doutt, f"{tag}_bsb")
    dp_sb = jnp.concatenate([dq.transpose(1, 0, 2).reshape(t, bw), _unheads_cols(dkt), _unheads_cols(dvt)],
                            axis=1).astype(BF16)

    dy, d_lng, d_lnb, d_cb = _conv_bwd1(dbranch[1], s["conv_y"], w["ln_g"], w["ln_b"], f"{tag}_bconv1")
    dp_conv, d_cw = _conv_bwd2(dy, s["conv_u"], s["p_conv"], w["conv_w"], f"{tag}_bconv2")
    grads.update(conv_w=d_cw[:CONV_WIDTH], conv_b=d_cb, ln_g=d_lng, ln_b=d_lnb)

    p_gla = s["p_gla"]
    r_block = (2 * kd + vd) // vd
    do, dr, grads["gnorm"] = _gla_post_bwd(dbranch[2], s["gla_o"], p_gla, r_block, w["gnorm"], f"{tag}_bglapost")
    gqf, gkf = p_gla[:, 0:kd], p_gla[:, kd:2 * kd]
    gq_b = gqf.astype(BF16)
    dq4, dk4, dv4, dla4 = _gla_bwd(
        s["gla_q"], _heads_cols(gq_b, GLA_HEADS, CHUNK),
        _heads_rows(gkf, GLA_HEADS, CHUNK), s["gla_k_kc"],
        _heads_rows(s["la"], GLA_HEADS, CHUNK), s["gla_la_kc"], s["gla_v"],
        s["states"], jnp.pad(s["states"][:, :-1], ((0, 0), (1, 0), (0, 0), (0, 0))),
        _heads_rows(do, GLA_HEADS, CHUNK), f"{tag}_bgla")
    dla = _unheads_cols(dla4)
    dpre_a, grads["b_alpha"] = _la_bwd(dla, s["pre_a"], w["b_alpha"], f"{tag}_bla")
    lr = p_gla[:, 2 * kd + 2 * vd:]
    dlr = _mm(dpre_a, w["alpha"], nt=True, out_dtype=BF16, name=f"{tag}_bdlr")
    grads["alpha"] = _mm(lr, dpre_a, tn=True, name=f"{tag}_bwalpha")
    dp_gla = jnp.concatenate([_unheads_rows(dq4).astype(BF16), _unheads_cols(dk4).astype(BF16),
                              _unheads_rows(dv4).astype(BF16), dr, dlr], axis=1)

    dps = dict(in_sb=dp_sb, in_conv=dp_conv, in_gla=dp_gla, in_gate=dp_gate)
    dh_parts = []
    for key, dp in dps.items():
        dh_parts.append(_mm(dp, w[key], name=f"{tag}_bdh_{key}"))
        grads[key] = _mm(dp, s["h"], tn=True, out_dtype=BF16, name=f"{tag}_bw_{key}")
    dx, grads["pre"] = _rms_bwd(dh_parts, x, w["pre"], dx_out, f"{tag}_brms")
    return dx, grads


def _local_step(x, target, layers):
    saved = []
    for l, w in enumerate(layers):
        x, s1 = _ffn_fwd(x, w["ffn1"], w["pre0"], w["post0"], f"l{l}_f1")
        x, s2 = _mixer_fwd(x, w["mix"], f"l{l}_mx")
        x, s3 = _ffn_fwd(x, w["ffn2"], w["pre2"], w["post2"], f"l{l}_f2")
        saved.append((s1, s2, s3))
    dx, sq = _loss_fwd(x, target, "loss")
    grads = [None] * len(layers)
    for l in reversed(range(len(layers))):
        w = layers[l]
        s1, s2, s3 = saved[l]
        dx, g3 = _ffn_bwd(dx, s3, w["ffn2"], w["pre2"], w["post2"], f"l{l}_f2")
        dx, g2 = _mixer_bwd(dx, s2, w["mix"], f"l{l}_mx")
        dx, g1 = _ffn_bwd(dx, s1, w["ffn1"], w["pre0"], w["post0"], f"l{l}_f1")
        grads[l] = dict(ffn1=g1, mix=g2, ffn2=g3)
    return sq, dx, grads


_ANY = pl.BlockSpec(memory_space=pl.ANY)
_MESH = pl.DeviceIdType.MESH
N_CHIPS = 4


def _all_gather(p, name):
    def body(p_ref, out_ref, send_sems, recv_sems, local_sem):
        x, y, c = lax.axis_index("x"), lax.axis_index("y"), lax.axis_index("c")
        me, sibling = (x, y, c), (x, y, 1 - c)
        chips = [(1 - x, y), (x, 1 - y), (1 - x, 1 - y)]

        def rows(px, py, pc):
            return out_ref.at[4 * px + 2 * py + pc]

        def copy(k, block, to, src=None):
            return pltpu.make_async_remote_copy(
                src_ref=rows(*block) if src is None else src, dst_ref=rows(*block),
                send_sem=send_sems.at[k], recv_sem=recv_sems.at[k], device_id=to, device_id_type=_MESH)

        mine = pltpu.make_async_copy(p_ref, rows(*me), local_sem)
        mine.start()
        first = [copy(0, me, sibling, src=p_ref)]
        first += [copy(1 + j, me, (*chip, c), src=p_ref) for j, chip in enumerate(chips)]
        for cp in first:
            cp.start()
        passed = [copy(4 + j, (*chip, c), sibling) for j, chip in enumerate(chips)]
        for j, chip in enumerate(chips):
            copy(1 + j, (*chip, c), me).wait_recv()
            passed[j].start()
        copy(0, sibling, me).wait_recv()
        for j, chip in enumerate(chips):
            copy(4 + j, (*chip, 1 - c), me).wait_recv()
        for cp in first + passed:
            cp.wait_send()
        mine.wait()

    return pl.pallas_call(
        body, name=name,
        out_shape=jax.ShapeDtypeStruct((N_DEV,) + p.shape, p.dtype),
        in_specs=[_ANY], out_specs=_ANY,
        scratch_shapes=[pltpu.SemaphoreType.DMA((7,)), pltpu.SemaphoreType.DMA((7,)), pltpu.SemaphoreType.DMA],
    )(p)


def _exchange_sibling(src, name):
    def body(src_ref, out_ref, send_sems, recv_sems):
        x, y, c = lax.axis_index("x"), lax.axis_index("y"), lax.axis_index("c")
        copies = [pltpu.make_async_remote_copy(
            src_ref=src_ref.at[j, 1 - c], dst_ref=out_ref.at[j], send_sem=send_sems.at[j],
            recv_sem=recv_sems.at[j], device_id=(x, y, 1 - c), device_id_type=_MESH) for j in range(N_CHIPS)]
        for cp in copies:
            cp.start()
        for cp in copies:
            cp.wait()

    return pl.pallas_call(
        body, name=name,
        out_shape=jax.ShapeDtypeStruct((N_CHIPS,) + src.shape[2:], src.dtype),
        in_specs=[_ANY], out_specs=_ANY,
        scratch_shapes=[pltpu.SemaphoreType.DMA((N_CHIPS,)), pltpu.SemaphoreType.DMA((N_CHIPS,))],
    )(src)


def _add_own(src, got, name):
    _, _, r, cols = src.shape
    tr = _div_tile(r, 512, 16)
    c = lax.axis_index("c").astype(jnp.int32).reshape(1)

    def kern(c_ref, a_ref, b_ref, o_ref):
        o_ref[...] = (a_ref[...].astype(F32) + b_ref[...].astype(F32)).astype(o_ref.dtype)

    grid_spec = pltpu.PrefetchScalarGridSpec(
        num_scalar_prefetch=1, grid=(N_CHIPS, r // tr),
        in_specs=[pl.BlockSpec((None, None, tr, cols), lambda j, i, c_ref: (j, c_ref[0], i, 0)),
                  pl.BlockSpec((None, tr, cols), lambda j, i, c_ref: (j, i, 0))],
        out_specs=pl.BlockSpec((None, tr, cols), lambda j, i, c_ref: (j, i, 0)))
    return pl.pallas_call(
        kern, name=name, grid_spec=grid_spec,
        out_shape=jax.ShapeDtypeStruct((N_CHIPS, r, cols), src.dtype),
        compiler_params=_params(("arbitrary", "arbitrary")),
    )(c, src, got)


def _exchange_chips(part, name):
    def body(part_ref, out_ref, send_sems, recv_sems, local_sem):
        x, y, c = lax.axis_index("x"), lax.axis_index("y"), lax.axis_index("c")
        my_chip = 2 * x + y
        chips = [(1 - x, y), (x, 1 - y), (1 - x, 1 - y)]
        mine = pltpu.make_async_copy(part_ref.at[my_chip], out_ref.at[my_chip], local_sem)
        mine.start()
        copies = []
        for k, (px, py) in enumerate(chips):
            copies.append(pltpu.make_async_remote_copy(
                src_ref=part_ref.at[2 * px + py], dst_ref=out_ref.at[my_chip],
                send_sem=send_sems.at[k], recv_sem=recv_sems.at[k], device_id=(px, py, c), device_id_type=_MESH))
        for cp in copies:
            cp.start()
        for k, (px, py) in enumerate(chips):
            pltpu.make_async_remote_copy(
                src_ref=part_ref.at[my_chip], dst_ref=out_ref.at[2 * px + py],
                send_sem=send_sems.at[k], recv_sem=recv_sems.at[k], device_id=(px, py, c),
                device_id_type=_MESH).wait_recv()
        for cp in copies:
            cp.wait_send()
        mine.wait()

    return pl.pallas_call(
        body, name=name,
        out_shape=jax.ShapeDtypeStruct(part.shape, part.dtype),
        in_specs=[_ANY], out_specs=_ANY,
        scratch_shapes=[pltpu.SemaphoreType.DMA((3,)), pltpu.SemaphoreType.DMA((3,)), pltpu.SemaphoreType.DMA],
    )(part)


def _sum_chips(parts, name):
    _, r, cols = parts.shape
    tr = _div_tile(r, 512, 16)

    def kern(p_ref, o_ref):
        p = [p_ref[j].astype(F32) for j in range(N_CHIPS)]
        o_ref[...] = ((p[0] + p[1]) + p[2]) + p[3]

    return pl.pallas_call(
        kern, name=name, grid=(r // tr,),
        in_specs=[pl.BlockSpec((4, tr, cols), lambda i: (0, i, 0))],
        out_specs=pl.BlockSpec((tr, cols), lambda i: (i, 0)),
        out_shape=jax.ShapeDtypeStruct((r, cols), F32),
        compiler_params=_params(("arbitrary",)),
    )(parts)


_SHARDED = {
    "norm_pre": (1, True), "norm_post": (1, True),
    "ffn1_w_gate": (1, False), "ffn1_w_up": (1, False), "ffn1_w_down": (0, False),
    "ffn2_w_gate": (1, False), "ffn2_w_up": (1, False), "ffn2_w_down": (0, False),
    "w_in": (1, False), "conv_w": (1, True), "gla_w_alpha": (1, False),
    "w_branch": (2, False), "w_out": (0, False),
}
_TRANSPOSED = ("ffn1_w_gate", "ffn1_w_up", "ffn2_w_gate", "ffn2_w_up", "w_in", "w_branch")


def _storage(name, a):
    return jnp.swapaxes(a, -1, -2) if name in _TRANSPOSED else a


def _storage_axis(name, shape):
    axis = _SHARDED[name][0]
    if name in _TRANSPOSED and axis >= len(shape) - 2:
        axis = 2 * len(shape) - 3 - axis
    return axis


_REPLICATED = ("conv_b", "conv_ln_g", "conv_ln_b", "gla_b_alpha", "gla_norm_g")
_WEIGHTS = ("norm_pre", "norm_post", "ffn1_w_gate", "ffn1_w_up", "ffn1_w_down", "ffn2_w_gate", "ffn2_w_up",
            "ffn2_w_down", "w_in", "conv_w", "conv_b", "conv_ln_g", "conv_ln_b", "gla_w_alpha", "gla_b_alpha",
            "gla_norm_g", "w_branch", "w_out")


PACK_ROW_UNIT = 16
BIG_PIECE = PACK_ROW_UNIT * PACK_COLS


def _rows_of(a, nlead):
    lead = a.shape[:nlead]
    n = math.prod(a.shape[nlead:])
    if n % PACK_COLS:
        flat = a.reshape(lead + (n,))
        flat = jnp.pad(flat, [(0, 0)] * nlead + [(0, -n % PACK_COLS)])
        n += -n % PACK_COLS
        a = flat
    rows = a.reshape(lead + (n // PACK_COLS, PACK_COLS))
    return jnp.pad(rows, [(0, 0)] * nlead + [(0, -rows.shape[nlead] % PACK_ROW_UNIT), (0, 0)])


def _padded_rows(n):
    rows = -(-n // PACK_COLS)
    return -(-rows // PACK_ROW_UNIT) * PACK_ROW_UNIT


def _is_big(shape, exact):
    return not exact and math.prod(shape) >= BIG_PIECE


def _pack_weights(shards, l):
    big, small = [], []
    for name, (_, exact) in _SHARDED.items():
        a = _storage(name, shards[name][l])
        if _is_big(a.shape, exact):
            big.append(_rows_of(a.astype(BF16), 0))
        elif exact:
            small.append(lax.bitcast_convert_type(a, BF16).reshape(-1))
        else:
            small.append(a.astype(BF16).reshape(-1))
    return jnp.concatenate(big + [_rows_of(jnp.concatenate(small), 0)], axis=0)


def _unpack_weights(gathered, shards, l):
    full, r0 = {}, 0

    def merge(seg, axis, shp):
        seg = jnp.moveaxis(seg, 0, axis)
        return seg.reshape(shp[:axis] + (N_DEV * shp[axis],) + shp[axis + 1:])

    def shape_axis(name):
        shp = jax.eval_shape(lambda a: _storage(name, a), shards[name][l]).shape
        return shp, _storage_axis(name, shp)

    for name, (_, exact) in _SHARDED.items():
        shp, axis = shape_axis(name)
        if _is_big(shp, exact):
            n = math.prod(shp)
            nr = _padded_rows(n)
            seg = gathered[:, r0:r0 + nr].reshape(N_DEV, -1)[:, :n] if n % PACK_COLS else \
                gathered[:, r0:r0 + n // PACK_COLS]
            full[name] = merge(seg.reshape((N_DEV,) + shp), axis, shp)
            r0 += nr
    flat, off = gathered[:, r0:].reshape(N_DEV, -1), 0
    for name, (_, exact) in _SHARDED.items():
        shp, axis = shape_axis(name)
        if not _is_big(shp, exact):
            n = math.prod(shp) * (2 if exact else 1)
            seg = flat[:, off:off + n]
            off += n
            if exact:
                seg = lax.bitcast_convert_type(seg.reshape((N_DEV,) + shp + (2,)), F32)
            full[name] = merge(seg.reshape((N_DEV,) + shp), axis, shp)
    return full


def _pack_grads(full_grads, repl_grads):
    big, small = [], []
    for name in _SHARDED:
        g = full_grads[name].astype(BF16)
        shp = g.shape
        axis = _storage_axis(name, shp)
        g = g.reshape(shp[:axis] + (N_DEV, shp[axis] // N_DEV) + shp[axis + 1:])
        g = jnp.moveaxis(g, axis, 0)
        if _is_big(g.shape[1:], False):
            big.append(_rows_of(g, 1))
        else:
            small.append(g.reshape(N_DEV, -1))
    for name in _REPLICATED:
        small.append(jnp.broadcast_to(repl_grads[name].reshape(1, -1).astype(BF16),
                                      (N_DEV, repl_grads[name].size)))
    rows = jnp.concatenate(big + [_rows_of(jnp.concatenate(small, axis=1), 1)], axis=1)
    return rows.reshape((N_CHIPS, 2) + rows.shape[1:])


def _unpack_grads(summed, shards):
    out, r0 = {}, 0

    def storage_shape(name):
        return jax.eval_shape(lambda a: _storage(name, a), shards[name][0]).shape

    for name in _SHARDED:
        shp = storage_shape(name)
        if _is_big(shp, False):
            n = math.prod(shp)
            seg = summed[r0:r0 + _padded_rows(n)].reshape(-1)[:n] if n % PACK_COLS else \
                summed[r0:r0 + n // PACK_COLS]
            out[name] = _storage(name, seg.reshape(shp))
            r0 += _padded_rows(n)
    flat, off = summed[r0:].reshape(-1), 0
    for name in tuple(_SHARDED) + _REPLICATED:
        shp = shards[name].shape[1:] if name in _REPLICATED else storage_shape(name)
        if name in _REPLICATED or not _is_big(shp, False):
            n = math.prod(shp)
            seg = flat[off:off + n].reshape(shp)
            out[name] = seg if name in _REPLICATED else _storage(name, seg)
            off += n
    return out


def _adamw_natural(w, g, m, v, name):
    shp = w.shape
    view = lambda a: a.reshape(-1, shp[-1])
    outs = _adamw(view(w), view(g), view(m), view(v), name)
    return [o.reshape(shp) for o in outs]


def _layer_weights(full, repl, l, d_model):
    bw = BRANCH_WIDTH
    w_in = full["w_in"]
    o_conv, o_gq, o_lr = 3 * bw, 5 * bw, 5 * bw + 2 * GLA_KEY_DIM + 2 * GLA_VALUE_DIM
    o_gate = o_lr + GLA_GATE_RANK
    lr_pad = LANE - GLA_GATE_RANK
    in_gla = jnp.concatenate([w_in[o_gq:o_gate], jnp.zeros((lr_pad, d_model), w_in.dtype)], axis=0)
    alpha = jnp.concatenate([full["gla_w_alpha"], jnp.zeros((lr_pad, GLA_KEY_DIM), BF16)], axis=0)
    row = lambda a: a.reshape(1, -1)
    mix = dict(pre=row(full["norm_pre"][1]), post=row(full["norm_post"][1]),
               in_sb=w_in[0:o_conv], in_conv=w_in[o_conv:o_gq], in_gla=in_gla, in_gate=w_in[o_gate:],
               conv_w=full["conv_w"], conv_b=row(repl["conv_b"][l]), ln_g=row(repl["conv_ln_g"][l]),
               ln_b=row(repl["conv_ln_b"][l]), alpha=alpha, b_alpha=row(repl["gla_b_alpha"][l]),
               gnorm=row(repl["gla_norm_g"][l]), branch=full["w_branch"], out=full["w_out"])
    return dict(
        pre0=row(full["norm_pre"][0]), post0=row(full["norm_post"][0]),
        pre2=row(full["norm_pre"][2]), post2=row(full["norm_post"][2]),
        ffn1=dict(gate=full["ffn1_w_gate"], up=full["ffn1_w_up"], down=full["ffn1_w_down"]),
        ffn2=dict(gate=full["ffn2_w_gate"], up=full["ffn2_w_up"], down=full["ffn2_w_down"]),
        mix=mix)


def _full_grads(g):
    mix = g["mix"]
    kd, vd = GLA_KEY_DIM, GLA_VALUE_DIM
    d_in = jnp.concatenate([mix["in_sb"], mix["in_conv"], mix["in_gla"][:2 * kd + 2 * vd + GLA_GATE_RANK],
                            mix["in_gate"]], axis=0)
    full = {
        "norm_pre": jnp.concatenate([g["ffn1"]["pre"], mix["pre"], g["ffn2"]["pre"]], axis=0),
        "norm_post": jnp.concatenate([g["ffn1"]["post"], mix["post"], g["ffn2"]["post"]], axis=0),
        "ffn1_w_gate": g["ffn1"]["gate"], "ffn1_w_up": g["ffn1"]["up"], "ffn1_w_down": g["ffn1"]["down"],
        "ffn2_w_gate": g["ffn2"]["gate"], "ffn2_w_up": g["ffn2"]["up"], "ffn2_w_down": g["ffn2"]["down"],
        "w_in": d_in, "conv_w": mix["conv_w"], "gla_w_alpha": mix["alpha"][:GLA_GATE_RANK],
        "w_branch": mix["branch"], "w_out": mix["out"],
    }
    repl = {"conv_b": mix["conv_b"], "conv_ln_g": mix["ln_g"], "conv_ln_b": mix["ln_b"],
            "gla_b_alpha": mix["b_alpha"], "gla_norm_g": mix["gnorm"]}
    return full, repl


def kernel(x, norm_pre, norm_post, ffn1_w_gate, ffn1_w_up, ffn1_w_down, ffn2_w_gate, ffn2_w_up, ffn2_w_down, w_in, conv_w, conv_b, conv_ln_g, conv_ln_b, gla_w_alpha, gla_b_alpha, gla_norm_g, w_branch, w_out, loss_target, m_norm_pre, m_norm_post, m_ffn1_w_gate, m_ffn1_w_up, m_ffn1_w_down, m_ffn2_w_gate, m_ffn2_w_up, m_ffn2_w_down, m_w_in, m_conv_w, m_conv_b, m_conv_ln_g, m_conv_ln_b, m_gla_w_alpha, m_gla_b_alpha, m_gla_norm_g, m_w_branch, m_w_out, v_norm_pre, v_norm_post, v_ffn1_w_gate, v_ffn1_w_up, v_ffn1_w_down, v_ffn2_w_gate, v_ffn2_w_up, v_ffn2_w_down, v_w_in, v_conv_w, v_conv_b, v_conv_ln_g, v_conv_ln_b, v_gla_w_alpha, v_gla_b_alpha, v_gla_norm_g, v_w_branch, v_w_out):
    weights = dict(norm_pre=norm_pre, norm_post=norm_post, ffn1_w_gate=ffn1_w_gate, ffn1_w_up=ffn1_w_up,
                   ffn1_w_down=ffn1_w_down, ffn2_w_gate=ffn2_w_gate, ffn2_w_up=ffn2_w_up, ffn2_w_down=ffn2_w_down,
                   w_in=w_in, conv_w=conv_w, conv_b=conv_b, conv_ln_g=conv_ln_g, conv_ln_b=conv_ln_b,
                   gla_w_alpha=gla_w_alpha, gla_b_alpha=gla_b_alpha, gla_norm_g=gla_norm_g, w_branch=w_branch,
                   w_out=w_out)
    moments_m = dict(norm_pre=m_norm_pre, norm_post=m_norm_post, ffn1_w_gate=m_ffn1_w_gate, ffn1_w_up=m_ffn1_w_up,
                     ffn1_w_down=m_ffn1_w_down, ffn2_w_gate=m_ffn2_w_gate, ffn2_w_up=m_ffn2_w_up,
                     ffn2_w_down=m_ffn2_w_down, w_in=m_w_in, conv_w=m_conv_w, conv_b=m_conv_b,
                     conv_ln_g=m_conv_ln_g, conv_ln_b=m_conv_ln_b, gla_w_alpha=m_gla_w_alpha,
                     gla_b_alpha=m_gla_b_alpha, gla_norm_g=m_gla_norm_g, w_branch=m_w_branch, w_out=m_w_out)
    moments_v = dict(norm_pre=v_norm_pre, norm_post=v_norm_post, ffn1_w_gate=v_ffn1_w_gate, ffn1_w_up=v_ffn1_w_up,
                     ffn1_w_down=v_ffn1_w_down, ffn2_w_gate=v_ffn2_w_gate, ffn2_w_up=v_ffn2_w_up,
                     ffn2_w_down=v_ffn2_w_down, w_in=v_w_in, conv_w=v_conv_w, conv_b=v_conv_b,
                     conv_ln_g=v_conv_ln_g, conv_ln_b=v_conv_ln_b, gla_w_alpha=v_gla_w_alpha,
                     gla_b_alpha=v_gla_b_alpha, gla_norm_g=v_gla_norm_g, w_branch=v_w_branch, w_out=v_w_out)
    n_layers = norm_pre.shape[0]
    t, d_model = x.shape[1], x.shape[2]

    layers = []
    for l in range(n_layers):
        gathered = _all_gather(_pack_weights(weights, l), f"gather_l{l}")
        full = _unpack_weights(gathered, weights, l)
        layers.append(_layer_weights(full, weights, l, d_model))

    sq, dx, grads = _local_step(x[0], loss_target[0], layers)
    loss = lax.psum(0.5 * jnp.sum(sq) / d_model, MESH_AXES)

    layer_grads = []
    for l in range(n_layers):
        packed = _pack_grads(*_full_grads(grads[l]))
        got = _exchange_sibling(packed, f"rs_sibling_l{l}")
        part = _add_own(packed, got, f"rs_add_l{l}")
        parts = _exchange_chips(part, f"rs_chips_l{l}")
        layer_grads.append(_unpack_grads(_sum_chips(parts, f"rs_sum_l{l}"), weights))

    g_out, d_out, m_out, v_out = [], [], [], []
    for name in _WEIGHTS:
        g = jnp.stack([lg[name] for lg in layer_grads])
        delta, new_m, new_v = _adamw_natural(weights[name], g, moments_m[name], moments_v[name], f"adamw_{name}")
        g_out.append(g)
        d_out.append(delta)
        m_out.append(new_m)
        v_out.append(new_v)
    return tuple([loss, dx[None]] + g_out + d_out + m_out + v_out)
```

```python
import functools
import math

import jax
import jax.numpy as jnp
from jax import lax
from jax.experimental import pallas as pl
from jax.experimental.pallas import tpu as pltpu

F32 = jnp.float32
BF16 = jnp.bfloat16

VMEM_LIMIT_BYTES = 48 * 1024 * 1024
LANE = 128

NORM_EPS = 1e-6
CHUNK = 64
N_BRANCHES = 3
BRANCH_WIDTH = 512
SB_HEADS = 8
SB_HEAD_DIM = 64
CONV_WIDTH = 31
CONV_HALO = 32
GLA_HEADS = 4
GLA_HEAD_K = 64
GLA_HEAD_V = 128
GLA_KEY_DIM = GLA_HEADS * GLA_HEAD_K
GLA_VALUE_DIM = GLA_HEADS * GLA_HEAD_V
GLA_GATE_RANK = 16
GLA_GATE_TAU = 16.0
SB_BQ = 1024
SB_BK = 128

ADAM_LR = 0.001
ADAM_B1 = 0.9
ADAM_B2 = 0.999
ADAM_EPS = 1e-08
ADAM_WD = 0.01
ADAM_STEP = 10

N_DEV = 8
MESH_AXES = ("x", "y", "c")
PACK_COLS = 1024


def _params(sem):
    return pltpu.CompilerParams(dimension_semantics=sem, vmem_limit_bytes=VMEM_LIMIT_BYTES)


def _div_tile(n, cap, unit):
    if n <= cap:
        return n
    best = None
    for t in range(unit, cap + 1, unit):
        if n % t == 0:
            best = t
    assert best is not None, (n, cap, unit)
    return best


_NN = (((1,), (0,)), ((), ()))
_NT = (((1,), (1,)), ((), ()))
_TN = (((0,), (0,)), ((), ()))


def _dot(a, b, dims=_NN):
    return lax.dot_general(a, b, dims, preferred_element_type=F32)


def _sigmoid(v):
    return 1.0 / (1.0 + jnp.exp(-v))


def _mm(a, b, *, nt=False, tn=False, out_dtype=F32, name):
    a = a.astype(BF16)
    b = b.astype(BF16)
    k, m = a.shape[::-1] if not tn else a.shape
    n = b.shape[0] if nt else b.shape[1]
    assert (b.shape[1] if nt else b.shape[0]) == k
    tm = _div_tile(m, 2048, LANE) if tn else _div_tile(m, 512, 16)
    tn_ = _div_tile(n, 2048, LANE)
    tk = _div_tile(k, 2048, LANE)
    nk = k // tk
    dims = _NT if nt else (_TN if tn else _NN)

    def kern(a_ref, b_ref, o_ref, acc_ref):
        kk = pl.program_id(2)

        @pl.when(kk == 0)
        def _():
            acc_ref[...] = jnp.zeros_like(acc_ref)

        acc_ref[...] += _dot(a_ref[...], b_ref[...], dims)

        @pl.when(kk == nk - 1)
        def _():
            o_ref[...] = acc_ref[...].astype(o_ref.dtype)

    b_spec = (pl.BlockSpec((tn_, tk), lambda i, j, kk: (j, kk)) if nt
              else pl.BlockSpec((tk, tn_), lambda i, j, kk: (kk, j)))
    return pl.pallas_call(
        kern, name=name,
        out_shape=jax.ShapeDtypeStruct((m, n), out_dtype),
        grid=(m // tm, n // tn_, nk),
        in_specs=[pl.BlockSpec((tk, tm), lambda i, j, kk: (kk, i)) if tn
                  else pl.BlockSpec((tm, tk), lambda i, j, kk: (i, kk)), b_spec],
        out_specs=pl.BlockSpec((tm, tn_), lambda i, j, kk: (i, j)),
        scratch_shapes=[pltpu.VMEM((tm, tn_), F32)],
        compiler_params=_params(("parallel", "parallel", "arbitrary")),
    )(a, b)


def _rowwise(name, body, mats, vecs, outs, sums=(), tm=256):
    mats = [m if isinstance(m, tuple) else (m, 0, m.shape[1]) for m in mats]
    t = mats[0][0].shape[0]
    tm = _div_tile(t, tm, 8)
    nm, nv, no, ns = len(mats), len(vecs), len(outs), len(sums)

    def kern(*refs):
        i = pl.program_id(0)
        ins = [r[...] for r in refs[:nm + nv]]
        res = body(*ins)
        out_vals, sum_vals = res[:no], res[no:]
        for r, val in zip(refs[nm + nv:nm + nv + no], out_vals):
            if isinstance(val, (list, tuple)):
                off = 0
                for piece in val:
                    w = piece.shape[1]
                    r[:, off:off + w] = piece.astype(r.dtype)
                    off += w
            else:
                r[...] = val.astype(r.dtype)
        if ns:
            sum_refs = refs[nm + nv + no:]

            @pl.when(i == 0)
            def _():
                for r in sum_refs:
                    r[...] = jnp.zeros_like(r)

            for r, val in zip(sum_refs, sum_vals):
                r[...] += jnp.sum(val, axis=0, keepdims=True)

    in_specs = [pl.BlockSpec((tm, w), functools.partial(lambda i, cb: (i, cb), cb=cb)) for (_, cb, w) in mats]
    in_specs += [pl.BlockSpec(v.shape, lambda i: (0, 0)) for v in vecs]
    out_specs = [pl.BlockSpec((tm, w), lambda i: (i, 0)) for (w, _) in outs]
    out_specs += [pl.BlockSpec((1, w), lambda i: (0, 0)) for w in sums]
    out_shape = [jax.ShapeDtypeStruct((t, w), dt) for (w, dt) in outs]
    out_shape += [jax.ShapeDtypeStruct((1, w), F32) for w in sums]
    return pl.pallas_call(
        kern, name=name, out_shape=out_shape, grid=(t // tm,),
        in_specs=in_specs, out_specs=out_specs,
        compiler_params=_params(("arbitrary",)),
    )(*[m[0] for m in mats], *vecs)


def _rms_fwd(x, g, name):
    d = x.shape[1]

    def body(xv, gv):
        r = lax.rsqrt(jnp.mean(xv * xv, axis=-1, keepdims=True) + NORM_EPS)
        return ((xv * r) * gv,)

    return _rowwise(name, body, [x], [g], [(d, BF16)])[0]


def _post_res(o, x, g, c, name):
    d = x.shape[1]

    def body(ov, xv, gv):
        r = lax.rsqrt(jnp.mean(ov * ov, axis=-1, keepdims=True) + NORM_EPS)
        return (xv + c * ((ov * r) * gv),)

    return _rowwise(name, body, [o, x], [g], [(d, F32)])[0]


def _post_bwd(dx, o, g, c, name):
    d = dx.shape[1]

    def body(dxv, ov, gv):
        r = lax.rsqrt(jnp.mean(ov * ov, axis=-1, keepdims=True) + NORM_EPS)
        n = ov * r
        dy = c * dxv
        dn = dy * gv
        do = r * (dn - n * jnp.mean(dn * n, axis=-1, keepdims=True))
        return (do, dy * n)

    return _rowwise(name, body, [dx, o], [g], [(d, BF16)], sums=[d])


def _rms_bwd(dh_parts, x, g, dx_res, name):
    d = x.shape[1]
    npart = len(dh_parts)

    def body(*vals):
        dh = vals[0]
        for p in vals[1:npart]:
            dh = dh + p
        xv, dres, gv = vals[npart], vals[npart + 1], vals[npart + 2]
        r = lax.rsqrt(jnp.mean(xv * xv, axis=-1, keepdims=True) + NORM_EPS)
        n = xv * r
        dn = dh * gv
        dx = dres + r * (dn - n * jnp.mean(dn * n, axis=-1, keepdims=True))
        return (dx, dh * n)

    return _rowwise(name, body, list(dh_parts) + [x, dx_res], [g], [(d, F32)], sums=[d], tm=128)


def _ffn_gate_up(h, wg, wu, name):
    t, d = h.shape
    f = wg.shape[0]
    tm = _div_tile(t, 512, 16)
    tn = _div_tile(f, 1536, LANE)

    def kern(h_ref, wg_ref, wu_ref, g_ref, u_ref, a_ref):
        hv = h_ref[...]
        g = _dot(hv, wg_ref[...], _NT).astype(BF16)
        u = _dot(hv, wu_ref[...], _NT).astype(BF16)
        g_ref[...] = g
        u_ref[...] = u
        gf = g.astype(F32)
        a_ref[...] = ((gf * _sigmoid(gf)) * u.astype(F32)).astype(BF16)

    w_spec = pl.BlockSpec((tn, d), lambda i, j: (j, 0))
    o_spec = pl.BlockSpec((tm, tn), lambda i, j: (i, j))
    return pl.pallas_call(
        kern, name=name, grid=(t // tm, f // tn),
        out_shape=[jax.ShapeDtypeStruct((t, f), BF16)] * 3,
        in_specs=[pl.BlockSpec((tm, d), lambda i, j: (i, 0)), w_spec, w_spec],
        out_specs=[o_spec, o_spec, o_spec],
        compiler_params=_params(("parallel", "parallel")),
    )(h, wg, wu)


def _ffn_dact(do, wd, gate, up, name):
    t, d = do.shape
    f = wd.shape[0]
    tm = _div_tile(t, 512, 16)
    tn = _div_tile(f, 1536, LANE)

    def kern(do_ref, wd_ref, g_ref, u_ref, dg_ref, du_ref):
        da = _dot(do_ref[...], wd_ref[...], _NT)
        gv = g_ref[...].astype(F32)
        uv = u_ref[...].astype(F32)
        sg = _sigmoid(gv)
        dg_ref[...] = (da * uv * (sg * (1.0 + gv * (1.0 - sg)))).astype(BF16)
        du_ref[...] = (da * (gv * sg)).astype(BF16)

    o_spec = pl.BlockSpec((tm, tn), lambda i, j: (i, j))
    return pl.pallas_call(
        kern, name=name, grid=(t // tm, f // tn),
        out_shape=[jax.ShapeDtypeStruct((t, f), BF16)] * 2,
        in_specs=[pl.BlockSpec((tm, d), lambda i, j: (i, 0)), pl.BlockSpec((tn, d), lambda i, j: (j, 0)),
                  o_spec, o_spec],
        out_specs=[o_spec, o_spec],
        compiler_params=_params(("parallel", "parallel")),
    )(do, wd, gate, up)


def _merge_fwd(bds, logits, name):
    d = bds[0].shape[1]

    def body(b0, b1, b2, l0, l1, l2):
        return (_sigmoid(l0) * b0 + _sigmoid(l1) * b1 + _sigmoid(l2) * b2,)

    mats = list(bds) + [(logits, j, d) for j in range(N_BRANCHES)]
    return _rowwise(name, body, mats, [], [(d, BF16)], tm=128)[0]


def _merge_bwd(dmerged, bds, logits, name):
    d = bds[0].shape[1]

    def body(dm, b0, b1, b2, l0, l1, l2):
        dbs, dls = [], []
        for b, l in ((b0, l0), (b1, l1), (b2, l2)):
            s = _sigmoid(l)
            dbs.append(dm * s)
            dls.append(dm * b * (s * (1.0 - s)))
        return (dbs[0], dbs[1], dbs[2], dls)

    mats = [dmerged] + list(bds) + [(logits, j, d) for j in range(N_BRANCHES)]
    outs = [(d, BF16)] * 3 + [(N_BRANCHES * d, BF16)]
    return _rowwise(name, body, mats, [], outs, tm=128)


def _la_fwd(pre, b, name):
    w = pre.shape[1]

    def body(pv, bv):
        p = pv + bv
        sp = jnp.maximum(-p, 0.0) + jnp.log(1.0 + jnp.exp(-jnp.abs(p)))
        return (-sp / GLA_GATE_TAU,)

    return _rowwise(name, body, [pre], [b], [(w, F32)])[0]


def _la_bwd(dla, pre, b, name):
    w = pre.shape[1]

    def body(dv, pv, bv):
        p = pv + bv
        dpre = (dv / GLA_GATE_TAU) * _sigmoid(-p)
        return (dpre, dpre)

    return _rowwise(name, body, [dla, pre], [b], [(w, BF16)], sums=[w])


def _gla_post_fwd(o, p_gla, r_block, gn, name):
    w = o.shape[1]

    def body(ov, rv, gv):
        pieces = []
        for h in range(GLA_HEADS):
            sl = slice(h * GLA_HEAD_V, (h + 1) * GLA_HEAD_V)
            oh = ov[:, sl]
            rr = lax.rsqrt(jnp.mean(oh * oh, axis=-1, keepdims=True) + NORM_EPS)
            rh = rv[:, sl]
            pieces.append(((oh * rr) * gv[:, sl]) * (rh * _sigmoid(rh)))
        return (pieces,)

    return _rowwise(name, body, [o, (p_gla, r_block, w)], [gn], [(w, BF16)])[0]


def _gla_post_bwd(dout, o, p_gla, r_block, gn, name):
    w = o.shape[1]

    def body(dv, ov, rv, gv):
        dos, drs, dgs = [], [], []
        for h in range(GLA_HEADS):
            sl = slice(h * GLA_HEAD_V, (h + 1) * GLA_HEAD_V)
            oh, rh, gh, dh = ov[:, sl], rv[:, sl], gv[:, sl], dv[:, sl]
            rr = lax.rsqrt(jnp.mean(oh * oh, axis=-1, keepdims=True) + NORM_EPS)
            nhat = oh * rr
            s = _sigmoid(rh)
            dn = dh * (rh * s)
            drs.append(dh * (nhat * gh) * (s * (1.0 + rh * (1.0 - s))))
            dnn = dn * gh
            dos.append(rr * (dnn - nhat * jnp.mean(dnn * nhat, axis=-1, keepdims=True)))
            dgs.append(dn * nhat)
        return (dos, drs, jnp.concatenate(dgs, axis=1))

    return _rowwise(name, body, [dout, o, (p_gla, r_block, w)], [gn], [(w, F32), (w, BF16)], sums=[w])


def _conv_bwd1(dout, y, ln_g, ln_b, name):
    w = y.shape[1]

    def body(dv, yv, gv, bv):
        mu = jnp.mean(yv, axis=-1, keepdims=True)
        yc = yv - mu
        rstd = lax.rsqrt(jnp.mean(yc * yc, axis=-1, keepdims=True) + NORM_EPS)
        xhat = yc * rstd
        yn = xhat * gv + bv
        s = _sigmoid(yn)
        dyn = dv * (s * (1.0 + yn * (1.0 - s)))
        dxh = dyn * gv
        dy = rstd * (dxh - jnp.mean(dxh, axis=-1, keepdims=True)
                     - xhat * jnp.mean(dxh * xhat, axis=-1, keepdims=True))
        return (dy, dyn * xhat, dyn, dy)

    return _rowwise(name, body, [dout, y], [ln_g, ln_b], [(w, F32)], sums=[w, w, w])


def _loss_fwd(y, target, name):
    d = y.shape[1]

    def body(yv, tv):
        e = yv - tv
        return (e / d, e * e)

    return _rowwise(name, body, [y, target], [], [(d, F32)], sums=[d])


def _adamw(w, g, m, v, name):
    cols = w.shape[1]
    c1 = 1.0 - ADAM_B1 ** ADAM_STEP
    c2 = 1.0 - ADAM_B2 ** ADAM_STEP

    def body(wv, gv, mv, vv):
        m2 = ADAM_B1 * mv + (1.0 - ADAM_B1) * gv
        v2 = ADAM_B2 * vv + (1.0 - ADAM_B2) * (gv * gv)
        m_hat = m2 / c1
        v_hat = v2 / c2
        delta = -ADAM_LR * (m_hat / (jnp.sqrt(v_hat) + ADAM_EPS) + ADAM_WD * wv)
        return (delta, m2, v2)

    return _rowwise(name, body, [w, g, m, v], [], [(cols, F32)] * 3)


def _split_bf16(v):
    hi = v.astype(BF16)
    lo = (v - hi.astype(F32)).astype(BF16)
    return hi, lo


def _tri(n, strict):
    r = lax.broadcasted_iota(jnp.int32, (n, n), 0)
    c = lax.broadcasted_iota(jnp.int32, (n, n), 1)
    return jnp.where(r > c if strict else r >= c, 1.0, 0.0).astype(BF16)


def _sb_weights(z, c, mx, diag_offset):
    u = jnp.exp(-jnp.abs(z))
    sp = jnp.maximum(z, 0.0) + jnp.log(1.0 + u)
    if diag_offset is None:
        mask = None
        spm = sp
    else:
        rows = lax.broadcasted_iota(jnp.int32, z.shape, 0)
        cols = lax.broadcasted_iota(jnp.int32, z.shape, 1)
        mask = cols + diag_offset < rows
        spm = jnp.where(mask, sp, 0.0)
    suf = _dot(spm.astype(BF16), mx)
    w = jnp.exp(((z - sp) - suf) - c)
    if mask is not None:
        w = jnp.where(mask, w, 0.0)
    return sp, w, mask, suf[:, 0:1] + spm[:, 0:1]


def _sb_fwd(q, kt, vt, name):
    nh, t, dh = q.shape
    nkb, bk = kt.shape[1], kt.shape[3]
    bq = min(SB_BQ, t)
    scale = dh ** -0.5
    per = bq // bk

    def kern(q_ref, kt_ref, vt_ref, o_ref, c_ref, zbuf, wbuf):
        qi = pl.program_id(1)
        top = (qi + 1) * per - 1
        ntiles = (qi + 1) * per
        qs = (q_ref[...].astype(F32) * scale).astype(BF16)
        mx = _tri(bk, True)
        o_ref[...] = jnp.zeros_like(o_ref)
        c_ref[...] = jnp.zeros_like(c_ref)
        zbuf[0] = _dot(qs, kt_ref[top])
        wbuf[0] = jnp.zeros((bq, bk), BF16)

        def tile(i, diag_offset):
            kb = top - i
            p = i % 2
            z = zbuf[p]
            w_prev = wbuf[p]
            c = c_ref[...]
            zbuf[1 - p] = _dot(qs, kt_ref[jnp.maximum(kb - 1, 0)])
            o_ref[...] += _dot(w_prev, vt_ref[jnp.minimum(kb + 1, nkb - 1)], _NT)
            _, w, _, tot = _sb_weights(z, c, mx, diag_offset)
            wbuf[1 - p] = w.astype(BF16)
            c_ref[...] = c + tot

        def diag_step(i, carry):
            tile(i, (per - 1 - i) * bk)
            return carry

        def step(i, carry):
            tile(i, None)
            return carry

        lax.fori_loop(0, per, diag_step, 0)
        lax.fori_loop(per, ntiles, step, 0)
        o_ref[...] += _dot(wbuf[ntiles % 2], vt_ref[0], _NT)

    return pl.pallas_call(
        kern, name=name,
        out_shape=jax.ShapeDtypeStruct((nh, t, dh), F32),
        grid=(nh, t // bq),
        in_specs=[pl.BlockSpec((None, bq, dh), lambda h, i: (h, i, 0)),
                  pl.BlockSpec((None, nkb, dh, bk), lambda h, i: (h, 0, 0, 0)),
                  pl.BlockSpec((None, nkb, dh, bk), lambda h, i: (h, 0, 0, 0))],
        out_specs=pl.BlockSpec((None, bq, dh), lambda h, i: (h, i, 0)),
        scratch_shapes=[pltpu.VMEM((bq, 1), F32), pltpu.VMEM((2, bq, bk), F32), pltpu.VMEM((2, bq, bk), BF16)],
        compiler_params=_params(("parallel", "arbitrary")),
    )(q, kt, vt)


def _sb_bwd(q, qt, kt, vt, out, dout, doutt, name):
    nh, t, dh = q.shape
    nkb, bk = kt.shape[1], kt.shape[3]
    bq = min(SB_BQ, t)
    scale = dh ** -0.5
    per = bq // bk

    def kern(q_ref, qt_ref, kt_ref, vt_ref, o_ref, do_ref, dot_ref, dq_ref, dkt_ref, dvt_ref, c_ref, ce_ref):
        qi = pl.program_id(1)
        top = (qi + 1) * per - 1

        @pl.when(qi == 0)
        def _():
            dkt_ref[...] = jnp.zeros_like(dkt_ref)
            dvt_ref[...] = jnp.zeros_like(dvt_ref)

        qs = (q_ref[...].astype(F32) * scale).astype(BF16)
        qts = (qt_ref[...].astype(F32) * scale).astype(BF16)
        dob = do_ref[...].astype(BF16)
        dotv = dot_ref[...]
        dsum = jnp.sum(dob.astype(F32) * o_ref[...], axis=1, keepdims=True)
        mx = _tri(bk, True)
        mi = _tri(bk, False)
        dq_ref[...] = jnp.zeros_like(dq_ref)
        c_ref[...] = jnp.zeros_like(c_ref)
        ce_ref[...] = jnp.zeros_like(ce_ref)

        def tile(kb, diag_offset):
            kt_blk = kt_ref[kb]
            z = _dot(qs, kt_blk)
            c = c_ref[...]
            sp, w, mask, tot = _sb_weights(z, c, mx, diag_offset)
            wb = w.astype(BF16)
            e = _dot(dob, vt_ref[kb]) * wb.astype(F32)
            before = dsum - (_dot(e.astype(BF16), mi) + ce_ref[...])
            sig = jnp.exp(z - sp)
            dz = e - sig * (e + before)
            if mask is not None:
                dz = jnp.where(mask, dz, 0.0)
            dz = dz.astype(BF16)
            dq_ref[...] += _dot(dz, kt_blk, _NT)
            dkt_ref[kb] += _dot(qts, dz)
            dvt_ref[kb] += _dot(dotv, wb)
            c_ref[...] = c + tot
            ce_ref[...] += jnp.sum(e, axis=1, keepdims=True)

        def diag_step(i, carry):
            tile(top - i, (per - 1 - i) * bk)
            return carry

        def step(i, carry):
            tile(top - i, None)
            return carry

        lax.fori_loop(0, per, diag_step, 0)
        lax.fori_loop(per, (qi + 1) * per, step, 0)
        dq_ref[...] = dq_ref[...] * scale

    row = pl.BlockSpec((None, bq, dh), lambda h, i: (h, i, 0))
    col = pl.BlockSpec((None, dh, bq), lambda h, i: (h, 0, i))
    whole = pl.BlockSpec((None, nkb, dh, bk), lambda h, i: (h, 0, 0, 0))
    return pl.pallas_call(
        kern, name=name,
        out_shape=[jax.ShapeDtypeStruct((nh, t, dh), F32),
                   jax.ShapeDtypeStruct((nh, nkb, dh, bk), F32),
                   jax.ShapeDtypeStruct((nh, nkb, dh, bk), F32)],
        grid=(nh, t // bq),
        in_specs=[row, col, whole, whole, row, row, col],
        out_specs=[row, whole, whole],
        scratch_shapes=[pltpu.VMEM((bq, 1), F32), pltpu.VMEM((bq, 1), F32)],
        compiler_params=_params(("parallel", "arbitrary")),
    )(q, qt, kt, vt, out, dout, doutt)


def _conv_fwd(p_conv, conv_w, conv_b, ln_g, ln_b, name):
    t, c2 = p_conv.shape
    c = c2 // 2
    tm = min(256, t)
    hb = tm // CONV_HALO

    def kern(a_ref, g_ref, ah_ref, gh_ref, w_ref, b_ref, lg_ref, lb_ref, o_ref, u_ref, y_ref, ubuf):
        i = pl.program_id(0)
        u = a_ref[...] * _sigmoid(g_ref[...])
        uh = ah_ref[...] * _sigmoid(gh_ref[...])
        ubuf[0:CONV_HALO, :] = jnp.where(i > 0, uh, 0.0)
        ubuf[CONV_HALO:CONV_HALO + tm, :] = u
        y = jnp.zeros((tm, c), F32) + b_ref[...]
        for j in range(CONV_WIDTH):
            off = CONV_HALO - (CONV_WIDTH - 1) + j
            y = y + ubuf[off:off + tm, :] * w_ref[j:j + 1, :]
        mu = jnp.mean(y, axis=-1, keepdims=True)
        yc = y - mu
        rstd = lax.rsqrt(jnp.mean(yc * yc, axis=-1, keepdims=True) + NORM_EPS)
        yn = (yc * rstd) * lg_ref[...] + lb_ref[...]
        o_ref[...] = (yn * _sigmoid(yn)).astype(o_ref.dtype)
        u_ref[...] = u
        y_ref[...] = y

    def halo(cb):
        return pl.BlockSpec((CONV_HALO, c), lambda i: (jnp.maximum(i * hb - 1, 0), cb))

    vec = pl.BlockSpec((1, c), lambda i: (0, 0))
    tile = pl.BlockSpec((tm, c), lambda i: (i, 0))
    return pl.pallas_call(
        kern, name=name,
        out_shape=[jax.ShapeDtypeStruct((t, c), BF16), jax.ShapeDtypeStruct((t, c), F32),
                   jax.ShapeDtypeStruct((t, c), F32)],
        grid=(t // tm,),
        in_specs=[tile, pl.BlockSpec((tm, c), lambda i: (i, 1)), halo(0), halo(1),
                  pl.BlockSpec(conv_w.shape, lambda i: (0, 0)), vec, vec, vec],
        out_specs=[tile, tile, tile],
        scratch_shapes=[pltpu.VMEM((tm + CONV_HALO, c), F32)],
        compiler_params=_params(("arbitrary",)),
    )(p_conv, p_conv, p_conv, p_conv, conv_w, conv_b, ln_g, ln_b)


def _conv_bwd2(dy, u, p_conv, conv_w, name):
    t, c = dy.shape
    tm = min(256, t)
    hb = tm // CONV_HALO
    nt = t // tm
    last_halo = t // CONV_HALO - 1

    def kern(dy_ref, dyh_ref, u_ref, uh_ref, a_ref, g_ref, w_ref, dp_ref, dw_ref, dybuf, ubuf):
        i = pl.program_id(0)

        @pl.when(i == 0)
        def _():
            dw_ref[...] = jnp.zeros_like(dw_ref)

        dyv = dy_ref[...]
        dybuf[0:tm, :] = dyv
        dybuf[tm:tm + CONV_HALO, :] = jnp.where(i < nt - 1, dyh_ref[...], 0.0)
        ubuf[0:CONV_HALO, :] = jnp.where(i > 0, uh_ref[...], 0.0)
        ubuf[CONV_HALO:CONV_HALO + tm, :] = u_ref[...]
        du = jnp.zeros((tm, c), F32)
        for j in range(CONV_WIDTH):
            off = CONV_WIDTH - 1 - j
            du = du + dybuf[off:off + tm, :] * w_ref[j:j + 1, :]
            uoff = CONV_HALO - (CONV_WIDTH - 1) + j
            dw_ref[j:j + 1, :] += jnp.sum(dyv * ubuf[uoff:uoff + tm, :], axis=0, keepdims=True)
        a = a_ref[...]
        s = _sigmoid(g_ref[...])
        dp_ref[:, 0:c] = (du * s).astype(dp_ref.dtype)
        dp_ref[:, c:2 * c] = (du * a * (s * (1.0 - s))).astype(dp_ref.dtype)

    tile = pl.BlockSpec((tm, c), lambda i: (i, 0))
    return pl.pallas_call(
        kern, name=name,
        out_shape=[jax.ShapeDtypeStruct((t, 2 * c), BF16), jax.ShapeDtypeStruct((CONV_HALO, c), F32)],
        grid=(nt,),
        in_specs=[tile,
                  pl.BlockSpec((CONV_HALO, c), lambda i: (jnp.minimum((i + 1) * hb, last_halo), 0)),
                  tile,
                  pl.BlockSpec((CONV_HALO, c), lambda i: (jnp.maximum(i * hb - 1, 0), 0)),
                  tile, pl.BlockSpec((tm, c), lambda i: (i, 1)),
                  pl.BlockSpec(conv_w.shape, lambda i: (0, 0))],
        out_specs=[pl.BlockSpec((tm, 2 * c), lambda i: (i, 0)),
                   pl.BlockSpec((CONV_HALO, c), lambda i: (0, 0))],
        scratch_shapes=[pltpu.VMEM((tm + CONV_HALO, c), F32), pltpu.VMEM((tm + CONV_HALO, c), F32)],
        compiler_params=_params(("arbitrary",)),
    )(dy, dy, u, u, p_conv, p_conv, conv_w)


GLA_CHUNKS_PER_STEP = 16


def _gla_fwd(q_ck, k_kc, la_kc, v, name):
    nh, nc, ch, dk = q_ck.shape
    dv = v.shape[3]
    qscale = dk ** -0.5
    cb = min(GLA_CHUNKS_PER_STEP, nc)

    def kern(q_ref, k_ref, la_ref, v_ref, o_ref, st_ref, state_ref):
        mx = _tri(ch, True)

        @pl.when(pl.program_id(1) == 0)
        def _():
            state_ref[...] = jnp.zeros_like(state_ref)

        def step(n, state):
            la = la_ref[n]
            hi, lo = _split_bf16(la)
            de = _dot(hi, mx) + _dot(lo, mx)
            lam = jnp.exp(jnp.sum(la, axis=1, keepdims=True))
            kd = (k_ref[n] * jnp.exp(de)).astype(BF16)
            state = lam * state + _dot(kd, v_ref[n])
            st_ref[n] = state
            qs = (q_ref[n].astype(F32) * qscale).astype(BF16)
            o_ref[n] = _dot(qs, state.astype(BF16))
            return state

        state_ref[...] = lax.fori_loop(0, cb, step, state_ref[...])

    def spec(a, b):
        return pl.BlockSpec((None, cb, a, b), lambda h, j: (h, j, 0, 0))

    return pl.pallas_call(
        kern, name=name,
        out_shape=[jax.ShapeDtypeStruct((nh, nc, ch, dv), F32), jax.ShapeDtypeStruct((nh, nc, dk, dv), F32)],
        grid=(nh, nc // cb),
        in_specs=[spec(ch, dk), spec(dk, ch), spec(dk, ch), spec(ch, dv)],
        out_specs=[spec(ch, dv), spec(dk, dv)],
        scratch_shapes=[pltpu.VMEM((dk, dv), F32)],
        compiler_params=_params(("parallel", "arbitrary")),
    )(q_ck, k_kc, la_kc, v)


def _gla_bwd(q_ck, q_kc, k_ck, k_kc, la_ck, la_kc, v, states, states_prev, do, name):
    nh, nc, ch, dk = q_ck.shape
    dv = v.shape[3]
    qscale = dk ** -0.5
    cb = min(GLA_CHUNKS_PER_STEP, nc)
    nb = nc // cb

    def kern(q_ref, qt_ref, kck_ref, kkc_ref, lack_ref, lakc_ref, v_ref, st_ref, stp_ref, do_ref,
             dq_ref, dk_ref, dv_ref, dla_ref, g_ref):
        mx = _tri(ch, True)
        mxt = jnp.where(lax.broadcasted_iota(jnp.int32, (ch, ch), 0)
                        < lax.broadcasted_iota(jnp.int32, (ch, ch), 1), 1.0, 0.0).astype(BF16)

        @pl.when(pl.program_id(1) == 0)
        def _():
            g_ref[...] = jnp.zeros_like(g_ref)

        def step(i, g):
            n = cb - 1 - i
            la_kc = lakc_ref[n]
            hi, lo = _split_bf16(la_kc)
            de_kc = _dot(hi, mx) + _dot(lo, mx)
            lam = jnp.exp(jnp.sum(la_kc, axis=1, keepdims=True))
            hi2, lo2 = _split_bf16(lack_ref[n])
            de_ck = _dot(mxt, hi2) + _dot(mxt, lo2)
            edk = jnp.exp(de_kc)
            kd_kc = kkc_ref[n] * edk
            kd_ck = (kck_ref[n] * jnp.exp(de_ck)).astype(BF16)
            dob = do_ref[n].astype(BF16)
            dq_ref[n] = _dot(dob, st_ref[n].astype(BF16), _NT) * qscale
            qts = (qt_ref[n].astype(F32) * qscale).astype(BF16)
            ds = _dot(qts, dob) + g
            dsb = ds.astype(BF16)
            dlam = jnp.sum(ds * stp_ref[n], axis=1, keepdims=True)
            dkd = _dot(dsb, v_ref[n], _NT)
            dv_ref[n] = _dot(kd_ck, dsb)
            dk_ref[n] = dkd * edk
            dde = dkd * kd_kc
            h3, l3 = _split_bf16(dde)
            dla_ref[n] = _dot(h3, mxt) + _dot(l3, mxt) + dlam * lam
            return lam * ds

        g_ref[...] = lax.fori_loop(0, cb, step, g_ref[...])

    def spec(a, b):
        return pl.BlockSpec((None, cb, a, b), lambda h, j: (h, nb - 1 - j, 0, 0))

    return pl.pallas_call(
        kern, name=name,
        out_shape=[jax.ShapeDtypeStruct((nh, nc, ch, dk), F32), jax.ShapeDtypeStruct((nh, nc, dk, ch), F32),
                   jax.ShapeDtypeStruct((nh, nc, ch, dv), F32), jax.ShapeDtypeStruct((nh, nc, dk, ch), F32)],
        grid=(nh, nb),
        in_specs=[spec(ch, dk), spec(dk, ch), spec(ch, dk), spec(dk, ch), spec(ch, dk), spec(dk, ch),
                  spec(ch, dv), spec(dk, dv), spec(dk, dv), spec(ch, dv)],
        out_specs=[spec(ch, dk), spec(dk, ch), spec(ch, dv), spec(dk, ch)],
        scratch_shapes=[pltpu.VMEM((dk, dv), F32)],
        compiler_params=_params(("parallel", "arbitrary")),
    )(q_ck, q_kc, k_ck, k_kc, la_ck, la_kc, v, states, states_prev, do)


def _heads_rows(a, nh, blk):
    t = a.shape[0]
    d = a.shape[1] // nh
    return a.reshape(t // blk, blk, nh, d).transpose(2, 0, 1, 3)


def _heads_cols(a, nh, blk):
    t = a.shape[0]
    d = a.shape[1] // nh
    return a.T.reshape(nh, d, t // blk, blk).transpose(0, 2, 1, 3)


def _unheads_rows(a):
    nh, nb, blk, d = a.shape
    return a.transpose(1, 2, 0, 3).reshape(nb * blk, nh * d)


def _unheads_cols(a):
    nh, nb, d, blk = a.shape
    return a.transpose(0, 2, 1, 3).reshape(nh * d, nb * blk).T


def _ffn_fwd(x, w, pre, post, tag):
    h = _rms_fwd(x, pre, f"{tag}_rms")
    gate, up, act = _ffn_gate_up(h, w["gate"], w["up"], f"{tag}_gateup")
    o = _mm(act, w["down"], name=f"{tag}_down")
    x_out = _post_res(o, x, post, 0.5, f"{tag}_res")
    return x_out, (x, h, gate, up, act, o)


def _ffn_bwd(dx_out, saved, w, pre, post, tag):
    x, h, gate, up, act, o = saved
    do, dpost = _post_bwd(dx_out, o, post, 0.5, f"{tag}_bres")
    d_down = _mm(act, do, tn=True, out_dtype=BF16, name=f"{tag}_bwdown")
    dgate, dup = _ffn_dact(do, w["down"], gate, up, f"{tag}_bdact")
    dh_g = _mm(dgate, w["gate"], name=f"{tag}_bdhg")
    dh_u = _mm(dup, w["up"], name=f"{tag}_bdhu")
    d_gate = _mm(dgate, h, tn=True, out_dtype=BF16, name=f"{tag}_bwgate")
    d_up = _mm(dup, h, tn=True, out_dtype=BF16, name=f"{tag}_bwup")
    dx, dpre = _rms_bwd([dh_g, dh_u], x, pre, dx_out, f"{tag}_brms")
    return dx, dict(gate=d_gate, up=d_up, down=d_down, pre=dpre, post=dpost)


def _mixer_fwd(x, w, tag):
    t, d = x.shape
    bw = BRANCH_WIDTH
    h = _rms_fwd(x, w["pre"], f"{tag}_rms")
    p_sb = _mm(h, w["in_sb"], nt=True, out_dtype=BF16, name=f"{tag}_insb")
    p_conv = _mm(h, w["in_conv"], nt=True, name=f"{tag}_inconv")
    p_gla = _mm(h, w["in_gla"], nt=True, name=f"{tag}_ingla")
    p_gate = _mm(h, w["in_gate"], nt=True, name=f"{tag}_ingate")

    bk = min(SB_BK, t)
    q = p_sb[:, 0:bw].reshape(t, SB_HEADS, SB_HEAD_DIM).transpose(1, 0, 2)
    kt = _heads_cols(p_sb[:, bw:2 * bw], SB_HEADS, bk)
    vt = _heads_cols(p_sb[:, 2 * bw:3 * bw], SB_HEADS, bk)
    sb_o = _sb_fwd(q, kt, vt, f"{tag}_sb")
    sb_out = sb_o.transpose(1, 0, 2).reshape(t, bw)

    conv_out, conv_u, conv_y = _conv_fwd(p_conv, w["conv_w"], w["conv_b"], w["ln_g"], w["ln_b"], f"{tag}_conv")

    kd, vd = GLA_KEY_DIM, GLA_VALUE_DIM
    lr = p_gla[:, 2 * kd + 2 * vd:]
    pre_a = _mm(lr, w["alpha"], name=f"{tag}_alpha")
    la = _la_fwd(pre_a, w["b_alpha"], f"{tag}_la")
    gq = _heads_rows(p_gla[:, 0:kd].astype(BF16), GLA_HEADS, CHUNK)
    gk_kc = _heads_cols(p_gla[:, kd:2 * kd], GLA_HEADS, CHUNK)
    gv = _heads_rows(p_gla[:, 2 * kd:2 * kd + vd].astype(BF16), GLA_HEADS, CHUNK)
    la_kc = _heads_cols(la, GLA_HEADS, CHUNK)
    gla_o4, states = _gla_fwd(gq, gk_kc, la_kc, gv, f"{tag}_gla")
    gla_o = _unheads_rows(gla_o4)
    r_block = (2 * kd + vd) // vd
    gla_out = _gla_post_fwd(gla_o, p_gla, r_block, w["gnorm"], f"{tag}_glapost")

    branches = (sb_out.astype(BF16), conv_out, gla_out)
    bds = [_mm(branches[j], w["branch"][j], nt=True, name=f"{tag}_br{j}") for j in range(N_BRANCHES)]
    merged = _merge_fwd(bds, p_gate, f"{tag}_merge")
    mo = _mm(merged, w["out"], name=f"{tag}_out")
    x_out = _post_res(mo, x, w["post"], 1.0, f"{tag}_res")
    saved = dict(x=x, h=h, p_sb=p_sb, p_conv=p_conv, p_gla=p_gla, p_gate=p_gate, sb_o=sb_o, conv_u=conv_u,
                 sb_q=q, sb_kt=kt, sb_vt=vt, gla_q=gq, gla_k_kc=gk_kc, gla_v=gv, gla_la_kc=la_kc,
                 conv_y=conv_y, pre_a=pre_a, la=la, states=states, gla_o=gla_o, branches=branches, bds=bds,
                 merged=merged, mo=mo)
    return x_out, saved


def _mixer_bwd(dx_out, s, w, tag):
    x = s["x"]
    t, d = x.shape
    bw = BRANCH_WIDTH
    kd, vd = GLA_KEY_DIM, GLA_VALUE_DIM
    grads = {}
    dmo, grads["post"] = _post_bwd(dx_out, s["mo"], w["post"], 1.0, f"{tag}_bres")
    dmerged = _mm(dmo, w["out"], nt=True, name=f"{tag}_bdmerged")
    grads["out"] = _mm(s["merged"], dmo, tn=True, out_dtype=BF16, name=f"{tag}_bwout")
    dbd0, dbd1, dbd2, dp_gate = _merge_bwd(dmerged, s["bds"], s["p_gate"], f"{tag}_bmerge")
    dbds = (dbd0, dbd1, dbd2)
    dbranch = [_mm(dbds[j], w["branch"][j], name=f"{tag}_bdbr{j}") for j in range(N_BRANCHES)]
    grads["branch"] = jnp.stack([_mm(dbds[j], s["branches"][j], tn=True, out_dtype=BF16, name=f"{tag}_bwbr{j}")
                                 for j in range(N_BRANCHES)])

    p_sb = s["p_sb"]
    bk = min(SB_BK, t)
    q, kt, vt = s["sb_q"], s["sb_kt"], s["sb_vt"]
    qt = p_sb[:, 0:bw].reshape(t, SB_HEADS, SB_HEAD_DIM).transpose(1, 2, 0)
    dsb = dbranch[0].reshape(t, SB_HEADS, SB_HEAD_DIM)
    dout = dsb.transpose(1, 0, 2)
    doutt = dsb.transpose(1, 2, 0).astype(BF16)
    dq, dkt, dvt = _sb_bwd(q, qt, kt, vt, s["sb_o"], dout, doutt, f"{tag}_bsb")
    dp_sb = jnp.concatenate([dq.transpose(1, 0, 2).reshape(t, bw), _unheads_cols(dkt), _unheads_cols(dvt)],
                            axis=1).astype(BF16)

    dy, d_lng, d_lnb, d_cb = _conv_bwd1(dbranch[1], s["conv_y"], w["ln_g"], w["ln_b"], f"{tag}_bconv1")
    dp_conv, d_cw = _conv_bwd2(dy, s["conv_u"], s["p_conv"], w["conv_w"], f"{tag}_bconv2")
    grads.update(conv_w=d_cw[:CONV_WIDTH], conv_b=d_cb, ln_g=d_lng, ln_b=d_lnb)

    p_gla = s["p_gla"]
    r_block = (2 * kd + vd) // vd
    do, dr, grads["gnorm"] = _gla_post_bwd(dbranch[2], s["gla_o"], p_gla, r_block, w["gnorm"], f"{tag}_bglapost")
    gqf, gkf = p_gla[:, 0:kd], p_gla[:, kd:2 * kd]
    gq_b = gqf.astype(BF16)
    dq4, dk4, dv4, dla4 = _gla_bwd(
        s["gla_q"], _heads_cols(gq_b, GLA_HEADS, CHUNK),
        _heads_rows(gkf, GLA_HEADS, CHUNK), s["gla_k_kc"],
        _heads_rows(s["la"], GLA_HEADS, CHUNK), s["gla_la_kc"], s["gla_v"],
        s["states"], jnp.pad(s["states"][:, :-1], ((0, 0), (1, 0), (0, 0), (0, 0))),
        _heads_rows(do, GLA_HEADS, CHUNK), f"{tag}_bgla")
    dla = _unheads_cols(dla4)
    dpre_a, grads["b_alpha"] = _la_bwd(dla, s["pre_a"], w["b_alpha"], f"{tag}_bla")
    lr = p_gla[:, 2 * kd + 2 * vd:]
    dlr = _mm(dpre_a, w["alpha"], nt=True, out_dtype=BF16, name=f"{tag}_bdlr")
    grads["alpha"] = _mm(lr, dpre_a, tn=True, name=f"{tag}_bwalpha")
    dp_gla = jnp.concatenate([_unheads_rows(dq4).astype(BF16), _unheads_cols(dk4).astype(BF16),
                              _unheads_rows(dv4).astype(BF16), dr, dlr], axis=1)

    dps = dict(in_sb=dp_sb, in_conv=dp_conv, in_gla=dp_gla, in_gate=dp_gate)
    dh_parts = []
    for key, dp in dps.items():
        dh_parts.append(_mm(dp, w[key], name=f"{tag}_bdh_{key}"))
        grads[key] = _mm(dp, s["h"], tn=True, out_dtype=BF16, name=f"{tag}_bw_{key}")
    dx, grads["pre"] = _rms_bwd(dh_parts, x, w["pre"], dx_out, f"{tag}_brms")
    return dx, grads


def _local_step(x, target, layers):
    saved = []
    for l, w in enumerate(layers):
        x, s1 = _ffn_fwd(x, w["ffn1"], w["pre0"], w["post0"], f"l{l}_f1")
        x, s2 = _mixer_fwd(x, w["mix"], f"l{l}_mx")
        x, s3 = _ffn_fwd(x, w["ffn2"], w["pre2"], w["post2"], f"l{l}_f2")
        saved.append((s1, s2, s3))
    dx, sq = _loss_fwd(x, target, "loss")
    grads = [None] * len(layers)
    for l in reversed(range(len(layers))):
        w = layers[l]
        s1, s2, s3 = saved[l]
        dx, g3 = _ffn_bwd(dx, s3, w["ffn2"], w["pre2"], w["post2"], f"l{l}_f2")
        dx, g2 = _mixer_bwd(dx, s2, w["mix"], f"l{l}_mx")
        dx, g1 = _ffn_bwd(dx, s1, w["ffn1"], w["pre0"], w["post0"], f"l{l}_f1")
        grads[l] = dict(ffn1=g1, mix=g2, ffn2=g3)
    return sq, dx, grads


_ANY = pl.BlockSpec(memory_space=pl.ANY)
_MESH = pl.DeviceIdType.MESH
N_CHIPS = 4


def _all_gather(p, name):
    def body(p_ref, out_ref, send_sems, recv_sems, local_sem):
        x, y, c = lax.axis_index("x"), lax.axis_index("y"), lax.axis_index("c")
        me, sibling = (x, y, c), (x, y, 1 - c)
        chips = [(1 - x, y), (x, 1 - y), (1 - x, 1 - y)]

        def rows(px, py, pc):
            return out_ref.at[4 * px + 2 * py + pc]

        def copy(k, block, to, src=None):
            return pltpu.make_async_remote_copy(
                src_ref=rows(*block) if src is None else src, dst_ref=rows(*block),
                send_sem=send_sems.at[k], recv_sem=recv_sems.at[k], device_id=to, device_id_type=_MESH)

        mine = pltpu.make_async_copy(p_ref, rows(*me), local_sem)
        mine.start()
        first = [copy(0, me, sibling, src=p_ref)]
        first += [copy(1 + j, me, (*chip, c), src=p_ref) for j, chip in enumerate(chips)]
        for cp in first:
            cp.start()
        passed = [copy(4 + j, (*chip, c), sibling) for j, chip in enumerate(chips)]
        for j, chip in enumerate(chips):
            copy(1 + j, (*chip, c), me).wait_recv()
            passed[j].start()
        copy(0, sibling, me).wait_recv()
        for j, chip in enumerate(chips):
            copy(4 + j, (*chip, 1 - c), me).wait_recv()
        for cp in first + passed:
            cp.wait_send()
        mine.wait()

    return pl.pallas_call(
        body, name=name,
        out_shape=jax.ShapeDtypeStruct((N_DEV,) + p.shape, p.dtype),
        in_specs=[_ANY], out_specs=_ANY,
        scratch_shapes=[pltpu.SemaphoreType.DMA((7,)), pltpu.SemaphoreType.DMA((7,)), pltpu.SemaphoreType.DMA],
    )(p)


def _exchange_sibling(src, name):
    def body(src_ref, out_ref, send_sems, recv_sems):
        x, y, c = lax.axis_index("x"), lax.axis_index("y"), lax.axis_index("c")
        copies = [pltpu.make_async_remote_copy(
            src_ref=src_ref.at[j, 1 - c], dst_ref=out_ref.at[j], send_sem=send_sems.at[j],
            recv_sem=recv_sems.at[j], device_id=(x, y, 1 - c), device_id_type=_MESH) for j in range(N_CHIPS)]
        for cp in copies:
            cp.start()
        for cp in copies:
            cp.wait()

    return pl.pallas_call(
        body, name=name,
        out_shape=jax.ShapeDtypeStruct((N_CHIPS,) + src.shape[2:], src.dtype),
        in_specs=[_ANY], out_specs=_ANY,
        scratch_shapes=[pltpu.SemaphoreType.DMA((N_CHIPS,)), pltpu.SemaphoreType.DMA((N_CHIPS,))],
    )(src)


def _add_own(src, got, name):
    _, _, r, cols = src.shape
    tr = _div_tile(r, 512, 16)
    c = lax.axis_index("c").astype(jnp.int32).reshape(1)

    def kern(c_ref, a_ref, b_ref, o_ref):
        o_ref[...] = (a_ref[...].astype(F32) + b_ref[...].astype(F32)).astype(o_ref.dtype)

    grid_spec = pltpu.PrefetchScalarGridSpec(
        num_scalar_prefetch=1, grid=(N_CHIPS, r // tr),
        in_specs=[pl.BlockSpec((None, None, tr, cols), lambda j, i, c_ref: (j, c_ref[0], i, 0)),
                  pl.BlockSpec((None, tr, cols), lambda j, i, c_ref: (j, i, 0))],
        out_specs=pl.BlockSpec((None, tr, cols), lambda j, i, c_ref: (j, i, 0)))
    return pl.pallas_call(
        kern, name=name, grid_spec=grid_spec,
        out_shape=jax.ShapeDtypeStruct((N_CHIPS, r, cols), src.dtype),
        compiler_params=_params(("arbitrary", "arbitrary")),
    )(c, src, got)


def _exchange_chips(part, name):
    def body(part_ref, out_ref, send_sems, recv_sems, local_sem):
        x, y, c = lax.axis_index("x"), lax.axis_index("y"), lax.axis_index("c")
        my_chip = 2 * x + y
        chips = [(1 - x, y), (x, 1 - y), (1 - x, 1 - y)]
        mine = pltpu.make_async_copy(part_ref.at[my_chip], out_ref.at[my_chip], local_sem)
        mine.start()
        copies = []
        for k, (px, py) in enumerate(chips):
            copies.append(pltpu.make_async_remote_copy(
                src_ref=part_ref.at[2 * px + py], dst_ref=out_ref.at[my_chip],
                send_sem=send_sems.at[k], recv_sem=recv_sems.at[k], device_id=(px, py, c), device_id_type=_MESH))
        for cp in copies:
            cp.start()
        for k, (px, py) in enumerate(chips):
            pltpu.make_async_remote_copy(
                src_ref=part_ref.at[my_chip], dst_ref=out_ref.at[2 * px + py],
                send_sem=send_sems.at[k], recv_sem=recv_sems.at[k], device_id=(px, py, c),
                device_id_type=_MESH).wait_recv()
        for cp in copies:
            cp.wait_send()
        mine.wait()

    return pl.pallas_call(
        body, name=name,
        out_shape=jax.ShapeDtypeStruct(part.shape, part.dtype),
        in_specs=[_ANY], out_specs=_ANY,
        scratch_shapes=[pltpu.SemaphoreType.DMA((3,)), pltpu.SemaphoreType.DMA((3,)), pltpu.SemaphoreType.DMA],
    )(part)


def _sum_chips(parts, name):
    _, r, cols = parts.shape
    tr = _div_tile(r, 512, 16)

    def kern(p_ref, o_ref):
        p = [p_ref[j].astype(F32) for j in range(N_CHIPS)]
        o_ref[...] = ((p[0] + p[1]) + p[2]) + p[3]

    return pl.pallas_call(
        kern, name=name, grid=(r // tr,),
        in_specs=[pl.BlockSpec((4, tr, cols), lambda i: (0, i, 0))],
        out_specs=pl.BlockSpec((tr, cols), lambda i: (i, 0)),
        out_shape=jax.ShapeDtypeStruct((r, cols), F32),
        compiler_params=_params(("arbitrary",)),
    )(parts)


_SHARDED = {
    "norm_pre": (1, True), "norm_post": (1, True),
    "ffn1_w_gate": (1, False), "ffn1_w_up": (1, False), "ffn1_w_down": (0, False),
    "ffn2_w_gate": (1, False), "ffn2_w_up": (1, False), "ffn2_w_down": (0, False),
    "w_in": (1, False), "conv_w": (1, True), "gla_w_alpha": (1, False),
    "w_branch": (2, False), "w_out": (0, False),
}
_TRANSPOSED = ("ffn1_w_gate", "ffn1_w_up", "ffn2_w_gate", "ffn2_w_up", "w_in", "w_branch")


def _storage(name, a):
    return jnp.swapaxes(a, -1, -2) if name in _TRANSPOSED else a


def _storage_axis(name, shape):
    axis = _SHARDED[name][0]
    if name in _TRANSPOSED and axis >= len(shape) - 2:
        axis = 2 * len(shape) - 3 - axis
    return axis


_REPLICATED = ("conv_b", "conv_ln_g", "conv_ln_b", "gla_b_alpha", "gla_norm_g")
_WEIGHTS = ("norm_pre", "norm_post", "ffn1_w_gate", "ffn1_w_up", "ffn1_w_down", "ffn2_w_gate", "ffn2_w_up",
            "ffn2_w_down", "w_in", "conv_w", "conv_b", "conv_ln_g", "conv_ln_b", "gla_w_alpha", "gla_b_alpha",
            "gla_norm_g", "w_branch", "w_out")


PACK_ROW_UNIT = 16
BIG_PIECE = PACK_ROW_UNIT * PACK_COLS


def _rows_of(a, nlead):
    lead = a.shape[:nlead]
    n = math.prod(a.shape[nlead:])
    if n % PACK_COLS:
        flat = a.reshape(lead + (n,))
        flat = jnp.pad(flat, [(0, 0)] * nlead + [(0, -n % PACK_COLS)])
        n += -n % PACK_COLS
        a = flat
    rows = a.reshape(lead + (n // PACK_COLS, PACK_COLS))
    return jnp.pad(rows, [(0, 0)] * nlead + [(0, -rows.shape[nlead] % PACK_ROW_UNIT), (0, 0)])


def _padded_rows(n):
    rows = -(-n // PACK_COLS)
    return -(-rows // PACK_ROW_UNIT) * PACK_ROW_UNIT


def _is_big(shape, exact):
    return not exact and math.prod(shape) >= BIG_PIECE


def _pack_weights(shards, l):
    big, small = [], []
    for name, (_, exact) in _SHARDED.items():
        a = _storage(name, shards[name][l])
        if _is_big(a.shape, exact):
            big.append(_rows_of(a.astype(BF16), 0))
        elif exact:
            small.append(lax.bitcast_convert_type(a, BF16).reshape(-1))
        else:
            small.append(a.astype(BF16).reshape(-1))
    return jnp.concatenate(big + [_rows_of(jnp.concatenate(small), 0)], axis=0)


def _unpack_weights(gathered, shards, l):
    full, r0 = {}, 0

    def merge(seg, axis, shp):
        seg = jnp.moveaxis(seg, 0, axis)
        return seg.reshape(shp[:axis] + (N_DEV * shp[axis],) + shp[axis + 1:])

    def shape_axis(name):
        shp = jax.eval_shape(lambda a: _storage(name, a), shards[name][l]).shape
        return shp, _storage_axis(name, shp)

    for name, (_, exact) in _SHARDED.items():
        shp, axis = shape_axis(name)
        if _is_big(shp, exact):
            n = math.prod(shp)
            nr = _padded_rows(n)
            seg = gathered[:, r0:r0 + nr].reshape(N_DEV, -1)[:, :n] if n % PACK_COLS else \
                gathered[:, r0:r0 + n // PACK_COLS]
            full[name] = merge(seg.reshape((N_DEV,) + shp), axis, shp)
            r0 += nr
    flat, off = gathered[:, r0:].reshape(N_DEV, -1), 0
    for name, (_, exact) in _SHARDED.items():
        shp, axis = shape_axis(name)
        if not _is_big(shp, exact):
            n = math.prod(shp) * (2 if exact else 1)
            seg = flat[:, off:off + n]
            off += n
            if exact:
                seg = lax.bitcast_convert_type(seg.reshape((N_DEV,) + shp + (2,)), F32)
            full[name] = merge(seg.reshape((N_DEV,) + shp), axis, shp)
    return full


def _pack_grads(full_grads, repl_grads):
    big, small = [], []
    for name in _SHARDED:
        g = full_grads[name].astype(BF16)
        shp = g.shape
        axis = _storage_axis(name, shp)
        g = g.reshape(shp[:axis] + (N_DEV, shp[axis] // N_DEV) + shp[axis + 1:])
        g = jnp.moveaxis(g, axis, 0)
        if _is_big(g.shape[1:], False):
            big.append(_rows_of(g, 1))
        else:
            small.append(g.reshape(N_DEV, -1))
    for name in _REPLICATED:
        small.append(jnp.broadcast_to(repl_grads[name].reshape(1, -1).astype(BF16),
                                      (N_DEV, repl_grads[name].size)))
    rows = jnp.concatenate(big + [_rows_of(jnp.concatenate(small, axis=1), 1)], axis=1)
    return rows.reshape((N_CHIPS, 2) + rows.shape[1:])


def _unpack_grads(summed, shards):
    out, r0 = {}, 0

    def storage_shape(name):
        return jax.eval_shape(lambda a: _storage(name, a), shards[name][0]).shape

    for name in _SHARDED:
        shp = storage_shape(name)
        if _is_big(shp, False):
            n = math.prod(shp)
            seg = summed[r0:r0 + _padded_rows(n)].reshape(-1)[:n] if n % PACK_COLS else \
                summed[r0:r0 + n // PACK_COLS]
            out[name] = _storage(name, seg.reshape(shp))
            r0 += _padded_rows(n)
    flat, off = summed[r0:].reshape(-1), 0
    for name in tuple(_SHARDED) + _REPLICATED:
        shp = shards[name].shape[1:] if name in _REPLICATED else storage_shape(name)
        if name in _REPLICATED or not _is_big(shp, False):
            n = math.prod(shp)
            seg = flat[off:off + n].reshape(shp)
            out[name] = seg if name in _REPLICATED else _storage(name, seg)
            off += n
    return out


def _adamw_natural(w, g, m, v, name):
    shp = w.shape
    view = lambda a: a.reshape(-1, shp[-1])
    outs = _adamw(view(w), view(g), view(m), view(v), name)
    return [o.reshape(shp) for o in outs]


def _layer_weights(full, repl, l, d_model):
    bw = BRANCH_WIDTH
    w_in = full["w_in"]
    o_conv, o_gq, o_lr = 3 * bw, 5 * bw, 5 * bw + 2 * GLA_KEY_DIM + 2 * GLA_VALUE_DIM
    o_gate = o_lr + GLA_GATE_RANK
    lr_pad = LANE - GLA_GATE_RANK
    in_gla = jnp.concatenate([w_in[o_gq:o_gate], jnp.zeros((lr_pad, d_model), w_in.dtype)], axis=0)
    alpha = jnp.concatenate([full["gla_w_alpha"], jnp.zeros((lr_pad, GLA_KEY_DIM), BF16)], axis=0)
    row = lambda a: a.reshape(1, -1)
    mix = dict(pre=row(full["norm_pre"][1]), post=row(full["norm_post"][1]),
               in_sb=w_in[0:o_conv], in_conv=w_in[o_conv:o_gq], in_gla=in_gla, in_gate=w_in[o_gate:],
               conv_w=full["conv_w"], conv_b=row(repl["conv_b"][l]), ln_g=row(repl["conv_ln_g"][l]),
               ln_b=row(repl["conv_ln_b"][l]), alpha=alpha, b_alpha=row(repl["gla_b_alpha"][l]),
               gnorm=row(repl["gla_norm_g"][l]), branch=full["w_branch"], out=full["w_out"])
    return dict(
        pre0=row(full["norm_pre"][0]), post0=row(full["norm_post"][0]),
        pre2=row(full["norm_pre"][2]), post2=row(full["norm_post"][2]),
        ffn1=dict(gate=full["ffn1_w_gate"], up=full["ffn1_w_up"], down=full["ffn1_w_down"]),
        ffn2=dict(gate=full["ffn2_w_gate"], up=full["ffn2_w_up"], down=full["ffn2_w_down"]),
        mix=mix)


def _full_grads(g):
    mix = g["mix"]
    kd, vd = GLA_KEY_DIM, GLA_VALUE_DIM
    d_in = jnp.concatenate([mix["in_sb"], mix["in_conv"], mix["in_gla"][:2 * kd + 2 * vd + GLA_GATE_RANK],
                            mix["in_gate"]], axis=0)
    full = {
        "norm_pre": jnp.concatenate([g["ffn1"]["pre"], mix["pre"], g["ffn2"]["pre"]], axis=0),
        "norm_post": jnp.concatenate([g["ffn1"]["post"], mix["post"], g["ffn2"]["post"]], axis=0),
        "ffn1_w_gate": g["ffn1"]["gate"], "ffn1_w_up": g["ffn1"]["up"], "ffn1_w_down": g["ffn1"]["down"],
        "ffn2_w_gate": g["ffn2"]["gate"], "ffn2_w_up": g["ffn2"]["up"], "ffn2_w_down": g["ffn2"]["down"],
        "w_in": d_in, "conv_w": mix["conv_w"], "gla_w_alpha": mix["alpha"][:GLA_GATE_RANK],
        "w_branch": mix["branch"], "w_out": mix["out"],
    }
    repl = {"conv_b": mix["conv_b"], "conv_ln_g": mix["ln_g"], "conv_ln_b": mix["ln_b"],
            "gla_b_alpha": mix["b_alpha"], "gla_norm_g": mix["gnorm"]}
    return full, repl


def kernel(x, norm_pre, norm_post, ffn1_w_gate, ffn1_w_up, ffn1_w_down, ffn2_w_gate, ffn2_w_up, ffn2_w_down, w_in, conv_w, conv_b, conv_ln_g, conv_ln_b, gla_w_alpha, gla_b_alpha, gla_norm_g, w_branch, w_out, loss_target, m_norm_pre, m_norm_post, m_ffn1_w_gate, m_ffn1_w_up, m_ffn1_w_down, m_ffn2_w_gate, m_ffn2_w_up, m_ffn2_w_down, m_w_in, m_conv_w, m_conv_b, m_conv_ln_g, m_conv_ln_b, m_gla_w_alpha, m_gla_b_alpha, m_gla_norm_g, m_w_branch, m_w_out, v_norm_pre, v_norm_post, v_ffn1_w_gate, v_ffn1_w_up, v_ffn1_w_down, v_ffn2_w_gate, v_ffn2_w_up, v_ffn2_w_down, v_w_in, v_conv_w, v_conv_b, v_conv_ln_g, v_conv_ln_b, v_gla_w_alpha, v_gla_b_alpha, v_gla_norm_g, v_w_branch, v_w_out):
    weights = dict(norm_pre=norm_pre, norm_post=norm_post, ffn1_w_gate=ffn1_w_gate, ffn1_w_up=ffn1_w_up,
                   ffn1_w_down=ffn1_w_down, ffn2_w_gate=ffn2_w_gate, ffn2_w_up=ffn2_w_up, ffn2_w_down=ffn2_w_down,
                   w_in=w_in, conv_w=conv_w, conv_b=conv_b, conv_ln_g=conv_ln_g, conv_ln_b=conv_ln_b,
                   gla_w_alpha=gla_w_alpha, gla_b_alpha=gla_b_alpha, gla_norm_g=gla_norm_g, w_branch=w_branch,
                   w_out=w_out)
    moments_m = dict(norm_pre=m_norm_pre, norm_post=m_norm_post, ffn1_w_gate=m_ffn1_w_gate, ffn1_w_up=m_ffn1_w_up,
                     ffn1_w_down=m_ffn1_w_down, ffn2_w_gate=m_ffn2_w_gate, ffn2_w_up=m_ffn2_w_up,
                     ffn2_w_down=m_ffn2_w_down, w_in=m_w_in, conv_w=m_conv_w, conv_b=m_conv_b,
                     conv_ln_g=m_conv_ln_g, conv_ln_b=m_conv_ln_b, gla_w_alpha=m_gla_w_alpha,
                     gla_b_alpha=m_gla_b_alpha, gla_norm_g=m_gla_norm_g, w_branch=m_w_branch, w_out=m_w_out)
    moments_v = dict(norm_pre=v_norm_pre, norm_post=v_norm_post, ffn1_w_gate=v_ffn1_w_gate, ffn1_w_up=v_ffn1_w_up,
                     ffn1_w_down=v_ffn1_w_down, ffn2_w_gate=v_ffn2_w_gate, ffn2_w_up=v_ffn2_w_up,
                     ffn2_w_down=v_ffn2_w_down, w_in=v_w_in, conv_w=v_conv_w, conv_b=v_conv_b,
                     conv_ln_g=v_conv_ln_g, conv_ln_b=v_conv_ln_b, gla_w_alpha=v_gla_w_alpha,
                     gla_b_alpha=v_gla_b_alpha, gla_norm_g=v_gla_norm_g, w_branch=v_w_branch, w_out=v_w_out)
    n_layers = norm_pre.shape[0]
    t, d_model = x.shape[1], x.shape[2]

    layers = []
    for l in range(n_layers):
        gathered = _all_gather(_pack_weights(weights, l), f"gather_l{l}")
        full = _unpack_weights(gathered, weights, l)
        layers.append(_layer_weights(full, weights, l, d_model))

    sq, dx, grads = _local_step(x[0], loss_target[0], layers)
    loss = lax.psum(0.5 * jnp.sum(sq) / d_model, MESH_AXES)

    layer_grads = []
    for l in range(n_layers):
        packed = _pack_grads(*_full_grads(grads[l]))
        got = _exchange_sibling(packed, f"rs_sibling_l{l}")
        part = _add_own(packed, got, f"rs_add_l{l}")
        parts = _exchange_chips(part, f"rs_chips_l{l}")
        layer_grads.append(_unpack_grads(_sum_chips(parts, f"rs_sum_l{l}"), weights))

    g_out, d_out, m_out, v_out = [], [], [], []
    for name in _WEIGHTS:
        g = jnp.stack([lg[name] for lg in layer_grads])
        delta, new_m, new_v = _adamw_natural(weights[name], g, moments_m[name], moments_v[name], f"adamw_{name}")
        g_out.append(g)
        d_out.append(delta)
        m_out.append(new_m)
        v_out.append(new_v)
    return tuple([loss, dx[None]] + g_out + d_out + m_out + v_out)
```

```python
import functools
import math

import jax
import jax.numpy as jnp
from jax import lax
from jax.experimental import pallas as pl
from jax.experimental.pallas import tpu as pltpu

F32 = jnp.float32
BF16 = jnp.bfloat16

VMEM_LIMIT_BYTES = 48 * 1024 * 1024
LANE = 128

NORM_EPS = 1e-6
CHUNK = 64
N_BRANCHES = 3
BRANCH_WIDTH = 512
SB_HEADS = 8
SB_HEAD_DIM = 64
CONV_WIDTH = 31
CONV_HALO = 32
GLA_HEADS = 4
GLA_HEAD_K = 64
GLA_HEAD_V = 128
GLA_KEY_DIM = GLA_HEADS * GLA_HEAD_K
GLA_VALUE_DIM = GLA_HEADS * GLA_HEAD_V
GLA_GATE_RANK = 16
GLA_GATE_TAU = 16.0
SB_BQ = 1024
SB_BK = 128

ADAM_LR = 0.001
ADAM_B1 = 0.9
ADAM_B2 = 0.999
ADAM_EPS = 1e-08
ADAM_WD = 0.01
ADAM_STEP = 10

N_DEV = 8
MESH_AXES = ("x", "y", "c")
PACK_COLS = 1024


def _params(sem):
    return pltpu.CompilerParams(dimension_semantics=sem, vmem_limit_bytes=VMEM_LIMIT_BYTES)


def _div_tile(n, cap, unit):
    if n <= cap:
        return n
    best = None
    for t in range(unit, cap + 1, unit):
        if n % t == 0:
            best = t
    assert best is not None, (n, cap, unit)
    return best


_NN = (((1,), (0,)), ((), ()))
_NT = (((1,), (1,)), ((), ()))
_TN = (((0,), (0,)), ((), ()))


def _dot(a, b, dims=_NN):
    return lax.dot_general(a, b, dims, preferred_element_type=F32)


def _sigmoid(v):
    return 1.0 / (1.0 + jnp.exp(-v))


def _mm(a, b, *, nt=False, tn=False, out_dtype=F32, name):
    a = a.astype(BF16)
    b = b.astype(BF16)
    k, m = a.shape[::-1] if not tn else a.shape
    n = b.shape[0] if nt else b.shape[1]
    assert (b.shape[1] if nt else b.shape[0]) == k
    tm = _div_tile(m, 2048, LANE) if tn else _div_tile(m, 512, 16)
    tn_ = _div_tile(n, 2048, LANE)
    tk = _div_tile(k, 2048, LANE)
    nk = k // tk
    dims = _NT if nt else (_TN if tn else _NN)

    def kern(a_ref, b_ref, o_ref, acc_ref):
        kk = pl.program_id(2)

        @pl.when(kk == 0)
        def _():
            acc_ref[...] = jnp.zeros_like(acc_ref)

        acc_ref[...] += _dot(a_ref[...], b_ref[...], dims)

        @pl.when(kk == nk - 1)
        def _():
            o_ref[...] = acc_ref[...].astype(o_ref.dtype)

    b_spec = (pl.BlockSpec((tn_, tk), lambda i, j, kk: (j, kk)) if nt
              else pl.BlockSpec((tk, tn_), lambda i, j, kk: (kk, j)))
    return pl.pallas_call(
        kern, name=name,
        out_shape=jax.ShapeDtypeStruct((m, n), out_dtype),
        grid=(m // tm, n // tn_, nk),
        in_specs=[pl.BlockSpec((tk, tm), lambda i, j, kk: (kk, i)) if tn
                  else pl.BlockSpec((tm, tk), lambda i, j, kk: (i, kk)), b_spec],
        out_specs=pl.BlockSpec((tm, tn_), lambda i, j, kk: (i, j)),
        scratch_shapes=[pltpu.VMEM((tm, tn_), F32)],
        compiler_params=_params(("parallel", "parallel", "arbitrary")),
    )(a, b)


def _rowwise(name, body, mats, vecs, outs, sums=(), tm=256):
    mats = [m if isinstance(m, tuple) else (m, 0, m.shape[1]) for m in mats]
    t = mats[0][0].shape[0]
    tm = _div_tile(t, tm, 8)
    nm, nv, no, ns = len(mats), len(vecs), len(outs), len(sums)

    def kern(*refs):
        i = pl.program_id(0)
        ins = [r[...] for r in refs[:nm + nv]]
        res = body(*ins)
        out_vals, sum_vals = res[:no], res[no:]
        for r, val in zip(refs[nm + nv:nm + nv + no], out_vals):
            if isinstance(val, (list, tuple)):
                off = 0
                for piece in val:
                    w = piece.shape[1]
                    r[:, off:off + w] = piece.astype(r.dtype)
                    off += w
            else:
                r[...] = val.astype(r.dtype)
        if ns:
            sum_refs = refs[nm + nv + no:]

            @pl.when(i == 0)
            def _():
                for r in sum_refs:
                    r[...] = jnp.zeros_like(r)

            for r, val in zip(sum_refs, sum_vals):
                r[...] += jnp.sum(val, axis=0, keepdims=True)

    in_specs = [pl.BlockSpec((tm, w), functools.partial(lambda i, cb: (i, cb), cb=cb)) for (_, cb, w) in mats]
    in_specs += [pl.BlockSpec(v.shape, lambda i: (0, 0)) for v in vecs]
    out_specs = [pl.BlockSpec((tm, w), lambda i: (i, 0)) for (w, _) in outs]
    out_specs += [pl.BlockSpec((1, w), lambda i: (0, 0)) for w in sums]
    out_shape = [jax.ShapeDtypeStruct((t, w), dt) for (w, dt) in outs]
    out_shape += [jax.ShapeDtypeStruct((1, w), F32) for w in sums]
    return pl.pallas_call(
        kern, name=name, out_shape=out_shape, grid=(t // tm,),
        in_specs=in_specs, out_specs=out_specs,
        compiler_params=_params(("arbitrary",)),
    )(*[m[0] for m in mats], *vecs)


def _rms_fwd(x, g, name):
    d = x.shape[1]

    def body(xv, gv):
        r = lax.rsqrt(jnp.mean(xv * xv, axis=-1, keepdims=True) + NORM_EPS)
        return ((xv * r) * gv,)

    return _rowwise(name, body, [x], [g], [(d, BF16)])[0]


def _post_res(o, x, g, c, name):
    d = x.shape[1]

    def body(ov, xv, gv):
        r = lax.rsqrt(jnp.mean(ov * ov, axis=-1, keepdims=True) + NORM_EPS)
        return (xv + c * ((ov * r) * gv),)

    return _rowwise(name, body, [o, x], [g], [(d, F32)])[0]


def _post_bwd(dx, o, g, c, name):
    d = dx.shape[1]

    def body(dxv, ov, gv):
        r = lax.rsqrt(jnp.mean(ov * ov, axis=-1, keepdims=True) + NORM_EPS)
        n = ov * r
        dy = c * dxv
        dn = dy * gv
        do = r * (dn - n * jnp.mean(dn * n, axis=-1, keepdims=True))
        return (do, dy * n)

    return _rowwise(name, body, [dx, o], [g], [(d, BF16)], sums=[d])


def _rms_bwd(dh_parts, x, g, dx_res, name):
    d = x.shape[1]
    npart = len(dh_parts)

    def body(*vals):
        dh = vals[0]
        for p in vals[1:npart]:
            dh = dh + p
        xv, dres, gv = vals[npart], vals[npart + 1], vals[npart + 2]
        r = lax.rsqrt(jnp.mean(xv * xv, axis=-1, keepdims=True) + NORM_EPS)
        n = xv * r
        dn = dh * gv
        dx = dres + r * (dn - n * jnp.mean(dn * n, axis=-1, keepdims=True))
        return (dx, dh * n)

    return _rowwise(name, body, list(dh_parts) + [x, dx_res], [g], [(d, F32)], sums=[d], tm=128)


def _ffn_gate_up(h, wg, wu, name):
    t, d = h.shape
    f = wg.shape[0]
    tm = _div_tile(t, 512, 16)
    tn = _div_tile(f, 1536, LANE)

    def kern(h_ref, wg_ref, wu_ref, g_ref, u_ref, a_ref):
        hv = h_ref[...]
        g = _dot(hv, wg_ref[...], _NT).astype(BF16)
        u = _dot(hv, wu_ref[...], _NT).astype(BF16)
        g_ref[...] = g
        u_ref[...] = u
        gf = g.astype(F32)
        a_ref[...] = ((gf * _sigmoid(gf)) * u.astype(F32)).astype(BF16)

    w_spec = pl.BlockSpec((tn, d), lambda i, j: (j, 0))
    o_spec = pl.BlockSpec((tm, tn), lambda i, j: (i, j))
    return pl.pallas_call(
        kern, name=name, grid=(t // tm, f // tn),
        out_shape=[jax.ShapeDtypeStruct((t, f), BF16)] * 3,
        in_specs=[pl.BlockSpec((tm, d), lambda i, j: (i, 0)), w_spec, w_spec],
        out_specs=[o_spec, o_spec, o_spec],
        compiler_params=_params(("parallel", "parallel")),
    )(h, wg, wu)


def _ffn_dact(do, wd, gate, up, name):
    t, d = do.shape
    f = wd.shape[0]
    tm = _div_tile(t, 512, 16)
    tn = _div_tile(f, 1536, LANE)

    def kern(do_ref, wd_ref, g_ref, u_ref, dg_ref, du_ref):
        da = _dot(do_ref[...], wd_ref[...], _NT)
        gv = g_ref[...].astype(F32)
        uv = u_ref[...].astype(F32)
        sg = _sigmoid(gv)
        dg_ref[...] = (da * uv * (sg * (1.0 + gv * (1.0 - sg)))).astype(BF16)
        du_ref[...] = (da * (gv * sg)).astype(BF16)

    o_spec = pl.BlockSpec((tm, tn), lambda i, j: (i, j))
    return pl.pallas_call(
        kern, name=name, grid=(t // tm, f // tn),
        out_shape=[jax.ShapeDtypeStruct((t, f), BF16)] * 2,
        in_specs=[pl.BlockSpec((tm, d), lambda i, j: (i, 0)), pl.BlockSpec((tn, d), lambda i, j: (j, 0)),
                  o_spec, o_spec],
        out_specs=[o_spec, o_spec],
        compiler_params=_params(("parallel", "parallel")),
    )(do, wd, gate, up)


def _merge_fwd(bds, logits, name):
    d = bds[0].shape[1]

    def body(b0, b1, b2, l0, l1, l2):
        return (_sigmoid(l0) * b0 + _sigmoid(l1) * b1 + _sigmoid(l2) * b2,)

    mats = list(bds) + [(logits, j, d) for j in range(N_BRANCHES)]
    return _rowwise(name, body, mats, [], [(d, BF16)], tm=128)[0]


def _merge_bwd(dmerged, bds, logits, name):
    d = bds[0].shape[1]

    def body(dm, b0, b1, b2, l0, l1, l2):
        dbs, dls = [], []
        for b, l in ((b0, l0), (b1, l1), (b2, l2)):
            s = _sigmoid(l)
            dbs.append(dm * s)
            dls.append(dm * b * (s * (1.0 - s)))
        return (dbs[0], dbs[1], dbs[2], dls)

    mats = [dmerged] + list(bds) + [(logits, j, d) for j in range(N_BRANCHES)]
    outs = [(d, BF16)] * 3 + [(N_BRANCHES * d, BF16)]
    return _rowwise(name, body, mats, [], outs, tm=128)


def _la_fwd(pre, b, name):
    w = pre.shape[1]

    def body(pv, bv):
        p = pv + bv
        sp = jnp.maximum(-p, 0.0) + jnp.log(1.0 + jnp.exp(-jnp.abs(p)))
        return (-sp / GLA_GATE_TAU,)

    return _rowwise(name, body, [pre], [b], [(w, F32)])[0]


def _la_bwd(dla, pre, b, name):
    w = pre.shape[1]

    def body(dv, pv, bv):
        p = pv + bv
        dpre = (dv / GLA_GATE_TAU) * _sigmoid(-p)
        return (dpre, dpre)

    return _rowwise(name, body, [dla, pre], [b], [(w, BF16)], sums=[w])


def _gla_post_fwd(o, p_gla, r_block, gn, name):
    w = o.shape[1]

    def body(ov, rv, gv):
        pieces = []
        for h in range(GLA_HEADS):
            sl = slice(h * GLA_HEAD_V, (h + 1) * GLA_HEAD_V)
            oh = ov[:, sl]
            rr = lax.rsqrt(jnp.mean(oh * oh, axis=-1, keepdims=True) + NORM_EPS)
            rh = rv[:, sl]
            pieces.append(((oh * rr) * gv[:, sl]) * (rh * _sigmoid(rh)))
        return (pieces,)

    return _rowwise(name, body, [o, (p_gla, r_block, w)], [gn], [(w, BF16)])[0]


def _gla_post_bwd(dout, o, p_gla, r_block, gn, name):
    w = o.shape[1]

    def body(dv, ov, rv, gv):
        dos, drs, dgs = [], [], []
        for h in range(GLA_HEADS):
            sl = slice(h * GLA_HEAD_V, (h + 1) * GLA_HEAD_V)
            oh, rh, gh, dh = ov[:, sl], rv[:, sl], gv[:, sl], dv[:, sl]
            rr = lax.rsqrt(jnp.mean(oh * oh, axis=-1, keepdims=True) + NORM_EPS)
            nhat = oh * rr
            s = _sigmoid(rh)
            dn = dh * (rh * s)
            drs.append(dh * (nhat * gh) * (s * (1.0 + rh * (1.0 - s))))
            dnn = dn * gh
            dos.append(rr * (dnn - nhat * jnp.mean(dnn * nhat, axis=-1, keepdims=True)))
            dgs.append(dn * nhat)
        return (dos, drs, jnp.concatenate(dgs, axis=1))

    return _rowwise(name, body, [dout, o, (p_gla, r_block, w)], [gn], [(w, F32), (w, BF16)], sums=[w])


def _conv_bwd1(dout, y, ln_g, ln_b, name):
    w = y.shape[1]

    def body(dv, yv, gv, bv):
        mu = jnp.mean(yv, axis=-1, keepdims=True)
        yc = yv - mu
        rstd = lax.rsqrt(jnp.mean(yc * yc, axis=-1, keepdims=True) + NORM_EPS)
        xhat = yc * rstd
        yn = xhat * gv + bv
        s = _sigmoid(yn)
        dyn = dv * (s * (1.0 + yn * (1.0 - s)))
        dxh = dyn * gv
        dy = rstd * (dxh - jnp.mean(dxh, axis=-1, keepdims=True)
                     - xhat * jnp.mean(dxh * xhat, axis=-1, keepdims=True))
        return (dy, dyn * xhat, dyn, dy)

    return _rowwise(name, body, [dout, y], [ln_g, ln_b], [(w, F32)], sums=[w, w, w])


def _loss_fwd(y, target, name):
    d = y.shape[1]

    def body(yv, tv):
        e = yv - tv
        return (e / d, e * e)

    return _rowwise(name, body, [y, target], [], [(d, F32)], sums=[d])


def _adamw(w, g, m, v, name):
    cols = w.shape[1]
    c1 = 1.0 - ADAM_B1 ** ADAM_STEP
    c2 = 1.0 - ADAM_B2 ** ADAM_STEP

    def body(wv, gv, mv, vv):
        m2 = ADAM_B1 * mv + (1.0 - ADAM_B1) * gv
        v2 = ADAM_B2 * vv + (1.0 - ADAM_B2) * (gv * gv)
        m_hat = m2 / c1
        v_hat = v2 / c2
        delta = -ADAM_LR * (m_hat / (jnp.sqrt(v_hat) + ADAM_EPS) + ADAM_WD * wv)
        return (delta, m2, v2)

    return _rowwise(name, body, [w, g, m, v], [], [(cols, F32)] * 3)


def _split_bf16(v):
    hi = v.astype(BF16)
    lo = (v - hi.astype(F32)).astype(BF16)
    return hi, lo


def _tri(n, strict):
    r = lax.broadcasted_iota(jnp.int32, (n, n), 0)
    c = lax.broadcasted_iota(jnp.int32, (n, n), 1)
    return jnp.where(r > c if strict else r >= c, 1.0, 0.0).astype(BF16)


def _sb_weights(z, c, mx, diag_offset):
    u = jnp.exp(-jnp.abs(z))
    sp = jnp.maximum(z, 0.0) + jnp.log(1.0 + u)
    if diag_offset is None:
        mask = None
        spm = sp
    else:
        rows = lax.broadcasted_iota(jnp.int32, z.shape, 0)
        cols = lax.broadcasted_iota(jnp.int32, z.shape, 1)
        mask = cols + diag_offset < rows
        spm = jnp.where(mask, sp, 0.0)
    suf = _dot(spm.astype(BF16), mx)
    w = jnp.exp(((z - sp) - suf) - c)
    if mask is not None:
        w = jnp.where(mask, w, 0.0)
    return sp, w, mask, suf[:, 0:1] + spm[:, 0:1]


def _sb_fwd(q, kt, vt, name):
    nh, t, dh = q.shape
    nkb, bk = kt.shape[1], kt.shape[3]
    bq = min(SB_BQ, t)
    scale = dh ** -0.5
    per = bq // bk

    def kern(q_ref, kt_ref, vt_ref, o_ref, c_ref, zbuf, wbuf):
        qi = pl.program_id(1)
        top = (qi + 1) * per - 1
        ntiles = (qi + 1) * per
        qs = (q_ref[...].astype(F32) * scale).astype(BF16)
        mx = _tri(bk, True)
        o_ref[...] = jnp.zeros_like(o_ref)
        c_ref[...] = jnp.zeros_like(c_ref)
        zbuf[0] = _dot(qs, kt_ref[top])
        wbuf[0] = jnp.zeros((bq, bk), BF16)

        def tile(i, r0, masked):
            kb = top - i
            p = i % 2
            z = zbuf[p, r0:, :]
            w_prev = wbuf[p]
            c = c_ref[r0:, :]
            zbuf[1 - p] = _dot(qs, kt_ref[jnp.maximum(kb - 1, 0)])
            o_ref[...] += _dot(w_prev, vt_ref[jnp.minimum(kb + 1, nkb - 1)], _NT)
            _, w, _, tot = _sb_weights(z, c, mx, 0 if masked else None)
            if r0:
                wbuf[1 - p, :r0, :] = jnp.zeros((r0, bk), BF16)
            wbuf[1 - p, r0:, :] = w.astype(BF16)
            c_ref[r0:, :] = c + tot

        def step(i, carry):
            tile(i, 0, False)
            return carry

        for i in range(per):
            tile(i, (per - 1 - i) * bk, True)
        lax.fori_loop(per, ntiles, step, 0)
        o_ref[...] += _dot(wbuf[ntiles % 2], vt_ref[0], _NT)

    return pl.pallas_call(
        kern, name=name,
        out_shape=jax.ShapeDtypeStruct((nh, t, dh), F32),
        grid=(nh, t // bq),
        in_specs=[pl.BlockSpec((None, bq, dh), lambda h, i: (h, i, 0)),
                  pl.BlockSpec((None, nkb, dh, bk), lambda h, i: (h, 0, 0, 0)),
                  pl.BlockSpec((None, nkb, dh, bk), lambda h, i: (h, 0, 0, 0))],
        out_specs=pl.BlockSpec((None, bq, dh), lambda h, i: (h, i, 0)),
        scratch_shapes=[pltpu.VMEM((bq, 1), F32), pltpu.VMEM((2, bq, bk), F32), pltpu.VMEM((2, bq, bk), BF16)],
        compiler_params=_params(("parallel", "arbitrary")),
    )(q, kt, vt)


def _sb_bwd(q, qt, kt, vt, out, dout, doutt, name):
    nh, t, dh = q.shape
    nkb, bk = kt.shape[1], kt.shape[3]
    bq = min(SB_BQ, t)
    scale = dh ** -0.5
    per = bq // bk

    def kern(q_ref, qt_ref, kt_ref, vt_ref, o_ref, do_ref, dot_ref, dq_ref, dkt_ref, dvt_ref, c_ref, ce_ref):
        qi = pl.program_id(1)
        top = (qi + 1) * per - 1

        @pl.when(qi == 0)
        def _():
            dkt_ref[...] = jnp.zeros_like(dkt_ref)
            dvt_ref[...] = jnp.zeros_like(dvt_ref)

        qs = (q_ref[...].astype(F32) * scale).astype(BF16)
        qts = (qt_ref[...].astype(F32) * scale).astype(BF16)
        dob = do_ref[...].astype(BF16)
        dotv = dot_ref[...]
        dsum = jnp.sum(dob.astype(F32) * o_ref[...], axis=1, keepdims=True)
        mx = _tri(bk, True)
        mi = _tri(bk, False)
        dq_ref[...] = jnp.zeros_like(dq_ref)
        c_ref[...] = jnp.zeros_like(c_ref)
        ce_ref[...] = jnp.zeros_like(ce_ref)

        def tile(kb, r0, masked):
            kt_blk = kt_ref[kb]
            z = _dot(qs[r0:], kt_blk)
            c = c_ref[r0:, :]
            sp, w, mask, tot = _sb_weights(z, c, mx, 0 if masked else None)
            wb = w.astype(BF16)
            e = _dot(dob[r0:], vt_ref[kb]) * wb.astype(F32)
            before = dsum[r0:] - (_dot(e.astype(BF16), mi) + ce_ref[r0:, :])
            sig = jnp.exp(z - sp)
            dz = e - sig * (e + before)
            if mask is not None:
                dz = jnp.where(mask, dz, 0.0)
            dz = dz.astype(BF16)
            dq_ref[r0:, :] += _dot(dz, kt_blk, _NT)
            dkt_ref[kb] += _dot(qts[:, r0:], dz)
            dvt_ref[kb] += _dot(dotv[:, r0:], wb)
            c_ref[r0:, :] = c + tot
            ce_ref[r0:, :] += jnp.sum(e, axis=1, keepdims=True)

        def step(i, carry):
            tile(top - i, 0, False)
            return carry

        for i in range(per):
            tile(top - i, (per - 1 - i) * bk, True)
        lax.fori_loop(per, (qi + 1) * per, step, 0)
        dq_ref[...] = dq_ref[...] * scale

    row = pl.BlockSpec((None, bq, dh), lambda h, i: (h, i, 0))
    col = pl.BlockSpec((None, dh, bq), lambda h, i: (h, 0, i))
    whole = pl.BlockSpec((None, nkb, dh, bk), lambda h, i: (h, 0, 0, 0))
    return pl.pallas_call(
        kern, name=name,
        out_shape=[jax.ShapeDtypeStruct((nh, t, dh), F32),
                   jax.ShapeDtypeStruct((nh, nkb, dh, bk), F32),
                   jax.ShapeDtypeStruct((nh, nkb, dh, bk), F32)],
        grid=(nh, t // bq),
        in_specs=[row, col, whole, whole, row, row, col],
        out_specs=[row, whole, whole],
        scratch_shapes=[pltpu.VMEM((bq, 1), F32), pltpu.VMEM((bq, 1), F32)],
        compiler_params=_params(("parallel", "arbitrary")),
    )(q, qt, kt, vt, out, dout, doutt)


def _conv_fwd(p_conv, conv_w, conv_b, ln_g, ln_b, name):
    t, c2 = p_conv.shape
    c = c2 // 2
    tm = min(256, t)
    hb = tm // CONV_HALO

    def kern(a_ref, g_ref, ah_ref, gh_ref, w_ref, b_ref, lg_ref, lb_ref, o_ref, u_ref, y_ref, ubuf):
        i = pl.program_id(0)
        u = a_ref[...] * _sigmoid(g_ref[...])
        uh = ah_ref[...] * _sigmoid(gh_ref[...])
        ubuf[0:CONV_HALO, :] = jnp.where(i > 0, uh, 0.0)
        ubuf[CONV_HALO:CONV_HALO + tm, :] = u
        y = jnp.zeros((tm, c), F32) + b_ref[...]
        for j in range(CONV_WIDTH):
            off = CONV_HALO - (CONV_WIDTH - 1) + j
            y = y + ubuf[off:off + tm, :] * w_ref[j:j + 1, :]
        mu = jnp.mean(y, axis=-1, keepdims=True)
        yc = y - mu
        rstd = lax.rsqrt(jnp.mean(yc * yc, axis=-1, keepdims=True) + NORM_EPS)
        yn = (yc * rstd) * lg_ref[...] + lb_ref[...]
        o_ref[...] = (yn * _sigmoid(yn)).astype(o_ref.dtype)
        u_ref[...] = u
        y_ref[...] = y

    def halo(cb):
        return pl.BlockSpec((CONV_HALO, c), lambda i: (jnp.maximum(i * hb - 1, 0), cb))

    vec = pl.BlockSpec((1, c), lambda i: (0, 0))
    tile = pl.BlockSpec((tm, c), lambda i: (i, 0))
    return pl.pallas_call(
        kern, name=name,
        out_shape=[jax.ShapeDtypeStruct((t, c), BF16), jax.ShapeDtypeStruct((t, c), F32),
                   jax.ShapeDtypeStruct((t, c), F32)],
        grid=(t // tm,),
        in_specs=[tile, pl.BlockSpec((tm, c), lambda i: (i, 1)), halo(0), halo(1),
                  pl.BlockSpec(conv_w.shape, lambda i: (0, 0)), vec, vec, vec],
        out_specs=[tile, tile, tile],
        scratch_shapes=[pltpu.VMEM((tm + CONV_HALO, c), F32)],
        compiler_params=_params(("arbitrary",)),
    )(p_conv, p_conv, p_conv, p_conv, conv_w, conv_b, ln_g, ln_b)


def _conv_bwd2(dy, u, p_conv, conv_w, name):
    t, c = dy.shape
    tm = min(256, t)
    hb = tm // CONV_HALO
    nt = t // tm
    last_halo = t // CONV_HALO - 1

    def kern(dy_ref, dyh_ref, u_ref, uh_ref, a_ref, g_ref, w_ref, dp_ref, dw_ref, dybuf, ubuf):
        i = pl.program_id(0)

        @pl.when(i == 0)
        def _():
            dw_ref[...] = jnp.zeros_like(dw_ref)

        dyv = dy_ref[...]
        dybuf[0:tm, :] = dyv
        dybuf[tm:tm + CONV_HALO, :] = jnp.where(i < nt - 1, dyh_ref[...], 0.0)
        ubuf[0:CONV_HALO, :] = jnp.where(i > 0, uh_ref[...], 0.0)
        ubuf[CONV_HALO:CONV_HALO + tm, :] = u_ref[...]
        du = jnp.zeros((tm, c), F32)
        for j in range(CONV_WIDTH):
            off = CONV_WIDTH - 1 - j
            du = du + dybuf[off:off + tm, :] * w_ref[j:j + 1, :]
            uoff = CONV_HALO - (CONV_WIDTH - 1) + j
            dw_ref[j:j + 1, :] += jnp.sum(dyv * ubuf[uoff:uoff + tm, :], axis=0, keepdims=True)
        a = a_ref[...]
        s = _sigmoid(g_ref[...])
        dp_ref[:, 0:c] = (du * s).astype(dp_ref.dtype)
        dp_ref[:, c:2 * c] = (du * a * (s * (1.0 - s))).astype(dp_ref.dtype)

    tile = pl.BlockSpec((tm, c), lambda i: (i, 0))
    return pl.pallas_call(
        kern, name=name,
        out_shape=[jax.ShapeDtypeStruct((t, 2 * c), BF16), jax.ShapeDtypeStruct((CONV_HALO, c), F32)],
        grid=(nt,),
        in_specs=[tile,
                  pl.BlockSpec((CONV_HALO, c), lambda i: (jnp.minimum((i + 1) * hb, last_halo), 0)),
                  tile,
                  pl.BlockSpec((CONV_HALO, c), lambda i: (jnp.maximum(i * hb - 1, 0), 0)),
                  tile, pl.BlockSpec((tm, c), lambda i: (i, 1)),
                  pl.BlockSpec(conv_w.shape, lambda i: (0, 0))],
        out_specs=[pl.BlockSpec((tm, 2 * c), lambda i: (i, 0)),
                   pl.BlockSpec((CONV_HALO, c), lambda i: (0, 0))],
        scratch_shapes=[pltpu.VMEM((tm + CONV_HALO, c), F32), pltpu.VMEM((tm + CONV_HALO, c), F32)],
        compiler_params=_params(("arbitrary",)),
    )(dy, dy, u, u, p_conv, p_conv, conv_w)


GLA_CHUNKS_PER_STEP = 16


def _gla_fwd(q_ck, k_kc, la_kc, v, name):
    nh, nc, ch, dk = q_ck.shape
    dv = v.shape[3]
    qscale = dk ** -0.5
    cb = min(GLA_CHUNKS_PER_STEP, nc)

    def kern(q_ref, k_ref, la_ref, v_ref, o_ref, st_ref, state_ref):
        mx = _tri(ch, True)

        @pl.when(pl.program_id(1) == 0)
        def _():
            state_ref[...] = jnp.zeros_like(state_ref)

        def step(n, state):
            la = la_ref[n]
            hi, lo = _split_bf16(la)
            de = _dot(hi, mx) + _dot(lo, mx)
            lam = jnp.exp(jnp.sum(la, axis=1, keepdims=True))
            kd = (k_ref[n] * jnp.exp(de)).astype(BF16)
            state = lam * state + _dot(kd, v_ref[n])
            st_ref[n] = state
            qs = (q_ref[n].astype(F32) * qscale).astype(BF16)
            o_ref[n] = _dot(qs, state.astype(BF16))
            return state

        state_ref[...] = lax.fori_loop(0, cb, step, state_ref[...])

    def spec(a, b):
        return pl.BlockSpec((None, cb, a, b), lambda h, j: (h, j, 0, 0))

    return pl.pallas_call(
        kern, name=name,
        out_shape=[jax.ShapeDtypeStruct((nh, nc, ch, dv), F32), jax.ShapeDtypeStruct((nh, nc, dk, dv), F32)],
        grid=(nh, nc // cb),
        in_specs=[spec(ch, dk), spec(dk, ch), spec(dk, ch), spec(ch, dv)],
        out_specs=[spec(ch, dv), spec(dk, dv)],
        scratch_shapes=[pltpu.VMEM((dk, dv), F32)],
        compiler_params=_params(("parallel", "arbitrary")),
    )(q_ck, k_kc, la_kc, v)


def _gla_bwd(q_ck, q_kc, k_ck, k_kc, la_ck, la_kc, v, states, states_prev, do, name):
    nh, nc, ch, dk = q_ck.shape
    dv = v.shape[3]
    qscale = dk ** -0.5
    cb = min(GLA_CHUNKS_PER_STEP, nc)
    nb = nc // cb

    def kern(q_ref, qt_ref, kck_ref, kkc_ref, lack_ref, lakc_ref, v_ref, st_ref, stp_ref, do_ref,
             dq_ref, dk_ref, dv_ref, dla_ref, g_ref):
        mx = _tri(ch, True)
        mxt = jnp.where(lax.broadcasted_iota(jnp.int32, (ch, ch), 0)
                        < lax.broadcasted_iota(jnp.int32, (ch, ch), 1), 1.0, 0.0).astype(BF16)

        @pl.when(pl.program_id(1) == 0)
        def _():
            g_ref[...] = jnp.zeros_like(g_ref)

        def step(i, g):
            n = cb - 1 - i
            la_kc = lakc_ref[n]
            hi, lo = _split_bf16(la_kc)
            de_kc = _dot(hi, mx) + _dot(lo, mx)
            lam = jnp.exp(jnp.sum(la_kc, axis=1, keepdims=True))
            hi2, lo2 = _split_bf16(lack_ref[n])
            de_ck = _dot(mxt, hi2) + _dot(mxt, lo2)
            edk = jnp.exp(de_kc)
            kd_kc = kkc_ref[n] * edk
            kd_ck = (kck_ref[n] * jnp.exp(de_ck)).astype(BF16)
            dob = do_ref[n].astype(BF16)
            dq_ref[n] = _dot(dob, st_ref[n].astype(BF16), _NT) * qscale
            qts = (qt_ref[n].astype(F32) * qscale).astype(BF16)
            ds = _dot(qts, dob) + g
            dsb = ds.astype(BF16)
            dlam = jnp.sum(ds * stp_ref[n], axis=1, keepdims=True)
            dkd = _dot(dsb, v_ref[n], _NT)
            dv_ref[n] = _dot(kd_ck, dsb)
            dk_ref[n] = dkd * edk
            dde = dkd * kd_kc
            h3, l3 = _split_bf16(dde)
            dla_ref[n] = _dot(h3, mxt) + _dot(l3, mxt) + dlam * lam
            return lam * ds

        g_ref[...] = lax.fori_loop(0, cb, step, g_ref[...])

    def spec(a, b):
        return pl.BlockSpec((None, cb, a, b), lambda h, j: (h, nb - 1 - j, 0, 0))

    return pl.pallas_call(
        kern, name=name,
        out_shape=[jax.ShapeDtypeStruct((nh, nc, ch, dk), F32), jax.ShapeDtypeStruct((nh, nc, dk, ch), F32),
                   jax.ShapeDtypeStruct((nh, nc, ch, dv), F32), jax.ShapeDtypeStruct((nh, nc, dk, ch), F32)],
        grid=(nh, nb),
        in_specs=[spec(ch, dk), spec(dk, ch), spec(ch, dk), spec(dk, ch), spec(ch, dk), spec(dk, ch),
                  spec(ch, dv), spec(dk, dv), spec(dk, dv), spec(ch, dv)],
        out_specs=[spec(ch, dk), spec(dk, ch), spec(ch, dv), spec(dk, ch)],
        scratch_shapes=[pltpu.VMEM((dk, dv), F32)],
        compiler_params=_params(("parallel", "arbitrary")),
    )(q_ck, q_kc, k_ck, k_kc, la_ck, la_kc, v, states, states_prev, do)


def _heads_rows(a, nh, blk):
    t = a.shape[0]
    d = a.shape[1] // nh
    return a.reshape(t // blk, blk, nh, d).transpose(2, 0, 1, 3)


def _heads_cols(a, nh, blk):
    t = a.shape[0]
    d = a.shape[1] // nh
    return a.T.reshape(nh, d, t // blk, blk).transpose(0, 2, 1, 3)


def _unheads_rows(a):
    nh, nb, blk, d = a.shape
    return a.transpose(1, 2, 0, 3).reshape(nb * blk, nh * d)


def _unheads_cols(a):
    nh, nb, d, blk = a.shape
    return a.transpose(0, 2, 1, 3).reshape(nh * d, nb * blk).T


def _ffn_fwd(x, w, pre, post, tag):
    h = _rms_fwd(x, pre, f"{tag}_rms")
    gate, up, act = _ffn_gate_up(h, w["gate"], w["up"], f"{tag}_gateup")
    o = _mm(act, w["down"], name=f"{tag}_down")
    x_out = _post_res(o, x, post, 0.5, f"{tag}_res")
    return x_out, (x, h, gate, up, act, o)


def _ffn_bwd(dx_out, saved, w, pre, post, tag):
    x, h, gate, up, act, o = saved
    do, dpost = _post_bwd(dx_out, o, post, 0.5, f"{tag}_bres")
    d_down = _mm(act, do, tn=True, out_dtype=BF16, name=f"{tag}_bwdown")
    dgate, dup = _ffn_dact(do, w["down"], gate, up, f"{tag}_bdact")
    dh_g = _mm(dgate, w["gate"], name=f"{tag}_bdhg")
    dh_u = _mm(dup, w["up"], name=f"{tag}_bdhu")
    d_gate = _mm(dgate, h, tn=True, out_dtype=BF16, name=f"{tag}_bwgate")
    d_up = _mm(dup, h, tn=True, out_dtype=BF16, name=f"{tag}_bwup")
    dx, dpre = _rms_bwd([dh_g, dh_u], x, pre, dx_out, f"{tag}_brms")
    return dx, dict(gate=d_gate, up=d_up, down=d_down, pre=dpre, post=dpost)


def _mixer_fwd(x, w, tag):
    t, d = x.shape
    bw = BRANCH_WIDTH
    h = _rms_fwd(x, w["pre"], f"{tag}_rms")
    p_sb = _mm(h, w["in_sb"], nt=True, out_dtype=BF16, name=f"{tag}_insb")
    p_conv = _mm(h, w["in_conv"], nt=True, name=f"{tag}_inconv")
    p_gla = _mm(h, w["in_gla"], nt=True, name=f"{tag}_ingla")
    p_gate = _mm(h, w["in_gate"], nt=True, name=f"{tag}_ingate")

    bk = min(SB_BK, t)
    q = p_sb[:, 0:bw].reshape(t, SB_HEADS, SB_HEAD_DIM).transpose(1, 0, 2)
    kt = _heads_cols(p_sb[:, bw:2 * bw], SB_HEADS, bk)
    vt = _heads_cols(p_sb[:, 2 * bw:3 * bw], SB_HEADS, bk)
    sb_o = _sb_fwd(q, kt, vt, f"{tag}_sb")
    sb_out = sb_o.transpose(1, 0, 2).reshape(t, bw)

    conv_out, conv_u, conv_y = _conv_fwd(p_conv, w["conv_w"], w["conv_b"], w["ln_g"], w["ln_b"], f"{tag}_conv")

    kd, vd = GLA_KEY_DIM, GLA_VALUE_DIM
    lr = p_gla[:, 2 * kd + 2 * vd:]
    pre_a = _mm(lr, w["alpha"], name=f"{tag}_alpha")
    la = _la_fwd(pre_a, w["b_alpha"], f"{tag}_la")
    gq = _heads_rows(p_gla[:, 0:kd].astype(BF16), GLA_HEADS, CHUNK)
    gk_kc = _heads_cols(p_gla[:, kd:2 * kd], GLA_HEADS, CHUNK)
    gv = _heads_rows(p_gla[:, 2 * kd:2 * kd + vd].astype(BF16), GLA_HEADS, CHUNK)
    la_kc = _heads_cols(la, GLA_HEADS, CHUNK)
    gla_o4, states = _gla_fwd(gq, gk_kc, la_kc, gv, f"{tag}_gla")
    gla_o = _unheads_rows(gla_o4)
    r_block = (2 * kd + vd) // vd
    gla_out = _gla_post_fwd(gla_o, p_gla, r_block, w["gnorm"], f"{tag}_glapost")

    branches = (sb_out.astype(BF16), conv_out, gla_out)
    bds = [_mm(branches[j], w["branch"][j], nt=True, name=f"{tag}_br{j}") for j in range(N_BRANCHES)]
    merged = _merge_fwd(bds, p_gate, f"{tag}_merge")
    mo = _mm(merged, w["out"], name=f"{tag}_out")
    x_out = _post_res(mo, x, w["post"], 1.0, f"{tag}_res")
    saved = dict(x=x, h=h, p_sb=p_sb, p_conv=p_conv, p_gla=p_gla, p_gate=p_gate, sb_o=sb_o, conv_u=conv_u,
                 sb_q=q, sb_kt=kt, sb_vt=vt, gla_q=gq, gla_k_kc=gk_kc, gla_v=gv, gla_la_kc=la_kc,
                 conv_y=conv_y, pre_a=pre_a, la=la, states=states, gla_o=gla_o, branches=branches, bds=bds,
                 merged=merged, mo=mo)
    return x_out, saved


def _mixer_bwd(dx_out, s, w, tag):
    x = s["x"]
    t, d = x.shape
    bw = BRANCH_WIDTH
    kd, vd = GLA_KEY_DIM, GLA_VALUE_DIM
    grads = {}
    dmo, grads["post"] = _post_bwd(dx_out, s["mo"], w["post"], 1.0, f"{tag}_bres")
    dmerged = _mm(dmo, w["out"], nt=True, name=f"{tag}_bdmerged")
    grads["out"] = _mm(s["merged"], dmo, tn=True, out_dtype=BF16, name=f"{tag}_bwout")
    dbd0, dbd1, dbd2, dp_gate = _merge_bwd(dmerged, s["bds"], s["p_gate"], f"{tag}_bmerge")
    dbds = (dbd0, dbd1, dbd2)
    dbranch = [_mm(dbds[j], w["branch"][j], name=f"{tag}_bdbr{j}") for j in range(N_BRANCHES)]
    grads["branch"] = jnp.stack([_mm(dbds[j], s["branches"][j], tn=True, out_dtype=BF16, name=f"{tag}_bwbr{j}")
                                 for j in range(N_BRANCHES)])

    p_sb = s["p_sb"]
    bk = min(SB_BK, t)
    q, kt, vt = s["sb_q"], s["sb_kt"], s["sb_vt"]
    qt = p_sb[:, 0:bw].reshape(t, SB_HEADS, SB_HEAD_DIM).transpose(1, 2, 0)
    dsb = dbranch[0].reshape(t, SB_HEADS, SB_HEAD_DIM)
    dout = dsb.transpose(1, 0, 2)
    doutt = dsb.transpose(1, 2, 0).astype(BF16)
    dq, dkt, dvt = _sb_bwd(q, qt, kt, vt, s["sb_o"], dout, doutt, f"{tag}_bsb")
    dp_sb = jnp.concatenate([dq.transpose(1, 0, 2).reshape(t, bw), _unheads_cols(dkt), _unheads_cols(dvt)],
                            axis=1).astype(BF16)

    dy, d_lng, d_lnb, d_cb = _conv_bwd1(dbranch[1], s["conv_y"], w["ln_g"], w["ln_b"], f"{tag}_bconv1")
    dp_conv, d_cw = _conv_bwd2(dy, s["conv_u"], s["p_conv"], w["conv_w"], f"{tag}_bconv2")
    grads.update(conv_w=d_cw[:CONV_WIDTH], conv_b=d_cb, ln_g=d_lng, ln_b=d_lnb)

    p_gla = s["p_gla"]
    r_block = (2 * kd + vd) // vd
    do, dr, grads["gnorm"] = _gla_post_bwd(dbranch[2], s["gla_o"], p_gla, r_block, w["gnorm"], f"{tag}_bglapost")
    gqf, gkf = p_gla[:, 0:kd], p_gla[:, kd:2 * kd]
    gq_b = gqf.astype(BF16)
    dq4, dk4, dv4, dla4 = _gla_bwd(
        s["gla_q"], _heads_cols(gq_b, GLA_HEADS, CHUNK),
        _heads_rows(gkf, GLA_HEADS, CHUNK), s["gla_k_kc"],
        _heads_rows(s["la"], GLA_HEADS, CHUNK), s["gla_la_kc"], s["gla_v"],
        s["states"], jnp.pad(s["states"][:, :-1], ((0, 0), (1, 0), (0, 0), (0, 0))),
        _heads_rows(do, GLA_HEADS, CHUNK), f"{tag}_bgla")
    dla = _unheads_cols(dla4)
    dpre_a, grads["b_alpha"] = _la_bwd(dla, s["pre_a"], w["b_alpha"], f"{tag}_bla")
    lr = p_gla[:, 2 * kd + 2 * vd:]
    dlr = _mm(dpre_a, w["alpha"], nt=True, out_dtype=BF16, name=f"{tag}_bdlr")
    grads["alpha"] = _mm(lr, dpre_a, tn=True, name=f"{tag}_bwalpha")
    dp_gla = jnp.concatenate([_unheads_rows(dq4).astype(BF16), _unheads_cols(dk4).astype(BF16),
                              _unheads_rows(dv4).astype(BF16), dr, dlr], axis=1)

    dps = dict(in_sb=dp_sb, in_conv=dp_conv, in_gla=dp_gla, in_gate=dp_gate)
    dh_parts = []
    for key, dp in dps.items():
        dh_parts.append(_mm(dp, w[key], name=f"{tag}_bdh_{key}"))
        grads[key] = _mm(dp, s["h"], tn=True, out_dtype=BF16, name=f"{tag}_bw_{key}")
    dx, grads["pre"] = _rms_bwd(dh_parts, x, w["pre"], dx_out, f"{tag}_brms")
    return dx, grads


def _local_step(x, target, layers):
    saved = []
    for l, w in enumerate(layers):
        x, s1 = _ffn_fwd(x, w["ffn1"], w["pre0"], w["post0"], f"l{l}_f1")
        x, s2 = _mixer_fwd(x, w["mix"], f"l{l}_mx")
        x, s3 = _ffn_fwd(x, w["ffn2"], w["pre2"], w["post2"], f"l{l}_f2")
        saved.append((s1, s2, s3))
    dx, sq = _loss_fwd(x, target, "loss")
    grads = [None] * len(layers)
    for l in reversed(range(len(layers))):
        w = layers[l]
        s1, s2, s3 = saved[l]
        dx, g3 = _ffn_bwd(dx, s3, w["ffn2"], w["pre2"], w["post2"], f"l{l}_f2")
        dx, g2 = _mixer_bwd(dx, s2, w["mix"], f"l{l}_mx")
        dx, g1 = _ffn_bwd(dx, s1, w["ffn1"], w["pre0"], w["post0"], f"l{l}_f1")
        grads[l] = dict(ffn1=g1, mix=g2, ffn2=g3)
    return sq, dx, grads


_ANY = pl.BlockSpec(memory_space=pl.ANY)
_MESH = pl.DeviceIdType.MESH
N_CHIPS = 4


def _all_gather(p, name):
    def body(p_ref, out_ref, send_sems, recv_sems, local_sem):
        x, y, c = lax.axis_index("x"), lax.axis_index("y"), lax.axis_index("c")
        me, sibling = (x, y, c), (x, y, 1 - c)
        chips = [(1 - x, y), (x, 1 - y), (1 - x, 1 - y)]

        def rows(px, py, pc):
            return out_ref.at[4 * px + 2 * py + pc]

        def copy(k, block, to, src=None):
            return pltpu.make_async_remote_copy(
                src_ref=rows(*block) if src is None else src, dst_ref=rows(*block),
                send_sem=send_sems.at[k], recv_sem=recv_sems.at[k], device_id=to, device_id_type=_MESH)

        mine = pltpu.make_async_copy(p_ref, rows(*me), local_sem)
        mine.start()
        first = [copy(0, me, sibling, src=p_ref)]
        first += [copy(1 + j, me, (*chip, c), src=p_ref) for j, chip in enumerate(chips)]
        for cp in first:
            cp.start()
        passed = [copy(4 + j, (*chip, c), sibling) for j, chip in enumerate(chips)]
        for j, chip in enumerate(chips):
            copy(1 + j, (*chip, c), me).wait_recv()
            passed[j].start()
        copy(0, sibling, me).wait_recv()
        for j, chip in enumerate(chips):
            copy(4 + j, (*chip, 1 - c), me).wait_recv()
        for cp in first + passed:
            cp.wait_send()
        mine.wait()

    return pl.pallas_call(
        body, name=name,
        out_shape=jax.ShapeDtypeStruct((N_DEV,) + p.shape, p.dtype),
        in_specs=[_ANY], out_specs=_ANY,
        scratch_shapes=[pltpu.SemaphoreType.DMA((7,)), pltpu.SemaphoreType.DMA((7,)), pltpu.SemaphoreType.DMA],
    )(p)


def _exchange_sibling(src, name):
    def body(src_ref, out_ref, send_sems, recv_sems):
        x, y, c = lax.axis_index("x"), lax.axis_index("y"), lax.axis_index("c")
        copies = [pltpu.make_async_remote_copy(
            src_ref=src_ref.at[j, 1 - c], dst_ref=out_ref.at[j], send_sem=send_sems.at[j],
            recv_sem=recv_sems.at[j], device_id=(x, y, 1 - c), device_id_type=_MESH) for j in range(N_CHIPS)]
        for cp in copies:
            cp.start()
        for cp in copies:
            cp.wait()

    return pl.pallas_call(
        body, name=name,
        out_shape=jax.ShapeDtypeStruct((N_CHIPS,) + src.shape[2:], src.dtype),
        in_specs=[_ANY], out_specs=_ANY,
        scratch_shapes=[pltpu.SemaphoreType.DMA((N_CHIPS,)), pltpu.SemaphoreType.DMA((N_CHIPS,))],
    )(src)


def _add_own(src, got, name):
    _, _, r, cols = src.shape
    tr = _div_tile(r, 512, 16)
    c = lax.axis_index("c").astype(jnp.int32).reshape(1)

    def kern(c_ref, a_ref, b_ref, o_ref):
        o_ref[...] = (a_ref[...].astype(F32) + b_ref[...].astype(F32)).astype(o_ref.dtype)

    grid_spec = pltpu.PrefetchScalarGridSpec(
        num_scalar_prefetch=1, grid=(N_CHIPS, r // tr),
        in_specs=[pl.BlockSpec((None, None, tr, cols), lambda j, i, c_ref: (j, c_ref[0], i, 0)),
                  pl.BlockSpec((None, tr, cols), lambda j, i, c_ref: (j, i, 0))],
        out_specs=pl.BlockSpec((None, tr, cols), lambda j, i, c_ref: (j, i, 0)))
    return pl.pallas_call(
        kern, name=name, grid_spec=grid_spec,
        out_shape=jax.ShapeDtypeStruct((N_CHIPS, r, cols), src.dtype),
        compiler_params=_params(("arbitrary", "arbitrary")),
    )(c, src, got)


def _exchange_chips(part, name):
    def body(part_ref, out_ref, send_sems, recv_sems, local_sem):
        x, y, c = lax.axis_index("x"), lax.axis_index("y"), lax.axis_index("c")
        my_chip = 2 * x + y
        chips = [(1 - x, y), (x, 1 - y), (1 - x, 1 - y)]
        mine = pltpu.make_async_copy(part_ref.at[my_chip], out_ref.at[my_chip], local_sem)
        mine.start()
        copies = []
        for k, (px, py) in enumerate(chips):
            copies.append(pltpu.make_async_remote_copy(
                src_ref=part_ref.at[2 * px + py], dst_ref=out_ref.at[my_chip],
                send_sem=send_sems.at[k], recv_sem=recv_sems.at[k], device_id=(px, py, c), device_id_type=_MESH))
        for cp in copies:
            cp.start()
        for k, (px, py) in enumerate(chips):
            pltpu.make_async_remote_copy(
                src_ref=part_ref.at[my_chip], dst_ref=out_ref.at[2 * px + py],
                send_sem=send_sems.at[k], recv_sem=recv_sems.at[k], device_id=(px, py, c),
                device_id_type=_MESH).wait_recv()
        for cp in copies:
            cp.wait_send()
        mine.wait()

    return pl.pallas_call(
        body, name=name,
        out_shape=jax.ShapeDtypeStruct(part.shape, part.dtype),
        in_specs=[_ANY], out_specs=_ANY,
        scratch_shapes=[pltpu.SemaphoreType.DMA((3,)), pltpu.SemaphoreType.DMA((3,)), pltpu.SemaphoreType.DMA],
    )(part)


def _sum_chips(parts, name):
    _, r, cols = parts.shape
    tr = _div_tile(r, 512, 16)

    def kern(p_ref, o_ref):
        p = [p_ref[j].astype(F32) for j in range(N_CHIPS)]
        o_ref[...] = ((p[0] + p[1]) + p[2]) + p[3]

    return pl.pallas_call(
        kern, name=name, grid=(r // tr,),
        in_specs=[pl.BlockSpec((4, tr, cols), lambda i: (0, i, 0))],
        out_specs=pl.BlockSpec((tr, cols), lambda i: (i, 0)),
        out_shape=jax.ShapeDtypeStruct((r, cols), F32),
        compiler_params=_params(("arbitrary",)),
    )(parts)


_SHARDED = {
    "norm_pre": (1, True), "norm_post": (1, True),
    "ffn1_w_gate": (1, False), "ffn1_w_up": (1, False), "ffn1_w_down": (0, False),
    "ffn2_w_gate": (1, False), "ffn2_w_up": (1, False), "ffn2_w_down": (0, False),
    "w_in": (1, False), "conv_w": (1, True), "gla_w_alpha": (1, False),
    "w_branch": (2, False), "w_out": (0, False),
}
_TRANSPOSED = ("ffn1_w_gate", "ffn1_w_up", "ffn2_w_gate", "ffn2_w_up", "w_in", "w_branch")


def _storage(name, a):
    return jnp.swapaxes(a, -1, -2) if name in _TRANSPOSED else a


def _storage_axis(name, shape):
    axis = _SHARDED[name][0]
    if name in _TRANSPOSED and axis >= len(shape) - 2:
        axis = 2 * len(shape) - 3 - axis
    return axis


_REPLICATED = ("conv_b", "conv_ln_g", "conv_ln_b", "gla_b_alpha", "gla_norm_g")
_WEIGHTS = ("norm_pre", "norm_post", "ffn1_w_gate", "ffn1_w_up", "ffn1_w_down", "ffn2_w_gate", "ffn2_w_up",
            "ffn2_w_down", "w_in", "conv_w", "conv_b", "conv_ln_g", "conv_ln_b", "gla_w_alpha", "gla_b_alpha",
            "gla_norm_g", "w_branch", "w_out")


PACK_ROW_UNIT = 16
BIG_PIECE = PACK_ROW_UNIT * PACK_COLS


def _rows_of(a, nlead):
    lead = a.shape[:nlead]
    n = math.prod(a.shape[nlead:])
    if n % PACK_COLS:
        flat = a.reshape(lead + (n,))
        flat = jnp.pad(flat, [(0, 0)] * nlead + [(0, -n % PACK_COLS)])
        n += -n % PACK_COLS
        a = flat
    rows = a.reshape(lead + (n // PACK_COLS, PACK_COLS))
    return jnp.pad(rows, [(0, 0)] * nlead + [(0, -rows.shape[nlead] % PACK_ROW_UNIT), (0, 0)])


def _padded_rows(n):
    rows = -(-n // PACK_COLS)
    return -(-rows // PACK_ROW_UNIT) * PACK_ROW_UNIT


def _is_big(shape, exact):
    return not exact and math.prod(shape) >= BIG_PIECE


def _pack_weights(shards, l):
    big, small = [], []
    for name, (_, exact) in _SHARDED.items():
        a = _storage(name, shards[name][l])
        if _is_big(a.shape, exact):
            big.append(_rows_of(a.astype(BF16), 0))
        elif exact:
            small.append(lax.bitcast_convert_type(a, BF16).reshape(-1))
        else:
            small.append(a.astype(BF16).reshape(-1))
    return jnp.concatenate(big + [_rows_of(jnp.concatenate(small), 0)], axis=0)


def _unpack_weights(gathered, shards, l):
    full, r0 = {}, 0

    def merge(seg, axis, shp):
        seg = jnp.moveaxis(seg, 0, axis)
        return seg.reshape(shp[:axis] + (N_DEV * shp[axis],) + shp[axis + 1:])

    def shape_axis(name):
        shp = jax.eval_shape(lambda a: _storage(name, a), shards[name][l]).shape
        return shp, _storage_axis(name, shp)

    for name, (_, exact) in _SHARDED.items():
        shp, axis = shape_axis(name)
        if _is_big(shp, exact):
            n = math.prod(shp)
            nr = _padded_rows(n)
            seg = gathered[:, r0:r0 + nr].reshape(N_DEV, -1)[:, :n] if n % PACK_COLS else \
                gathered[:, r0:r0 + n // PACK_COLS]
            full[name] = merge(seg.reshape((N_DEV,) + shp), axis, shp)
            r0 += nr
    flat, off = gathered[:, r0:].reshape(N_DEV, -1), 0
    for name, (_, exact) in _SHARDED.items():
        shp, axis = shape_axis(name)
        if not _is_big(shp, exact):
            n = math.prod(shp) * (2 if exact else 1)
            seg = flat[:, off:off + n]
            off += n
            if exact:
                seg = lax.bitcast_convert_type(seg.reshape((N_DEV,) + shp + (2,)), F32)
            full[name] = merge(seg.reshape((N_DEV,) + shp), axis, shp)
    return full


def _pack_grads(full_grads, repl_grads):
    big, small = [], []
    for name in _SHARDED:
        g = full_grads[name].astype(BF16)
        shp = g.shape
        axis = _storage_axis(name, shp)
        g = g.reshape(shp[:axis] + (N_DEV, shp[axis] // N_DEV) + shp[axis + 1:])
        g = jnp.moveaxis(g, axis, 0)
        if _is_big(g.shape[1:], False):
            big.append(_rows_of(g, 1))
        else:
            small.append(g.reshape(N_DEV, -1))
    for name in _REPLICATED:
        small.append(jnp.broadcast_to(repl_grads[name].reshape(1, -1).astype(BF16),
                                      (N_DEV, repl_grads[name].size)))
    rows = jnp.concatenate(big + [_rows_of(jnp.concatenate(small, axis=1), 1)], axis=1)
    return rows.reshape((N_CHIPS, 2) + rows.shape[1:])


def _unpack_grads(summed, shards):
    out, r0 = {}, 0

    def storage_shape(name):
        return jax.eval_shape(lambda a: _storage(name, a), shards[name][0]).shape

    for name in _SHARDED:
        shp = storage_shape(name)
        if _is_big(shp, False):
            n = math.prod(shp)
            seg = summed[r0:r0 + _padded_rows(n)].reshape(-1)[:n] if n % PACK_COLS else \
                summed[r0:r0 + n // PACK_COLS]
            out[name] = _storage(name, seg.reshape(shp))
            r0 += _padded_rows(n)
    flat, off = summed[r0:].reshape(-1), 0
    for name in tuple(_SHARDED) + _REPLICATED:
        shp = shards[name].shape[1:] if name in _REPLICATED else storage_shape(name)
        if name in _REPLICATED or not _is_big(shp, False):
            n = math.prod(shp)
            seg = flat[off:off + n].reshape(shp)
            out[name] = seg if name in _REPLICATED else _storage(name, seg)
            off += n
    return out


def _adamw_natural(w, g, m, v, name):
    shp = w.shape
    view = lambda a: a.reshape(-1, shp[-1])
    outs = _adamw(view(w), view(g), view(m), view(v), name)
    return [o.reshape(shp) for o in outs]


def _layer_weights(full, repl, l, d_model):
    bw = BRANCH_WIDTH
    w_in = full["w_in"]
    o_conv, o_gq, o_lr = 3 * bw, 5 * bw, 5 * bw + 2 * GLA_KEY_DIM + 2 * GLA_VALUE_DIM
    o_gate = o_lr + GLA_GATE_RANK
    lr_pad = LANE - GLA_GATE_RANK
    in_gla = jnp.concatenate([w_in[o_gq:o_gate], jnp.zeros((lr_pad, d_model), w_in.dtype)], axis=0)
    alpha = jnp.concatenate([full["gla_w_alpha"], jnp.zeros((lr_pad, GLA_KEY_DIM), BF16)], axis=0)
    row = lambda a: a.reshape(1, -1)
    mix = dict(pre=row(full["norm_pre"][1]), post=row(full["norm_post"][1]),
               in_sb=w_in[0:o_conv], in_conv=w_in[o_conv:o_gq], in_gla=in_gla, in_gate=w_in[o_gate:],
               conv_w=full["conv_w"], conv_b=row(repl["conv_b"][l]), ln_g=row(repl["conv_ln_g"][l]),
               ln_b=row(repl["conv_ln_b"][l]), alpha=alpha, b_alpha=row(repl["gla_b_alpha"][l]),
               gnorm=row(repl["gla_norm_g"][l]), branch=full["w_branch"], out=full["w_out"])
    return dict(
        pre0=row(full["norm_pre"][0]), post0=row(full["norm_post"][0]),
        pre2=row(full["norm_pre"][2]), post2=row(full["norm_post"][2]),
        ffn1=dict(gate=full["ffn1_w_gate"], up=full["ffn1_w_up"], down=full["ffn1_w_down"]),
        ffn2=dict(gate=full["ffn2_w_gate"], up=full["ffn2_w_up"], down=full["ffn2_w_down"]),
        mix=mix)


def _full_grads(g):
    mix = g["mix"]
    kd, vd = GLA_KEY_DIM, GLA_VALUE_DIM
    d_in = jnp.concatenate([mix["in_sb"], mix["in_conv"], mix["in_gla"][:2 * kd + 2 * vd + GLA_GATE_RANK],
                            mix["in_gate"]], axis=0)
    full = {
        "norm_pre": jnp.concatenate([g["ffn1"]["pre"], mix["pre"], g["ffn2"]["pre"]], axis=0),
        "norm_post": jnp.concatenate([g["ffn1"]["post"], mix["post"], g["ffn2"]["post"]], axis=0),
        "ffn1_w_gate": g["ffn1"]["gate"], "ffn1_w_up": g["ffn1"]["up"], "ffn1_w_down": g["ffn1"]["down"],
        "ffn2_w_gate": g["ffn2"]["gate"], "ffn2_w_up": g["ffn2"]["up"], "ffn2_w_down": g["ffn2"]["down"],
        "w_in": d_in, "conv_w": mix["conv_w"], "gla_w_alpha": mix["alpha"][:GLA_GATE_RANK],
        "w_branch": mix["branch"], "w_out": mix["out"],
    }
    repl = {"conv_b": mix["conv_b"], "conv_ln_g": mix["ln_g"], "conv_ln_b": mix["ln_b"],
            "gla_b_alpha": mix["b_alpha"], "gla_norm_g": mix["gnorm"]}
    return full, repl


def kernel(x, norm_pre, norm_post, ffn1_w_gate, ffn1_w_up, ffn1_w_down, ffn2_w_gate, ffn2_w_up, ffn2_w_down, w_in, conv_w, conv_b, conv_ln_g, conv_ln_b, gla_w_alpha, gla_b_alpha, gla_norm_g, w_branch, w_out, loss_target, m_norm_pre, m_norm_post, m_ffn1_w_gate, m_ffn1_w_up, m_ffn1_w_down, m_ffn2_w_gate, m_ffn2_w_up, m_ffn2_w_down, m_w_in, m_conv_w, m_conv_b, m_conv_ln_g, m_conv_ln_b, m_gla_w_alpha, m_gla_b_alpha, m_gla_norm_g, m_w_branch, m_w_out, v_norm_pre, v_norm_post, v_ffn1_w_gate, v_ffn1_w_up, v_ffn1_w_down, v_ffn2_w_gate, v_ffn2_w_up, v_ffn2_w_down, v_w_in, v_conv_w, v_conv_b, v_conv_ln_g, v_conv_ln_b, v_gla_w_alpha, v_gla_b_alpha, v_gla_norm_g, v_w_branch, v_w_out):
    weights = dict(norm_pre=norm_pre, norm_post=norm_post, ffn1_w_gate=ffn1_w_gate, ffn1_w_up=ffn1_w_up,
                   ffn1_w_down=ffn1_w_down, ffn2_w_gate=ffn2_w_gate, ffn2_w_up=ffn2_w_up, ffn2_w_down=ffn2_w_down,
                   w_in=w_in, conv_w=conv_w, conv_b=conv_b, conv_ln_g=conv_ln_g, conv_ln_b=conv_ln_b,
                   gla_w_alpha=gla_w_alpha, gla_b_alpha=gla_b_alpha, gla_norm_g=gla_norm_g, w_branch=w_branch,
                   w_out=w_out)
    moments_m = dict(norm_pre=m_norm_pre, norm_post=m_norm_post, ffn1_w_gate=m_ffn1_w_gate, ffn1_w_up=m_ffn1_w_up,
                     ffn1_w_down=m_ffn1_w_down, ffn2_w_gate=m_ffn2_w_gate, ffn2_w_up=m_ffn2_w_up,
                     ffn2_w_down=m_ffn2_w_down, w_in=m_w_in, conv_w=m_conv_w, conv_b=m_conv_b,
                     conv_ln_g=m_conv_ln_g, conv_ln_b=m_conv_ln_b, gla_w_alpha=m_gla_w_alpha,
                     gla_b_alpha=m_gla_b_alpha, gla_norm_g=m_gla_norm_g, w_branch=m_w_branch, w_out=m_w_out)
    moments_v = dict(norm_pre=v_norm_pre, norm_post=v_norm_post, ffn1_w_gate=v_ffn1_w_gate, ffn1_w_up=v_ffn1_w_up,
                     ffn1_w_down=v_ffn1_w_down, ffn2_w_gate=v_ffn2_w_gate, ffn2_w_up=v_ffn2_w_up,
                     ffn2_w_down=v_ffn2_w_down, w_in=v_w_in, conv_w=v_conv_w, conv_b=v_conv_b,
                     conv_ln_g=v_conv_ln_g, conv_ln_b=v_conv_ln_b, gla_w_alpha=v_gla_w_alpha,
                     gla_b_alpha=v_gla_b_alpha, gla_norm_g=v_gla_norm_g, w_branch=v_w_branch, w_out=v_w_out)
    n_layers = norm_pre.shape[0]
    t, d_model = x.shape[1], x.shape[2]

    layers = []
    for l in range(n_layers):
        gathered = _all_gather(_pack_weights(weights, l), f"gather_l{l}")
        full = _unpack_weights(gathered, weights, l)
        layers.append(_layer_weights(full, weights, l, d_model))

    sq, dx, grads = _local_step(x[0], loss_target[0], layers)
    loss = lax.psum(0.5 * jnp.sum(sq) / d_model, MESH_AXES)

    layer_grads = []
    for l in range(n_layers):
        packed = _pack_grads(*_full_grads(grads[l]))
        got = _exchange_sibling(packed, f"rs_sibling_l{l}")
        part = _add_own(packed, got, f"rs_add_l{l}")
        parts = _exchange_chips(part, f"rs_chips_l{l}")
        layer_grads.append(_unpack_grads(_sum_chips(parts, f"rs_sum_l{l}"), weights))

    g_out, d_out, m_out, v_out = [], [], [], []
    for name in _WEIGHTS:
        g = jnp.stack([lg[name] for lg in layer_grads])
        delta, new_m, new_v = _adamw_natural(weights[name], g, moments_m[name], moments_v[name], f"adamw_{name}")
        g_out.append(g)
        d_out.append(delta)
        m_out.append(new_m)
        v_out.append(new_v)
    return tuple([loss, dx[None]] + g_out + d_out + m_out + v_out)
```

```python
import functools
import math

import jax
import jax.numpy as jnp
from jax import lax
from jax.experimental import pallas as pl
from jax.experimental.pallas import tpu as pltpu

F32 = jnp.float32
BF16 = jnp.bfloat16

VMEM_LIMIT_BYTES = 48 * 1024 * 1024
LANE = 128

NORM_EPS = 1e-6
CHUNK = 64
N_BRANCHES = 3
BRANCH_WIDTH = 512
SB_HEADS = 8
SB_HEAD_DIM = 64
CONV_WIDTH = 31
CONV_HALO = 32
GLA_HEADS = 4
GLA_HEAD_K = 64
GLA_HEAD_V = 128
GLA_KEY_DIM = GLA_HEADS * GLA_HEAD_K
GLA_VALUE_DIM = GLA_HEADS * GLA_HEAD_V
GLA_GATE_RANK = 16
GLA_GATE_TAU = 16.0
SB_BQ = 1024
SB_BK = 128

ADAM_LR = 0.001
ADAM_B1 = 0.9
ADAM_B2 = 0.999
ADAM_EPS = 1e-08
ADAM_WD = 0.01
ADAM_STEP = 10

N_DEV = 8
MESH_AXES = ("x", "y", "c")
PACK_COLS = 1024


def _params(sem):
    return pltpu.CompilerParams(dimension_semantics=sem, vmem_limit_bytes=VMEM_LIMIT_BYTES)


def _div_tile(n, cap, unit):
    if n <= cap:
        return n
    best = None
    for t in range(unit, cap + 1, unit):
        if n % t == 0:
            best = t
    assert best is not None, (n, cap, unit)
    return best


_NN = (((1,), (0,)), ((), ()))
_NT = (((1,), (1,)), ((), ()))
_TN = (((0,), (0,)), ((), ()))


def _dot(a, b, dims=_NN):
    return lax.dot_general(a, b, dims, preferred_element_type=F32)


def _sigmoid(v):
    return 1.0 / (1.0 + jnp.exp(-v))


def _mm(a, b, *, nt=False, tn=False, out_dtype=F32, name):
    a = a.astype(BF16)
    b = b.astype(BF16)
    k, m = a.shape[::-1] if not tn else a.shape
    n = b.shape[0] if nt else b.shape[1]
    assert (b.shape[1] if nt else b.shape[0]) == k
    tm = _div_tile(m, 2048, LANE) if tn else _div_tile(m, 512, 16)
    tn_ = _div_tile(n, 2048, LANE)
    tk = _div_tile(k, 2048, LANE)
    nk = k // tk
    dims = _NT if nt else (_TN if tn else _NN)

    def kern(a_ref, b_ref, o_ref, acc_ref):
        kk = pl.program_id(2)

        @pl.when(kk == 0)
        def _():
            acc_ref[...] = jnp.zeros_like(acc_ref)

        acc_ref[...] += _dot(a_ref[...], b_ref[...], dims)

        @pl.when(kk == nk - 1)
        def _():
            o_ref[...] = acc_ref[...].astype(o_ref.dtype)

    b_spec = (pl.BlockSpec((tn_, tk), lambda i, j, kk: (j, kk)) if nt
              else pl.BlockSpec((tk, tn_), lambda i, j, kk: (kk, j)))
    return pl.pallas_call(
        kern, name=name,
        out_shape=jax.ShapeDtypeStruct((m, n), out_dtype),
        grid=(m // tm, n // tn_, nk),
        in_specs=[pl.BlockSpec((tk, tm), lambda i, j, kk: (kk, i)) if tn
                  else pl.BlockSpec((tm, tk), lambda i, j, kk: (i, kk)), b_spec],
        out_specs=pl.BlockSpec((tm, tn_), lambda i, j, kk: (i, j)),
        scratch_shapes=[pltpu.VMEM((tm, tn_), F32)],
        compiler_params=_params(("parallel", "parallel", "arbitrary")),
    )(a, b)


def _rowwise(name, body, mats, vecs, outs, sums=(), tm=256):
    mats = [m if isinstance(m, tuple) else (m, 0, m.shape[1]) for m in mats]
    t = mats[0][0].shape[0]
    tm = _div_tile(t, tm, 8)
    nm, nv, no, ns = len(mats), len(vecs), len(outs), len(sums)

    def kern(*refs):
        i = pl.program_id(0)
        ins = [r[...] for r in refs[:nm + nv]]
        res = body(*ins)
        out_vals, sum_vals = res[:no], res[no:]
        for r, val in zip(refs[nm + nv:nm + nv + no], out_vals):
            if isinstance(val, (list, tuple)):
                off = 0
                for piece in val:
                    w = piece.shape[1]
                    r[:, off:off + w] = piece.astype(r.dtype)
                    off += w
            else:
                r[...] = val.astype(r.dtype)
        if ns:
            sum_refs = refs[nm + nv + no:]

            @pl.when(i == 0)
            def _():
                for r in sum_refs:
                    r[...] = jnp.zeros_like(r)

            for r, val in zip(sum_refs, sum_vals):
                r[...] += jnp.sum(val, axis=0, keepdims=True)

    in_specs = [pl.BlockSpec((tm, w), functools.partial(lambda i, cb: (i, cb), cb=cb)) for (_, cb, w) in mats]
    in_specs += [pl.BlockSpec(v.shape, lambda i: (0, 0)) for v in vecs]
    out_specs = [pl.BlockSpec((tm, w), lambda i: (i, 0)) for (w, _) in outs]
    out_specs += [pl.BlockSpec((1, w), lambda i: (0, 0)) for w in sums]
    out_shape = [jax.ShapeDtypeStruct((t, w), dt) for (w, dt) in outs]
    out_shape += [jax.ShapeDtypeStruct((1, w), F32) for w in sums]
    return pl.pallas_call(
        kern, name=name, out_shape=out_shape, grid=(t // tm,),
        in_specs=in_specs, out_specs=out_specs,
        compiler_params=_params(("arbitrary",)),
    )(*[m[0] for m in mats], *vecs)


def _rms_fwd(x, g, name):
    d = x.shape[1]

    def body(xv, gv):
        r = lax.rsqrt(jnp.mean(xv * xv, axis=-1, keepdims=True) + NORM_EPS)
        return ((xv * r) * gv,)

    return _rowwise(name, body, [x], [g], [(d, BF16)])[0]


def _post_res(o, x, g, c, name):
    d = x.shape[1]

    def body(ov, xv, gv):
        r = lax.rsqrt(jnp.mean(ov * ov, axis=-1, keepdims=True) + NORM_EPS)
        return (xv + c * ((ov * r) * gv),)

    return _rowwise(name, body, [o, x], [g], [(d, F32)])[0]


def _post_bwd(dx, o, g, c, name):
    d = dx.shape[1]

    def body(dxv, ov, gv):
        r = lax.rsqrt(jnp.mean(ov * ov, axis=-1, keepdims=True) + NORM_EPS)
        n = ov * r
        dy = c * dxv
        dn = dy * gv
        do = r * (dn - n * jnp.mean(dn * n, axis=-1, keepdims=True))
        return (do, dy * n)

    return _rowwise(name, body, [dx, o], [g], [(d, BF16)], sums=[d])


def _rms_bwd(dh_parts, x, g, dx_res, name):
    d = x.shape[1]
    npart = len(dh_parts)

    def body(*vals):
        dh = vals[0]
        for p in vals[1:npart]:
            dh = dh + p
        xv, dres, gv = vals[npart], vals[npart + 1], vals[npart + 2]
        r = lax.rsqrt(jnp.mean(xv * xv, axis=-1, keepdims=True) + NORM_EPS)
        n = xv * r
        dn = dh * gv
        dx = dres + r * (dn - n * jnp.mean(dn * n, axis=-1, keepdims=True))
        return (dx, dh * n)

    return _rowwise(name, body, list(dh_parts) + [x, dx_res], [g], [(d, F32)], sums=[d], tm=128)


def _ffn_gate_up(h, wg, wu, name):
    t, d = h.shape
    f = wg.shape[0]
    tm = _div_tile(t, 512, 16)
    tn = _div_tile(f, 1536, LANE)

    def kern(h_ref, wg_ref, wu_ref, g_ref, u_ref, a_ref):
        hv = h_ref[...]
        g = _dot(hv, wg_ref[...], _NT).astype(BF16)
        u = _dot(hv, wu_ref[...], _NT).astype(BF16)
        g_ref[...] = g
        u_ref[...] = u
        gf = g.astype(F32)
        a_ref[...] = ((gf * _sigmoid(gf)) * u.astype(F32)).astype(BF16)

    w_spec = pl.BlockSpec((tn, d), lambda i, j: (j, 0))
    o_spec = pl.BlockSpec((tm, tn), lambda i, j: (i, j))
    return pl.pallas_call(
        kern, name=name, grid=(t // tm, f // tn),
        out_shape=[jax.ShapeDtypeStruct((t, f), BF16)] * 3,
        in_specs=[pl.BlockSpec((tm, d), lambda i, j: (i, 0)), w_spec, w_spec],
        out_specs=[o_spec, o_spec, o_spec],
        compiler_params=_params(("parallel", "parallel")),
    )(h, wg, wu)


def _ffn_dact(do, wd, gate, up, name):
    t, d = do.shape
    f = wd.shape[0]
    tm = _div_tile(t, 512, 16)
    tn = _div_tile(f, 1536, LANE)

    def kern(do_ref, wd_ref, g_ref, u_ref, dg_ref, du_ref):
        da = _dot(do_ref[...], wd_ref[...], _NT)
        gv = g_ref[...].astype(F32)
        uv = u_ref[...].astype(F32)
        sg = _sigmoid(gv)
        dg_ref[...] = (da * uv * (sg * (1.0 + gv * (1.0 - sg)))).astype(BF16)
        du_ref[...] = (da * (gv * sg)).astype(BF16)

    o_spec = pl.BlockSpec((tm, tn), lambda i, j: (i, j))
    return pl.pallas_call(
        kern, name=name, grid=(t // tm, f // tn),
        out_shape=[jax.ShapeDtypeStruct((t, f), BF16)] * 2,
        in_specs=[pl.BlockSpec((tm, d), lambda i, j: (i, 0)), pl.BlockSpec((tn, d), lambda i, j: (j, 0)),
                  o_spec, o_spec],
        out_specs=[o_spec, o_spec],
        compiler_params=_params(("parallel", "parallel")),
    )(do, wd, gate, up)


def _merge_fwd(bds, logits, name):
    d = bds[0].shape[1]

    def body(b0, b1, b2, l0, l1, l2):
        return (_sigmoid(l0) * b0 + _sigmoid(l1) * b1 + _sigmoid(l2) * b2,)

    mats = list(bds) + [(logits, j, d) for j in range(N_BRANCHES)]
    return _rowwise(name, body, mats, [], [(d, BF16)], tm=128)[0]


def _merge_bwd(dmerged, bds, logits, name):
    d = bds[0].shape[1]

    def body(dm, b0, b1, b2, l0, l1, l2):
        dbs, dls = [], []
        for b, l in ((b0, l0), (b1, l1), (b2, l2)):
            s = _sigmoid(l)
            dbs.append(dm * s)
            dls.append(dm * b * (s * (1.0 - s)))
        return (dbs[0], dbs[1], dbs[2], dls)

    mats = [dmerged] + list(bds) + [(logits, j, d) for j in range(N_BRANCHES)]
    outs = [(d, BF16)] * 3 + [(N_BRANCHES * d, BF16)]
    return _rowwise(name, body, mats, [], outs, tm=128)


def _la_fwd(pre, b, name):
    w = pre.shape[1]

    def body(pv, bv):
        p = pv + bv
        sp = jnp.maximum(-p, 0.0) + jnp.log(1.0 + jnp.exp(-jnp.abs(p)))
        return (-sp / GLA_GATE_TAU,)

    return _rowwise(name, body, [pre], [b], [(w, F32)])[0]


def _la_bwd(dla, pre, b, name):
    w = pre.shape[1]

    def body(dv, pv, bv):
        p = pv + bv
        dpre = (dv / GLA_GATE_TAU) * _sigmoid(-p)
        return (dpre, dpre)

    return _rowwise(name, body, [dla, pre], [b], [(w, BF16)], sums=[w])


def _gla_post_fwd(o, p_gla, r_block, gn, name):
    w = o.shape[1]

    def body(ov, rv, gv):
        pieces = []
        for h in range(GLA_HEADS):
            sl = slice(h * GLA_HEAD_V, (h + 1) * GLA_HEAD_V)
            oh = ov[:, sl]
            rr = lax.rsqrt(jnp.mean(oh * oh, axis=-1, keepdims=True) + NORM_EPS)
            rh = rv[:, sl]
            pieces.append(((oh * rr) * gv[:, sl]) * (rh * _sigmoid(rh)))
        return (pieces,)

    return _rowwise(name, body, [o, (p_gla, r_block, w)], [gn], [(w, BF16)])[0]


def _gla_post_bwd(dout, o, p_gla, r_block, gn, name):
    w = o.shape[1]

    def body(dv, ov, rv, gv):
        dos, drs, dgs = [], [], []
        for h in range(GLA_HEADS):
            sl = slice(h * GLA_HEAD_V, (h + 1) * GLA_HEAD_V)
            oh, rh, gh, dh = ov[:, sl], rv[:, sl], gv[:, sl], dv[:, sl]
            rr = lax.rsqrt(jnp.mean(oh * oh, axis=-1, keepdims=True) + NORM_EPS)
            nhat = oh * rr
            s = _sigmoid(rh)
            dn = dh * (rh * s)
            drs.append(dh * (nhat * gh) * (s * (1.0 + rh * (1.0 - s))))
            dnn = dn * gh
            dos.append(rr * (dnn - nhat * jnp.mean(dnn * nhat, axis=-1, keepdims=True)))
            dgs.append(dn * nhat)
        return (dos, drs, jnp.concatenate(dgs, axis=1))

    return _rowwise(name, body, [dout, o, (p_gla, r_block, w)], [gn], [(w, F32), (w, BF16)], sums=[w])


def _conv_bwd1(dout, y, ln_g, ln_b, name):
    w = y.shape[1]

    def body(dv, yv, gv, bv):
        mu = jnp.mean(yv, axis=-1, keepdims=True)
        yc = yv - mu
        rstd = lax.rsqrt(jnp.mean(yc * yc, axis=-1, keepdims=True) + NORM_EPS)
        xhat = yc * rstd
        yn = xhat * gv + bv
        s = _sigmoid(yn)
        dyn = dv * (s * (1.0 + yn * (1.0 - s)))
        dxh = dyn * gv
        dy = rstd * (dxh - jnp.mean(dxh, axis=-1, keepdims=True)
                     - xhat * jnp.mean(dxh * xhat, axis=-1, keepdims=True))
        return (dy, dyn * xhat, dyn, dy)

    return _rowwise(name, body, [dout, y], [ln_g, ln_b], [(w, F32)], sums=[w, w, w])


def _loss_fwd(y, target, name):
    d = y.shape[1]

    def body(yv, tv):
        e = yv - tv
        return (e / d, e * e)

    return _rowwise(name, body, [y, target], [], [(d, F32)], sums=[d])


def _adamw(w, g, m, v, name):
    cols = w.shape[1]
    c1 = 1.0 - ADAM_B1 ** ADAM_STEP
    c2 = 1.0 - ADAM_B2 ** ADAM_STEP

    def body(wv, gv, mv, vv):
        m2 = ADAM_B1 * mv + (1.0 - ADAM_B1) * gv
        v2 = ADAM_B2 * vv + (1.0 - ADAM_B2) * (gv * gv)
        m_hat = m2 / c1
        v_hat = v2 / c2
        delta = -ADAM_LR * (m_hat / (jnp.sqrt(v_hat) + ADAM_EPS) + ADAM_WD * wv)
        return (delta, m2, v2)

    return _rowwise(name, body, [w, g, m, v], [], [(cols, F32)] * 3)


def _split_bf16(v):
    hi = v.astype(BF16)
    lo = (v - hi.astype(F32)).astype(BF16)
    return hi, lo


def _tri(n, strict):
    r = lax.broadcasted_iota(jnp.int32, (n, n), 0)
    c = lax.broadcasted_iota(jnp.int32, (n, n), 1)
    return jnp.where(r > c if strict else r >= c, 1.0, 0.0).astype(BF16)


def _sb_weights(z, c, mx, diag_offset):
    u = jnp.exp(-jnp.abs(z))
    sp = jnp.maximum(z, 0.0) + jnp.log(1.0 + u)
    if diag_offset is None:
        mask = None
        spm = sp
    else:
        rows = lax.broadcasted_iota(jnp.int32, z.shape, 0)
        cols = lax.broadcasted_iota(jnp.int32, z.shape, 1)
        mask = cols + diag_offset < rows
        spm = jnp.where(mask, sp, 0.0)
    suf = _dot(spm.astype(BF16), mx)
    w = jnp.exp(((z - sp) - suf) - c)
    if mask is not None:
        w = jnp.where(mask, w, 0.0)
    return sp, w, mask, suf[:, 0:1] + spm[:, 0:1]


def _sb_fwd(q, kt, vt, name):
    nh, t, dh = q.shape
    nkb, bk = kt.shape[1], kt.shape[3]
    bq = min(SB_BQ, t)
    scale = dh ** -0.5
    per = bq // bk

    def kern(q_ref, kt_ref, vt_ref, o_ref, c_ref, zbuf, wbuf):
        qi = pl.program_id(1)
        top = (qi + 1) * per - 1
        ntiles = (qi + 1) * per
        qs = (q_ref[...].astype(F32) * scale).astype(BF16)
        mx = _tri(bk, True)
        o_ref[...] = jnp.zeros_like(o_ref)
        c_ref[...] = jnp.zeros_like(c_ref)
        zbuf[0] = _dot(qs, kt_ref[top])
        wbuf[0] = jnp.zeros((bq, bk), BF16)

        def tile(i, r0, masked):
            kb = top - i
            p = i % 2
            z = zbuf[p, r0:, :]
            w_prev = wbuf[p]
            c = c_ref[r0:, :]
            zbuf[1 - p] = _dot(qs, kt_ref[jnp.maximum(kb - 1, 0)])
            o_ref[...] += _dot(w_prev, vt_ref[jnp.minimum(kb + 1, nkb - 1)], _NT)
            _, w, _, tot = _sb_weights(z, c, mx, 0 if masked else None)
            if r0:
                wbuf[1 - p, :r0, :] = jnp.zeros((r0, bk), BF16)
            wbuf[1 - p, r0:, :] = w.astype(BF16)
            c_ref[r0:, :] = c + tot

        def step(i, carry):
            tile(i, 0, False)
            return carry

        for i in range(per):
            tile(i, (per - 1 - i) * bk, True)
        lax.fori_loop(per, ntiles, step, 0)
        o_ref[...] += _dot(wbuf[ntiles % 2], vt_ref[0], _NT)

    return pl.pallas_call(
        kern, name=name,
        out_shape=jax.ShapeDtypeStruct((nh, t, dh), F32),
        grid=(nh, t // bq),
        in_specs=[pl.BlockSpec((None, bq, dh), lambda h, i: (h, i, 0)),
                  pl.BlockSpec((None, nkb, dh, bk), lambda h, i: (h, 0, 0, 0)),
                  pl.BlockSpec((None, nkb, dh, bk), lambda h, i: (h, 0, 0, 0))],
        out_specs=pl.BlockSpec((None, bq, dh), lambda h, i: (h, i, 0)),
        scratch_shapes=[pltpu.VMEM((bq, 1), F32), pltpu.VMEM((2, bq, bk), F32), pltpu.VMEM((2, bq, bk), BF16)],
        compiler_params=_params(("parallel", "arbitrary")),
    )(q, kt, vt)


def _sb_bwd(q, qt, kt, vt, out, dout, doutt, name):
    nh, t, dh = q.shape
    nkb, bk = kt.shape[1], kt.shape[3]
    bq = min(SB_BQ, t)
    scale = dh ** -0.5
    per = bq // bk

    def kern(q_ref, qt_ref, kt_ref, vt_ref, o_ref, do_ref, dot_ref, dq_ref, dkt_ref, dvt_ref,
             c_ref, ce_ref, zbuf, dwbuf):
        qi = pl.program_id(1)
        top = (qi + 1) * per - 1

        @pl.when(qi == 0)
        def _():
            dkt_ref[...] = jnp.zeros_like(dkt_ref)
            dvt_ref[...] = jnp.zeros_like(dvt_ref)

        qs = (q_ref[...].astype(F32) * scale).astype(BF16)
        qts = (qt_ref[...].astype(F32) * scale).astype(BF16)
        dob = do_ref[...].astype(BF16)
        dotv = dot_ref[...]
        dsum = jnp.sum(dob.astype(F32) * o_ref[...], axis=1, keepdims=True)
        mx = _tri(bk, True)
        mi = _tri(bk, False)
        dq_ref[...] = jnp.zeros_like(dq_ref)
        c_ref[...] = jnp.zeros_like(c_ref)
        ce_ref[...] = jnp.zeros_like(ce_ref)

        zbuf[...] = _dot(qs, kt_ref[top])
        dwbuf[...] = _dot(dob, vt_ref[top])

        def tile(kb, r0, masked):
            kt_blk = kt_ref[kb]
            z = zbuf[r0:, :]
            dw = dwbuf[r0:, :]
            c = c_ref[r0:, :]
            sp, w, mask, tot = _sb_weights(z, c, mx, 0 if masked else None)
            wb = w.astype(BF16)
            e = dw * wb.astype(F32)
            before = dsum[r0:] - (_dot(e.astype(BF16), mi) + ce_ref[r0:, :])
            sig = jnp.exp(z - sp)
            dz = e - sig * (e + before)
            if mask is not None:
                dz = jnp.where(mask, dz, 0.0)
            dz = dz.astype(BF16)
            dq_ref[r0:, :] += _dot(dz, kt_blk, _NT)
            dkt_ref[kb] += _dot(qts[:, r0:], dz)
            dvt_ref[kb] += _dot(dotv[:, r0:], wb)
            c_ref[r0:, :] = c + tot
            ce_ref[r0:, :] += jnp.sum(e, axis=1, keepdims=True)
            kb_next = jnp.maximum(kb - 1, 0)
            zbuf[...] = _dot(qs, kt_ref[kb_next])
            dwbuf[...] = _dot(dob, vt_ref[kb_next])

        def step(i, carry):
            tile(top - i, 0, False)
            return carry

        for i in range(per):
            tile(top - i, (per - 1 - i) * bk, True)
        lax.fori_loop(per, (qi + 1) * per, step, 0)
        dq_ref[...] = dq_ref[...] * scale

    row = pl.BlockSpec((None, bq, dh), lambda h, i: (h, i, 0))
    col = pl.BlockSpec((None, dh, bq), lambda h, i: (h, 0, i))
    whole = pl.BlockSpec((None, nkb, dh, bk), lambda h, i: (h, 0, 0, 0))
    return pl.pallas_call(
        kern, name=name,
        out_shape=[jax.ShapeDtypeStruct((nh, t, dh), F32),
                   jax.ShapeDtypeStruct((nh, nkb, dh, bk), F32),
                   jax.ShapeDtypeStruct((nh, nkb, dh, bk), F32)],
        grid=(nh, t // bq),
        in_specs=[row, col, whole, whole, row, row, col],
        out_specs=[row, whole, whole],
        scratch_shapes=[pltpu.VMEM((bq, 1), F32), pltpu.VMEM((bq, 1), F32), pltpu.VMEM((bq, bk), F32),
                        pltpu.VMEM((bq, bk), F32)],
        compiler_params=_params(("parallel", "arbitrary")),
    )(q, qt, kt, vt, out, dout, doutt)


def _conv_fwd(p_conv, conv_w, conv_b, ln_g, ln_b, name):
    t, c2 = p_conv.shape
    c = c2 // 2
    tm = min(256, t)
    hb = tm // CONV_HALO

    def kern(a_ref, g_ref, ah_ref, gh_ref, w_ref, b_ref, lg_ref, lb_ref, o_ref, u_ref, y_ref, ubuf):
        i = pl.program_id(0)
        u = a_ref[...] * _sigmoid(g_ref[...])
        uh = ah_ref[...] * _sigmoid(gh_ref[...])
        ubuf[0:CONV_HALO, :] = jnp.where(i > 0, uh, 0.0)
        ubuf[CONV_HALO:CONV_HALO + tm, :] = u
        y = jnp.zeros((tm, c), F32) + b_ref[...]
        for j in range(CONV_WIDTH):
            off = CONV_HALO - (CONV_WIDTH - 1) + j
            y = y + ubuf[off:off + tm, :] * w_ref[j:j + 1, :]
        mu = jnp.mean(y, axis=-1, keepdims=True)
        yc = y - mu
        rstd = lax.rsqrt(jnp.mean(yc * yc, axis=-1, keepdims=True) + NORM_EPS)
        yn = (yc * rstd) * lg_ref[...] + lb_ref[...]
        o_ref[...] = (yn * _sigmoid(yn)).astype(o_ref.dtype)
        u_ref[...] = u
        y_ref[...] = y

    def halo(cb):
        return pl.BlockSpec((CONV_HALO, c), lambda i: (jnp.maximum(i * hb - 1, 0), cb))

    vec = pl.BlockSpec((1, c), lambda i: (0, 0))
    tile = pl.BlockSpec((tm, c), lambda i: (i, 0))
    return pl.pallas_call(
        kern, name=name,
        out_shape=[jax.ShapeDtypeStruct((t, c), BF16), jax.ShapeDtypeStruct((t, c), F32),
                   jax.ShapeDtypeStruct((t, c), F32)],
        grid=(t // tm,),
        in_specs=[tile, pl.BlockSpec((tm, c), lambda i: (i, 1)), halo(0), halo(1),
                  pl.BlockSpec(conv_w.shape, lambda i: (0, 0)), vec, vec, vec],
        out_specs=[tile, tile, tile],
        scratch_shapes=[pltpu.VMEM((tm + CONV_HALO, c), F32)],
        compiler_params=_params(("arbitrary",)),
    )(p_conv, p_conv, p_conv, p_conv, conv_w, conv_b, ln_g, ln_b)


def _conv_bwd2(dy, u, p_conv, conv_w, name):
    t, c = dy.shape
    tm = min(256, t)
    hb = tm // CONV_HALO
    nt = t // tm
    last_halo = t // CONV_HALO - 1

    def kern(dy_ref, dyh_ref, u_ref, uh_ref, a_ref, g_ref, w_ref, dp_ref, dw_ref, dybuf, ubuf):
        i = pl.program_id(0)

        @pl.when(i == 0)
        def _():
            dw_ref[...] = jnp.zeros_like(dw_ref)

        dyv = dy_ref[...]
        dybuf[0:tm, :] = dyv
        dybuf[tm:tm + CONV_HALO, :] = jnp.where(i < nt - 1, dyh_ref[...], 0.0)
        ubuf[0:CONV_HALO, :] = jnp.where(i > 0, uh_ref[...], 0.0)
        ubuf[CONV_HALO:CONV_HALO + tm, :] = u_ref[...]
        du = jnp.zeros((tm, c), F32)
        for j in range(CONV_WIDTH):
            off = CONV_WIDTH - 1 - j
            du = du + dybuf[off:off + tm, :] * w_ref[j:j + 1, :]
            uoff = CONV_HALO - (CONV_WIDTH - 1) + j
            dw_ref[j:j + 1, :] += jnp.sum(dyv * ubuf[uoff:uoff + tm, :], axis=0, keepdims=True)
        a = a_ref[...]
        s = _sigmoid(g_ref[...])
        dp_ref[:, 0:c] = (du * s).astype(dp_ref.dtype)
        dp_ref[:, c:2 * c] = (du * a * (s * (1.0 - s))).astype(dp_ref.dtype)

    tile = pl.BlockSpec((tm, c), lambda i: (i, 0))
    return pl.pallas_call(
        kern, name=name,
        out_shape=[jax.ShapeDtypeStruct((t, 2 * c), BF16), jax.ShapeDtypeStruct((CONV_HALO, c), F32)],
        grid=(nt,),
        in_specs=[tile,
                  pl.BlockSpec((CONV_HALO, c), lambda i: (jnp.minimum((i + 1) * hb, last_halo), 0)),
                  tile,
                  pl.BlockSpec((CONV_HALO, c), lambda i: (jnp.maximum(i * hb - 1, 0), 0)),
                  tile, pl.BlockSpec((tm, c), lambda i: (i, 1)),
                  pl.BlockSpec(conv_w.shape, lambda i: (0, 0))],
        out_specs=[pl.BlockSpec((tm, 2 * c), lambda i: (i, 0)),
                   pl.BlockSpec((CONV_HALO, c), lambda i: (0, 0))],
        scratch_shapes=[pltpu.VMEM((tm + CONV_HALO, c), F32), pltpu.VMEM((tm + CONV_HALO, c), F32)],
        compiler_params=_params(("arbitrary",)),
    )(dy, dy, u, u, p_conv, p_conv, conv_w)


GLA_CHUNKS_PER_STEP = 16


def _gla_fwd(q_ck, k_kc, la_kc, v, name):
    nh, nc, ch, dk = q_ck.shape
    dv = v.shape[3]
    qscale = dk ** -0.5
    cb = min(GLA_CHUNKS_PER_STEP, nc)

    def kern(q_ref, k_ref, la_ref, v_ref, o_ref, st_ref, state_ref):
        mx = _tri(ch, True)

        @pl.when(pl.program_id(1) == 0)
        def _():
            state_ref[...] = jnp.zeros_like(state_ref)

        def step(n, state):
            la = la_ref[n]
            hi, lo = _split_bf16(la)
            de = _dot(hi, mx) + _dot(lo, mx)
            lam = jnp.exp(jnp.sum(la, axis=1, keepdims=True))
            kd = (k_ref[n] * jnp.exp(de)).astype(BF16)
            state = lam * state + _dot(kd, v_ref[n])
            st_ref[n] = state
            qs = (q_ref[n].astype(F32) * qscale).astype(BF16)
            o_ref[n] = _dot(qs, state.astype(BF16))
            return state

        state_ref[...] = lax.fori_loop(0, cb, step, state_ref[...])

    def spec(a, b):
        return pl.BlockSpec((None, cb, a, b), lambda h, j: (h, j, 0, 0))

    return pl.pallas_call(
        kern, name=name,
        out_shape=[jax.ShapeDtypeStruct((nh, nc, ch, dv), F32), jax.ShapeDtypeStruct((nh, nc, dk, dv), F32)],
        grid=(nh, nc // cb),
        in_specs=[spec(ch, dk), spec(dk, ch), spec(dk, ch), spec(ch, dv)],
        out_specs=[spec(ch, dv), spec(dk, dv)],
        scratch_shapes=[pltpu.VMEM((dk, dv), F32)],
        compiler_params=_params(("parallel", "arbitrary")),
    )(q_ck, k_kc, la_kc, v)


def _gla_bwd(q_ck, q_kc, k_ck, k_kc, la_ck, la_kc, v, states, states_prev, do, name):
    nh, nc, ch, dk = q_ck.shape
    dv = v.shape[3]
    qscale = dk ** -0.5
    cb = min(GLA_CHUNKS_PER_STEP, nc)
    nb = nc // cb

    def kern(q_ref, qt_ref, kck_ref, kkc_ref, lack_ref, lakc_ref, v_ref, st_ref, stp_ref, do_ref,
             dq_ref, dk_ref, dv_ref, dla_ref, g_ref):
        mx = _tri(ch, True)
        mxt = jnp.where(lax.broadcasted_iota(jnp.int32, (ch, ch), 0)
                        < lax.broadcasted_iota(jnp.int32, (ch, ch), 1), 1.0, 0.0).astype(BF16)

        @pl.when(pl.program_id(1) == 0)
        def _():
            g_ref[...] = jnp.zeros_like(g_ref)

        def step(i, g):
            n = cb - 1 - i
            la_kc = lakc_ref[n]
            hi, lo = _split_bf16(la_kc)
            de_kc = _dot(hi, mx) + _dot(lo, mx)
            lam = jnp.exp(jnp.sum(la_kc, axis=1, keepdims=True))
            hi2, lo2 = _split_bf16(lack_ref[n])
            de_ck = _dot(mxt, hi2) + _dot(mxt, lo2)
            edk = jnp.exp(de_kc)
            kd_kc = kkc_ref[n] * edk
            kd_ck = (kck_ref[n] * jnp.exp(de_ck)).astype(BF16)
            dob = do_ref[n].astype(BF16)
            dq_ref[n] = _dot(dob, st_ref[n].astype(BF16), _NT) * qscale
            qts = (qt_ref[n].astype(F32) * qscale).astype(BF16)
            ds = _dot(qts, dob) + g
            dsb = ds.astype(BF16)
            dlam = jnp.sum(ds * stp_ref[n], axis=1, keepdims=True)
            dkd = _dot(dsb, v_ref[n], _NT)
            dv_ref[n] = _dot(kd_ck, dsb)
            dk_ref[n] = dkd * edk
            dde = dkd * kd_kc
            h3, l3 = _split_bf16(dde)
            dla_ref[n] = _dot(h3, mxt) + _dot(l3, mxt) + dlam * lam
            return lam * ds

        g_ref[...] = lax.fori_loop(0, cb, step, g_ref[...])

    def spec(a, b):
        return pl.BlockSpec((None, cb, a, b), lambda h, j: (h, nb - 1 - j, 0, 0))

    return pl.pallas_call(
        kern, name=name,
        out_shape=[jax.ShapeDtypeStruct((nh, nc, ch, dk), F32), jax.ShapeDtypeStruct((nh, nc, dk, ch), F32),
                   jax.ShapeDtypeStruct((nh, nc, ch, dv), F32), jax.ShapeDtypeStruct((nh, nc, dk, ch), F32)],
        grid=(nh, nb),
        in_specs=[spec(ch, dk), spec(dk, ch), spec(ch, dk), spec(dk, ch), spec(ch, dk), spec(dk, ch),
                  spec(ch, dv), spec(dk, dv), spec(dk, dv), spec(ch, dv)],
        out_specs=[spec(ch, dk), spec(dk, ch), spec(ch, dv), spec(dk, ch)],
        scratch_shapes=[pltpu.VMEM((dk, dv), F32)],
        compiler_params=_params(("parallel", "arbitrary")),
    )(q_ck, q_kc, k_ck, k_kc, la_ck, la_kc, v, states, states_prev, do)


def _heads_rows(a, nh, blk):
    t = a.shape[0]
    d = a.shape[1] // nh
    return a.reshape(t // blk, blk, nh, d).transpose(2, 0, 1, 3)


def _heads_cols(a, nh, blk):
    t = a.shape[0]
    d = a.shape[1] // nh
    return a.T.reshape(nh, d, t // blk, blk).transpose(0, 2, 1, 3)


def _unheads_rows(a):
    nh, nb, blk, d = a.shape
    return a.transpose(1, 2, 0, 3).reshape(nb * blk, nh * d)


def _unheads_cols(a):
    nh, nb, d, blk = a.shape
    return a.transpose(0, 2, 1, 3).reshape(nh * d, nb * blk).T


def _ffn_fwd(x, w, pre, post, tag):
    h = _rms_fwd(x, pre, f"{tag}_rms")
    gate, up, act = _ffn_gate_up(h, w["gate"], w["up"], f"{tag}_gateup")
    o = _mm(act, w["down"], name=f"{tag}_down")
    x_out = _post_res(o, x, post, 0.5, f"{tag}_res")
    return x_out, (x, h, gate, up, act, o)


def _ffn_bwd(dx_out, saved, w, pre, post, tag):
    x, h, gate, up, act, o = saved
    do, dpost = _post_bwd(dx_out, o, post, 0.5, f"{tag}_bres")
    d_down = _mm(act, do, tn=True, out_dtype=BF16, name=f"{tag}_bwdown")
    dgate, dup = _ffn_dact(do, w["down"], gate, up, f"{tag}_bdact")
    dh_g = _mm(dgate, w["gate"], name=f"{tag}_bdhg")
    dh_u = _mm(dup, w["up"], name=f"{tag}_bdhu")
    d_gate = _mm(dgate, h, tn=True, out_dtype=BF16, name=f"{tag}_bwgate")
    d_up = _mm(dup, h, tn=True, out_dtype=BF16, name=f"{tag}_bwup")
    dx, dpre = _rms_bwd([dh_g, dh_u], x, pre, dx_out, f"{tag}_brms")
    return dx, dict(gate=d_gate, up=d_up, down=d_down, pre=dpre, post=dpost)


def _mixer_fwd(x, w, tag):
    t, d = x.shape
    bw = BRANCH_WIDTH
    h = _rms_fwd(x, w["pre"], f"{tag}_rms")
    p_sb = _mm(h, w["in_sb"], nt=True, out_dtype=BF16, name=f"{tag}_insb")
    p_conv = _mm(h, w["in_conv"], nt=True, name=f"{tag}_inconv")
    p_gla = _mm(h, w["in_gla"], nt=True, name=f"{tag}_ingla")
    p_gate = _mm(h, w["in_gate"], nt=True, name=f"{tag}_ingate")

    bk = min(SB_BK, t)
    q = p_sb[:, 0:bw].reshape(t, SB_HEADS, SB_HEAD_DIM).transpose(1, 0, 2)
    kt = _heads_cols(p_sb[:, bw:2 * bw], SB_HEADS, bk)
    vt = _heads_cols(p_sb[:, 2 * bw:3 * bw], SB_HEADS, bk)
    sb_o = _sb_fwd(q, kt, vt, f"{tag}_sb")
    sb_out = sb_o.transpose(1, 0, 2).reshape(t, bw)

    conv_out, conv_u, conv_y = _conv_fwd(p_conv, w["conv_w"], w["conv_b"], w["ln_g"], w["ln_b"], f"{tag}_conv")

    kd, vd = GLA_KEY_DIM, GLA_VALUE_DIM
    lr = p_gla[:, 2 * kd + 2 * vd:]
    pre_a = _mm(lr, w["alpha"], name=f"{tag}_alpha")
    la = _la_fwd(pre_a, w["b_alpha"], f"{tag}_la")
    gq = _heads_rows(p_gla[:, 0:kd].astype(BF16), GLA_HEADS, CHUNK)
    gk_kc = _heads_cols(p_gla[:, kd:2 * kd], GLA_HEADS, CHUNK)
    gv = _heads_rows(p_gla[:, 2 * kd:2 * kd + vd].astype(BF16), GLA_HEADS, CHUNK)
    la_kc = _heads_cols(la, GLA_HEADS, CHUNK)
    gla_o4, states = _gla_fwd(gq, gk_kc, la_kc, gv, f"{tag}_gla")
    gla_o = _unheads_rows(gla_o4)
    r_block = (2 * kd + vd) // vd
    gla_out = _gla_post_fwd(gla_o, p_gla, r_block, w["gnorm"], f"{tag}_glapost")

    branches = (sb_out.astype(BF16), conv_out, gla_out)
    bds = [_mm(branches[j], w["branch"][j], nt=True, name=f"{tag}_br{j}") for j in range(N_BRANCHES)]
    merged = _merge_fwd(bds, p_gate, f"{tag}_merge")
    mo = _mm(merged, w["out"], name=f"{tag}_out")
    x_out = _post_res(mo, x, w["post"], 1.0, f"{tag}_res")
    saved = dict(x=x, h=h, p_sb=p_sb, p_conv=p_conv, p_gla=p_gla, p_gate=p_gate, sb_o=sb_o, conv_u=conv_u,
                 sb_q=q, sb_kt=kt, sb_vt=vt, gla_q=gq, gla_k_kc=gk_kc, gla_v=gv, gla_la_kc=la_kc,
                 conv_y=conv_y, pre_a=pre_a, la=la, states=states, gla_o=gla_o, branches=branches, bds=bds,
                 merged=merged, mo=mo)
    return x_out, saved


def _mixer_bwd(dx_out, s, w, tag):
    x = s["x"]
    t, d = x.shape
    bw = BRANCH_WIDTH
    kd, vd = GLA_KEY_DIM, GLA_VALUE_DIM
    grads = {}
    dmo, grads["post"] = _post_bwd(dx_out, s["mo"], w["post"], 1.0, f"{tag}_bres")
    dmerged = _mm(dmo, w["out"], nt=True, name=f"{tag}_bdmerged")
    grads["out"] = _mm(s["merged"], dmo, tn=True, out_dtype=BF16, name=f"{tag}_bwout")
    dbd0, dbd1, dbd2, dp_gate = _merge_bwd(dmerged, s["bds"], s["p_gate"], f"{tag}_bmerge")
    dbds = (dbd0, dbd1, dbd2)
    dbranch = [_mm(dbds[j], w["branch"][j], name=f"{tag}_bdbr{j}") for j in range(N_BRANCHES)]
    grads["branch"] = jnp.stack([_mm(dbds[j], s["branches"][j], tn=True, out_dtype=BF16, name=f"{tag}_bwbr{j}")
                                 for j in range(N_BRANCHES)])

    p_sb = s["p_sb"]
    bk = min(SB_BK, t)
    q, kt, vt = s["sb_q"], s["sb_kt"], s["sb_vt"]
    qt = p_sb[:, 0:bw].reshape(t, SB_HEADS, SB_HEAD_DIM).transpose(1, 2, 0)
    dsb = dbranch[0].reshape(t, SB_HEADS, SB_HEAD_DIM)
    dout = dsb.transpose(1, 0, 2)
    doutt = dsb.transpose(1, 2, 0).astype(BF16)
    dq, dkt, dvt = _sb_bwd(q, qt, kt, vt, s["sb_o"], dout, doutt, f"{tag}_bsb")
    dp_sb = jnp.concatenate([dq.transpose(1, 0, 2).reshape(t, bw), _unheads_cols(dkt), _unheads_cols(dvt)],
                            axis=1).astype(BF16)

    dy, d_lng, d_lnb, d_cb = _conv_bwd1(dbranch[1], s["conv_y"], w["ln_g"], w["ln_b"], f"{tag}_bconv1")
    dp_conv, d_cw = _conv_bwd2(dy, s["conv_u"], s["p_conv"], w["conv_w"], f"{tag}_bconv2")
    grads.update(conv_w=d_cw[:CONV_WIDTH], conv_b=d_cb, ln_g=d_lng, ln_b=d_lnb)

    p_gla = s["p_gla"]
    r_block = (2 * kd + vd) // vd
    do, dr, grads["gnorm"] = _gla_post_bwd(dbranch[2], s["gla_o"], p_gla, r_block, w["gnorm"], f"{tag}_bglapost")
    gqf, gkf = p_gla[:, 0:kd], p_gla[:, kd:2 * kd]
    gq_b = gqf.astype(BF16)
    dq4, dk4, dv4, dla4 = _gla_bwd(
        s["gla_q"], _heads_cols(gq_b, GLA_HEADS, CHUNK),
        _heads_rows(gkf, GLA_HEADS, CHUNK), s["gla_k_kc"],
        _heads_rows(s["la"], GLA_HEADS, CHUNK), s["gla_la_kc"], s["gla_v"],
        s["states"], jnp.pad(s["states"][:, :-1], ((0, 0), (1, 0), (0, 0), (0, 0))),
        _heads_rows(do, GLA_HEADS, CHUNK), f"{tag}_bgla")
    dla = _unheads_cols(dla4)
    dpre_a, grads["b_alpha"] = _la_bwd(dla, s["pre_a"], w["b_alpha"], f"{tag}_bla")
    lr = p_gla[:, 2 * kd + 2 * vd:]
    dlr = _mm(dpre_a, w["alpha"], nt=True, out_dtype=BF16, name=f"{tag}_bdlr")
    grads["alpha"] = _mm(lr, dpre_a, tn=True, name=f"{tag}_bwalpha")
    dp_gla = jnp.concatenate([_unheads_rows(dq4).astype(BF16), _unheads_cols(dk4).astype(BF16),
                              _unheads_rows(dv4).astype(BF16), dr, dlr], axis=1)

    dps = dict(in_sb=dp_sb, in_conv=dp_conv, in_gla=dp_gla, in_gate=dp_gate)
    dh_parts = []
    for key, dp in dps.items():
        dh_parts.append(_mm(dp, w[key], name=f"{tag}_bdh_{key}"))
        grads[key] = _mm(dp, s["h"], tn=True, out_dtype=BF16, name=f"{tag}_bw_{key}")
    dx, grads["pre"] = _rms_bwd(dh_parts, x, w["pre"], dx_out, f"{tag}_brms")
    return dx, grads


def _local_step(x, target, layers):
    saved = []
    for l, w in enumerate(layers):
        x, s1 = _ffn_fwd(x, w["ffn1"], w["pre0"], w["post0"], f"l{l}_f1")
        x, s2 = _mixer_fwd(x, w["mix"], f"l{l}_mx")
        x, s3 = _ffn_fwd(x, w["ffn2"], w["pre2"], w["post2"], f"l{l}_f2")
        saved.append((s1, s2, s3))
    dx, sq = _loss_fwd(x, target, "loss")
    grads = [None] * len(layers)
    for l in reversed(range(len(layers))):
        w = layers[l]
        s1, s2, s3 = saved[l]
        dx, g3 = _ffn_bwd(dx, s3, w["ffn2"], w["pre2"], w["post2"], f"l{l}_f2")
        dx, g2 = _mixer_bwd(dx, s2, w["mix"], f"l{l}_mx")
        dx, g1 = _ffn_bwd(dx, s1, w["ffn1"], w["pre0"], w["post0"], f"l{l}_f1")
        grads[l] = dict(ffn1=g1, mix=g2, ffn2=g3)
    return sq, dx, grads


_ANY = pl.BlockSpec(memory_space=pl.ANY)
_MESH = pl.DeviceIdType.MESH
N_CHIPS = 4


def _all_gather(p, name):
    def body(p_ref, out_ref, send_sems, recv_sems, local_sem):
        x, y, c = lax.axis_index("x"), lax.axis_index("y"), lax.axis_index("c")
        me, sibling = (x, y, c), (x, y, 1 - c)
        chips = [(1 - x, y), (x, 1 - y), (1 - x, 1 - y)]

        def rows(px, py, pc):
            return out_ref.at[4 * px + 2 * py + pc]

        def copy(k, block, to, src=None):
            return pltpu.make_async_remote_copy(
                src_ref=rows(*block) if src is None else src, dst_ref=rows(*block),
                send_sem=send_sems.at[k], recv_sem=recv_sems.at[k], device_id=to, device_id_type=_MESH)

        mine = pltpu.make_async_copy(p_ref, rows(*me), local_sem)
        mine.start()
        first = [copy(0, me, sibling, src=p_ref)]
        first += [copy(1 + j, me, (*chip, c), src=p_ref) for j, chip in enumerate(chips)]
        for cp in first:
            cp.start()
        passed = [copy(4 + j, (*chip, c), sibling) for j, chip in enumerate(chips)]
        for j, chip in enumerate(chips):
            copy(1 + j, (*chip, c), me).wait_recv()
            passed[j].start()
        copy(0, sibling, me).wait_recv()
        for j, chip in enumerate(chips):
            copy(4 + j, (*chip, 1 - c), me).wait_recv()
        for cp in first + passed:
            cp.wait_send()
        mine.wait()

    return pl.pallas_call(
        body, name=name,
        out_shape=jax.ShapeDtypeStruct((N_DEV,) + p.shape, p.dtype),
        in_specs=[_ANY], out_specs=_ANY,
        scratch_shapes=[pltpu.SemaphoreType.DMA((7,)), pltpu.SemaphoreType.DMA((7,)), pltpu.SemaphoreType.DMA],
    )(p)


def _exchange_sibling(src, name):
    def body(src_ref, out_ref, send_sems, recv_sems):
        x, y, c = lax.axis_index("x"), lax.axis_index("y"), lax.axis_index("c")
        copies = [pltpu.make_async_remote_copy(
            src_ref=src_ref.at[j, 1 - c], dst_ref=out_ref.at[j], send_sem=send_sems.at[j],
            recv_sem=recv_sems.at[j], device_id=(x, y, 1 - c), device_id_type=_MESH) for j in range(N_CHIPS)]
        for cp in copies:
            cp.start()
        for cp in copies:
            cp.wait()

    return pl.pallas_call(
        body, name=name,
        out_shape=jax.ShapeDtypeStruct((N_CHIPS,) + src.shape[2:], src.dtype),
        in_specs=[_ANY], out_specs=_ANY,
        scratch_shapes=[pltpu.SemaphoreType.DMA((N_CHIPS,)), pltpu.SemaphoreType.DMA((N_CHIPS,))],
    )(src)


def _add_own(src, got, name):
    _, _, r, cols = src.shape
    tr = _div_tile(r, 512, 16)
    c = lax.axis_index("c").astype(jnp.int32).reshape(1)

    def kern(c_ref, a_ref, b_ref, o_ref):
        o_ref[...] = (a_ref[...].astype(F32) + b_ref[...].astype(F32)).astype(o_ref.dtype)

    grid_spec = pltpu.PrefetchScalarGridSpec(
        num_scalar_prefetch=1, grid=(N_CHIPS, r // tr),
        in_specs=[pl.BlockSpec((None, None, tr, cols), lambda j, i, c_ref: (j, c_ref[0], i, 0)),
                  pl.BlockSpec((None, tr, cols), lambda j, i, c_ref: (j, i, 0))],
        out_specs=pl.BlockSpec((None, tr, cols), lambda j, i, c_ref: (j, i, 0)))
    return pl.pallas_call(
        kern, name=name, grid_spec=grid_spec,
        out_shape=jax.ShapeDtypeStruct((N_CHIPS, r, cols), src.dtype),
        compiler_params=_params(("arbitrary", "arbitrary")),
    )(c, src, got)


def _exchange_chips(part, name):
    def body(part_ref, out_ref, send_sems, recv_sems, local_sem):
        x, y, c = lax.axis_index("x"), lax.axis_index("y"), lax.axis_index("c")
        my_chip = 2 * x + y
        chips = [(1 - x, y), (x, 1 - y), (1 - x, 1 - y)]
        mine = pltpu.make_async_copy(part_ref.at[my_chip], out_ref.at[my_chip], local_sem)
        mine.start()
        copies = []
        for k, (px, py) in enumerate(chips):
            copies.append(pltpu.make_async_remote_copy(
                src_ref=part_ref.at[2 * px + py], dst_ref=out_ref.at[my_chip],
                send_sem=send_sems.at[k], recv_sem=recv_sems.at[k], device_id=(px, py, c), device_id_type=_MESH))
        for cp in copies:
            cp.start()
        for k, (px, py) in enumerate(chips):
            pltpu.make_async_remote_copy(
                src_ref=part_ref.at[my_chip], dst_ref=out_ref.at[2 * px + py],
                send_sem=send_sems.at[k], recv_sem=recv_sems.at[k], device_id=(px, py, c),
                device_id_type=_MESH).wait_recv()
        for cp in copies:
            cp.wait_send()
        mine.wait()

    return pl.pallas_call(
        body, name=name,
        out_shape=jax.ShapeDtypeStruct(part.shape, part.dtype),
        in_specs=[_ANY], out_specs=_ANY,
        scratch_shapes=[pltpu.SemaphoreType.DMA((3,)), pltpu.SemaphoreType.DMA((3,)), pltpu.SemaphoreType.DMA],
    )(part)


def _sum_chips(parts, name):
    _, r, cols = parts.shape
    tr = _div_tile(r, 512, 16)

    def kern(p_ref, o_ref):
        p = [p_ref[j].astype(F32) for j in range(N_CHIPS)]
        o_ref[...] = ((p[0] + p[1]) + p[2]) + p[3]

    return pl.pallas_call(
        kern, name=name, grid=(r // tr,),
        in_specs=[pl.BlockSpec((4, tr, cols), lambda i: (0, i, 0))],
        out_specs=pl.BlockSpec((tr, cols), lambda i: (i, 0)),
        out_shape=jax.ShapeDtypeStruct((r, cols), F32),
        compiler_params=_params(("arbitrary",)),
    )(parts)


_SHARDED = {
    "norm_pre": (1, True), "norm_post": (1, True),
    "ffn1_w_gate": (1, False), "ffn1_w_up": (1, False), "ffn1_w_down": (0, False),
    "ffn2_w_gate": (1, False), "ffn2_w_up": (1, False), "ffn2_w_down": (0, False),
    "w_in": (1, False), "conv_w": (1, True), "gla_w_alpha": (1, False),
    "w_branch": (2, False), "w_out": (0, False),
}
_TRANSPOSED = ("ffn1_w_gate", "ffn1_w_up", "ffn2_w_gate", "ffn2_w_up", "w_in", "w_branch")


def _storage(name, a):
    return jnp.swapaxes(a, -1, -2) if name in _TRANSPOSED else a


def _storage_axis(name, shape):
    axis = _SHARDED[name][0]
    if name in _TRANSPOSED and axis >= len(shape) - 2:
        axis = 2 * len(shape) - 3 - axis
    return axis


_REPLICATED = ("conv_b", "conv_ln_g", "conv_ln_b", "gla_b_alpha", "gla_norm_g")
_WEIGHTS = ("norm_pre", "norm_post", "ffn1_w_gate", "ffn1_w_up", "ffn1_w_down", "ffn2_w_gate", "ffn2_w_up",
            "ffn2_w_down", "w_in", "conv_w", "conv_b", "conv_ln_g", "conv_ln_b", "gla_w_alpha", "gla_b_alpha",
            "gla_norm_g", "w_branch", "w_out")


PACK_ROW_UNIT = 16
BIG_PIECE = PACK_ROW_UNIT * PACK_COLS


def _rows_of(a, nlead):
    lead = a.shape[:nlead]
    n = math.prod(a.shape[nlead:])
    if n % PACK_COLS:
        flat = a.reshape(lead + (n,))
        flat = jnp.pad(flat, [(0, 0)] * nlead + [(0, -n % PACK_COLS)])
        n += -n % PACK_COLS
        a = flat
    rows = a.reshape(lead + (n // PACK_COLS, PACK_COLS))
    return jnp.pad(rows, [(0, 0)] * nlead + [(0, -rows.shape[nlead] % PACK_ROW_UNIT), (0, 0)])


def _padded_rows(n):
    rows = -(-n // PACK_COLS)
    return -(-rows // PACK_ROW_UNIT) * PACK_ROW_UNIT


def _is_big(shape, exact):
    return not exact and math.prod(shape) >= BIG_PIECE


def _pack_weights(shards, l):
    big, small = [], []
    for name, (_, exact) in _SHARDED.items():
        a = _storage(name, shards[name][l])
        if _is_big(a.shape, exact):
            big.append(_rows_of(a.astype(BF16), 0))
        elif exact:
            small.append(lax.bitcast_convert_type(a, BF16).reshape(-1))
        else:
            small.append(a.astype(BF16).reshape(-1))
    return jnp.concatenate(big + [_rows_of(jnp.concatenate(small), 0)], axis=0)


def _unpack_weights(gathered, shards, l):
    full, r0 = {}, 0

    def merge(seg, axis, shp):
        seg = jnp.moveaxis(seg, 0, axis)
        return seg.reshape(shp[:axis] + (N_DEV * shp[axis],) + shp[axis + 1:])

    def shape_axis(name):
        shp = jax.eval_shape(lambda a: _storage(name, a), shards[name][l]).shape
        return shp, _storage_axis(name, shp)

    for name, (_, exact) in _SHARDED.items():
        shp, axis = shape_axis(name)
        if _is_big(shp, exact):
            n = math.prod(shp)
            nr = _padded_rows(n)
            seg = gathered[:, r0:r0 + nr].reshape(N_DEV, -1)[:, :n] if n % PACK_COLS else \
                gathered[:, r0:r0 + n // PACK_COLS]
            full[name] = merge(seg.reshape((N_DEV,) + shp), axis, shp)
            r0 += nr
    flat, off = gathered[:, r0:].reshape(N_DEV, -1), 0
    for name, (_, exact) in _SHARDED.items():
        shp, axis = shape_axis(name)
        if not _is_big(shp, exact):
            n = math.prod(shp) * (2 if exact else 1)
            seg = flat[:, off:off + n]
            off += n
            if exact:
                seg = lax.bitcast_convert_type(seg.reshape((N_DEV,) + shp + (2,)), F32)
            full[name] = merge(seg.reshape((N_DEV,) + shp), axis, shp)
    return full


def _pack_grads(full_grads, repl_grads):
    big, small = [], []
    for name in _SHARDED:
        g = full_grads[name].astype(BF16)
        shp = g.shape
        axis = _storage_axis(name, shp)
        g = g.reshape(shp[:axis] + (N_DEV, shp[axis] // N_DEV) + shp[axis + 1:])
        g = jnp.moveaxis(g, axis, 0)
        if _is_big(g.shape[1:], False):
            big.append(_rows_of(g, 1))
        else:
            small.append(g.reshape(N_DEV, -1))
    for name in _REPLICATED:
        small.append(jnp.broadcast_to(repl_grads[name].reshape(1, -1).astype(BF16),
                                      (N_DEV, repl_grads[name].size)))
    rows = jnp.concatenate(big + [_rows_of(jnp.concatenate(small, axis=1), 1)], axis=1)
    return rows.reshape((N_CHIPS, 2) + rows.shape[1:])


def _unpack_grads(summed, shards):
    out, r0 = {}, 0

    def storage_shape(name):
        return jax.eval_shape(lambda a: _storage(name, a), shards[name][0]).shape

    for name in _SHARDED:
        shp = storage_shape(name)
        if _is_big(shp, False):
            n = math.prod(shp)
            seg = summed[r0:r0 + _padded_rows(n)].reshape(-1)[:n] if n % PACK_COLS else \
                summed[r0:r0 + n // PACK_COLS]
            out[name] = _storage(name, seg.reshape(shp))
            r0 += _padded_rows(n)
    flat, off = summed[r0:].reshape(-1), 0
    for name in tuple(_SHARDED) + _REPLICATED:
        shp = shards[name].shape[1:] if name in _REPLICATED else storage_shape(name)
        if name in _REPLICATED or not _is_big(shp, False):
            n = math.prod(shp)
            seg = flat[off:off + n].reshape(shp)
            out[name] = seg if name in _REPLICATED else _storage(name, seg)
            off += n
    return out


def _adamw_natural(w, g, m, v, name):
    shp = w.shape
    view = lambda a: a.reshape(-1, shp[-1])
    outs = _adamw(view(w), view(g), view(m), view(v), name)
    return [o.reshape(shp) for o in outs]


def _layer_weights(full, repl, l, d_model):
    bw = BRANCH_WIDTH
    w_in = full["w_in"]
    o_conv, o_gq, o_lr = 3 * bw, 5 * bw, 5 * bw + 2 * GLA_KEY_DIM + 2 * GLA_VALUE_DIM
    o_gate = o_lr + GLA_GATE_RANK
    lr_pad = LANE - GLA_GATE_RANK
    in_gla = jnp.concatenate([w_in[o_gq:o_gate], jnp.zeros((lr_pad, d_model), w_in.dtype)], axis=0)
    alpha = jnp.concatenate([full["gla_w_alpha"], jnp.zeros((lr_pad, GLA_KEY_DIM), BF16)], axis=0)
    row = lambda a: a.reshape(1, -1)
    mix = dict(pre=row(full["norm_pre"][1]), post=row(full["norm_post"][1]),
               in_sb=w_in[0:o_conv], in_conv=w_in[o_conv:o_gq], in_gla=in_gla, in_gate=w_in[o_gate:],
               conv_w=full["conv_w"], conv_b=row(repl["conv_b"][l]), ln_g=row(repl["conv_ln_g"][l]),
               ln_b=row(repl["conv_ln_b"][l]), alpha=alpha, b_alpha=row(repl["gla_b_alpha"][l]),
               gnorm=row(repl["gla_norm_g"][l]), branch=full["w_branch"], out=full["w_out"])
    return dict(
        pre0=row(full["norm_pre"][0]), post0=row(full["norm_post"][0]),
        pre2=row(full["norm_pre"][2]), post2=row(full["norm_post"][2]),
        ffn1=dict(gate=full["ffn1_w_gate"], up=full["ffn1_w_up"], down=full["ffn1_w_down"]),
        ffn2=dict(gate=full["ffn2_w_gate"], up=full["ffn2_w_up"], down=full["ffn2_w_down"]),
        mix=mix)


def _full_grads(g):
    mix = g["mix"]
    kd, vd = GLA_KEY_DIM, GLA_VALUE_DIM
    d_in = jnp.concatenate([mix["in_sb"], mix["in_conv"], mix["in_gla"][:2 * kd + 2 * vd + GLA_GATE_RANK],
                            mix["in_gate"]], axis=0)
    full = {
        "norm_pre": jnp.concatenate([g["ffn1"]["pre"], mix["pre"], g["ffn2"]["pre"]], axis=0),
        "norm_post": jnp.concatenate([g["ffn1"]["post"], mix["post"], g["ffn2"]["post"]], axis=0),
        "ffn1_w_gate": g["ffn1"]["gate"], "ffn1_w_up": g["ffn1"]["up"], "ffn1_w_down": g["ffn1"]["down"],
        "ffn2_w_gate": g["ffn2"]["gate"], "ffn2_w_up": g["ffn2"]["up"], "ffn2_w_down": g["ffn2"]["down"],
        "w_in": d_in, "conv_w": mix["conv_w"], "gla_w_alpha": mix["alpha"][:GLA_GATE_RANK],
        "w_branch": mix["branch"], "w_out": mix["out"],
    }
    repl = {"conv_b": mix["conv_b"], "conv_ln_g": mix["ln_g"], "conv_ln_b": mix["ln_b"],
            "gla_b_alpha": mix["b_alpha"], "gla_norm_g": mix["gnorm"]}
    return full, repl


def kernel(x, norm_pre, norm_post, ffn1_w_gate, ffn1_w_up, ffn1_w_down, ffn2_w_gate, ffn2_w_up, ffn2_w_down, w_in, conv_w, conv_b, conv_ln_g, conv_ln_b, gla_w_alpha, gla_b_alpha, gla_norm_g, w_branch, w_out, loss_target, m_norm_pre, m_norm_post, m_ffn1_w_gate, m_ffn1_w_up, m_ffn1_w_down, m_ffn2_w_gate, m_ffn2_w_up, m_ffn2_w_down, m_w_in, m_conv_w, m_conv_b, m_conv_ln_g, m_conv_ln_b, m_gla_w_alpha, m_gla_b_alpha, m_gla_norm_g, m_w_branch, m_w_out, v_norm_pre, v_norm_post, v_ffn1_w_gate, v_ffn1_w_up, v_ffn1_w_down, v_ffn2_w_gate, v_ffn2_w_up, v_ffn2_w_down, v_w_in, v_conv_w, v_conv_b, v_conv_ln_g, v_conv_ln_b, v_gla_w_alpha, v_gla_b_alpha, v_gla_norm_g, v_w_branch, v_w_out):
    weights = dict(norm_pre=norm_pre, norm_post=norm_post, ffn1_w_gate=ffn1_w_gate, ffn1_w_up=ffn1_w_up,
                   ffn1_w_down=ffn1_w_down, ffn2_w_gate=ffn2_w_gate, ffn2_w_up=ffn2_w_up, ffn2_w_down=ffn2_w_down,
                   w_in=w_in, conv_w=conv_w, conv_b=conv_b, conv_ln_g=conv_ln_g, conv_ln_b=conv_ln_b,
                   gla_w_alpha=gla_w_alpha, gla_b_alpha=gla_b_alpha, gla_norm_g=gla_norm_g, w_branch=w_branch,
                   w_out=w_out)
    moments_m = dict(norm_pre=m_norm_pre, norm_post=m_norm_post, ffn1_w_gate=m_ffn1_w_gate, ffn1_w_up=m_ffn1_w_up,
                     ffn1_w_down=m_ffn1_w_down, ffn2_w_gate=m_ffn2_w_gate, ffn2_w_up=m_ffn2_w_up,
                     ffn2_w_down=m_ffn2_w_down, w_in=m_w_in, conv_w=m_conv_w, conv_b=m_conv_b,
                     conv_ln_g=m_conv_ln_g, conv_ln_b=m_conv_ln_b, gla_w_alpha=m_gla_w_alpha,
                     gla_b_alpha=m_gla_b_alpha, gla_norm_g=m_gla_norm_g, w_branch=m_w_branch, w_out=m_w_out)
    moments_v = dict(norm_pre=v_norm_pre, norm_post=v_norm_post, ffn1_w_gate=v_ffn1_w_gate, ffn1_w_up=v_ffn1_w_up,
                     ffn1_w_down=v_ffn1_w_down, ffn2_w_gate=v_ffn2_w_gate, ffn2_w_up=v_ffn2_w_up,
                     ffn2_w_down=v_ffn2_w_down, w_in=v_w_in, conv_w=v_conv_w, conv_b=v_conv_b,
                     conv_ln_g=v_conv_ln_g, conv_ln_b=v_conv_ln_b, gla_w_alpha=v_gla_w_alpha,
                     gla_b_alpha=v_gla_b_alpha, gla_norm_g=v_gla_norm_g, w_branch=v_w_branch, w_out=v_w_out)
    n_layers = norm_pre.shape[0]
    t, d_model = x.shape[1], x.shape[2]

    layers = []
    for l in range(n_layers):
        gathered = _all_gather(_pack_weights(weights, l), f"gather_l{l}")
        full = _unpack_weights(gathered, weights, l)
        layers.append(_layer_weights(full, weights, l, d_model))

    sq, dx, grads = _local_step(x[0], loss_target[0], layers)
    loss = lax.psum(0.5 * jnp.sum(sq) / d_model, MESH_AXES)

    layer_grads = []
    for l in range(n_layers):
        packed = _pack_grads(*_full_grads(grads[l]))
        got = _exchange_sibling(packed, f"rs_sibling_l{l}")
        part = _add_own(packed, got, f"rs_add_l{l}")
        parts = _exchange_chips(part, f"rs_chips_l{l}")
        layer_grads.append(_unpack_grads(_sum_chips(parts, f"rs_sum_l{l}"), weights))

    g_out, d_out, m_out, v_out = [], [], [], []
    for name in _WEIGHTS:
        g = jnp.stack([lg[name] for lg in layer_grads])
        delta, new_m, new_v = _adamw_natural(weights[name], g, moments_m[name], moments_v[name], f"adamw_{name}")
        g_out.append(g)
        d_out.append(delta)
        m_out.append(new_m)
        v_out.append(new_v)
    return tuple([loss, dx[None]] + g_out + d_out + m_out + v_out)
```

```python
import functools
import math

import jax
import jax.numpy as jnp
from jax import lax
from jax.experimental import pallas as pl
from jax.experimental.pallas import tpu as pltpu

F32 = jnp.float32
BF16 = jnp.bfloat16

VMEM_LIMIT_BYTES = 48 * 1024 * 1024
LANE = 128

NORM_EPS = 1e-6
CHUNK = 64
N_BRANCHES = 3
BRANCH_WIDTH = 512
SB_HEADS = 8
SB_HEAD_DIM = 64
CONV_WIDTH = 31
CONV_HALO = 32
GLA_HEADS = 4
GLA_HEAD_K = 64
GLA_HEAD_V = 128
GLA_KEY_DIM = GLA_HEADS * GLA_HEAD_K
GLA_VALUE_DIM = GLA_HEADS * GLA_HEAD_V
GLA_GATE_RANK = 16
GLA_GATE_TAU = 16.0
SB_BQ = 1024
SB_BK = 128

ADAM_LR = 0.001
ADAM_B1 = 0.9
ADAM_B2 = 0.999
ADAM_EPS = 1e-08
ADAM_WD = 0.01
ADAM_STEP = 10

N_DEV = 8
MESH_AXES = ("x", "y", "c")
PACK_COLS = 1024


def _params(sem):
    return pltpu.CompilerParams(dimension_semantics=sem, vmem_limit_bytes=VMEM_LIMIT_BYTES)


def _div_tile(n, cap, unit):
    if n <= cap:
        return n
    best = None
    for t in range(unit, cap + 1, unit):
        if n % t == 0:
            best = t
    assert best is not None, (n, cap, unit)
    return best


_NN = (((1,), (0,)), ((), ()))
_NT = (((1,), (1,)), ((), ()))
_TN = (((0,), (0,)), ((), ()))


def _dot(a, b, dims=_NN):
    return lax.dot_general(a, b, dims, preferred_element_type=F32)


def _sigmoid(v):
    return 1.0 / (1.0 + jnp.exp(-v))


def _mm(a, b, *, nt=False, tn=False, out_dtype=F32, name):
    a = a.astype(BF16)
    b = b.astype(BF16)
    k, m = a.shape[::-1] if not tn else a.shape
    n = b.shape[0] if nt else b.shape[1]
    assert (b.shape[1] if nt else b.shape[0]) == k
    tm = _div_tile(m, 2048, LANE) if tn else _div_tile(m, 512, 16)
    tn_ = _div_tile(n, 2048, LANE)
    tk = _div_tile(k, 2048, LANE)
    nk = k // tk
    dims = _NT if nt else (_TN if tn else _NN)

    def kern(a_ref, b_ref, o_ref, acc_ref):
        kk = pl.program_id(2)

        @pl.when(kk == 0)
        def _():
            acc_ref[...] = jnp.zeros_like(acc_ref)

        acc_ref[...] += _dot(a_ref[...], b_ref[...], dims)

        @pl.when(kk == nk - 1)
        def _():
            o_ref[...] = acc_ref[...].astype(o_ref.dtype)

    b_spec = (pl.BlockSpec((tn_, tk), lambda i, j, kk: (j, kk)) if nt
              else pl.BlockSpec((tk, tn_), lambda i, j, kk: (kk, j)))
    return pl.pallas_call(
        kern, name=name,
        out_shape=jax.ShapeDtypeStruct((m, n), out_dtype),
        grid=(m // tm, n // tn_, nk),
        in_specs=[pl.BlockSpec((tk, tm), lambda i, j, kk: (kk, i)) if tn
                  else pl.BlockSpec((tm, tk), lambda i, j, kk: (i, kk)), b_spec],
        out_specs=pl.BlockSpec((tm, tn_), lambda i, j, kk: (i, j)),
        scratch_shapes=[pltpu.VMEM((tm, tn_), F32)],
        compiler_params=_params(("parallel", "parallel", "arbitrary")),
    )(a, b)


def _rowwise(name, body, mats, vecs, outs, sums=(), tm=256):
    mats = [m if isinstance(m, tuple) else (m, 0, m.shape[1]) for m in mats]
    t = mats[0][0].shape[0]
    tm = _div_tile(t, tm, 8)
    nm, nv, no, ns = len(mats), len(vecs), len(outs), len(sums)

    def kern(*refs):
        i = pl.program_id(0)
        ins = [r[...] for r in refs[:nm + nv]]
        res = body(*ins)
        out_vals, sum_vals = res[:no], res[no:]
        for r, val in zip(refs[nm + nv:nm + nv + no], out_vals):
            if isinstance(val, (list, tuple)):
                off = 0
                for piece in val:
                    w = piece.shape[1]
                    r[:, off:off + w] = piece.astype(r.dtype)
                    off += w
            else:
                r[...] = val.astype(r.dtype)
        if ns:
            sum_refs = refs[nm + nv + no:]

            @pl.when(i == 0)
            def _():
                for r in sum_refs:
                    r[...] = jnp.zeros_like(r)

            for r, val in zip(sum_refs, sum_vals):
                r[...] += jnp.sum(val, axis=0, keepdims=True)

    in_specs = [pl.BlockSpec((tm, w), functools.partial(lambda i, cb: (i, cb), cb=cb)) for (_, cb, w) in mats]
    in_specs += [pl.BlockSpec(v.shape, lambda i: (0, 0)) for v in vecs]
    out_specs = [pl.BlockSpec((tm, w), lambda i: (i, 0)) for (w, _) in outs]
    out_specs += [pl.BlockSpec((1, w), lambda i: (0, 0)) for w in sums]
    out_shape = [jax.ShapeDtypeStruct((t, w), dt) for (w, dt) in outs]
    out_shape += [jax.ShapeDtypeStruct((1, w), F32) for w in sums]
    return pl.pallas_call(
        kern, name=name, out_shape=out_shape, grid=(t // tm,),
        in_specs=in_specs, out_specs=out_specs,
        compiler_params=_params(("arbitrary",)),
    )(*[m[0] for m in mats], *vecs)


def _rms_fwd(x, g, name):
    d = x.shape[1]

    def body(xv, gv):
        r = lax.rsqrt(jnp.mean(xv * xv, axis=-1, keepdims=True) + NORM_EPS)
        return ((xv * r) * gv,)

    return _rowwise(name, body, [x], [g], [(d, BF16)])[0]


def _post_res(o, x, g, c, name):
    d = x.shape[1]

    def body(ov, xv, gv):
        r = lax.rsqrt(jnp.mean(ov * ov, axis=-1, keepdims=True) + NORM_EPS)
        return (xv + c * ((ov * r) * gv),)

    return _rowwise(name, body, [o, x], [g], [(d, F32)])[0]


def _post_bwd(dx, o, g, c, name):
    d = dx.shape[1]

    def body(dxv, ov, gv):
        r = lax.rsqrt(jnp.mean(ov * ov, axis=-1, keepdims=True) + NORM_EPS)
        n = ov * r
        dy = c * dxv
        dn = dy * gv
        do = r * (dn - n * jnp.mean(dn * n, axis=-1, keepdims=True))
        return (do, dy * n)

    return _rowwise(name, body, [dx, o], [g], [(d, BF16)], sums=[d])


def _rms_bwd(dh_parts, x, g, dx_res, name):
    d = x.shape[1]
    npart = len(dh_parts)

    def body(*vals):
        dh = vals[0]
        for p in vals[1:npart]:
            dh = dh + p
        xv, dres, gv = vals[npart], vals[npart + 1], vals[npart + 2]
        r = lax.rsqrt(jnp.mean(xv * xv, axis=-1, keepdims=True) + NORM_EPS)
        n = xv * r
        dn = dh * gv
        dx = dres + r * (dn - n * jnp.mean(dn * n, axis=-1, keepdims=True))
        return (dx, dh * n)

    return _rowwise(name, body, list(dh_parts) + [x, dx_res], [g], [(d, F32)], sums=[d], tm=128)


def _ffn_gate_up(h, wg, wu, name):
    t, d = h.shape
    f = wg.shape[0]
    tm = _div_tile(t, 512, 16)
    tn = _div_tile(f, 1536, LANE)

    def kern(h_ref, wg_ref, wu_ref, g_ref, u_ref, a_ref):
        hv = h_ref[...]
        g = _dot(hv, wg_ref[...], _NT).astype(BF16)
        u = _dot(hv, wu_ref[...], _NT).astype(BF16)
        g_ref[...] = g
        u_ref[...] = u
        gf = g.astype(F32)
        a_ref[...] = ((gf * _sigmoid(gf)) * u.astype(F32)).astype(BF16)

    w_spec = pl.BlockSpec((tn, d), lambda i, j: (j, 0))
    o_spec = pl.BlockSpec((tm, tn), lambda i, j: (i, j))
    return pl.pallas_call(
        kern, name=name, grid=(t // tm, f // tn),
        out_shape=[jax.ShapeDtypeStruct((t, f), BF16)] * 3,
        in_specs=[pl.BlockSpec((tm, d), lambda i, j: (i, 0)), w_spec, w_spec],
        out_specs=[o_spec, o_spec, o_spec],
        compiler_params=_params(("parallel", "parallel")),
    )(h, wg, wu)


def _ffn_dact(do, wd, gate, up, name):
    t, d = do.shape
    f = wd.shape[0]
    tm = _div_tile(t, 512, 16)
    tn = _div_tile(f, 1536, LANE)

    def kern(do_ref, wd_ref, g_ref, u_ref, dg_ref, du_ref):
        da = _dot(do_ref[...], wd_ref[...], _NT)
        gv = g_ref[...].astype(F32)
        uv = u_ref[...].astype(F32)
        sg = _sigmoid(gv)
        dg_ref[...] = (da * uv * (sg * (1.0 + gv * (1.0 - sg)))).astype(BF16)
        du_ref[...] = (da * (gv * sg)).astype(BF16)

    o_spec = pl.BlockSpec((tm, tn), lambda i, j: (i, j))
    return pl.pallas_call(
        kern, name=name, grid=(t // tm, f // tn),
        out_shape=[jax.ShapeDtypeStruct((t, f), BF16)] * 2,
        in_specs=[pl.BlockSpec((tm, d), lambda i, j: (i, 0)), pl.BlockSpec((tn, d), lambda i, j: (j, 0)),
                  o_spec, o_spec],
        out_specs=[o_spec, o_spec],
        compiler_params=_params(("parallel", "parallel")),
    )(do, wd, gate, up)


def _merge_fwd(bds, logits, name):
    d = bds[0].shape[1]

    def body(b0, b1, b2, l0, l1, l2):
        return (_sigmoid(l0) * b0 + _sigmoid(l1) * b1 + _sigmoid(l2) * b2,)

    mats = list(bds) + [(logits, j, d) for j in range(N_BRANCHES)]
    return _rowwise(name, body, mats, [], [(d, BF16)], tm=128)[0]


def _merge_bwd(dmerged, bds, logits, name):
    d = bds[0].shape[1]

    def body(dm, b0, b1, b2, l0, l1, l2):
        dbs, dls = [], []
        for b, l in ((b0, l0), (b1, l1), (b2, l2)):
            s = _sigmoid(l)
            dbs.append(dm * s)
            dls.append(dm * b * (s * (1.0 - s)))
        return (dbs[0], dbs[1], dbs[2], dls)

    mats = [dmerged] + list(bds) + [(logits, j, d) for j in range(N_BRANCHES)]
    outs = [(d, BF16)] * 3 + [(N_BRANCHES * d, BF16)]
    return _rowwise(name, body, mats, [], outs, tm=128)


def _la_fwd(pre, b, name):
    w = pre.shape[1]

    def body(pv, bv):
        p = pv + bv
        sp = jnp.maximum(-p, 0.0) + jnp.log(1.0 + jnp.exp(-jnp.abs(p)))
        return (-sp / GLA_GATE_TAU,)

    return _rowwise(name, body, [pre], [b], [(w, F32)])[0]


def _la_bwd(dla, pre, b, name):
    w = pre.shape[1]

    def body(dv, pv, bv):
        p = pv + bv
        dpre = (dv / GLA_GATE_TAU) * _sigmoid(-p)
        return (dpre, dpre)

    return _rowwise(name, body, [dla, pre], [b], [(w, BF16)], sums=[w])


def _gla_post_fwd(o, p_gla, r_block, gn, name):
    w = o.shape[1]

    def body(ov, rv, gv):
        pieces = []
        for h in range(GLA_HEADS):
            sl = slice(h * GLA_HEAD_V, (h + 1) * GLA_HEAD_V)
            oh = ov[:, sl]
            rr = lax.rsqrt(jnp.mean(oh * oh, axis=-1, keepdims=True) + NORM_EPS)
            rh = rv[:, sl]
            pieces.append(((oh * rr) * gv[:, sl]) * (rh * _sigmoid(rh)))
        return (pieces,)

    return _rowwise(name, body, [o, (p_gla, r_block, w)], [gn], [(w, BF16)])[0]


def _gla_post_bwd(dout, o, p_gla, r_block, gn, name):
    w = o.shape[1]

    def body(dv, ov, rv, gv):
        dos, drs, dgs = [], [], []
        for h in range(GLA_HEADS):
            sl = slice(h * GLA_HEAD_V, (h + 1) * GLA_HEAD_V)
            oh, rh, gh, dh = ov[:, sl], rv[:, sl], gv[:, sl], dv[:, sl]
            rr = lax.rsqrt(jnp.mean(oh * oh, axis=-1, keepdims=True) + NORM_EPS)
            nhat = oh * rr
            s = _sigmoid(rh)
            dn = dh * (rh * s)
            drs.append(dh * (nhat * gh) * (s * (1.0 + rh * (1.0 - s))))
            dnn = dn * gh
            dos.append(rr * (dnn - nhat * jnp.mean(dnn * nhat, axis=-1, keepdims=True)))
            dgs.append(dn * nhat)
        return (dos, drs, jnp.concatenate(dgs, axis=1))

    return _rowwise(name, body, [dout, o, (p_gla, r_block, w)], [gn], [(w, F32), (w, BF16)], sums=[w])


def _conv_bwd1(dout, y, ln_g, ln_b, name):
    w = y.shape[1]

    def body(dv, yv, gv, bv):
        mu = jnp.mean(yv, axis=-1, keepdims=True)
        yc = yv - mu
        rstd = lax.rsqrt(jnp.mean(yc * yc, axis=-1, keepdims=True) + NORM_EPS)
        xhat = yc * rstd
        yn = xhat * gv + bv
        s = _sigmoid(yn)
        dyn = dv * (s * (1.0 + yn * (1.0 - s)))
        dxh = dyn * gv
        dy = rstd * (dxh - jnp.mean(dxh, axis=-1, keepdims=True)
                     - xhat * jnp.mean(dxh * xhat, axis=-1, keepdims=True))
        return (dy, dyn * xhat, dyn, dy)

    return _rowwise(name, body, [dout, y], [ln_g, ln_b], [(w, F32)], sums=[w, w, w])


def _loss_fwd(y, target, name):
    d = y.shape[1]

    def body(yv, tv):
        e = yv - tv
        return (e / d, e * e)

    return _rowwise(name, body, [y, target], [], [(d, F32)], sums=[d])


def _adamw(w, g, m, v, name):
    cols = w.shape[1]
    c1 = 1.0 - ADAM_B1 ** ADAM_STEP
    c2 = 1.0 - ADAM_B2 ** ADAM_STEP

    def body(wv, gv, mv, vv):
        m2 = ADAM_B1 * mv + (1.0 - ADAM_B1) * gv
        v2 = ADAM_B2 * vv + (1.0 - ADAM_B2) * (gv * gv)
        m_hat = m2 / c1
        v_hat = v2 / c2
        delta = -ADAM_LR * (m_hat / (jnp.sqrt(v_hat) + ADAM_EPS) + ADAM_WD * wv)
        return (delta, m2, v2)

    return _rowwise(name, body, [w, g, m, v], [], [(cols, F32)] * 3)


def _split_bf16(v):
    hi = v.astype(BF16)
    lo = (v - hi.astype(F32)).astype(BF16)
    return hi, lo


def _tri(n, strict):
    r = lax.broadcasted_iota(jnp.int32, (n, n), 0)
    c = lax.broadcasted_iota(jnp.int32, (n, n), 1)
    return jnp.where(r > c if strict else r >= c, 1.0, 0.0).astype(BF16)


def _sb_weights(z, c, mx, diag_offset):
    u = jnp.exp(-jnp.abs(z))
    sp = jnp.maximum(z, 0.0) + jnp.log(1.0 + u)
    if diag_offset is None:
        mask = None
        spm = sp
    else:
        rows = lax.broadcasted_iota(jnp.int32, z.shape, 0)
        cols = lax.broadcasted_iota(jnp.int32, z.shape, 1)
        mask = cols + diag_offset < rows
        spm = jnp.where(mask, sp, 0.0)
    suf = _dot(spm.astype(BF16), mx)
    w = jnp.exp(((z - sp) - suf) - c)
    if mask is not None:
        w = jnp.where(mask, w, 0.0)
    return sp, w, mask, suf[:, 0:1] + spm[:, 0:1]


def _sb_fwd(q, kt, vt, name):
    nh, t, dh = q.shape
    nkb, bk = kt.shape[1], kt.shape[3]
    bq = min(SB_BQ, t)
    scale = dh ** -0.5
    per = bq // bk

    def kern(q_ref, kt_ref, vt_ref, o_ref, c_ref, zbuf, wbuf):
        qi = pl.program_id(1)
        top = (qi + 1) * per - 1
        ntiles = (qi + 1) * per
        qs = (q_ref[...].astype(F32) * scale).astype(BF16)
        mx = _tri(bk, True)
        o_ref[...] = jnp.zeros_like(o_ref)
        c_ref[...] = jnp.zeros_like(c_ref)
        zbuf[0] = _dot(qs, kt_ref[top])
        wbuf[0] = jnp.zeros((bq, bk), BF16)

        def tile(i, r0, masked):
            kb = top - i
            p = i % 2
            z = zbuf[p, r0:, :]
            w_prev = wbuf[p]
            c = c_ref[r0:, :]
            zbuf[1 - p] = _dot(qs, kt_ref[jnp.maximum(kb - 1, 0)])
            o_ref[...] += _dot(w_prev, vt_ref[jnp.minimum(kb + 1, nkb - 1)], _NT)
            _, w, _, tot = _sb_weights(z, c, mx, 0 if masked else None)
            if r0:
                wbuf[1 - p, :r0, :] = jnp.zeros((r0, bk), BF16)
            wbuf[1 - p, r0:, :] = w.astype(BF16)
            c_ref[r0:, :] = c + tot

        def step(i, carry):
            tile(i, 0, False)
            return carry

        for i in range(per):
            tile(i, (per - 1 - i) * bk, True)
        lax.fori_loop(per, ntiles, step, 0)
        o_ref[...] += _dot(wbuf[ntiles % 2], vt_ref[0], _NT)

    return pl.pallas_call(
        kern, name=name,
        out_shape=jax.ShapeDtypeStruct((nh, t, dh), F32),
        grid=(nh, t // bq),
        in_specs=[pl.BlockSpec((None, bq, dh), lambda h, i: (h, i, 0)),
                  pl.BlockSpec((None, nkb, dh, bk), lambda h, i: (h, 0, 0, 0)),
                  pl.BlockSpec((None, nkb, dh, bk), lambda h, i: (h, 0, 0, 0))],
        out_specs=pl.BlockSpec((None, bq, dh), lambda h, i: (h, i, 0)),
        scratch_shapes=[pltpu.VMEM((bq, 1), F32), pltpu.VMEM((2, bq, bk), F32), pltpu.VMEM((2, bq, bk), BF16)],
        compiler_params=_params(("parallel", "arbitrary")),
    )(q, kt, vt)


def _sb_bwd(q, qt, kt, vt, out, dout, doutt, name):
    nh, t, dh = q.shape
    nkb, bk = kt.shape[1], kt.shape[3]
    bq = min(SB_BQ, t)
    scale = dh ** -0.5
    per = bq // bk

    def kern(q_ref, qt_ref, kt_ref, vt_ref, o_ref, do_ref, dot_ref, dq_ref, dkt_ref, dvt_ref,
             c_ref, ce_ref, zbuf, dwbuf):
        qi = pl.program_id(1)
        top = (qi + 1) * per - 1

        @pl.when(qi == 0)
        def _():
            dkt_ref[...] = jnp.zeros_like(dkt_ref)
            dvt_ref[...] = jnp.zeros_like(dvt_ref)

        qs = (q_ref[...].astype(F32) * scale).astype(BF16)
        qts = (qt_ref[...].astype(F32) * scale).astype(BF16)
        dob = do_ref[...].astype(BF16)
        dotv = dot_ref[...]
        dsum = jnp.sum(dob.astype(F32) * o_ref[...], axis=1, keepdims=True)
        mx = _tri(bk, True)
        mi = _tri(bk, False)
        dq_ref[...] = jnp.zeros_like(dq_ref)
        c_ref[...] = jnp.zeros_like(c_ref)
        ce_ref[...] = jnp.zeros_like(ce_ref)

        zbuf[...] = _dot(qs, kt_ref[top])
        dwbuf[...] = _dot(dob, vt_ref[top])

        def tile(kb, r0, masked):
            kt_blk = kt_ref[kb]
            z = zbuf[r0:, :]
            dw = dwbuf[r0:, :]
            c = c_ref[r0:, :]
            sp, w, mask, tot = _sb_weights(z, c, mx, 0 if masked else None)
            wb = w.astype(BF16)
            e = dw * wb.astype(F32)
            before = dsum[r0:] - (_dot(e.astype(BF16), mi) + ce_ref[r0:, :])
            sig = jnp.exp(z - sp)
            dz = e - sig * (e + before)
            if mask is not None:
                dz = jnp.where(mask, dz, 0.0)
            dz = dz.astype(BF16)
            dq_ref[r0:, :] += _dot(dz, kt_blk, _NT)
            dkt_ref[kb] += _dot(qts[:, r0:], dz)
            dvt_ref[kb] += _dot(dotv[:, r0:], wb)
            c_ref[r0:, :] = c + tot
            ce_ref[r0:, :] += jnp.sum(e, axis=1, keepdims=True)
            kb_next = jnp.maximum(kb - 1, 0)
            zbuf[...] = _dot(qs, kt_ref[kb_next])
            dwbuf[...] = _dot(dob, vt_ref[kb_next])

        def step(i, carry):
            tile(top - i, 0, False)
            return carry

        for i in range(per):
            tile(top - i, (per - 1 - i) * bk, True)
        lax.fori_loop(per, (qi + 1) * per, step, 0)
        dq_ref[...] = dq_ref[...] * scale

    row = pl.BlockSpec((None, bq, dh), lambda h, i: (h, i, 0))
    col = pl.BlockSpec((None, dh, bq), lambda h, i: (h, 0, i))
    whole = pl.BlockSpec((None, nkb, dh, bk), lambda h, i: (h, 0, 0, 0))
    return pl.pallas_call(
        kern, name=name,
        out_shape=[jax.ShapeDtypeStruct((nh, t, dh), F32),
                   jax.ShapeDtypeStruct((nh, nkb, dh, bk), F32),
                   jax.ShapeDtypeStruct((nh, nkb, dh, bk), F32)],
        grid=(nh, t // bq),
        in_specs=[row, col, whole, whole, row, row, col],
        out_specs=[row, whole, whole],
        scratch_shapes=[pltpu.VMEM((bq, 1), F32), pltpu.VMEM((bq, 1), F32), pltpu.VMEM((bq, bk), F32),
                        pltpu.VMEM((bq, bk), F32)],
        compiler_params=_params(("parallel", "arbitrary")),
    )(q, qt, kt, vt, out, dout, doutt)


def _conv_fwd(p_conv, conv_w, conv_b, ln_g, ln_b, name):
    t, c2 = p_conv.shape
    c = c2 // 2
    tm = min(256, t)
    hb = tm // CONV_HALO

    def kern(a_ref, g_ref, ah_ref, gh_ref, w_ref, b_ref, lg_ref, lb_ref, o_ref, u_ref, y_ref, ubuf):
        i = pl.program_id(0)
        u = a_ref[...] * _sigmoid(g_ref[...])
        uh = ah_ref[...] * _sigmoid(gh_ref[...])
        ubuf[0:CONV_HALO, :] = jnp.where(i > 0, uh, 0.0)
        ubuf[CONV_HALO:CONV_HALO + tm, :] = u
        y = jnp.zeros((tm, c), F32) + b_ref[...]
        for j in range(CONV_WIDTH):
            off = CONV_HALO - (CONV_WIDTH - 1) + j
            y = y + ubuf[off:off + tm, :] * w_ref[j:j + 1, :]
        mu = jnp.mean(y, axis=-1, keepdims=True)
        yc = y - mu
        rstd = lax.rsqrt(jnp.mean(yc * yc, axis=-1, keepdims=True) + NORM_EPS)
        yn = (yc * rstd) * lg_ref[...] + lb_ref[...]
        o_ref[...] = (yn * _sigmoid(yn)).astype(o_ref.dtype)
        u_ref[...] = u
        y_ref[...] = y

    def halo(cb):
        return pl.BlockSpec((CONV_HALO, c), lambda i: (jnp.maximum(i * hb - 1, 0), cb))

    vec = pl.BlockSpec((1, c), lambda i: (0, 0))
    tile = pl.BlockSpec((tm, c), lambda i: (i, 0))
    return pl.pallas_call(
        kern, name=name,
        out_shape=[jax.ShapeDtypeStruct((t, c), BF16), jax.ShapeDtypeStruct((t, c), F32),
                   jax.ShapeDtypeStruct((t, c), F32)],
        grid=(t // tm,),
        in_specs=[tile, pl.BlockSpec((tm, c), lambda i: (i, 1)), halo(0), halo(1),
                  pl.BlockSpec(conv_w.shape, lambda i: (0, 0)), vec, vec, vec],
        out_specs=[tile, tile, tile],
        scratch_shapes=[pltpu.VMEM((tm + CONV_HALO, c), F32)],
        compiler_params=_params(("arbitrary",)),
    )(p_conv, p_conv, p_conv, p_conv, conv_w, conv_b, ln_g, ln_b)


def _conv_bwd2(dy, u, p_conv, conv_w, name):
    t, c = dy.shape
    tm = min(2 * CONV_HALO, t)
    hb = tm // CONV_HALO
    nt = t // tm
    last_halo = t // CONV_HALO - 1

    def kern(dy_ref, dyh_ref, u_ref, uh_ref, a_ref, g_ref, w_ref, dp_ref, dw_ref, dybuf, ubuf):
        i = pl.program_id(0)

        @pl.when(i == 0)
        def _():
            dw_ref[...] = jnp.zeros_like(dw_ref)

        dyv = dy_ref[...]
        dybuf[0:tm, :] = dyv
        dybuf[tm:tm + CONV_HALO, :] = jnp.where(i < nt - 1, dyh_ref[...], 0.0)
        ubuf[0:CONV_HALO, :] = jnp.where(i > 0, uh_ref[...], 0.0)
        ubuf[CONV_HALO:CONV_HALO + tm, :] = u_ref[...]
        du = jnp.zeros((tm, c), F32)
        for j in range(CONV_WIDTH):
            off = CONV_WIDTH - 1 - j
            du = du + dybuf[off:off + tm, :] * w_ref[j:j + 1, :]
            uoff = CONV_HALO - (CONV_WIDTH - 1) + j
            dw_ref[j:j + 1, :] += jnp.sum(dyv * ubuf[uoff:uoff + tm, :], axis=0, keepdims=True)
        a = a_ref[...]
        s = _sigmoid(g_ref[...])
        dp_ref[:, 0:c] = (du * s).astype(dp_ref.dtype)
        dp_ref[:, c:2 * c] = (du * a * (s * (1.0 - s))).astype(dp_ref.dtype)

    tile = pl.BlockSpec((tm, c), lambda i: (i, 0))
    return pl.pallas_call(
        kern, name=name,
        out_shape=[jax.ShapeDtypeStruct((t, 2 * c), BF16), jax.ShapeDtypeStruct((CONV_HALO, c), F32)],
        grid=(nt,),
        in_specs=[tile,
                  pl.BlockSpec((CONV_HALO, c), lambda i: (jnp.minimum((i + 1) * hb, last_halo), 0)),
                  tile,
                  pl.BlockSpec((CONV_HALO, c), lambda i: (jnp.maximum(i * hb - 1, 0), 0)),
                  tile, pl.BlockSpec((tm, c), lambda i: (i, 1)),
                  pl.BlockSpec(conv_w.shape, lambda i: (0, 0))],
        out_specs=[pl.BlockSpec((tm, 2 * c), lambda i: (i, 0)),
                   pl.BlockSpec((CONV_HALO, c), lambda i: (0, 0))],
        scratch_shapes=[pltpu.VMEM((tm + CONV_HALO, c), F32), pltpu.VMEM((tm + CONV_HALO, c), F32)],
        compiler_params=_params(("arbitrary",)),
    )(dy, dy, u, u, p_conv, p_conv, conv_w)


GLA_CHUNKS_PER_STEP = 16


def _gla_fwd(q_ck, k_kc, la_kc, v, name):
    nh, nc, ch, dk = q_ck.shape
    dv = v.shape[3]
    qscale = dk ** -0.5
    cb = min(GLA_CHUNKS_PER_STEP, nc)

    def kern(q_ref, k_ref, la_ref, v_ref, o_ref, st_ref, state_ref):
        mx = _tri(ch, True)

        @pl.when(pl.program_id(1) == 0)
        def _():
            state_ref[...] = jnp.zeros_like(state_ref)

        def step(n, state):
            la = la_ref[n]
            hi, lo = _split_bf16(la)
            de = _dot(hi, mx) + _dot(lo, mx)
            lam = jnp.exp(jnp.sum(la, axis=1, keepdims=True))
            kd = (k_ref[n] * jnp.exp(de)).astype(BF16)
            state = lam * state + _dot(kd, v_ref[n])
            st_ref[n] = state
            qs = (q_ref[n].astype(F32) * qscale).astype(BF16)
            o_ref[n] = _dot(qs, state.astype(BF16))
            return state

        state_ref[...] = lax.fori_loop(0, cb, step, state_ref[...])

    def spec(a, b):
        return pl.BlockSpec((None, cb, a, b), lambda h, j: (h, j, 0, 0))

    return pl.pallas_call(
        kern, name=name,
        out_shape=[jax.ShapeDtypeStruct((nh, nc, ch, dv), F32), jax.ShapeDtypeStruct((nh, nc, dk, dv), F32)],
        grid=(nh, nc // cb),
        in_specs=[spec(ch, dk), spec(dk, ch), spec(dk, ch), spec(ch, dv)],
        out_specs=[spec(ch, dv), spec(dk, dv)],
        scratch_shapes=[pltpu.VMEM((dk, dv), F32)],
        compiler_params=_params(("parallel", "arbitrary")),
    )(q_ck, k_kc, la_kc, v)


def _gla_bwd(q_ck, q_kc, k_ck, k_kc, la_ck, la_kc, v, states, states_prev, do, name):
    nh, nc, ch, dk = q_ck.shape
    dv = v.shape[3]
    qscale = dk ** -0.5
    cb = min(GLA_CHUNKS_PER_STEP, nc)
    nb = nc // cb

    def kern(q_ref, qt_ref, kck_ref, kkc_ref, lack_ref, lakc_ref, v_ref, st_ref, stp_ref, do_ref,
             dq_ref, dk_ref, dv_ref, dla_ref, g_ref):
        mx = _tri(ch, True)
        mxt = jnp.where(lax.broadcasted_iota(jnp.int32, (ch, ch), 0)
                        < lax.broadcasted_iota(jnp.int32, (ch, ch), 1), 1.0, 0.0).astype(BF16)

        @pl.when(pl.program_id(1) == 0)
        def _():
            g_ref[...] = jnp.zeros_like(g_ref)

        def step(i, g):
            n = cb - 1 - i
            la_kc = lakc_ref[n]
            hi, lo = _split_bf16(la_kc)
            de_kc = _dot(hi, mx) + _dot(lo, mx)
            lam = jnp.exp(jnp.sum(la_kc, axis=1, keepdims=True))
            hi2, lo2 = _split_bf16(lack_ref[n])
            de_ck = _dot(mxt, hi2) + _dot(mxt, lo2)
            edk = jnp.exp(de_kc)
            kd_kc = kkc_ref[n] * edk
            kd_ck = (kck_ref[n] * jnp.exp(de_ck)).astype(BF16)
            dob = do_ref[n].astype(BF16)
            dq_ref[n] = _dot(dob, st_ref[n].astype(BF16), _NT) * qscale
            qts = (qt_ref[n].astype(F32) * qscale).astype(BF16)
            ds = _dot(qts, dob) + g
            dsb = ds.astype(BF16)
            dlam = jnp.sum(ds * stp_ref[n], axis=1, keepdims=True)
            dkd = _dot(dsb, v_ref[n], _NT)
            dv_ref[n] = _dot(kd_ck, dsb)
            dk_ref[n] = dkd * edk
            dde = dkd * kd_kc
            h3, l3 = _split_bf16(dde)
            dla_ref[n] = _dot(h3, mxt) + _dot(l3, mxt) + dlam * lam
            return lam * ds

        g_ref[...] = lax.fori_loop(0, cb, step, g_ref[...])

    def spec(a, b):
        return pl.BlockSpec((None, cb, a, b), lambda h, j: (h, nb - 1 - j, 0, 0))

    return pl.pallas_call(
        kern, name=name,
        out_shape=[jax.ShapeDtypeStruct((nh, nc, ch, dk), F32), jax.ShapeDtypeStruct((nh, nc, dk, ch), F32),
                   jax.ShapeDtypeStruct((nh, nc, ch, dv), F32), jax.ShapeDtypeStruct((nh, nc, dk, ch), F32)],
        grid=(nh, nb),
        in_specs=[spec(ch, dk), spec(dk, ch), spec(ch, dk), spec(dk, ch), spec(ch, dk), spec(dk, ch),
                  spec(ch, dv), spec(dk, dv), spec(dk, dv), spec(ch, dv)],
        out_specs=[spec(ch, dk), spec(dk, ch), spec(ch, dv), spec(dk, ch)],
        scratch_shapes=[pltpu.VMEM((dk, dv), F32)],
        compiler_params=_params(("parallel", "arbitrary")),
    )(q_ck, q_kc, k_ck, k_kc, la_ck, la_kc, v, states, states_prev, do)


def _heads_rows(a, nh, blk):
    t = a.shape[0]
    d = a.shape[1] // nh
    return a.reshape(t // blk, blk, nh, d).transpose(2, 0, 1, 3)


def _heads_cols(a, nh, blk):
    t = a.shape[0]
    d = a.shape[1] // nh
    return a.T.reshape(nh, d, t // blk, blk).transpose(0, 2, 1, 3)


def _unheads_rows(a):
    nh, nb, blk, d = a.shape
    return a.transpose(1, 2, 0, 3).reshape(nb * blk, nh * d)


def _unheads_cols(a):
    nh, nb, d, blk = a.shape
    return a.transpose(0, 2, 1, 3).reshape(nh * d, nb * blk).T


def _ffn_fwd(x, w, pre, post, tag):
    h = _rms_fwd(x, pre, f"{tag}_rms")
    gate, up, act = _ffn_gate_up(h, w["gate"], w["up"], f"{tag}_gateup")
    o = _mm(act, w["down"], name=f"{tag}_down")
    x_out = _post_res(o, x, post, 0.5, f"{tag}_res")
    return x_out, (x, h, gate, up, act, o)


def _ffn_bwd(dx_out, saved, w, pre, post, tag):
    x, h, gate, up, act, o = saved
    do, dpost = _post_bwd(dx_out, o, post, 0.5, f"{tag}_bres")
    d_down = _mm(act, do, tn=True, out_dtype=BF16, name=f"{tag}_bwdown")
    dgate, dup = _ffn_dact(do, w["down"], gate, up, f"{tag}_bdact")
    dh_g = _mm(dgate, w["gate"], name=f"{tag}_bdhg")
    dh_u = _mm(dup, w["up"], name=f"{tag}_bdhu")
    d_gate = _mm(dgate, h, tn=True, out_dtype=BF16, name=f"{tag}_bwgate")
    d_up = _mm(dup, h, tn=True, out_dtype=BF16, name=f"{tag}_bwup")
    dx, dpre = _rms_bwd([dh_g, dh_u], x, pre, dx_out, f"{tag}_brms")
    return dx, dict(gate=d_gate, up=d_up, down=d_down, pre=dpre, post=dpost)


def _mixer_fwd(x, w, tag):
    t, d = x.shape
    bw = BRANCH_WIDTH
    h = _rms_fwd(x, w["pre"], f"{tag}_rms")
    p_sb = _mm(h, w["in_sb"], nt=True, out_dtype=BF16, name=f"{tag}_insb")
    p_conv = _mm(h, w["in_conv"], nt=True, name=f"{tag}_inconv")
    p_gla = _mm(h, w["in_gla"], nt=True, name=f"{tag}_ingla")
    p_gate = _mm(h, w["in_gate"], nt=True, name=f"{tag}_ingate")

    bk = min(SB_BK, t)
    q = p_sb[:, 0:bw].reshape(t, SB_HEADS, SB_HEAD_DIM).transpose(1, 0, 2)
    kt = _heads_cols(p_sb[:, bw:2 * bw], SB_HEADS, bk)
    vt = _heads_cols(p_sb[:, 2 * bw:3 * bw], SB_HEADS, bk)
    sb_o = _sb_fwd(q, kt, vt, f"{tag}_sb")
    sb_out = sb_o.transpose(1, 0, 2).reshape(t, bw)

    conv_out, conv_u, conv_y = _conv_fwd(p_conv, w["conv_w"], w["conv_b"], w["ln_g"], w["ln_b"], f"{tag}_conv")

    kd, vd = GLA_KEY_DIM, GLA_VALUE_DIM
    lr = p_gla[:, 2 * kd + 2 * vd:]
    pre_a = _mm(lr, w["alpha"], name=f"{tag}_alpha")
    la = _la_fwd(pre_a, w["b_alpha"], f"{tag}_la")
    gq = _heads_rows(p_gla[:, 0:kd].astype(BF16), GLA_HEADS, CHUNK)
    gk_kc = _heads_cols(p_gla[:, kd:2 * kd], GLA_HEADS, CHUNK)
    gv = _heads_rows(p_gla[:, 2 * kd:2 * kd + vd].astype(BF16), GLA_HEADS, CHUNK)
    la_kc = _heads_cols(la, GLA_HEADS, CHUNK)
    gla_o4, states = _gla_fwd(gq, gk_kc, la_kc, gv, f"{tag}_gla")
    gla_o = _unheads_rows(gla_o4)
    r_block = (2 * kd + vd) // vd
    gla_out = _gla_post_fwd(gla_o, p_gla, r_block, w["gnorm"], f"{tag}_glapost")

    branches = (sb_out.astype(BF16), conv_out, gla_out)
    bds = [_mm(branches[j], w["branch"][j], nt=True, name=f"{tag}_br{j}") for j in range(N_BRANCHES)]
    merged = _merge_fwd(bds, p_gate, f"{tag}_merge")
    mo = _mm(merged, w["out"], name=f"{tag}_out")
    x_out = _post_res(mo, x, w["post"], 1.0, f"{tag}_res")
    saved = dict(x=x, h=h, p_sb=p_sb, p_conv=p_conv, p_gla=p_gla, p_gate=p_gate, sb_o=sb_o, conv_u=conv_u,
                 sb_q=q, sb_kt=kt, sb_vt=vt, gla_q=gq, gla_k_kc=gk_kc, gla_v=gv, gla_la_kc=la_kc,
                 conv_y=conv_y, pre_a=pre_a, la=la, states=states, gla_o=gla_o, branches=branches, bds=bds,
                 merged=merged, mo=mo)
    return x_out, saved


def _mixer_bwd(dx_out, s, w, tag):
    x = s["x"]
    t, d = x.shape
    bw = BRANCH_WIDTH
    kd, vd = GLA_KEY_DIM, GLA_VALUE_DIM
    grads = {}
    dmo, grads["post"] = _post_bwd(dx_out, s["mo"], w["post"], 1.0, f"{tag}_bres")
    dmerged = _mm(dmo, w["out"], nt=True, name=f"{tag}_bdmerged")
    grads["out"] = _mm(s["merged"], dmo, tn=True, out_dtype=BF16, name=f"{tag}_bwout")
    dbd0, dbd1, dbd2, dp_gate = _merge_bwd(dmerged, s["bds"], s["p_gate"], f"{tag}_bmerge")
    dbds = (dbd0, dbd1, dbd2)
    dbranch = [_mm(dbds[j], w["branch"][j], name=f"{tag}_bdbr{j}") for j in range(N_BRANCHES)]
    grads["branch"] = jnp.stack([_mm(dbds[j], s["branches"][j], tn=True, out_dtype=BF16, name=f"{tag}_bwbr{j}")
                                 for j in range(N_BRANCHES)])

    p_sb = s["p_sb"]
    bk = min(SB_BK, t)
    q, kt, vt = s["sb_q"], s["sb_kt"], s["sb_vt"]
    qt = p_sb[:, 0:bw].reshape(t, SB_HEADS, SB_HEAD_DIM).transpose(1, 2, 0)
    dsb = dbranch[0].reshape(t, SB_HEADS, SB_HEAD_DIM)
    dout = dsb.transpose(1, 0, 2)
    doutt = dsb.transpose(1, 2, 0).astype(BF16)
    dq, dkt, dvt = _sb_bwd(q, qt, kt, vt, s["sb_o"], dout, doutt, f"{tag}_bsb")
    dp_sb = jnp.concatenate([dq.transpose(1, 0, 2).reshape(t, bw), _unheads_cols(dkt), _unheads_cols(dvt)],
                            axis=1).astype(BF16)

    dy, d_lng, d_lnb, d_cb = _conv_bwd1(dbranch[1], s["conv_y"], w["ln_g"], w["ln_b"], f"{tag}_bconv1")
    dp_conv, d_cw = _conv_bwd2(dy, s["conv_u"], s["p_conv"], w["conv_w"], f"{tag}_bconv2")
    grads.update(conv_w=d_cw[:CONV_WIDTH], conv_b=d_cb, ln_g=d_lng, ln_b=d_lnb)

    p_gla = s["p_gla"]
    r_block = (2 * kd + vd) // vd
    do, dr, grads["gnorm"] = _gla_post_bwd(dbranch[2], s["gla_o"], p_gla, r_block, w["gnorm"], f"{tag}_bglapost")
    gqf, gkf = p_gla[:, 0:kd], p_gla[:, kd:2 * kd]
    gq_b = gqf.astype(BF16)
    dq4, dk4, dv4, dla4 = _gla_bwd(
        s["gla_q"], _heads_cols(gq_b, GLA_HEADS, CHUNK),
        _heads_rows(gkf, GLA_HEADS, CHUNK), s["gla_k_kc"],
        _heads_rows(s["la"], GLA_HEADS, CHUNK), s["gla_la_kc"], s["gla_v"],
        s["states"], jnp.pad(s["states"][:, :-1], ((0, 0), (1, 0), (0, 0), (0, 0))),
        _heads_rows(do, GLA_HEADS, CHUNK), f"{tag}_bgla")
    dla = _unheads_cols(dla4)
    dpre_a, grads["b_alpha"] = _la_bwd(dla, s["pre_a"], w["b_alpha"], f"{tag}_bla")
    lr = p_gla[:, 2 * kd + 2 * vd:]
    dlr = _mm(dpre_a, w["alpha"], nt=True, out_dtype=BF16, name=f"{tag}_bdlr")
    grads["alpha"] = _mm(lr, dpre_a, tn=True, name=f"{tag}_bwalpha")
    dp_gla = jnp.concatenate([_unheads_rows(dq4).astype(BF16), _unheads_cols(dk4).astype(BF16),
                              _unheads_rows(dv4).astype(BF16), dr, dlr], axis=1)

    dps = dict(in_sb=dp_sb, in_conv=dp_conv, in_gla=dp_gla, in_gate=dp_gate)
    dh_parts = []
    for key, dp in dps.items():
        dh_parts.append(_mm(dp, w[key], name=f"{tag}_bdh_{key}"))
        grads[key] = _mm(dp, s["h"], tn=True, out_dtype=BF16, name=f"{tag}_bw_{key}")
    dx, grads["pre"] = _rms_bwd(dh_parts, x, w["pre"], dx_out, f"{tag}_brms")
    return dx, grads


def _local_step(x, target, layers):
    saved = []
    for l, w in enumerate(layers):
        x, s1 = _ffn_fwd(x, w["ffn1"], w["pre0"], w["post0"], f"l{l}_f1")
        x, s2 = _mixer_fwd(x, w["mix"], f"l{l}_mx")
        x, s3 = _ffn_fwd(x, w["ffn2"], w["pre2"], w["post2"], f"l{l}_f2")
        saved.append((s1, s2, s3))
    dx, sq = _loss_fwd(x, target, "loss")
    grads = [None] * len(layers)
    for l in reversed(range(len(layers))):
        w = layers[l]
        s1, s2, s3 = saved[l]
        dx, g3 = _ffn_bwd(dx, s3, w["ffn2"], w["pre2"], w["post2"], f"l{l}_f2")
        dx, g2 = _mixer_bwd(dx, s2, w["mix"], f"l{l}_mx")
        dx, g1 = _ffn_bwd(dx, s1, w["ffn1"], w["pre0"], w["post0"], f"l{l}_f1")
        grads[l] = dict(ffn1=g1, mix=g2, ffn2=g3)
    return sq, dx, grads


_ANY = pl.BlockSpec(memory_space=pl.ANY)
_MESH = pl.DeviceIdType.MESH
N_CHIPS = 4


def _all_gather(p, name):
    def body(p_ref, out_ref, send_sems, recv_sems, local_sem):
        x, y, c = lax.axis_index("x"), lax.axis_index("y"), lax.axis_index("c")
        me, sibling = (x, y, c), (x, y, 1 - c)
        chips = [(1 - x, y), (x, 1 - y), (1 - x, 1 - y)]

        def rows(px, py, pc):
            return out_ref.at[4 * px + 2 * py + pc]

        def copy(k, block, to, src=None):
            return pltpu.make_async_remote_copy(
                src_ref=rows(*block) if src is None else src, dst_ref=rows(*block),
                send_sem=send_sems.at[k], recv_sem=recv_sems.at[k], device_id=to, device_id_type=_MESH)

        mine = pltpu.make_async_copy(p_ref, rows(*me), local_sem)
        mine.start()
        first = [copy(0, me, sibling, src=p_ref)]
        first += [copy(1 + j, me, (*chip, c), src=p_ref) for j, chip in enumerate(chips)]
        for cp in first:
            cp.start()
        passed = [copy(4 + j, (*chip, c), sibling) for j, chip in enumerate(chips)]
        for j, chip in enumerate(chips):
            copy(1 + j, (*chip, c), me).wait_recv()
            passed[j].start()
        copy(0, sibling, me).wait_recv()
        for j, chip in enumerate(chips):
            copy(4 + j, (*chip, 1 - c), me).wait_recv()
        for cp in first + passed:
            cp.wait_send()
        mine.wait()

    return pl.pallas_call(
        body, name=name,
        out_shape=jax.ShapeDtypeStruct((N_DEV,) + p.shape, p.dtype),
        in_specs=[_ANY], out_specs=_ANY,
        scratch_shapes=[pltpu.SemaphoreType.DMA((7,)), pltpu.SemaphoreType.DMA((7,)), pltpu.SemaphoreType.DMA],
    )(p)


def _exchange_sibling(src, name):
    def body(src_ref, out_ref, send_sems, recv_sems):
        x, y, c = lax.axis_index("x"), lax.axis_index("y"), lax.axis_index("c")
        copies = [pltpu.make_async_remote_copy(
            src_ref=src_ref.at[j, 1 - c], dst_ref=out_ref.at[j], send_sem=send_sems.at[j],
            recv_sem=recv_sems.at[j], device_id=(x, y, 1 - c), device_id_type=_MESH) for j in range(N_CHIPS)]
        for cp in copies:
            cp.start()
        for cp in copies:
            cp.wait()

    return pl.pallas_call(
        body, name=name,
        out_shape=jax.ShapeDtypeStruct((N_CHIPS,) + src.shape[2:], src.dtype),
        in_specs=[_ANY], out_specs=_ANY,
        scratch_shapes=[pltpu.SemaphoreType.DMA((N_CHIPS,)), pltpu.SemaphoreType.DMA((N_CHIPS,))],
    )(src)


def _add_own(src, got, name):
    _, _, r, cols = src.shape
    tr = _div_tile(r, 512, 16)
    c = lax.axis_index("c").astype(jnp.int32).reshape(1)

    def kern(c_ref, a_ref, b_ref, o_ref):
        o_ref[...] = (a_ref[...].astype(F32) + b_ref[...].astype(F32)).astype(o_ref.dtype)

    grid_spec = pltpu.PrefetchScalarGridSpec(
        num_scalar_prefetch=1, grid=(N_CHIPS, r // tr),
        in_specs=[pl.BlockSpec((None, None, tr, cols), lambda j, i, c_ref: (j, c_ref[0], i, 0)),
                  pl.BlockSpec((None, tr, cols), lambda j, i, c_ref: (j, i, 0))],
        out_specs=pl.BlockSpec((None, tr, cols), lambda j, i, c_ref: (j, i, 0)))
    return pl.pallas_call(
        kern, name=name, grid_spec=grid_spec,
        out_shape=jax.ShapeDtypeStruct((N_CHIPS, r, cols), src.dtype),
        compiler_params=_params(("arbitrary", "arbitrary")),
    )(c, src, got)


def _exchange_chips(part, name):
    def body(part_ref, out_ref, send_sems, recv_sems, local_sem):
        x, y, c = lax.axis_index("x"), lax.axis_index("y"), lax.axis_index("c")
        my_chip = 2 * x + y
        chips = [(1 - x, y), (x, 1 - y), (1 - x, 1 - y)]
        mine = pltpu.make_async_copy(part_ref.at[my_chip], out_ref.at[my_chip], local_sem)
        mine.start()
        copies = []
        for k, (px, py) in enumerate(chips):
            copies.append(pltpu.make_async_remote_copy(
                src_ref=part_ref.at[2 * px + py], dst_ref=out_ref.at[my_chip],
                send_sem=send_sems.at[k], recv_sem=recv_sems.at[k], device_id=(px, py, c), device_id_type=_MESH))
        for cp in copies:
            cp.start()
        for k, (px, py) in enumerate(chips):
            pltpu.make_async_remote_copy(
                src_ref=part_ref.at[my_chip], dst_ref=out_ref.at[2 * px + py],
                send_sem=send_sems.at[k], recv_sem=recv_sems.at[k], device_id=(px, py, c),
                device_id_type=_MESH).wait_recv()
        for cp in copies:
            cp.wait_send()
        mine.wait()

    return pl.pallas_call(
        body, name=name,
        out_shape=jax.ShapeDtypeStruct(part.shape, part.dtype),
        in_specs=[_ANY], out_specs=_ANY,
        scratch_shapes=[pltpu.SemaphoreType.DMA((3,)), pltpu.SemaphoreType.DMA((3,)), pltpu.SemaphoreType.DMA],
    )(part)


def _sum_chips(parts, name):
    _, r, cols = parts.shape
    tr = _div_tile(r, 512, 16)

    def kern(p_ref, o_ref):
        p = [p_ref[j].astype(F32) for j in range(N_CHIPS)]
        o_ref[...] = ((p[0] + p[1]) + p[2]) + p[3]

    return pl.pallas_call(
        kern, name=name, grid=(r // tr,),
        in_specs=[pl.BlockSpec((4, tr, cols), lambda i: (0, i, 0))],
        out_specs=pl.BlockSpec((tr, cols), lambda i: (i, 0)),
        out_shape=jax.ShapeDtypeStruct((r, cols), F32),
        compiler_params=_params(("arbitrary",)),
    )(parts)


_SHARDED = {
    "norm_pre": (1, True), "norm_post": (1, True),
    "ffn1_w_gate": (1, False), "ffn1_w_up": (1, False), "ffn1_w_down": (0, False),
    "ffn2_w_gate": (1, False), "ffn2_w_up": (1, False), "ffn2_w_down": (0, False),
    "w_in": (1, False), "conv_w": (1, True), "gla_w_alpha": (1, False),
    "w_branch": (2, False), "w_out": (0, False),
}
_TRANSPOSED = ("ffn1_w_gate", "ffn1_w_up", "ffn2_w_gate", "ffn2_w_up", "w_in", "w_branch")


def _storage(name, a):
    return jnp.swapaxes(a, -1, -2) if name in _TRANSPOSED else a


def _storage_axis(name, shape):
    axis = _SHARDED[name][0]
    if name in _TRANSPOSED and axis >= len(shape) - 2:
        axis = 2 * len(shape) - 3 - axis
    return axis


_REPLICATED = ("conv_b", "conv_ln_g", "conv_ln_b", "gla_b_alpha", "gla_norm_g")
_WEIGHTS = ("norm_pre", "norm_post", "ffn1_w_gate", "ffn1_w_up", "ffn1_w_down", "ffn2_w_gate", "ffn2_w_up",
            "ffn2_w_down", "w_in", "conv_w", "conv_b", "conv_ln_g", "conv_ln_b", "gla_w_alpha", "gla_b_alpha",
            "gla_norm_g", "w_branch", "w_out")


PACK_ROW_UNIT = 16
BIG_PIECE = PACK_ROW_UNIT * PACK_COLS


def _rows_of(a, nlead):
    lead = a.shape[:nlead]
    n = math.prod(a.shape[nlead:])
    if n % PACK_COLS:
        flat = a.reshape(lead + (n,))
        flat = jnp.pad(flat, [(0, 0)] * nlead + [(0, -n % PACK_COLS)])
        n += -n % PACK_COLS
        a = flat
    rows = a.reshape(lead + (n // PACK_COLS, PACK_COLS))
    return jnp.pad(rows, [(0, 0)] * nlead + [(0, -rows.shape[nlead] % PACK_ROW_UNIT), (0, 0)])


def _padded_rows(n):
    rows = -(-n // PACK_COLS)
    return -(-rows // PACK_ROW_UNIT) * PACK_ROW_UNIT


def _is_big(shape, exact):
    return not exact and math.prod(shape) >= BIG_PIECE


def _pack_weights(shards, l):
    big, small = [], []
    for name, (_, exact) in _SHARDED.items():
        a = _storage(name, shards[name][l])
        if _is_big(a.shape, exact):
            big.append(_rows_of(a.astype(BF16), 0))
        elif exact:
            small.append(lax.bitcast_convert_type(a, BF16).reshape(-1))
        else:
            small.append(a.astype(BF16).reshape(-1))
    return jnp.concatenate(big + [_rows_of(jnp.concatenate(small), 0)], axis=0)


def _unpack_weights(gathered, shards, l):
    full, r0 = {}, 0

    def merge(seg, axis, shp):
        seg = jnp.moveaxis(seg, 0, axis)
        return seg.reshape(shp[:axis] + (N_DEV * shp[axis],) + shp[axis + 1:])

    def shape_axis(name):
        shp = jax.eval_shape(lambda a: _storage(name, a), shards[name][l]).shape
        return shp, _storage_axis(name, shp)

    for name, (_, exact) in _SHARDED.items():
        shp, axis = shape_axis(name)
        if _is_big(shp, exact):
            n = math.prod(shp)
            nr = _padded_rows(n)
            seg = gathered[:, r0:r0 + nr].reshape(N_DEV, -1)[:, :n] if n % PACK_COLS else \
                gathered[:, r0:r0 + n // PACK_COLS]
            full[name] = merge(seg.reshape((N_DEV,) + shp), axis, shp)
            r0 += nr
    flat, off = gathered[:, r0:].reshape(N_DEV, -1), 0
    for name, (_, exact) in _SHARDED.items():
        shp, axis = shape_axis(name)
        if not _is_big(shp, exact):
            n = math.prod(shp) * (2 if exact else 1)
            seg = flat[:, off:off + n]
            off += n
            if exact:
                seg = lax.bitcast_convert_type(seg.reshape((N_DEV,) + shp + (2,)), F32)
            full[name] = merge(seg.reshape((N_DEV,) + shp), axis, shp)
    return full


def _pack_grads(full_grads, repl_grads):
    big, small = [], []
    for name in _SHARDED:
        g = full_grads[name].astype(BF16)
        shp = g.shape
        axis = _storage_axis(name, shp)
        g = g.reshape(shp[:axis] + (N_DEV, shp[axis] // N_DEV) + shp[axis + 1:])
        g = jnp.moveaxis(g, axis, 0)
        if _is_big(g.shape[1:], False):
            big.append(_rows_of(g, 1))
        else:
            small.append(g.reshape(N_DEV, -1))
    for name in _REPLICATED:
        small.append(jnp.broadcast_to(repl_grads[name].reshape(1, -1).astype(BF16),
                                      (N_DEV, repl_grads[name].size)))
    rows = jnp.concatenate(big + [_rows_of(jnp.concatenate(small, axis=1), 1)], axis=1)
    return rows.reshape((N_CHIPS, 2) + rows.shape[1:])


def _unpack_grads(summed, shards):
    out, r0 = {}, 0

    def storage_shape(name):
        return jax.eval_shape(lambda a: _storage(name, a), shards[name][0]).shape

    for name in _SHARDED:
        shp = storage_shape(name)
        if _is_big(shp, False):
            n = math.prod(shp)
            seg = summed[r0:r0 + _padded_rows(n)].reshape(-1)[:n] if n % PACK_COLS else \
                summed[r0:r0 + n // PACK_COLS]
            out[name] = _storage(name, seg.reshape(shp))
            r0 += _padded_rows(n)
    flat, off = summed[r0:].reshape(-1), 0
    for name in tuple(_SHARDED) + _REPLICATED:
        shp = shards[name].shape[1:] if name in _REPLICATED else storage_shape(name)
        if name in _REPLICATED or not _is_big(shp, False):
            n = math.prod(shp)
            seg = flat[off:off + n].reshape(shp)
            out[name] = seg if name in _REPLICATED else _storage(name, seg)
            off += n
    return out


def _adamw_natural(w, g, m, v, name):
    shp = w.shape
    view = lambda a: a.reshape(-1, shp[-1])
    outs = _adamw(view(w), view(g), view(m), view(v), name)
    return [o.reshape(shp) for o in outs]


def _layer_weights(full, repl, l, d_model):
    bw = BRANCH_WIDTH
    w_in = full["w_in"]
    o_conv, o_gq, o_lr = 3 * bw, 5 * bw, 5 * bw + 2 * GLA_KEY_DIM + 2 * GLA_VALUE_DIM
    o_gate = o_lr + GLA_GATE_RANK
    lr_pad = LANE - GLA_GATE_RANK
    in_gla = jnp.concatenate([w_in[o_gq:o_gate], jnp.zeros((lr_pad, d_model), w_in.dtype)], axis=0)
    alpha = jnp.concatenate([full["gla_w_alpha"], jnp.zeros((lr_pad, GLA_KEY_DIM), BF16)], axis=0)
    row = lambda a: a.reshape(1, -1)
    mix = dict(pre=row(full["norm_pre"][1]), post=row(full["norm_post"][1]),
               in_sb=w_in[0:o_conv], in_conv=w_in[o_conv:o_gq], in_gla=in_gla, in_gate=w_in[o_gate:],
               conv_w=full["conv_w"], conv_b=row(repl["conv_b"][l]), ln_g=row(repl["conv_ln_g"][l]),
               ln_b=row(repl["conv_ln_b"][l]), alpha=alpha, b_alpha=row(repl["gla_b_alpha"][l]),
               gnorm=row(repl["gla_norm_g"][l]), branch=full["w_branch"], out=full["w_out"])
    return dict(
        pre0=row(full["norm_pre"][0]), post0=row(full["norm_post"][0]),
        pre2=row(full["norm_pre"][2]), post2=row(full["norm_post"][2]),
        ffn1=dict(gate=full["ffn1_w_gate"], up=full["ffn1_w_up"], down=full["ffn1_w_down"]),
        ffn2=dict(gate=full["ffn2_w_gate"], up=full["ffn2_w_up"], down=full["ffn2_w_down"]),
        mix=mix)


def _full_grads(g):
    mix = g["mix"]
    kd, vd = GLA_KEY_DIM, GLA_VALUE_DIM
    d_in = jnp.concatenate([mix["in_sb"], mix["in_conv"], mix["in_gla"][:2 * kd + 2 * vd + GLA_GATE_RANK],
                            mix["in_gate"]], axis=0)
    full = {
        "norm_pre": jnp.concatenate([g["ffn1"]["pre"], mix["pre"], g["ffn2"]["pre"]], axis=0),
        "norm_post": jnp.concatenate([g["ffn1"]["post"], mix["post"], g["ffn2"]["post"]], axis=0),
        "ffn1_w_gate": g["ffn1"]["gate"], "ffn1_w_up": g["ffn1"]["up"], "ffn1_w_down": g["ffn1"]["down"],
        "ffn2_w_gate": g["ffn2"]["gate"], "ffn2_w_up": g["ffn2"]["up"], "ffn2_w_down": g["ffn2"]["down"],
        "w_in": d_in, "conv_w": mix["conv_w"], "gla_w_alpha": mix["alpha"][:GLA_GATE_RANK],
        "w_branch": mix["branch"], "w_out": mix["out"],
    }
    repl = {"conv_b": mix["conv_b"], "conv_ln_g": mix["ln_g"], "conv_ln_b": mix["ln_b"],
            "gla_b_alpha": mix["b_alpha"], "gla_norm_g": mix["gnorm"]}
    return full, repl


def kernel(x, norm_pre, norm_post, ffn1_w_gate, ffn1_w_up, ffn1_w_down, ffn2_w_gate, ffn2_w_up, ffn2_w_down, w_in, conv_w, conv_b, conv_ln_g, conv_ln_b, gla_w_alpha, gla_b_alpha, gla_norm_g, w_branch, w_out, loss_target, m_norm_pre, m_norm_post, m_ffn1_w_gate, m_ffn1_w_up, m_ffn1_w_down, m_ffn2_w_gate, m_ffn2_w_up, m_ffn2_w_down, m_w_in, m_conv_w, m_conv_b, m_conv_ln_g, m_conv_ln_b, m_gla_w_alpha, m_gla_b_alpha, m_gla_norm_g, m_w_branch, m_w_out, v_norm_pre, v_norm_post, v_ffn1_w_gate, v_ffn1_w_up, v_ffn1_w_down, v_ffn2_w_gate, v_ffn2_w_up, v_ffn2_w_down, v_w_in, v_conv_w, v_conv_b, v_conv_ln_g, v_conv_ln_b, v_gla_w_alpha, v_gla_b_alpha, v_gla_norm_g, v_w_branch, v_w_out):
    weights = dict(norm_pre=norm_pre, norm_post=norm_post, ffn1_w_gate=ffn1_w_gate, ffn1_w_up=ffn1_w_up,
                   ffn1_w_down=ffn1_w_down, ffn2_w_gate=ffn2_w_gate, ffn2_w_up=ffn2_w_up, ffn2_w_down=ffn2_w_down,
                   w_in=w_in, conv_w=conv_w, conv_b=conv_b, conv_ln_g=conv_ln_g, conv_ln_b=conv_ln_b,
                   gla_w_alpha=gla_w_alpha, gla_b_alpha=gla_b_alpha, gla_norm_g=gla_norm_g, w_branch=w_branch,
                   w_out=w_out)
    moments_m = dict(norm_pre=m_norm_pre, norm_post=m_norm_post, ffn1_w_gate=m_ffn1_w_gate, ffn1_w_up=m_ffn1_w_up,
                     ffn1_w_down=m_ffn1_w_down, ffn2_w_gate=m_ffn2_w_gate, ffn2_w_up=m_ffn2_w_up,
                     ffn2_w_down=m_ffn2_w_down, w_in=m_w_in, conv_w=m_conv_w, conv_b=m_conv_b,
                     conv_ln_g=m_conv_ln_g, conv_ln_b=m_conv_ln_b, gla_w_alpha=m_gla_w_alpha,
                     gla_b_alpha=m_gla_b_alpha, gla_norm_g=m_gla_norm_g, w_branch=m_w_branch, w_out=m_w_out)
    moments_v = dict(norm_pre=v_norm_pre, norm_post=v_norm_post, ffn1_w_gate=v_ffn1_w_gate, ffn1_w_up=v_ffn1_w_up,
                     ffn1_w_down=v_ffn1_w_down, ffn2_w_gate=v_ffn2_w_gate, ffn2_w_up=v_ffn2_w_up,
                     ffn2_w_down=v_ffn2_w_down, w_in=v_w_in, conv_w=v_conv_w, conv_b=v_conv_b,
                     conv_ln_g=v_conv_ln_g, conv_ln_b=v_conv_ln_b, gla_w_alpha=v_gla_w_alpha,
                     gla_b_alpha=v_gla_b_alpha, gla_norm_g=v_gla_norm_g, w_branch=v_w_branch, w_out=v_w_out)
    n_layers = norm_pre.shape[0]
    t, d_model = x.shape[1], x.shape[2]

    layers = []
    for l in range(n_layers):
        gathered = _all_gather(_pack_weights(weights, l), f"gather_l{l}")
        full = _unpack_weights(gathered, weights, l)
        layers.append(_layer_weights(full, weights, l, d_model))

    sq, dx, grads = _local_step(x[0], loss_target[0], layers)
    loss = lax.psum(0.5 * jnp.sum(sq) / d_model, MESH_AXES)

    layer_grads = []
    for l in range(n_layers):
        packed = _pack_grads(*_full_grads(grads[l]))
        got = _exchange_sibling(packed, f"rs_sibling_l{l}")
        part = _add_own(packed, got, f"rs_add_l{l}")
        parts = _exchange_chips(part, f"rs_chips_l{l}")
        layer_grads.append(_unpack_grads(_sum_chips(parts, f"rs_sum_l{l}"), weights))

    g_out, d_out, m_out, v_out = [], [], [], []
    for name in _WEIGHTS:
        g = jnp.stack([lg[name] for lg in layer_grads])
        delta, new_m, new_v = _adamw_natural(weights[name], g, moments_m[name], moments_v[name], f"adamw_{name}")
        g_out.append(g)
        d_out.append(delta)
        m_out.append(new_m)
        v_out.append(new_v)
    return tuple([loss, dx[None]] + g_out + d_out + m_out + v_out)
```
